```python
import math
import jax
import jax.numpy as jnp
from jax import lax
import numpy as np

D_MODEL = 1024
BATCH = 4
SEQ = 8192
DEPTH = 2

GRID_W = 64
N_MEM = 256
HEAD_DIM = 64
N_MIXERS = 4
MIXER_WIDTH = D_MODEL // N_MIXERS
MIXER_HEADS = MIXER_WIDTH // HEAD_DIM
D_MIX = N_MIXERS * MIXER_WIDTH

NA_KH_MAX = 8
NA_KW = 16
NA_QCB = 16
NA_KCB = 32

RET_CHUNK = 128
ROPE_BASE = 10000.0

POOL_WINDOWS = (2, 4, 8, 16)
POOL_GROUP = MIXER_WIDTH // 4

GDN_CHUNK = 64
CONV_K = 4

MEM_HEADS = 4
MEM_HEAD_DIM = D_MODEL // MEM_HEADS

N_GROUPS = 4
EXPERTS_PER_GROUP = 8
N_EXPERTS = N_GROUPS * EXPERTS_PER_GROUP
TOP_K = 2
D_EXPERT = D_MODEL // 2
MOE_BLOCK = 128

EPS = 1e-6
F32 = jnp.float32

IN_SPLITS = (3 * MIXER_WIDTH, 7 * MIXER_WIDTH, 8 * MIXER_WIDTH, 12 * MIXER_WIDTH)
P_IN = 12 * MIXER_WIDTH + 4 * MIXER_HEADS

kernel_name = 'hybrid_parallel_heads_encoder'


def rmsnorm(x, g):
    xf = x.astype(F32)
    y = xf * lax.rsqrt(jnp.mean(xf * xf, axis=-1, keepdims=True) + EPS)
    return (y * g.astype(F32)).astype(x.dtype)


def neighbourhood_attention(q, k, v, rpb):
    b, s, _ = q.shape
    h, dh = MIXER_HEADS, HEAD_DIM
    rows = s // GRID_W
    kh = min(NA_KH_MAX, rows)
    ncb = GRID_W // NA_QCB
    r = np.arange(rows)
    key_rows = np.clip(r - kh // 2, 0, rows - kh)[:, None] + np.arange(kh)[None, :]
    band = np.clip(np.arange(ncb) * NA_QCB - NA_KW // 2, 0, GRID_W - NA_KCB)
    key_cols = band[:, None] + np.arange(NA_KCB)[None, :]
    q_cols = np.arange(GRID_W).reshape(ncb, NA_QCB)
    win = np.clip(q_cols - NA_KW // 2, 0, GRID_W - NA_KW)
    col_ok = (key_cols[:, None, :] >= win[:, :, None]) & (key_cols[:, None, :] < win[:, :, None] + NA_KW)
    n_keys = kh * NA_KCB
    mask = np.broadcast_to(col_ok[:, :, None, :], (ncb, NA_QCB, kh, NA_KCB)).reshape(ncb, NA_QCB, n_keys)
    key_idx = (key_rows[:, None, :, None] * GRID_W + key_cols[None, :, None, :]).reshape(rows, ncb, n_keys)
    d_row = key_rows - r[:, None] + (NA_KH_MAX - 1)
    d_col = np.clip(key_cols[:, None, :] - q_cols[:, :, None], -(NA_KW - 1), NA_KW - 1) + (NA_KW - 1)
    bias = rpb[:, d_row[:, None, None, :, None], d_col[None, :, :, None, :]]
    bias = bias.reshape(h, rows, ncb, NA_QCB, n_keys).astype(F32)
    qg = q.reshape(b, rows, ncb, NA_QCB, h, dh)
    kg = k.reshape(b, s, h, dh)[:, key_idx]
    vg = v.reshape(b, s, h, dh)[:, key_idx]
    scores = jnp.einsum('brnqhd,brnkhd->bhrnqk', qg, kg).astype(F32) * (dh ** -0.5) + bias
    scores = jnp.where(mask, scores, -jnp.inf)
    p = jax.nn.softmax(scores, axis=-1).astype(v.dtype)
    o = jnp.einsum('bhrnqk,brnkhd->brnqhd', p, vg)
    return o.reshape(b, s, h * dh)


def rotary(x):
    s, dh = x.shape[1], x.shape[-1]
    half = dh // 2
    inv = ROPE_BASE ** (-jnp.arange(half, dtype=F32) / half)
    ang = jnp.arange(s, dtype=F32)[:, None] * inv[None, :]
    cos = jnp.cos(ang)[None, :, None, :]
    sin = jnp.sin(ang)[None, :, None, :]
    x1, x2 = x[..., :half], x[..., half:]
    return jnp.concatenate([x1 * cos - x2 * sin, x1 * sin + x2 * cos], axis=-1)


def retention_direction(q, k, v, log_gamma, include_diag):
    b, h, s, d = q.shape
    c = RET_CHUNK
    nc = s // c
    q, k, v = (t.reshape(b, h, nc, c, d) for t in (q, k, v))
    pos = jnp.arange(c, dtype=F32)
    diff = pos[:, None] - pos[None, :]
    keep = diff >= 0 if include_diag else diff > 0
    decay = jnp.where(keep, jnp.exp(log_gamma[:, None, None] * jnp.maximum(diff, 0.0)), 0.0)
    scores = jnp.einsum('bhnid,bhnjd->bhnij', q, k) * decay[:, None]
    o = jnp.einsum('bhnij,bhnjd->bhnid', scores, v)
    k_dec = k * jnp.exp(log_gamma[:, None] * (c - 1 - pos))[:, None, :, None]
    kv = jnp.einsum('bhnjd,bhnje->nbhde', k_dec, v)
    chunk_decay = jnp.exp(log_gamma * c)[None, :, None, None]

    def step(state, kv_n):
        return state * chunk_decay + kv_n, state

    _, prev = lax.scan(step, jnp.zeros((b, h, d, d), kv.dtype), kv)
    q_dec = q * jnp.exp(log_gamma[:, None] * (pos + 1.0))[:, None, :, None]
    o = o + jnp.einsum('bhnid,nbhde->bhnie', q_dec, prev)
    return o.reshape(b, h, s, d)


def retention_mixer(q, k, v, gate, norm_g):
    b, s, _ = q.shape
    h, dh = MIXER_HEADS, HEAD_DIM
    heads = lambda t: t.astype(F32).reshape(b, s, h, dh)
    qh = jnp.swapaxes(rotary(heads(q)), 1, 2)
    kh = jnp.swapaxes(rotary(heads(k)) * (dh ** -0.5), 1, 2)
    vh = jnp.swapaxes(heads(v), 1, 2)
    log_g_fwd = jnp.log1p(-jnp.exp2(-5.0 - jnp.arange(h, dtype=F32)))
    log_g_bwd = log_g_fwd[::-1]
    flip = lambda t: jnp.flip(t, axis=2)
    o_f = retention_direction(qh, kh, vh, log_g_fwd, True)
    o_b = flip(retention_direction(flip(qh), flip(kh), flip(vh), log_g_bwd, False))
    o = jnp.swapaxes(o_f + o_b, 1, 2)
    mu = jnp.mean(o, axis=-1, keepdims=True)
    var = jnp.mean(jnp.square(o - mu), axis=-1, keepdims=True)
    o = ((o - mu) * lax.rsqrt(var + EPS)).reshape(b, s, h * dh) * norm_g.astype(F32)
    o = o * jax.nn.silu(gate.astype(F32))
    return o.astype(q.dtype)


def multiscale_pool(u, w_pool, scale):
    b, s, _ = u.shape
    uf = u.astype(F32)
    csum = jnp.concatenate([jnp.zeros((b, 1, MIXER_WIDTH), F32), jnp.cumsum(uf, axis=1)], axis=1)
    t = np.arange(s)
    diffs = []
    for gi, win in enumerate(POOL_WINDOWS):
        lo = np.clip(t - win // 2, 0, s)
        hi = np.clip(t + win // 2, 0, s)
        sl = slice(gi * POOL_GROUP, (gi + 1) * POOL_GROUP)
        mean = (csum[:, hi, sl] - csum[:, lo, sl]) / (hi - lo).astype(np.float32)[None, :, None]
        diffs.append(mean - uf[..., sl])
    d = jnp.stack(diffs, axis=2)
    y = jnp.einsum('bsgc,gce->bsge', d, w_pool.astype(F32)).reshape(b, s, MIXER_WIDTH) * scale.astype(F32)
    return y.astype(u.dtype)


def short_conv(u, w):
    pad = (CONV_K // 2, CONV_K - 1 - CONV_K // 2)
    return lax.conv_general_dilated(u, w[:, None, :], window_strides=(1,), padding=(pad,),
                                    dimension_numbers=('NWC', 'WIO', 'NWC'),
                                    feature_group_count=u.shape[-1])


def gated_delta_direction(q, k, v, g, beta):
    b, h, s, d = q.shape
    c = GDN_CHUNK
    nc = s // c
    q, k, v = (t.reshape(b, h, nc, c, d) for t in (q, k, v))
    g = g.reshape(b, h, nc, c)
    beta = beta.reshape(b, h, nc, c)
    gc = jnp.cumsum(g, axis=-1)
    idx = jnp.arange(c)
    incl = idx[:, None] >= idx[None, :]
    strict = idx[:, None] > idx[None, :]
    gdiff = gc[..., :, None] - gc[..., None, :]
    decay = jnp.where(incl, jnp.exp(jnp.where(incl, gdiff, 0.0)), 0.0)
    k_beta = k * beta[..., None]
    lmat = jnp.where(strict, jnp.einsum('bhnid,bhnjd->bhnij', k_beta, k) * decay, 0.0)
    rhs = jnp.concatenate([v * beta[..., None], k_beta * jnp.exp(gc)[..., None]], axis=-1)
    sol = lax.linalg.triangular_solve(lmat + jnp.eye(c, dtype=F32), rhs, left_side=True, lower=True)
    u_val, w_dec = sol[..., :d], sol[..., d:]
    attn = jnp.einsum('bhnid,bhnjd->bhnij', q, k) * decay
    q_dec = q * jnp.exp(gc)[..., None]
    g_last = gc[..., -1]
    k_state = k * jnp.exp(g_last[..., None] - gc)[..., None]

    def step(state, inp):
        q_n, u_n, w_n, a_n, ks_n, gl_n = inp
        v_new = u_n - jnp.einsum('bhcd,bhde->bhce', w_n, state)
        o_n = jnp.einsum('bhcd,bhde->bhce', q_n, state) + jnp.einsum('bhij,bhje->bhie', a_n, v_new)
        state = state * jnp.exp(gl_n)[..., None, None] + jnp.einsum('bhcd,bhce->bhde', ks_n, v_new)
        return state, o_n

    to_scan = lambda t: jnp.moveaxis(t, 2, 0)
    xs = (to_scan(q_dec), to_scan(u_val), to_scan(w_dec), to_scan(attn), to_scan(k_state), to_scan(g_last))
    _, o = lax.scan(step, jnp.zeros((b, h, d, d), F32), xs)
    return jnp.moveaxis(o, 0, 2).reshape(b, h, s, d)


def gated_deltanet_mixer(qkv, gate, ab, conv_w, a_log, dt_bias, norm_g):
    b, s, _ = qkv.shape
    h, dh = MIXER_HEADS, HEAD_DIM
    u = jax.nn.silu(short_conv(qkv.astype(F32), conv_w.astype(F32)))
    q, k, v = (jnp.swapaxes(t.reshape(b, s, h, dh), 1, 2) for t in jnp.split(u, 3, axis=-1))
    q = q * lax.rsqrt(jnp.sum(q * q, axis=-1, keepdims=True) + EPS) * (dh ** -0.5)
    k = k * lax.rsqrt(jnp.sum(k * k, axis=-1, keepdims=True) + EPS)
    a_f, a_b, b_f, b_b = jnp.transpose(ab.astype(F32).reshape(b, s, 4, h), (2, 0, 3, 1))
    a_log = a_log.astype(F32)
    dt_bias = dt_bias.astype(F32)
    g_f = -jnp.exp(a_log[0])[None, :, None] * jax.nn.softplus(a_f + dt_bias[0][None, :, None])
    g_b = -jnp.exp(a_log[1])[None, :, None] * jax.nn.softplus(a_b + dt_bias[1][None, :, None])
    flip = lambda t: jnp.flip(t, axis=2)
    o_f = gated_delta_direction(q, k, v, g_f, jax.nn.sigmoid(b_f))
    o_b = flip(gated_delta_direction(flip(q), flip(k), flip(v), flip(g_b), flip(jax.nn.sigmoid(b_b))))
    o = jnp.swapaxes(o_f + o_b, 1, 2)
    o = o * lax.rsqrt(jnp.mean(o * o, axis=-1, keepdims=True) + EPS) * norm_g.astype(F32)
    o = o * jax.nn.silu(gate.astype(F32).reshape(b, s, h, dh))
    return o.reshape(b, s, MIXER_WIDTH).astype(qkv.dtype)


def memory_cross_attention(hq, m, w_q, w_k, w_v, w_o):
    b, s, _ = hq.shape
    nm = m.shape[1]
    q = (hq @ w_q).reshape(b, s, MEM_HEADS, MEM_HEAD_DIM)
    k = (m @ w_k).reshape(b, nm, MEM_HEADS, MEM_HEAD_DIM)
    v = (m @ w_v).reshape(b, nm, MEM_HEADS, MEM_HEAD_DIM)
    scores = jnp.einsum('bshe,bmhe->bhsm', q, k).astype(F32) * (MEM_HEAD_DIM ** -0.5)
    p = jax.nn.softmax(scores, axis=-1).astype(v.dtype)
    o = jnp.einsum('bhsm,bmhe->bshe', p, v).reshape(b, s, D_MODEL)
    return o @ w_o


def hierarchical_moe(hx, w_group, b_group, w_router, b_router, w_gate, w_up, w_down):
    b, s, d = hx.shape
    n = b * s
    hf = hx.reshape(n, d)
    g_prob = jax.nn.softmax((hf @ w_group).astype(F32) + b_group.astype(F32), axis=-1)
    g_p, g_idx = lax.top_k(g_prob, 1)
    e_logits = ((hf @ w_router).astype(F32) + b_router.astype(F32)).reshape(n, N_GROUPS, EXPERTS_PER_GROUP)
    e_logits = jnp.take_along_axis(e_logits, g_idx[:, :, None], axis=1)[:, 0]
    e_p, e_loc = lax.top_k(jax.nn.softmax(e_logits, axis=-1), TOP_K)
    gate_w = g_p * e_p / jnp.sum(e_p, axis=-1, keepdims=True)
    expert = g_idx * EXPERTS_PER_GROUP + e_loc
    a = n * TOP_K
    flat_e = expert.reshape(a)
    order = jnp.argsort(flat_e)
    e_sorted = flat_e[order]
    tok_sorted = (order // TOP_K).astype(jnp.int32)
    w_sorted = gate_w.reshape(a)[order]
    counts = jax.ops.segment_sum(jnp.ones((a,), jnp.int32), flat_e, num_segments=N_EXPERTS)
    padded = (counts + MOE_BLOCK - 1) // MOE_BLOCK * MOE_BLOCK
    pad_end = jnp.cumsum(padded)
    pad_start = pad_end - padded
    start = jnp.cumsum(counts) - counts
    dest = pad_start[e_sorted] + jnp.arange(a, dtype=jnp.int32) - start[e_sorted]
    cap = a + N_EXPERTS * MOE_BLOCK
    n_blocks = cap // MOE_BLOCK
    buf_tok = jnp.zeros((cap,), jnp.int32).at[dest].set(tok_sorted)
    buf_w = jnp.zeros((cap,), F32).at[dest].set(w_sorted)
    blk_start = jnp.arange(n_blocks, dtype=jnp.int32) * MOE_BLOCK
    blk_expert = jnp.minimum(jnp.searchsorted(pad_end, blk_start, side='right'), N_EXPERTS - 1)

    def expert_block(args):
        tok, wt, e = args
        xb = hf[tok]
        y = (jax.nn.silu(xb @ w_gate[e]) * (xb @ w_up[e])) @ w_down[e]
        return y * wt[:, None].astype(y.dtype)

    ys = lax.map(expert_block, (buf_tok.reshape(n_blocks, MOE_BLOCK), buf_w.reshape(n_blocks, MOE_BLOCK), blk_expert))
    out = jax.ops.segment_sum(ys.reshape(cap, d), buf_tok, num_segments=n)
    return out.reshape(b, s, d)


def setup_inputs(seed: int = 0) -> dict:
    key = jax.random.key(seed)
    k = jax.random.split(key, 28)
    L, D, W, H = DEPTH, D_MODEL, MIXER_WIDTH, MIXER_HEADS

    def nrm(i, shape, scale):
        return jax.random.normal(k[i], shape, F32) * scale

    def gain(i, shape):
        return 1.0 + 0.05 * jax.random.normal(k[i], shape, F32)

    dt = jnp.exp(jax.random.uniform(k[10], (L, 2, H), F32, math.log(1e-3), math.log(1e-1)))
    return {
        'x': nrm(0, (BATCH, SEQ, D), 1.0),
        'mem': nrm(1, (BATCH, N_MEM, D), 1.0),
        'mix_norm_g': gain(2, (L, D)),
        'w_in': nrm(3, (L, D, P_IN), D ** -0.5),
        'na_rpb': nrm(4, (L, H, 2 * NA_KH_MAX - 1, 2 * NA_KW - 1), 0.1),
        'ret_norm_g': gain(5, (L, W)),
        'pool_w': nrm(6, (L, len(POOL_WINDOWS), POOL_GROUP, POOL_GROUP), POOL_GROUP ** -0.5),
        'pool_scale': gain(7, (L, W)),
        'gdn_conv_w': nrm(8, (L, CONV_K, 3 * W), CONV_K ** -0.5),
        'gdn_a_log': jnp.log(jax.random.uniform(k[9], (L, 2, H), F32, 1.0, 16.0)),
        'gdn_dt_bias': dt + jnp.log(-jnp.expm1(-dt)),
        'gdn_norm_g': gain(11, (L, HEAD_DIM)),
        'w_out': nrm(12, (L, D_MIX, D), D_MIX ** -0.5),
        'mem_q_norm_g': gain(13, (L, D)),
        'mem_kv_norm_g': gain(14, (L, D)),
        'mem_w_q': nrm(15, (L, D, D), D ** -0.5),
        'mem_w_k': nrm(16, (L, D, D), D ** -0.5),
        'mem_w_v': nrm(17, (L, D, D), D ** -0.5),
        'mem_w_o': nrm(18, (L, D, D), D ** -0.5),
        'ffn_norm_g': gain(19, (L, D)),
        'w_group': nrm(20, (L, D, N_GROUPS), D ** -0.5),
        'b_group': nrm(21, (L, N_GROUPS), 0.01),
        'w_router': nrm(22, (L, D, N_EXPERTS), D ** -0.5),
        'b_router': nrm(23, (L, N_EXPERTS), 0.01),
        'w_gate': nrm(24, (L, N_EXPERTS, D, D_EXPERT), D ** -0.5),
        'w_up': nrm(25, (L, N_EXPERTS, D, D_EXPERT), D ** -0.5),
        'w_down': nrm(26, (L, N_EXPERTS, D_EXPERT, D), D_EXPERT ** -0.5),
        'final_norm_g': gain(27, (D,)),
    }


def reference(x, mem, mix_norm_g, w_in, na_rpb, ret_norm_g, pool_w, pool_scale, gdn_conv_w,
              gdn_a_log, gdn_dt_bias, gdn_norm_g, w_out, mem_q_norm_g, mem_kv_norm_g, mem_w_q,
              mem_w_k, mem_w_v, mem_w_o, ffn_norm_g, w_group, b_group, w_router, b_router,
              w_gate, w_up, w_down, final_norm_g):
    W = MIXER_WIDTH
    for l in range(DEPTH):
        h = rmsnorm(x, mix_norm_g[l])
        proj = h @ w_in[l]
        p_a, p_b, p_c, p_d, p_ab = jnp.split(proj, IN_SPLITS, axis=-1)
        qa, ka, va = jnp.split(p_a, 3, axis=-1)
        y_a = neighbourhood_attention(qa, ka, va, na_rpb[l])
        qb, kb, vb, gb = jnp.split(p_b, 4, axis=-1)
        y_b = retention_mixer(qb, kb, vb, gb, ret_norm_g[l])
        y_c = multiscale_pool(p_c, pool_w[l], pool_scale[l])
        y_d = gated_deltanet_mixer(p_d[..., :3 * W], p_d[..., 3 * W:], p_ab, gdn_conv_w[l],
                                   gdn_a_log[l], gdn_dt_bias[l], gdn_norm_g[l])
        x = x + jnp.concatenate([y_a, y_b, y_c, y_d], axis=-1) @ w_out[l]
        x = x + memory_cross_attention(rmsnorm(x, mem_q_norm_g[l]), rmsnorm(mem, mem_kv_norm_g[l]),
                                       mem_w_q[l], mem_w_k[l], mem_w_v[l], mem_w_o[l])
        x = x + hierarchical_moe(rmsnorm(x, ffn_norm_g[l]), w_group[l], b_group[l], w_router[l],
                                 b_router[l], w_gate[l], w_up[l], w_down[l])
    return rmsnorm(x, final_norm_g)
```

```python
import functools
import math

import jax
import jax.numpy as jnp
import numpy as np
from jax import lax
from jax.experimental import pallas as pl
from jax.experimental.pallas import tpu as pltpu

F32 = jnp.float32
BF16 = jnp.bfloat16
HIGHEST = lax.Precision.HIGHEST

D_MODEL = 1024
GRID_W = 64
HEAD_DIM = 64
MIXER_WIDTH = 256
MIXER_HEADS = 4
NA_KH = 8
NA_KW = 16
RET_CHUNK = 128
ROPE_BASE = 10000.0
POOL_WINDOWS = (2, 4, 8, 16)
POOL_GROUP = 64
GDN_CHUNK = 64
CONV_K = 4
MEM_HEADS = 4
MEM_HEAD_DIM = 256
N_GROUPS = 4
EXPERTS_PER_GROUP = 8
N_EXPERTS = 32
TOP_K = 2
D_EXPERT = 512
EPS = 1e-6

P_IN_PAD = 3200
V7X_VMEM_LIMIT = 56 * 1024 * 1024
LANES = 128
SUBLANES = 8
NEG_BIG = -1e30

PROJ_TILE = 512
POOL_TILE = 512
POST_TILE = 512
NA_ROWS_PER_STEP = 8
GDN_CHUNKS_PER_STEP = 2
MOE_BLOCK = 256
HALO = 8


def _cparams(*sem):
    return pltpu.CompilerParams(dimension_semantics=sem, vmem_limit_bytes=V7X_VMEM_LIMIT)


def _dot(a, b):
    return jnp.dot(a, b, preferred_element_type=F32)


def _dot_nt(a, b):
    return lax.dot_general(a, b, (((1,), (1,)), ((), ())), preferred_element_type=F32)


def _dot_tn(a, b, precision=None):
    return lax.dot_general(a, b, (((0,), (0,)), ((), ())), preferred_element_type=F32, precision=precision)


def _silu(x):
    return x * (1.0 / (1.0 + jnp.exp(-x)))


def _rms(x, g):
    return x * lax.rsqrt(jnp.mean(x * x, axis=-1, keepdims=True) + EPS) * g


def _norm_proj_kernel(has_y, *refs):
    if has_y:
        x_ref, y_ref, g_ref, w_ref, xo_ref, pa_ref, pb_ref, pc_ref, pd_ref, pab_ref = refs
        x = x_ref[...] + y_ref[:, :D_MODEL] + y_ref[:, D_MODEL:]
        xo_ref[...] = x
    else:
        x_ref, g_ref, w_ref, pa_ref, pb_ref, pc_ref, pd_ref, pab_ref = refs
        x = x_ref[...]
    h = _rms(x, g_ref[...]).astype(BF16)
    w = MIXER_WIDTH
    pa_ref[...] = _dot(h, w_ref[:, 0:3 * w]).astype(BF16)
    pb_ref[...] = _dot(h, w_ref[:, 3 * w:7 * w])
    pc_ref[...] = _dot(h, w_ref[:, 7 * w:8 * w])
    pd_ref[...] = _dot(h, w_ref[:, 8 * w:12 * w])
    pab_ref[...] = _dot(h, w_ref[:, 12 * w:P_IN_PAD])


def _norm_proj(x, y, g, w_pad):
    n = x.shape[0]
    tm = PROJ_TILE
    w = MIXER_WIDTH
    row = lambda c: pl.BlockSpec((tm, c), lambda i: (i, 0))
    const = lambda a: pl.BlockSpec(a.shape, lambda i: (0,) * a.ndim)
    outs = [jax.ShapeDtypeStruct((n, 3 * w), BF16), jax.ShapeDtypeStruct((n, 4 * w), F32),
            jax.ShapeDtypeStruct((n, w), F32), jax.ShapeDtypeStruct((n, 4 * w), F32),
            jax.ShapeDtypeStruct((n, LANES), F32)]
    out_specs = [row(3 * w), row(4 * w), row(w), row(4 * w), row(LANES)]
    if y is None:
        ins, in_specs = (x, g, w_pad), [row(D_MODEL), const(g), const(w_pad)]
    else:
        ins, in_specs = (x, y, g, w_pad), [row(D_MODEL), row(2 * D_MODEL), const(g), const(w_pad)]
        outs = [jax.ShapeDtypeStruct((n, D_MODEL), F32)] + outs
        out_specs = [row(D_MODEL)] + out_specs
    res = pl.pallas_call(
        functools.partial(_norm_proj_kernel, y is not None),
        grid=(n // tm,), in_specs=in_specs, out_specs=out_specs, out_shape=outs,
        compiler_params=_cparams("parallel"),
    )(*ins)
    return res if y is not None else [x] + list(res)


def _na_bias_table(rpb):
    e = np.arange(NA_KH)
    j = np.arange(NA_KH)
    d_row = j[None, :] - e[:, None] + (NA_KH - 1)
    qc = np.arange(GRID_W)
    kc = np.arange(GRID_W)
    d_col = np.clip(kc[None, :] - qc[:, None], -(NA_KW - 1), NA_KW - 1) + (NA_KW - 1)
    win = np.clip(qc - NA_KW // 2, 0, GRID_W - NA_KW)
    ok = (kc[None, :] >= win[:, None]) & (kc[None, :] < win[:, None] + NA_KW)
    bias = rpb.astype(F32)[:, d_row[:, None, :, None], d_col[None, :, None, :]]
    bias = jnp.where(ok[None, None, :, None, :], bias, NEG_BIG)
    return bias.reshape(MIXER_HEADS, NA_KH, GRID_W, NA_KH * GRID_W)


def _na_kernel(rows, q_ref, k_ref, v_ref, t_ref, o_ref):
    i = pl.program_id(1)
    dh = HEAD_DIM

    def row_body(rr, carry):
        r = i * NA_ROWS_PER_STEP + rr
        kr0 = jnp.clip(r - NA_KH // 2, 0, rows - NA_KH)
        e = r - kr0
        q = q_ref[pl.ds(pl.multiple_of(rr * GRID_W, GRID_W), GRID_W), :]
        k0 = pl.multiple_of(kr0 * GRID_W, GRID_W)
        kb = k_ref[pl.ds(k0, NA_KH * GRID_W), :]
        vb = v_ref[pl.ds(k0, NA_KH * GRID_W), :]
        outs = []
        for h in range(MIXER_HEADS):
            sl = slice(h * dh, (h + 1) * dh)
            s = _dot_nt(q[:, sl], kb[:, sl]) * (dh ** -0.5) + t_ref[h, e]
            m = jnp.max(s, axis=-1, keepdims=True)
            p = jnp.exp(s - m)
            l = jnp.sum(p, axis=-1, keepdims=True)
            outs.append(_dot(p.astype(BF16), vb[:, sl]) / l)
        o_ref[pl.ds(pl.multiple_of(rr * GRID_W, GRID_W), GRID_W), :] = jnp.concatenate(outs, axis=-1).astype(BF16)
        return carry

    lax.fori_loop(0, NA_ROWS_PER_STEP, row_body, 0)


def _neighbourhood_attention(p_a, table, b, s):
    rows = s // GRID_W
    assert rows >= NA_KH and rows % NA_ROWS_PER_STEP == 0
    steps = rows // NA_ROWS_PER_STEP
    tq = NA_ROWS_PER_STEP * GRID_W
    w = MIXER_WIDTH
    return pl.pallas_call(
        functools.partial(_na_kernel, rows),
        grid=(b, steps),
        in_specs=[pl.BlockSpec((tq, w), lambda bi, i: (bi * steps + i, 0)),
                  pl.BlockSpec((s, w), lambda bi, i: (bi, 1)),
                  pl.BlockSpec((s, w), lambda bi, i: (bi, 2)),
                  pl.BlockSpec(table.shape, lambda bi, i: (0, 0, 0, 0))],
        out_specs=pl.BlockSpec((tq, w), lambda bi, i: (bi * steps + i, 0)),
        out_shape=jax.ShapeDtypeStruct((b * s, w), BF16),
        compiler_params=_cparams("parallel", "arbitrary"),
    )(p_a, p_a, p_a, table)


def _retention_tables(s):
    h, dh, c = MIXER_HEADS, HEAD_DIM, RET_CHUNK
    half = dh // 2
    inv = ROPE_BASE ** (-jnp.arange(half, dtype=F32) / half)
    ang = jnp.arange(s, dtype=F32)[:, None] * inv[None, :]
    cos, sin = jnp.cos(ang), jnp.sin(ang)
    zero = jnp.zeros_like(sin)
    cos_t = jnp.tile(jnp.concatenate([cos, cos], axis=-1), (1, h))
    sin_lo = jnp.tile(jnp.concatenate([-sin, zero], axis=-1), (1, h))
    sin_hi = jnp.tile(jnp.concatenate([zero, sin], axis=-1), (1, h))
    log_f = np.log1p(-np.exp2(-5.0 - np.arange(h, dtype=np.float64)))
    log_b = log_f[::-1]
    pos = np.arange(c, dtype=np.float64)
    diff = pos[:, None] - pos[None, :]
    dmat = np.where(diff >= 0, np.exp(log_f[:, None, None] * np.maximum(diff, 0.0)), 0.0) \
        + np.where(diff < 0, np.exp(log_b[:, None, None] * np.maximum(-diff, 0.0)), 0.0)
    lanes = lambda t: np.repeat(t.T, dh, axis=1)
    dec = np.stack([lanes(np.exp(log_f[:, None] * (pos + 1.0))),
                    lanes(np.exp(log_f[:, None] * (c - 1.0 - pos))),
                    lanes(np.exp(log_b[:, None] * (c - pos))),
                    lanes(np.exp(log_b[:, None] * pos))])
    chunk_f = [float(np.exp(v * c)) for v in log_f]
    chunk_b = [float(np.exp(v * c)) for v in log_b]
    return cos_t, sin_lo, sin_hi, jnp.asarray(dmat, F32), jnp.asarray(dec, F32), chunk_f, chunk_b


def _rotary(x, cos, sin_lo, sin_hi):
    w = x.shape[-1]
    return x * cos + pltpu.roll(x, w - HEAD_DIM // 2, 1) * sin_lo + pltpu.roll(x, HEAD_DIM // 2, 1) * sin_hi


def _retention_kernel(nc, chunk_f, chunk_b, q_ref, k_ref, v_ref, gate_ref, cos_ref, slo_ref, shi_ref,
                      dmat_ref, dec_ref, ng_ref, o_ref, sf_ref, sb_ref, sball_ref):
    t = pl.program_id(1)
    dh = HEAD_DIM
    cos, slo, shi = cos_ref[...], slo_ref[...], shi_ref[...]
    kr = _rotary(k_ref[...], cos, slo, shi) * (dh ** -0.5)
    v = v_ref[...]

    @pl.when(t == 0)
    def _():
        sf_ref[...] = jnp.zeros_like(sf_ref)
        sb_ref[...] = jnp.zeros_like(sb_ref)

    @pl.when(t < nc)
    def _():
        c = nc - 1 - t
        kd = (kr * dec_ref[3]).astype(BF16)
        vb = v.astype(BF16)
        for h in range(MIXER_HEADS):
            sl = slice(h * dh, (h + 1) * dh)
            sball_ref[c, h] = sb_ref[h]
            sb_ref[h] = sb_ref[h] * chunk_b[h] + _dot_tn(kd[:, sl], vb[:, sl])

    @pl.when(t >= nc)
    def _():
        c = t - nc
        qr = _rotary(q_ref[...], cos, slo, shi)
        qb = qr.astype(BF16)
        kb = kr.astype(BF16)
        vb = v.astype(BF16)
        qf = (qr * dec_ref[0]).astype(BF16)
        qbk = (qr * dec_ref[2]).astype(BF16)
        kd = (kr * dec_ref[1]).astype(BF16)
        outs = []
        for h in range(MIXER_HEADS):
            sl = slice(h * dh, (h + 1) * dh)
            sc = _dot_nt(qb[:, sl], kb[:, sl]) * dmat_ref[h]
            o = _dot(sc.astype(BF16), vb[:, sl])
            o = o + _dot(qf[:, sl], sf_ref[h].astype(BF16))
            o = o + _dot(qbk[:, sl], sball_ref[c, h].astype(BF16))
            sf_ref[h] = sf_ref[h] * chunk_f[h] + _dot_tn(kd[:, sl], vb[:, sl])
            mu = jnp.mean(o, axis=-1, keepdims=True)
            oc = o - mu
            var = jnp.mean(oc * oc, axis=-1, keepdims=True)
            outs.append(oc * lax.rsqrt(var + EPS))
        y = jnp.concatenate(outs, axis=-1) * ng_ref[...]
        o_ref[...] = (y * _silu(gate_ref[...])).astype(BF16)


def _retention(p_b, norm_g, tables, b, s):
    cos_t, sin_lo, sin_hi, dmat, dec, chunk_f, chunk_b = tables
    c = RET_CHUNK
    nc = s // c
    w = MIXER_WIDTH
    chunk = lambda t: jnp.where(t < nc, nc - 1 - t, t - nc)
    col = lambda j: pl.BlockSpec((c, w), lambda bi, t: (bi * nc + chunk(t), j))
    tab = pl.BlockSpec((c, w), lambda bi, t: (chunk(t), 0))
    const = lambda a: pl.BlockSpec(a.shape, lambda bi, t: (0,) * a.ndim)
    return pl.pallas_call(
        functools.partial(_retention_kernel, nc, chunk_f, chunk_b),
        grid=(b, 2 * nc),
        in_specs=[col(0), col(1), col(2), col(3), tab, tab, tab, const(dmat), const(dec), const(norm_g)],
        out_specs=pl.BlockSpec((c, w), lambda bi, t: (bi * nc + jnp.maximum(t - nc, 0), 0)),
        out_shape=jax.ShapeDtypeStruct((b * s, w), BF16),
        scratch_shapes=[pltpu.VMEM((MIXER_HEADS, HEAD_DIM, HEAD_DIM), F32),
                        pltpu.VMEM((MIXER_HEADS, HEAD_DIM, HEAD_DIM), F32),
                        pltpu.VMEM((nc, MIXER_HEADS, HEAD_DIM, HEAD_DIM), F32)],
        compiler_params=_cparams("arbitrary", "arbitrary"),
    )(p_b, p_b, p_b, p_b, cos_t, sin_lo, sin_hi, dmat, dec, norm_g)


def _halo_specs(tile, width, n_rows, block_of):
    per = tile // HALO
    last = n_rows // HALO - 1
    main = pl.BlockSpec((tile, width), lambda *g: (block_of(*g), 0))
    prev = pl.BlockSpec((HALO, width), lambda *g: (jnp.maximum(block_of(*g) * per - 1, 0), 0))
    nxt = pl.BlockSpec((HALO, width), lambda *g: (jnp.minimum((block_of(*g) + 1) * per, last), 0))
    return main, prev, nxt


def _fill_padded(pad_ref, main, prev, nxt, first, last):
    t = main.shape[0]
    pad_ref[0:HALO, :] = jnp.where(first, 0.0, prev)
    pad_ref[HALO:HALO + t, :] = main
    pad_ref[HALO + t:2 * HALO + t, :] = jnp.where(last, 0.0, nxt)


def _pool_kernel(s, tiles_per_seq, u_ref, prev_ref, next_ref, w_ref, scale_ref, o_ref, pad_ref):
    t = POOL_TILE
    tseq = pl.program_id(0) % tiles_per_seq
    u = u_ref[...]
    _fill_padded(pad_ref, u, prev_ref[...], next_ref[...], tseq == 0, tseq == tiles_per_seq - 1)
    lane = lax.broadcasted_iota(jnp.int32, (1, MIXER_WIDTH), 1)
    half = jnp.full((1, MIXER_WIDTH), POOL_WINDOWS[0] // 2, jnp.int32)
    for gi in range(1, len(POOL_WINDOWS)):
        half = jnp.where(lane >= gi * POOL_GROUP, POOL_WINDOWS[gi] // 2, half)
    max_half = POOL_WINDOWS[-1] // 2
    acc = jnp.zeros((t, MIXER_WIDTH), F32)
    for d in range(-max_half, max_half):
        inside = (d >= -half) & (d < half)
        acc = acc + jnp.where(inside, pad_ref[HALO + d:HALO + d + t, :], 0.0)
    pos = tseq * t + lax.broadcasted_iota(jnp.int32, (t, 1), 0)
    count = jnp.minimum(pos + half, s) - jnp.maximum(pos - half, 0)
    diff = acc / count.astype(F32) - u
    o_ref[...] = (_dot(diff.astype(BF16), w_ref[...]) * scale_ref[...]).astype(BF16)


def _pool(p_c, w_blockdiag, scale, b, s):
    t = POOL_TILE
    tiles_per_seq = s // t
    main, prev, nxt = _halo_specs(t, MIXER_WIDTH, b * s, lambda i: i)
    const = lambda a: pl.BlockSpec(a.shape, lambda i: (0,) * a.ndim)
    return pl.pallas_call(
        functools.partial(_pool_kernel, s, tiles_per_seq),
        grid=(b * tiles_per_seq,),
        in_specs=[main, prev, nxt, const(w_blockdiag), const(scale)],
        out_specs=pl.BlockSpec((t, MIXER_WIDTH), lambda i: (i, 0)),
        out_shape=jax.ShapeDtypeStruct((b * s, MIXER_WIDTH), BF16),
        scratch_shapes=[pltpu.VMEM((t + 2 * HALO, MIXER_WIDTH), F32)],
        compiler_params=_cparams("parallel"),
    )(p_c, p_c, p_c, w_blockdiag, scale)


def _gdn_direction(forward, u, ab, alog, dtb, tri_col, tri_row, s_ref, o_ref):
    c, dh, w = GDN_CHUNK, HEAD_DIM, MIXER_WIDTH
    t = u.shape[0]
    a_off, b_off = (0, 2 * MIXER_HEADS) if forward else (MIXER_HEADS, 3 * MIXER_HEADS)
    x = ab + dtb
    g_all = -jnp.exp(alog) * (jnp.maximum(x, 0.0) + jnp.log(1.0 + jnp.exp(-jnp.abs(x))))
    beta_all = 1.0 / (1.0 + jnp.exp(-ab))
    gc_col = jnp.dot(tri_col, g_all, preferred_element_type=F32, precision=HIGHEST)
    gc_row = _dot_tn(g_all, tri_row, precision=HIGHEST)
    ii = lax.broadcasted_iota(jnp.int32, (c, c), 0)
    jj = lax.broadcasted_iota(jnp.int32, (c, c), 1)
    incl = (ii >= jj) if forward else (ii <= jj)
    strict = (ii > jj) if forward else (ii < jj)
    order = range(t // c) if forward else range(t // c - 1, -1, -1)
    last = c - 1 if forward else 0
    for ci in order:
        rs = slice(ci * c, (ci + 1) * c)
        for h in range(MIXER_HEADS):
            q = u[rs, h * dh:(h + 1) * dh]
            k = u[rs, w + h * dh:w + (h + 1) * dh]
            v = u[rs, 2 * w + h * dh:2 * w + (h + 1) * dh]
            q = q * lax.rsqrt(jnp.sum(q * q, axis=-1, keepdims=True) + EPS) * (dh ** -0.5)
            k = k * lax.rsqrt(jnp.sum(k * k, axis=-1, keepdims=True) + EPS)
            gcc = gc_col[rs, a_off + h:a_off + h + 1]
            gcr = gc_row[a_off + h:a_off + h + 1, rs]
            beta = beta_all[rs, b_off + h:b_off + h + 1]
            g_last = gcc[last:last + 1, :]
            decay = jnp.where(incl, jnp.exp(jnp.where(incl, gcc - gcr, 0.0)), 0.0)
            kq = _dot_nt(jnp.concatenate([k, q], axis=0).astype(BF16), k.astype(BF16))
            lmat = jnp.where(strict, kq[:c] * beta * decay, 0.0)
            attn = kq[c:] * decay
            xinv = -lmat
            lp = lmat
            for _ in range(int(math.log2(c)) - 1):
                lpb = lp.astype(BF16)
                lp = _dot(lpb, lpb)
                xinv = xinv + lp + _dot(xinv.astype(BF16), lp.astype(BF16))
            rhs = jnp.concatenate([v * beta, k * (beta * jnp.exp(gcc))], axis=-1)
            sol = rhs + _dot(xinv.astype(BF16), rhs.astype(BF16))
            u_val, w_dec = sol[:, :dh], sol[:, dh:]
            q_dec = q * jnp.exp(gcc)
            k_state = k * jnp.exp(g_last - gcc)
            state = s_ref[h]
            ws = _dot(jnp.concatenate([w_dec, q_dec], axis=0).astype(BF16), state.astype(BF16))
            v_new = u_val - ws[:c]
            o = ws[c:] + _dot(attn.astype(BF16), v_new.astype(BF16))
            s_ref[h] = state * jnp.exp(g_last) + _dot_tn(k_state.astype(BF16), v_new.astype(BF16))
            o_ref[rs, h * dh:(h + 1) * dh] = o


def _gdn_kernel(nb, uf_ref, pf_ref, nf_ref, abf_ref, ub_ref, pb_ref, nb_ref, abb_ref, cw_ref, alog_ref,
                dtb_ref, tril_ref, triu_ref, of_ref, ob_ref, sf_ref, sb_ref, pad_ref):
    n = pl.program_id(1)
    t = uf_ref.shape[0]

    @pl.when(n == 0)
    def _():
        sf_ref[...] = jnp.zeros_like(sf_ref)
        sb_ref[...] = jnp.zeros_like(sb_ref)

    def conv_silu(main_ref, prev_ref, next_ref, blk):
        _fill_padded(pad_ref, main_ref[...], prev_ref[...], next_ref[...], blk == 0, blk == nb - 1)
        acc = jnp.zeros((t, 3 * MIXER_WIDTH), F32)
        for k in range(CONV_K):
            off = HALO + k - CONV_K // 2
            acc = acc + pad_ref[off:off + t, :] * cw_ref[k:k + 1, :]
        return _silu(acc)

    alog, dtb = alog_ref[...], dtb_ref[...]
    tril, triu = tril_ref[...], triu_ref[...]
    u_f = conv_silu(uf_ref, pf_ref, nf_ref, n)
    _gdn_direction(True, u_f, abf_ref[...], alog, dtb, tril, triu, sf_ref, of_ref)
    u_b = conv_silu(ub_ref, pb_ref, nb_ref, nb - 1 - n)
    _gdn_direction(False, u_b, abb_ref[...], alog, dtb, triu, tril, sb_ref, ob_ref)


def _gated_deltanet(p_d, p_ab, conv_w, alog_vec, dtb_vec, b, s):
    c = GDN_CHUNK
    t = c * GDN_CHUNKS_PER_STEP
    nb = s // t
    w = MIXER_WIDTH
    r = np.arange(t)
    same = (r[:, None] // c) == (r[None, :] // c)
    tril = jnp.asarray(same & (r[:, None] >= r[None, :]), F32)
    triu = jnp.asarray(same & (r[:, None] <= r[None, :]), F32)
    fwd = lambda bi, n: bi * nb + n
    bwd = lambda bi, n: bi * nb + nb - 1 - n
    mf, pf, nf = _halo_specs(t, 3 * w, b * s, fwd)
    mb, pb_, nb_ = _halo_specs(t, 3 * w, b * s, bwd)
    abf = pl.BlockSpec((t, LANES), lambda bi, n: (fwd(bi, n), 0))
    abb = pl.BlockSpec((t, LANES), lambda bi, n: (bwd(bi, n), 0))
    const = lambda a: pl.BlockSpec(a.shape, lambda bi, n: (0,) * a.ndim)
    return pl.pallas_call(
        functools.partial(_gdn_kernel, nb),
        grid=(b, nb),
        in_specs=[mf, pf, nf, abf, mb, pb_, nb_, abb, const(conv_w), const(alog_vec), const(dtb_vec),
                  const(tril), const(triu)],
        out_specs=[pl.BlockSpec((t, w), lambda bi, n: (fwd(bi, n), 0)),
                   pl.BlockSpec((t, w), lambda bi, n: (bwd(bi, n), 0))],
        out_shape=[jax.ShapeDtypeStruct((b * s, w), F32), jax.ShapeDtypeStruct((b * s, w), F32)],
        scratch_shapes=[pltpu.VMEM((MIXER_HEADS, HEAD_DIM, HEAD_DIM), F32),
                        pltpu.VMEM((MIXER_HEADS, HEAD_DIM, HEAD_DIM), F32),
                        pltpu.VMEM((t + 2 * HALO, 3 * w), F32)],
        compiler_params=_cparams("arbitrary", "arbitrary"),
    )(p_d, p_d, p_d, p_ab, p_d, p_d, p_d, p_ab, conv_w, alog_vec, dtb_vec, tril, triu)


def _mem_kv_kernel(m_ref, g_ref, wk_ref, wv_ref, k_ref, v_ref):
    h = _rms(m_ref[0], g_ref[...]).astype(BF16)
    k_ref[0] = _dot(h, wk_ref[...]).astype(BF16)
    v_ref[0] = _dot(h, wv_ref[...]).astype(BF16)


def _mem_kv(mem, g, wk, wv):
    b, nm, d = mem.shape
    const = lambda a: pl.BlockSpec(a.shape, lambda i: (0,) * a.ndim)
    blk = pl.BlockSpec((1, nm, d), lambda i: (i, 0, 0))
    return pl.pallas_call(
        _mem_kv_kernel, grid=(b,),
        in_specs=[blk, const(g), const(wk), const(wv)], out_specs=[blk, blk],
        out_shape=[jax.ShapeDtypeStruct((b, nm, d), BF16)] * 2,
        compiler_params=_cparams("parallel"),
    )(mem, g, wk, wv)


def _post_kernel(x_ref, ya_ref, yb_ref, yc_ref, of_ref, ob_ref, gate_ref, seg_ref, gg_ref, wout_ref, gq_ref,
                 wq_ref, k_ref, v_ref, wo_ref, gf_ref, wr_ref, br_ref, xo_ref, hx_ref, lg_ref):
    w = MIXER_WIDTH
    o = of_ref[...] + ob_ref[...]
    ms = jnp.dot(o * o, seg_ref[...], preferred_element_type=F32, precision=HIGHEST)
    yd = o * lax.rsqrt(ms + EPS) * gg_ref[...] * _silu(gate_ref[...])
    mix = _dot(ya_ref[...], wout_ref[0:w, :]) + _dot(yb_ref[...], wout_ref[w:2 * w, :])
    mix = mix + _dot(yc_ref[...], wout_ref[2 * w:3 * w, :]) + _dot(yd.astype(BF16), wout_ref[3 * w:4 * w, :])
    x1 = x_ref[...] + mix
    q = _dot(_rms(x1, gq_ref[...]).astype(BF16), wq_ref[...]).astype(BF16)
    heads = []
    for h in range(MEM_HEADS):
        sl = slice(h * MEM_HEAD_DIM, (h + 1) * MEM_HEAD_DIM)
        sc = _dot_nt(q[:, sl], k_ref[0, :, sl]) * (MEM_HEAD_DIM ** -0.5)
        p = jnp.exp(sc - jnp.max(sc, axis=-1, keepdims=True))
        l = jnp.sum(p, axis=-1, keepdims=True)
        heads.append((_dot(p.astype(BF16), v_ref[0, :, sl]) / l).astype(BF16))
    x2 = x1 + _dot(jnp.concatenate(heads, axis=-1), wo_ref[...])
    xo_ref[...] = x2
    hx = _rms(x2, gf_ref[...])
    hx_ref[...] = hx
    lg_ref[...] = jnp.dot(hx, wr_ref[...], preferred_element_type=F32, precision=HIGHEST) + br_ref[...]


def _post(x, y_a, y_b, y_c, o_f, o_b, p_d, seg, gdn_g, w_out, g_q, w_q, k_mem, v_mem, w_o, g_f, w_r, b_r, s):
    n = x.shape[0]
    tm = POST_TILE
    w = MIXER_WIDTH
    per_seq = s // tm
    row = lambda c: pl.BlockSpec((tm, c), lambda i: (i, 0))
    const = lambda a: pl.BlockSpec(a.shape, lambda i: (0,) * a.ndim)
    kv = pl.BlockSpec((1,) + k_mem.shape[1:], lambda i: (i // per_seq, 0, 0))
    return pl.pallas_call(
        _post_kernel, grid=(n // tm,),
        in_specs=[row(D_MODEL), row(w), row(w), row(w), row(w), row(w),
                  pl.BlockSpec((tm, w), lambda i: (i, 3)), const(seg), const(gdn_g), const(w_out), const(g_q),
                  const(w_q), kv, kv, const(w_o), const(g_f), const(w_r), const(b_r)],
        out_specs=[row(D_MODEL), row(D_MODEL), row(LANES)],
        out_shape=[jax.ShapeDtypeStruct((n, D_MODEL), F32), jax.ShapeDtypeStruct((n, D_MODEL), F32),
                   jax.ShapeDtypeStruct((n, LANES), F32)],
        compiler_params=_cparams("parallel"),
    )(x, y_a, y_b, y_c, o_f, o_b, p_d, seg, gdn_g, w_out, g_q, w_q, k_mem, v_mem, w_o, g_f, w_r, b_r)


def _route(logits, b_total):
    n = logits.shape[0]
    g_prob = jax.nn.softmax(logits[:, :N_GROUPS], axis=-1)
    g_p, g_idx = lax.top_k(g_prob, 1)
    e_logits = logits[:, N_GROUPS:N_GROUPS + N_EXPERTS].reshape(n, N_GROUPS, EXPERTS_PER_GROUP)
    e_logits = jnp.take_along_axis(e_logits, g_idx[:, :, None], axis=1)[:, 0]
    e_p, e_loc = lax.top_k(jax.nn.softmax(e_logits, axis=-1), TOP_K)
    gate_w = (g_p * e_p / jnp.sum(e_p, axis=-1, keepdims=True)).reshape(-1)
    flat_e = (g_idx * EXPERTS_PER_GROUP + e_loc).reshape(-1).astype(jnp.int32)
    a = n * TOP_K
    blk = MOE_BLOCK
    order = jnp.argsort(flat_e).astype(jnp.int32)
    counts = jnp.sum(flat_e[:, None] == jnp.arange(N_EXPERTS, dtype=jnp.int32)[None, :], axis=0, dtype=jnp.int32)
    padded = (counts + blk - 1) // blk * blk
    pad_end = jnp.cumsum(padded)
    pad_start = pad_end - padded
    start = jnp.cumsum(counts) - counts
    n_blocks = a // blk + N_EXPERTS
    blk_start = jnp.arange(n_blocks, dtype=jnp.int32) * blk
    blk_expert = jnp.minimum(jnp.searchsorted(pad_end, blk_start, side='right'), N_EXPERTS - 1).astype(jnp.int32)
    blk_valid = jnp.clip(counts[blk_expert] - (blk_start - pad_start[blk_expert]), 0, blk).astype(jnp.int32)
    row_e = jnp.repeat(blk_expert, blk)
    rank = jnp.arange(n_blocks * blk, dtype=jnp.int32) - pad_start[row_e]
    row_ok = rank < counts[row_e]
    row_assign = jnp.where(row_ok, order[jnp.clip(start[row_e] + rank, 0, a - 1)], 0)
    row_w = jnp.where(row_ok, gate_w[row_assign], 0.0).reshape(-1, 1)
    del b_total
    return blk_expert, blk_valid, row_assign, row_w


def _moe_kernel(n_blocks, be_ref, bv_ref, ra_ref, hx_hbm, rw_ref, wg_ref, wu_ref, wd_ref, y_hbm,
                xbuf, ybuf, gsem, ssem):
    j = pl.program_id(0)
    blk = MOE_BLOCK
    slot = j % 2
    sub = SUBLANES

    def row_copy_in(jj, sl, i):
        tok = ra_ref[jj * blk + i] // TOP_K
        return pltpu.make_async_copy(hx_hbm.at[pl.ds(pl.multiple_of(tok * sub, sub), sub), :],
                                     xbuf.at[sl, pl.ds(pl.multiple_of(i * sub, sub), sub), :], gsem.at[sl])

    def row_copy_out(jj, i):
        dst = ra_ref[jj * blk + i]
        return pltpu.make_async_copy(ybuf.at[pl.ds(pl.multiple_of(i * sub, sub), sub), :],
                                     y_hbm.at[pl.ds(pl.multiple_of(dst * sub, sub), sub), :], ssem.at[0])

    def start_gather(jj, sl):
        @pl.when(bv_ref[jj] > 0)
        def _():
            lax.fori_loop(0, blk, lambda i, c: (row_copy_in(jj, sl, i).start(), c)[1], 0)

    def wait_scatter(jj):
        lax.fori_loop(0, bv_ref[jj], lambda i, c: (row_copy_out(jj, i).wait(), c)[1], 0)

    @pl.when(j == 0)
    def _():
        start_gather(0, 0)

    @pl.when(j + 1 < n_blocks)
    def _():
        start_gather(j + 1, 1 - slot)

    @pl.when(bv_ref[j] > 0)
    def _():
        lax.fori_loop(0, blk, lambda i, c: (row_copy_in(j, slot, i).wait(), c)[1], 0)
        xs = jnp.concatenate([xbuf[slot, pl.ds(s8, blk, stride=sub), :] for s8 in range(sub)], axis=1).astype(BF16)
        g = _dot(xs, wg_ref[0])
        u = _dot(xs, wu_ref[0])
        y = _dot((_silu(g) * u).astype(BF16), wd_ref[0]) * rw_ref[...]

        @pl.when(j > 0)
        def _():
            wait_scatter(j - 1)

        for s8 in range(sub):
            ybuf[pl.ds(s8, blk, stride=sub), :] = y[:, s8 * LANES:(s8 + 1) * LANES]
        lax.fori_loop(0, bv_ref[j], lambda i, c: (row_copy_out(j, i).start(), c)[1], 0)

    @pl.when((bv_ref[j] == 0) & (j > 0))
    def _():
        wait_scatter(j - 1)

    @pl.when(j == n_blocks - 1)
    def _():
        wait_scatter(j)


def _moe(hx_tiles, blk_expert, blk_valid, row_assign, row_w, w_gate, w_up, w_down, n):
    blk = MOE_BLOCK
    n_blocks = blk_expert.shape[0]
    grid_spec = pltpu.PrefetchScalarGridSpec(
        num_scalar_prefetch=3, grid=(n_blocks,),
        in_specs=[pl.BlockSpec(memory_space=pl.ANY),
                  pl.BlockSpec((blk, 1), lambda j, be, bv, ra: (j, 0)),
                  pl.BlockSpec((1, D_MODEL, D_EXPERT), lambda j, be, bv, ra: (be[j], 0, 0)),
                  pl.BlockSpec((1, D_MODEL, D_EXPERT), lambda j, be, bv, ra: (be[j], 0, 0)),
                  pl.BlockSpec((1, D_EXPERT, D_MODEL), lambda j, be, bv, ra: (be[j], 0, 0))],
        out_specs=pl.BlockSpec(memory_space=pl.ANY),
        scratch_shapes=[pltpu.VMEM((2, blk * SUBLANES, LANES), F32), pltpu.VMEM((blk * SUBLANES, LANES), F32),
                        pltpu.SemaphoreType.DMA((2,)), pltpu.SemaphoreType.DMA((1,))])
    return pl.pallas_call(
        functools.partial(_moe_kernel, n_blocks), grid_spec=grid_spec,
        out_shape=jax.ShapeDtypeStruct((n * TOP_K * SUBLANES, LANES), F32),
        compiler_params=_cparams("arbitrary"),
    )(blk_expert, blk_valid, row_assign, hx_tiles, row_w, w_gate, w_up, w_down)


def _final_kernel(x_ref, y_ref, g_ref, o_ref):
    x = x_ref[...] + y_ref[:, :D_MODEL] + y_ref[:, D_MODEL:]
    o_ref[...] = _rms(x, g_ref[...])


def _final(x, y, g):
    n = x.shape[0]
    tm = PROJ_TILE
    return pl.pallas_call(
        _final_kernel, grid=(n // tm,),
        in_specs=[pl.BlockSpec((tm, D_MODEL), lambda i: (i, 0)), pl.BlockSpec((tm, 2 * D_MODEL), lambda i: (i, 0)),
                  pl.BlockSpec(g.shape, lambda i: (0, 0))],
        out_specs=pl.BlockSpec((tm, D_MODEL), lambda i: (i, 0)),
        out_shape=jax.ShapeDtypeStruct((n, D_MODEL), F32),
        compiler_params=_cparams("parallel"),
    )(x, y, g)


def _lane_vec(v, width=LANES):
    v = v.reshape(1, -1).astype(F32)
    return jnp.pad(v, ((0, 0), (0, width - v.shape[1])))


def _block_diag(blocks):
    g, c, _ = blocks.shape
    eye = jnp.eye(g, dtype=blocks.dtype)
    return (eye[:, None, :, None] * blocks[:, :, None, :]).reshape(g * c, g * c)


def kernel(x, mem, mix_norm_g, w_in, na_rpb, ret_norm_g, pool_w, pool_scale, gdn_conv_w, gdn_a_log, gdn_dt_bias, gdn_norm_g, w_out, mem_q_norm_g, mem_kv_norm_g, mem_w_q, mem_w_k, mem_w_v, mem_w_o, ffn_norm_g, w_group, b_group, w_router, b_router, w_gate, w_up, w_down, final_norm_g):
    b, s, d = x.shape
    n = b * s
    depth = w_in.shape[0]
    row = lambda v: v.reshape(1, -1).astype(F32)
    ret_tables = _retention_tables(s)
    seg = _block_diag(jnp.full((MIXER_HEADS, HEAD_DIM, HEAD_DIM), 1.0 / HEAD_DIM, F32))
    xs = x.reshape(n, d)
    y_moe = None
    for l in range(depth):
        w_pad = jnp.pad(w_in[l], ((0, 0), (0, P_IN_PAD - w_in.shape[2]))).astype(BF16)
        xs, p_a, p_b, p_c, p_d, p_ab = _norm_proj(xs, y_moe, row(mix_norm_g[l]), w_pad)
        y_a = _neighbourhood_attention(p_a, _na_bias_table(na_rpb[l]), b, s)
        y_b = _retention(p_b, row(ret_norm_g[l]), ret_tables, b, s)
        y_c = _pool(p_c, _block_diag(pool_w[l]).astype(BF16), row(pool_scale[l]), b, s)
        o_f, o_b = _gated_deltanet(p_d, p_ab, gdn_conv_w[l].astype(F32), _lane_vec(gdn_a_log[l]),
                                   _lane_vec(gdn_dt_bias[l]), b, s)
        k_mem, v_mem = _mem_kv(mem, row(mem_kv_norm_g[l]), mem_w_k[l].astype(BF16), mem_w_v[l].astype(BF16))
        w_r = jnp.pad(jnp.concatenate([w_group[l], w_router[l]], axis=1),
                      ((0, 0), (0, LANES - N_GROUPS - N_EXPERTS))).astype(F32)
        b_r = _lane_vec(jnp.concatenate([b_group[l], b_router[l]]))
        xs, hx, logits = _post(xs, y_a, y_b, y_c, o_f, o_b, p_d, seg, row(jnp.tile(gdn_norm_g[l], MIXER_HEADS)),
                               w_out[l].astype(BF16), row(mem_q_norm_g[l]), mem_w_q[l].astype(BF16), k_mem, v_mem,
                               mem_w_o[l].astype(BF16), row(ffn_norm_g[l]), w_r, b_r, s)
        blk_expert, blk_valid, row_assign, row_w = _route(logits, b)
        y_moe = _moe(hx.reshape(n * SUBLANES, LANES), blk_expert, blk_valid, row_assign, row_w,
                     w_gate[l].astype(BF16), w_up[l].astype(BF16), w_down[l].astype(BF16), n)
        y_moe = y_moe.reshape(n, TOP_K * D_MODEL)
    return _final(xs, y_moe, row(final_norm_g)).reshape(b, s, d)
```

```python
import functools
import math

import jax
import jax.numpy as jnp
import numpy as np
from jax import lax
from jax.experimental import pallas as pl
from jax.experimental.pallas import tpu as pltpu

F32 = jnp.float32
BF16 = jnp.bfloat16
HIGHEST = lax.Precision.HIGHEST

D_MODEL = 1024
GRID_W = 64
HEAD_DIM = 64
MIXER_WIDTH = 256
MIXER_HEADS = 4
NA_KH = 8
NA_KW = 16
RET_CHUNK = 128
ROPE_BASE = 10000.0
POOL_WINDOWS = (2, 4, 8, 16)
POOL_GROUP = 64
GDN_CHUNK = 64
CONV_K = 4
MEM_HEADS = 4
MEM_HEAD_DIM = 256
N_GROUPS = 4
EXPERTS_PER_GROUP = 8
N_EXPERTS = 32
TOP_K = 2
D_EXPERT = 512
EPS = 1e-6

P_IN_PAD = 3200
V7X_VMEM_LIMIT = 56 * 1024 * 1024
LANES = 128
SUBLANES = 8
ROW_TILES = D_MODEL // LANES
NEG_BIG = -1e30

PROJ_TILE = 512
POOL_TILE = 512
POST_TILE = 512
NA_ROWS_PER_STEP = 8
GDN_CHUNKS_PER_STEP = 2
MOE_BLOCK = 256
DMA_UNROLL = 8
HALO = 8


def _cparams(*sem):
    return pltpu.CompilerParams(dimension_semantics=sem, vmem_limit_bytes=V7X_VMEM_LIMIT)


def _dot(a, b):
    return jnp.dot(a, b, preferred_element_type=F32)


def _dot_nt(a, b):
    return lax.dot_general(a, b, (((1,), (1,)), ((), ())), preferred_element_type=F32)


def _dot_tn(a, b, precision=None):
    return lax.dot_general(a, b, (((0,), (0,)), ((), ())), preferred_element_type=F32, precision=precision)


def _silu(x):
    return x * (1.0 / (1.0 + jnp.exp(-x)))


def _rms(x, g):
    return x * lax.rsqrt(jnp.mean(x * x, axis=-1, keepdims=True) + EPS) * g


def _from_row_tiles(ref, first, count, stride):
    return jnp.concatenate([ref[pl.ds(first + k, count, stride=stride), :] for k in range(ROW_TILES)], axis=1)


def _to_row_tiles(ref, x):
    count = x.shape[0]
    for k in range(ROW_TILES):
        ref[pl.ds(k, count, stride=ROW_TILES), :] = x[:, k * LANES:(k + 1) * LANES]


def _add_moe(x, y_ref):
    t = x.shape[0]
    stride = TOP_K * ROW_TILES
    return x + _from_row_tiles(y_ref, 0, t, stride) + _from_row_tiles(y_ref, ROW_TILES, t, stride)


def _norm_proj_kernel(has_y, *refs):
    if has_y:
        x_ref, y_ref, g_ref, w_ref, xo_ref, pa_ref, pb_ref, pc_ref, pd_ref, pab_ref = refs
        x = _add_moe(x_ref[...], y_ref)
        xo_ref[...] = x
    else:
        x_ref, g_ref, w_ref, pa_ref, pb_ref, pc_ref, pd_ref, pab_ref = refs
        x = x_ref[...]
    h = _rms(x, g_ref[...]).astype(BF16)
    w = MIXER_WIDTH
    pa_ref[...] = _dot(h, w_ref[:, 0:3 * w]).astype(BF16)
    pb_ref[...] = _dot(h, w_ref[:, 3 * w:7 * w])
    pc_ref[...] = _dot(h, w_ref[:, 7 * w:8 * w])
    pd_ref[...] = _dot(h, w_ref[:, 8 * w:12 * w])
    pab_ref[...] = _dot(h, w_ref[:, 12 * w:P_IN_PAD])


def _moe_out_spec(tile):
    return pl.BlockSpec((tile * TOP_K * ROW_TILES, LANES), lambda i: (i, 0))


def _norm_proj(x, y, g, w_pad):
    n = x.shape[0]
    tm = PROJ_TILE
    w = MIXER_WIDTH
    row = lambda c: pl.BlockSpec((tm, c), lambda i: (i, 0))
    const = lambda a: pl.BlockSpec(a.shape, lambda i: (0,) * a.ndim)
    outs = [jax.ShapeDtypeStruct((n, 3 * w), BF16), jax.ShapeDtypeStruct((n, 4 * w), F32),
            jax.ShapeDtypeStruct((n, w), F32), jax.ShapeDtypeStruct((n, 4 * w), F32),
            jax.ShapeDtypeStruct((n, LANES), F32)]
    out_specs = [row(3 * w), row(4 * w), row(w), row(4 * w), row(LANES)]
    if y is None:
        ins, in_specs = (x, g, w_pad), [row(D_MODEL), const(g), const(w_pad)]
    else:
        ins, in_specs = (x, y, g, w_pad), [row(D_MODEL), _moe_out_spec(tm), const(g), const(w_pad)]
        outs = [jax.ShapeDtypeStruct((n, D_MODEL), F32)] + outs
        out_specs = [row(D_MODEL)] + out_specs
    res = pl.pallas_call(
        functools.partial(_norm_proj_kernel, y is not None),
        grid=(n // tm,), in_specs=in_specs, out_specs=out_specs, out_shape=outs,
        compiler_params=_cparams("parallel"),
    )(*ins)
    return res if y is not None else [x] + list(res)


def _na_bias_table(rpb):
    qc = np.arange(GRID_W)
    kc = np.arange(GRID_W)
    d_col = np.clip(kc[None, :] - qc[:, None], -(NA_KW - 1), NA_KW - 1) + (NA_KW - 1)
    pick = np.zeros((2 * NA_KW - 1, GRID_W * GRID_W), np.float32)
    pick[d_col.reshape(-1), np.arange(GRID_W * GRID_W)] = 1.0
    win = np.clip(qc - NA_KW // 2, 0, GRID_W - NA_KW)
    ok = (kc[None, :] >= win[:, None]) & (kc[None, :] < win[:, None] + NA_KW)
    cols = jnp.einsum('hrc,cx->hrx', rpb.astype(F32), jnp.asarray(pick), precision=HIGHEST)
    cols = jnp.where(ok[None, None], cols.reshape(MIXER_HEADS, 2 * NA_KH - 1, GRID_W, GRID_W), NEG_BIG)
    per_e = [jnp.transpose(cols[:, NA_KH - 1 - e:2 * NA_KH - 1 - e], (0, 2, 1, 3)) for e in range(NA_KH)]
    return jnp.stack(per_e, axis=1).reshape(MIXER_HEADS, NA_KH, GRID_W, NA_KH * GRID_W)


def _na_kernel(rows, q_ref, k_ref, v_ref, t_ref, o_ref):
    i = pl.program_id(1)
    dh = HEAD_DIM

    def row_body(rr, carry):
        r = i * NA_ROWS_PER_STEP + rr
        kr0 = jnp.clip(r - NA_KH // 2, 0, rows - NA_KH)
        e = r - kr0
        q = q_ref[pl.ds(pl.multiple_of(rr * GRID_W, GRID_W), GRID_W), :]
        k0 = pl.multiple_of(kr0 * GRID_W, GRID_W)
        kb = k_ref[pl.ds(k0, NA_KH * GRID_W), :]
        vb = v_ref[pl.ds(k0, NA_KH * GRID_W), :]
        outs = []
        for h in range(MIXER_HEADS):
            sl = slice(h * dh, (h + 1) * dh)
            s = _dot_nt(q[:, sl], kb[:, sl]) * (dh ** -0.5) + t_ref[h, e]
            m = jnp.max(s, axis=-1, keepdims=True)
            p = jnp.exp(s - m)
            l = jnp.sum(p, axis=-1, keepdims=True)
            outs.append(_dot(p.astype(BF16), vb[:, sl]) / l)
        o_ref[pl.ds(pl.multiple_of(rr * GRID_W, GRID_W), GRID_W), :] = jnp.concatenate(outs, axis=-1).astype(BF16)
        return carry

    lax.fori_loop(0, NA_ROWS_PER_STEP, row_body, 0)


def _neighbourhood_attention(p_a, table, b, s):
    rows = s // GRID_W
    assert rows >= NA_KH and rows % NA_ROWS_PER_STEP == 0
    steps = rows // NA_ROWS_PER_STEP
    tq = NA_ROWS_PER_STEP * GRID_W
    w = MIXER_WIDTH
    return pl.pallas_call(
        functools.partial(_na_kernel, rows),
        grid=(b, steps),
        in_specs=[pl.BlockSpec((tq, w), lambda bi, i: (bi * steps + i, 0)),
                  pl.BlockSpec((s, w), lambda bi, i: (bi, 1)),
                  pl.BlockSpec((s, w), lambda bi, i: (bi, 2)),
                  pl.BlockSpec(table.shape, lambda bi, i: (0, 0, 0, 0))],
        out_specs=pl.BlockSpec((tq, w), lambda bi, i: (bi * steps + i, 0)),
        out_shape=jax.ShapeDtypeStruct((b * s, w), BF16),
        compiler_params=_cparams("parallel", "arbitrary"),
    )(p_a, p_a, p_a, table)


def _retention_tables(s):
    h, dh, c = MIXER_HEADS, HEAD_DIM, RET_CHUNK
    half = dh // 2
    inv = ROPE_BASE ** (-jnp.arange(half, dtype=F32) / half)
    ang = jnp.arange(s, dtype=F32)[:, None] * inv[None, :]
    cos, sin = jnp.cos(ang), jnp.sin(ang)
    zero = jnp.zeros_like(sin)
    cos_t = jnp.tile(jnp.concatenate([cos, cos], axis=-1), (1, h))
    sin_lo = jnp.tile(jnp.concatenate([-sin, zero], axis=-1), (1, h))
    sin_hi = jnp.tile(jnp.concatenate([zero, sin], axis=-1), (1, h))
    log_f = np.log1p(-np.exp2(-5.0 - np.arange(h, dtype=np.float64)))
    log_b = log_f[::-1]
    pos = np.arange(c, dtype=np.float64)
    diff = pos[:, None] - pos[None, :]
    dmat = np.where(diff >= 0, np.exp(log_f[:, None, None] * np.maximum(diff, 0.0)), 0.0) \
        + np.where(diff < 0, np.exp(log_b[:, None, None] * np.maximum(-diff, 0.0)), 0.0)
    lanes = lambda t: np.repeat(t.T, dh, axis=1)
    dec = np.stack([lanes(np.exp(log_f[:, None] * (pos + 1.0))),
                    lanes(np.exp(log_f[:, None] * (c - 1.0 - pos))),
                    lanes(np.exp(log_b[:, None] * (c - pos))),
                    lanes(np.exp(log_b[:, None] * pos))])
    chunk_f = [float(np.exp(v * c)) for v in log_f]
    chunk_b = [float(np.exp(v * c)) for v in log_b]
    return cos_t, sin_lo, sin_hi, jnp.asarray(dmat, F32), jnp.asarray(dec, F32), chunk_f, chunk_b


def _rotary(x, cos, sin_lo, sin_hi):
    w = x.shape[-1]
    return x * cos + pltpu.roll(x, w - HEAD_DIM // 2, 1) * sin_lo + pltpu.roll(x, HEAD_DIM // 2, 1) * sin_hi


def _retention_kernel(nc, chunk_f, chunk_b, q_ref, k_ref, v_ref, gate_ref, cos_ref, slo_ref, shi_ref,
                      dmat_ref, dec_ref, ng_ref, o_ref, sf_ref, sb_ref, sball_ref):
    t = pl.program_id(1)
    dh = HEAD_DIM
    cos, slo, shi = cos_ref[...], slo_ref[...], shi_ref[...]
    kr = _rotary(k_ref[...], cos, slo, shi) * (dh ** -0.5)
    v = v_ref[...]

    @pl.when(t == 0)
    def _():
        sf_ref[...] = jnp.zeros_like(sf_ref)
        sb_ref[...] = jnp.zeros_like(sb_ref)

    @pl.when(t < nc)
    def _():
        c = nc - 1 - t
        kd = (kr * dec_ref[3]).astype(BF16)
        vb = v.astype(BF16)
        for h in range(MIXER_HEADS):
            sl = slice(h * dh, (h + 1) * dh)
            sball_ref[c, h] = sb_ref[h]
            sb_ref[h] = sb_ref[h] * chunk_b[h] + _dot_tn(kd[:, sl], vb[:, sl])

    @pl.when(t >= nc)
    def _():
        c = t - nc
        qr = _rotary(q_ref[...], cos, slo, shi)
        qb = qr.astype(BF16)
        kb = kr.astype(BF16)
        vb = v.astype(BF16)
        qf = (qr * dec_ref[0]).astype(BF16)
        qbk = (qr * dec_ref[2]).astype(BF16)
        kd = (kr * dec_ref[1]).astype(BF16)
        outs = []
        for h in range(MIXER_HEADS):
            sl = slice(h * dh, (h + 1) * dh)
            sc = _dot_nt(qb[:, sl], kb[:, sl]) * dmat_ref[h]
            o = _dot(sc.astype(BF16), vb[:, sl])
            o = o + _dot(qf[:, sl], sf_ref[h].astype(BF16))
            o = o + _dot(qbk[:, sl], sball_ref[c, h].astype(BF16))
            sf_ref[h] = sf_ref[h] * chunk_f[h] + _dot_tn(kd[:, sl], vb[:, sl])
            mu = jnp.mean(o, axis=-1, keepdims=True)
            oc = o - mu
            var = jnp.mean(oc * oc, axis=-1, keepdims=True)
            outs.append(oc * lax.rsqrt(var + EPS))
        y = jnp.concatenate(outs, axis=-1) * ng_ref[...]
        o_ref[...] = (y * _silu(gate_ref[...])).astype(BF16)


def _retention(p_b, norm_g, tables, b, s):
    cos_t, sin_lo, sin_hi, dmat, dec, chunk_f, chunk_b = tables
    c = RET_CHUNK
    nc = s // c
    w = MIXER_WIDTH
    chunk = lambda t: jnp.where(t < nc, nc - 1 - t, t - nc)
    col = lambda j: pl.BlockSpec((c, w), lambda bi, t: (bi * nc + chunk(t), j))
    tab = pl.BlockSpec((c, w), lambda bi, t: (chunk(t), 0))
    const = lambda a: pl.BlockSpec(a.shape, lambda bi, t: (0,) * a.ndim)
    return pl.pallas_call(
        functools.partial(_retention_kernel, nc, chunk_f, chunk_b),
        grid=(b, 2 * nc),
        in_specs=[col(0), col(1), col(2), col(3), tab, tab, tab, const(dmat), const(dec), const(norm_g)],
        out_specs=pl.BlockSpec((c, w), lambda bi, t: (bi * nc + jnp.maximum(t - nc, 0), 0)),
        out_shape=jax.ShapeDtypeStruct((b * s, w), BF16),
        scratch_shapes=[pltpu.VMEM((MIXER_HEADS, HEAD_DIM, HEAD_DIM), F32),
                        pltpu.VMEM((MIXER_HEADS, HEAD_DIM, HEAD_DIM), F32),
                        pltpu.VMEM((nc, MIXER_HEADS, HEAD_DIM, HEAD_DIM), F32)],
        compiler_params=_cparams("arbitrary", "arbitrary"),
    )(p_b, p_b, p_b, p_b, cos_t, sin_lo, sin_hi, dmat, dec, norm_g)


def _halo_specs(tile, width, n_rows, block_of):
    per = tile // HALO
    last = n_rows // HALO - 1
    main = pl.BlockSpec((tile, width), lambda *g: (block_of(*g), 0))
    prev = pl.BlockSpec((HALO, width), lambda *g: (jnp.maximum(block_of(*g) * per - 1, 0), 0))
    nxt = pl.BlockSpec((HALO, width), lambda *g: (jnp.minimum((block_of(*g) + 1) * per, last), 0))
    return main, prev, nxt


def _fill_padded(pad_ref, main, prev, nxt, first, last):
    t = main.shape[0]
    pad_ref[0:HALO, :] = jnp.where(first, 0.0, prev)
    pad_ref[HALO:HALO + t, :] = main
    pad_ref[HALO + t:2 * HALO + t, :] = jnp.where(last, 0.0, nxt)


def _pool_kernel(s, tiles_per_seq, u_ref, prev_ref, next_ref, w_ref, scale_ref, o_ref, pad_ref):
    t = POOL_TILE
    tseq = pl.program_id(0) % tiles_per_seq
    u = u_ref[...]
    _fill_padded(pad_ref, u, prev_ref[...], next_ref[...], tseq == 0, tseq == tiles_per_seq - 1)
    lane = lax.broadcasted_iota(jnp.int32, (1, MIXER_WIDTH), 1)
    half = jnp.full((1, MIXER_WIDTH), POOL_WINDOWS[0] // 2, jnp.int32)
    for gi in range(1, len(POOL_WINDOWS)):
        half = jnp.where(lane >= gi * POOL_GROUP, POOL_WINDOWS[gi] // 2, half)
    max_half = POOL_WINDOWS[-1] // 2
    acc = jnp.zeros((t, MIXER_WIDTH), F32)
    for d in range(-max_half, max_half):
        inside = (d >= -half) & (d < half)
        acc = acc + jnp.where(inside, pad_ref[HALO + d:HALO + d + t, :], 0.0)
    pos = tseq * t + lax.broadcasted_iota(jnp.int32, (t, 1), 0)
    count = jnp.minimum(pos + half, s) - jnp.maximum(pos - half, 0)
    diff = acc / count.astype(F32) - u
    o_ref[...] = (_dot(diff.astype(BF16), w_ref[...]) * scale_ref[...]).astype(BF16)


def _pool(p_c, w_blockdiag, scale, b, s):
    t = POOL_TILE
    tiles_per_seq = s // t
    main, prev, nxt = _halo_specs(t, MIXER_WIDTH, b * s, lambda i: i)
    const = lambda a: pl.BlockSpec(a.shape, lambda i: (0,) * a.ndim)
    return pl.pallas_call(
        functools.partial(_pool_kernel, s, tiles_per_seq),
        grid=(b * tiles_per_seq,),
        in_specs=[main, prev, nxt, const(w_blockdiag), const(scale)],
        out_specs=pl.BlockSpec((t, MIXER_WIDTH), lambda i: (i, 0)),
        out_shape=jax.ShapeDtypeStruct((b * s, MIXER_WIDTH), BF16),
        scratch_shapes=[pltpu.VMEM((t + 2 * HALO, MIXER_WIDTH), F32)],
        compiler_params=_cparams("parallel"),
    )(p_c, p_c, p_c, w_blockdiag, scale)


def _gdn_prepare(forward, u, ab, alog, dtb, tri_col, tri_row):
    c, dh, w = GDN_CHUNK, HEAD_DIM, MIXER_WIDTH
    t = u.shape[0]
    a_off, b_off = (0, 2 * MIXER_HEADS) if forward else (MIXER_HEADS, 3 * MIXER_HEADS)
    x = ab + dtb
    g_all = -jnp.exp(alog) * (jnp.maximum(x, 0.0) + jnp.log(1.0 + jnp.exp(-jnp.abs(x))))
    beta_all = 1.0 / (1.0 + jnp.exp(-ab))
    gc_col = jnp.dot(tri_col, g_all, preferred_element_type=F32, precision=HIGHEST)
    gc_row = _dot_tn(g_all, tri_row, precision=HIGHEST)
    ii = lax.broadcasted_iota(jnp.int32, (c, c), 0)
    jj = lax.broadcasted_iota(jnp.int32, (c, c), 1)
    incl = (ii >= jj) if forward else (ii <= jj)
    strict = (ii > jj) if forward else (ii < jj)
    last = c - 1 if forward else 0
    items = [(ci, h) for ci in range(t // c) for h in range(MIXER_HEADS)]
    rows = lambda ci: slice(ci * c, (ci + 1) * c)
    part = lambda base: [u[rows(ci), base + h * dh:base + (h + 1) * dh] for ci, h in items]
    q, k, v = part(0), part(w), part(2 * w)
    q = [a * lax.rsqrt(jnp.sum(a * a, axis=-1, keepdims=True) + EPS) * (dh ** -0.5) for a in q]
    k = [a * lax.rsqrt(jnp.sum(a * a, axis=-1, keepdims=True) + EPS) for a in k]
    gcc = [gc_col[rows(ci), a_off + h:a_off + h + 1] for ci, h in items]
    gcr = [gc_row[a_off + h:a_off + h + 1, rows(ci)] for ci, h in items]
    beta = [beta_all[rows(ci), b_off + h:b_off + h + 1] for ci, h in items]
    g_last = [a[last:last + 1, :] for a in gcc]
    decay = [jnp.where(incl, jnp.exp(jnp.where(incl, a - b, 0.0)), 0.0) for a, b in zip(gcc, gcr)]
    kq = [_dot_nt(jnp.concatenate([kk, qq], axis=0).astype(BF16), kk.astype(BF16)) for kk, qq in zip(k, q)]
    lmat = [jnp.where(strict, m[:c] * b * d, 0.0) for m, b, d in zip(kq, beta, decay)]
    attn = [(m[c:] * d).astype(BF16) for m, d in zip(kq, decay)]
    xinv = [-a for a in lmat]
    lp = [a.astype(BF16) for a in lmat]
    for _ in range(int(math.log2(c)) - 1):
        lp32 = [_dot(a, a) for a in lp]
        lp = [a.astype(BF16) for a in lp32]
        xinv = [xi + a32 + _dot(xi.astype(BF16), a) for xi, a32, a in zip(xinv, lp32, lp)]
    egc = [jnp.exp(a) for a in gcc]
    rhs = [jnp.concatenate([vv * b, kk * (b * e)], axis=-1) for vv, kk, b, e in zip(v, k, beta, egc)]
    sol = [r + _dot(xi.astype(BF16), r.astype(BF16)) for xi, r in zip(xinv, rhs)]
    wq = [jnp.concatenate([so[:, dh:], qq * e], axis=0).astype(BF16) for so, qq, e in zip(sol, q, egc)]
    u_val = [so[:, :dh] for so in sol]
    k_state = [(kk * jnp.exp(gl - a)).astype(BF16) for kk, gl, a in zip(k, g_last, gcc)]
    e_last = [jnp.exp(gl) for gl in g_last]
    keyed = lambda vals: dict(zip(items, vals))
    return dict(wq=keyed(wq), u_val=keyed(u_val), attn=keyed(attn), k_state=keyed(k_state), e_last=keyed(e_last))


def _gdn_kernel(nb, uf_ref, pf_ref, nf_ref, abf_ref, ub_ref, pb_ref, nb_ref, abb_ref, cw_ref, alog_ref,
                dtb_ref, tril_ref, triu_ref, of_ref, ob_ref, sf_ref, sb_ref, pad_ref):
    n = pl.program_id(1)
    t = uf_ref.shape[0]
    c, dh = GDN_CHUNK, HEAD_DIM
    nch = t // c

    @pl.when(n == 0)
    def _():
        sf_ref[...] = jnp.zeros_like(sf_ref)
        sb_ref[...] = jnp.zeros_like(sb_ref)

    def conv_silu(main_ref, prev_ref, next_ref, blk):
        _fill_padded(pad_ref, main_ref[...], prev_ref[...], next_ref[...], blk == 0, blk == nb - 1)
        acc = jnp.zeros((t, 3 * MIXER_WIDTH), F32)
        for k in range(CONV_K):
            off = HALO + k - CONV_K // 2
            acc = acc + pad_ref[off:off + t, :] * cw_ref[k:k + 1, :]
        return _silu(acc)

    alog, dtb = alog_ref[...], dtb_ref[...]
    tril, triu = tril_ref[...], triu_ref[...]
    u_f = conv_silu(uf_ref, pf_ref, nf_ref, n)
    u_b = conv_silu(ub_ref, pb_ref, nb_ref, nb - 1 - n)
    prep_f = _gdn_prepare(True, u_f, abf_ref[...], alog, dtb, tril, triu)
    prep_b = _gdn_prepare(False, u_b, abb_ref[...], alog, dtb, triu, tril)

    chains = [(prep_f, sf_ref, of_ref, h, True) for h in range(MIXER_HEADS)] + \
             [(prep_b, sb_ref, ob_ref, h, False) for h in range(MIXER_HEADS)]
    state = [s_ref[h] for _, s_ref, _, h, _ in chains]
    for step in range(nch):
        key = [(step if fw else nch - 1 - step, h) for _, _, _, h, fw in chains]
        ws = [_dot(p['wq'][kk], st.astype(BF16)) for (p, *_), kk, st in zip(chains, key, state)]
        v_new = [p['u_val'][kk] - a[:c] for (p, *_), kk, a in zip(chains, key, ws)]
        vb = [a.astype(BF16) for a in v_new]
        out = [a[c:] + _dot(p['attn'][kk], b) for (p, *_), kk, a, b in zip(chains, key, ws, vb)]
        state = [st * p['e_last'][kk] + _dot_tn(p['k_state'][kk], b)
                 for (p, *_), kk, st, b in zip(chains, key, state, vb)]
        for (_, _, o_ref, h, _), (ci, _), o in zip(chains, key, out):
            o_ref[ci * c:(ci + 1) * c, h * dh:(h + 1) * dh] = o
    for (_, s_ref, _, h, _), st in zip(chains, state):
        s_ref[h] = st


def _gated_deltanet(p_d, p_ab, conv_w, alog_vec, dtb_vec, b, s):
    c = GDN_CHUNK
    t = c * GDN_CHUNKS_PER_STEP
    nb = s // t
    w = MIXER_WIDTH
    r = np.arange(t)
    same = (r[:, None] // c) == (r[None, :] // c)
    tril = jnp.asarray(same & (r[:, None] >= r[None, :]), F32)
    triu = jnp.asarray(same & (r[:, None] <= r[None, :]), F32)
    fwd = lambda bi, n: bi * nb + n
    bwd = lambda bi, n: bi * nb + nb - 1 - n
    mf, pf, nf = _halo_specs(t, 3 * w, b * s, fwd)
    mb, pb_, nb_ = _halo_specs(t, 3 * w, b * s, bwd)
    abf = pl.BlockSpec((t, LANES), lambda bi, n: (fwd(bi, n), 0))
    abb = pl.BlockSpec((t, LANES), lambda bi, n: (bwd(bi, n), 0))
    const = lambda a: pl.BlockSpec(a.shape, lambda bi, n: (0,) * a.ndim)
    return pl.pallas_call(
        functools.partial(_gdn_kernel, nb),
        grid=(b, nb),
        in_specs=[mf, pf, nf, abf, mb, pb_, nb_, abb, const(conv_w), const(alog_vec), const(dtb_vec),
                  const(tril), const(triu)],
        out_specs=[pl.BlockSpec((t, w), lambda bi, n: (fwd(bi, n), 0)),
                   pl.BlockSpec((t, w), lambda bi, n: (bwd(bi, n), 0))],
        out_shape=[jax.ShapeDtypeStruct((b * s, w), F32), jax.ShapeDtypeStruct((b * s, w), F32)],
        scratch_shapes=[pltpu.VMEM((MIXER_HEADS, HEAD_DIM, HEAD_DIM), F32),
                        pltpu.VMEM((MIXER_HEADS, HEAD_DIM, HEAD_DIM), F32),
                        pltpu.VMEM((t + 2 * HALO, 3 * w), F32)],
        compiler_params=_cparams("arbitrary", "arbitrary"),
    )(p_d, p_d, p_d, p_ab, p_d, p_d, p_d, p_ab, conv_w, alog_vec, dtb_vec, tril, triu)


def _mem_kv_kernel(m_ref, g_ref, wk_ref, wv_ref, k_ref, v_ref):
    h = _rms(m_ref[0], g_ref[...]).astype(BF16)
    k_ref[0] = _dot(h, wk_ref[...]).astype(BF16)
    v_ref[0] = _dot(h, wv_ref[...]).astype(BF16)


def _mem_kv(mem, g, wk, wv):
    b, nm, d = mem.shape
    const = lambda a: pl.BlockSpec(a.shape, lambda i: (0,) * a.ndim)
    blk = pl.BlockSpec((1, nm, d), lambda i: (i, 0, 0))
    return pl.pallas_call(
        _mem_kv_kernel, grid=(b,),
        in_specs=[blk, const(g), const(wk), const(wv)], out_specs=[blk, blk],
        out_shape=[jax.ShapeDtypeStruct((b, nm, d), BF16)] * 2,
        compiler_params=_cparams("parallel"),
    )(mem, g, wk, wv)


def _route_lanes(lg):
    lane_i = lax.broadcasted_iota(jnp.int32, lg.shape, 1)
    lane = lane_i.astype(F32)
    first_at = lambda vals, m: jnp.min(jnp.where(vals == m, lane, float(LANES)), axis=-1, keepdims=True)
    g_mask = lane_i < N_GROUPS
    gl = jnp.where(g_mask, lg, NEG_BIG)
    g_max = jnp.max(gl, axis=-1, keepdims=True)
    g_sum = jnp.sum(jnp.where(g_mask, jnp.exp(gl - g_max), 0.0), axis=-1, keepdims=True)
    g_p = 1.0 / g_sum
    g_idx = first_at(gl, g_max)
    lo = N_GROUPS + EXPERTS_PER_GROUP * g_idx
    e_mask = (lane >= lo) & (lane < lo + EXPERTS_PER_GROUP)
    el = jnp.where(e_mask, lg, NEG_BIG)
    m1 = jnp.max(el, axis=-1, keepdims=True)
    i1 = first_at(el, m1)
    el2 = jnp.where(lane == i1, NEG_BIG, el)
    m2 = jnp.max(el2, axis=-1, keepdims=True)
    i2 = first_at(el2, m2)
    r = jnp.exp(m2 - m1)
    w1 = g_p / (1.0 + r)
    w2 = g_p * r / (1.0 + r)
    out = jnp.where(lane_i == 0, i1 - N_GROUPS, 0.0)
    out = jnp.where(lane_i == 1, i2 - N_GROUPS, out)
    out = jnp.where(lane_i == 2, w1, out)
    return jnp.where(lane_i == 3, w2, out)


def _post_kernel(x_ref, ya_ref, yb_ref, yc_ref, of_ref, ob_ref, gate_ref, seg_ref, gg_ref, wout_ref, gq_ref,
                 wq_ref, k_ref, v_ref, wo_ref, gf_ref, wr_ref, br_ref, xo_ref, hx_ref, rt_ref):
    w = MIXER_WIDTH
    o = of_ref[...] + ob_ref[...]
    ms = jnp.dot(o * o, seg_ref[...], preferred_element_type=F32, precision=HIGHEST)
    yd = o * lax.rsqrt(ms + EPS) * gg_ref[...] * _silu(gate_ref[...])
    mix = _dot(ya_ref[...], wout_ref[0:w, :]) + _dot(yb_ref[...], wout_ref[w:2 * w, :])
    mix = mix + _dot(yc_ref[...], wout_ref[2 * w:3 * w, :]) + _dot(yd.astype(BF16), wout_ref[3 * w:4 * w, :])
    x1 = x_ref[...] + mix
    q = _dot(_rms(x1, gq_ref[...]).astype(BF16), wq_ref[...]).astype(BF16)
    heads = []
    for h in range(MEM_HEADS):
        sl = slice(h * MEM_HEAD_DIM, (h + 1) * MEM_HEAD_DIM)
        sc = _dot_nt(q[:, sl], k_ref[0, :, sl]) * (MEM_HEAD_DIM ** -0.5)
        p = jnp.exp(sc - jnp.max(sc, axis=-1, keepdims=True))
        l = jnp.sum(p, axis=-1, keepdims=True)
        heads.append((_dot(p.astype(BF16), v_ref[0, :, sl]) / l).astype(BF16))
    x2 = x1 + _dot(jnp.concatenate(heads, axis=-1), wo_ref[...])
    xo_ref[...] = x2
    hx = _rms(x2, gf_ref[...])
    _to_row_tiles(hx_ref, hx)
    lg = jnp.dot(hx, wr_ref[...], preferred_element_type=F32, precision=HIGHEST) + br_ref[...]
    rt_ref[...] = _route_lanes(lg)


def _post(x, y_a, y_b, y_c, o_f, o_b, p_d, seg, gdn_g, w_out, g_q, w_q, k_mem, v_mem, w_o, g_f, w_r, b_r, s):
    n = x.shape[0]
    tm = POST_TILE
    w = MIXER_WIDTH
    per_seq = s // tm
    row = lambda c: pl.BlockSpec((tm, c), lambda i: (i, 0))
    const = lambda a: pl.BlockSpec(a.shape, lambda i: (0,) * a.ndim)
    kv = pl.BlockSpec((1,) + k_mem.shape[1:], lambda i: (i // per_seq, 0, 0))
    return pl.pallas_call(
        _post_kernel, grid=(n // tm,),
        in_specs=[row(D_MODEL), row(w), row(w), row(w), row(w), row(w),
                  pl.BlockSpec((tm, w), lambda i: (i, 3)), const(seg), const(gdn_g), const(w_out), const(g_q),
                  const(w_q), kv, kv, const(w_o), const(g_f), const(w_r), const(b_r)],
        out_specs=[row(D_MODEL), pl.BlockSpec((tm * ROW_TILES, LANES), lambda i: (i, 0)), row(LANES)],
        out_shape=[jax.ShapeDtypeStruct((n, D_MODEL), F32), jax.ShapeDtypeStruct((n * ROW_TILES, LANES), F32),
                   jax.ShapeDtypeStruct((n, LANES), F32)],
        compiler_params=_cparams("parallel"),
    )(x, y_a, y_b, y_c, o_f, o_b, p_d, seg, gdn_g, w_out, g_q, w_q, k_mem, v_mem, w_o, g_f, w_r, b_r)


def _dispatch(route):
    n = route.shape[0]
    a = n * TOP_K
    blk = MOE_BLOCK
    flat_e = route[:, 0:TOP_K].astype(jnp.int32).reshape(a)
    gate_w = route[:, TOP_K:2 * TOP_K].reshape(a)
    order = jnp.argsort(flat_e).astype(jnp.int32)
    experts = jnp.arange(N_EXPERTS, dtype=jnp.int32)
    counts = jnp.sum(flat_e[None, :] == experts[:, None], axis=1, dtype=jnp.int32)
    padded = (counts + blk - 1) // blk * blk
    pad_end = jnp.cumsum(padded)
    pad_start = pad_end - padded
    start = jnp.cumsum(counts) - counts
    n_blocks = a // blk + N_EXPERTS
    blk_start = jnp.arange(n_blocks, dtype=jnp.int32) * blk
    blk_expert = jnp.minimum(jnp.sum(pad_end[None, :] <= blk_start[:, None], axis=1), N_EXPERTS - 1).astype(jnp.int32)
    blk_valid = jnp.clip(counts[blk_expert] - (blk_start - pad_start[blk_expert]), 0, blk).astype(jnp.int32)
    row_e = jnp.repeat(blk_expert, blk)
    pos = jnp.arange(n_blocks * blk, dtype=jnp.int32)
    rank = pos - pad_start[row_e]
    row_ok = rank < counts[row_e]
    assign = order[jnp.clip(start[row_e] + rank, 0, a - 1)]
    row_dst = jnp.where(row_ok, assign, 0)
    row_w = jnp.where(row_ok, gate_w[assign], 0.0).reshape(-1, 1)
    return blk_expert, blk_valid, row_dst, row_w


def _moe_kernel(n_blocks, be_ref, bv_ref, dst_ref, hx_hbm, rw_ref, wg_ref, wu_ref, wd_ref, y_hbm,
                xbuf, ybuf, gsem, ssem):
    j = pl.program_id(0)
    blk = MOE_BLOCK
    slot = j % 2
    rt = ROW_TILES
    tile = lambda ref, i: ref.at[pl.ds(pl.multiple_of(i * rt, rt), rt), :]

    def start_gather(jj, sl):
        @pl.when(bv_ref[jj] > 0)
        def _():
            def body(i, carry):
                tok = dst_ref[jj * blk + i] // TOP_K
                pltpu.make_async_copy(tile(hx_hbm, tok), tile(xbuf.at[sl], i), gsem.at[sl]).start()
                return carry
            lax.fori_loop(0, blk, body, 0, unroll=DMA_UNROLL)

    def wait_gather(sl):
        pltpu.make_async_copy(hx_hbm.at[pl.ds(0, blk * rt), :], xbuf.at[sl], gsem.at[sl]).wait()

    def start_scatter():
        def body(i, carry):
            pltpu.make_async_copy(tile(ybuf, i), tile(y_hbm, dst_ref[j * blk + i]), ssem.at[0]).start()
            return carry

        @pl.when(bv_ref[j] == blk)
        def _():
            lax.fori_loop(0, blk, body, 0, unroll=DMA_UNROLL)

        @pl.when(bv_ref[j] < blk)
        def _():
            lax.fori_loop(0, bv_ref[j], body, 0)

    def wait_scatter(jj):
        @pl.when(bv_ref[jj] > 0)
        def _():
            rows = pl.ds(0, bv_ref[jj] * rt)
            pltpu.make_async_copy(ybuf.at[rows, :], y_hbm.at[rows, :], ssem.at[0]).wait()

    @pl.when(j == 0)
    def _():
        start_gather(0, 0)

    @pl.when(j + 1 < n_blocks)
    def _():
        start_gather(j + 1, 1 - slot)

    @pl.when(bv_ref[j] > 0)
    def _():
        wait_gather(slot)
        xs = _from_row_tiles(xbuf.at[slot], 0, blk, rt).astype(BF16)
        g = _dot(xs, wg_ref[0])
        u = _dot(xs, wu_ref[0])
        y = _dot((_silu(g) * u).astype(BF16), wd_ref[0]) * rw_ref[...]

        @pl.when(j > 0)
        def _():
            wait_scatter(j - 1)

        _to_row_tiles(ybuf, y)
        start_scatter()

    @pl.when((bv_ref[j] == 0) & (j > 0))
    def _():
        wait_scatter(j - 1)

    @pl.when(j == n_blocks - 1)
    def _():
        wait_scatter(j)


def _moe(hx_tiles, blk_expert, blk_valid, row_dst, row_w, w_gate, w_up, w_down, n):
    blk = MOE_BLOCK
    n_blocks = blk_expert.shape[0]
    imap = lambda f: (lambda j, be, bv, dst: f(j, be))
    grid_spec = pltpu.PrefetchScalarGridSpec(
        num_scalar_prefetch=3, grid=(n_blocks,),
        in_specs=[pl.BlockSpec(memory_space=pl.ANY),
                  pl.BlockSpec((blk, 1), imap(lambda j, be: (j, 0))),
                  pl.BlockSpec((1, D_MODEL, D_EXPERT), imap(lambda j, be: (be[j], 0, 0))),
                  pl.BlockSpec((1, D_MODEL, D_EXPERT), imap(lambda j, be: (be[j], 0, 0))),
                  pl.BlockSpec((1, D_EXPERT, D_MODEL), imap(lambda j, be: (be[j], 0, 0)))],
        out_specs=pl.BlockSpec(memory_space=pl.ANY),
        scratch_shapes=[pltpu.VMEM((2, blk * ROW_TILES, LANES), F32), pltpu.VMEM((blk * ROW_TILES, LANES), F32),
                        pltpu.SemaphoreType.DMA((2,)), pltpu.SemaphoreType.DMA((1,))])
    return pl.pallas_call(
        functools.partial(_moe_kernel, n_blocks), grid_spec=grid_spec,
        out_shape=jax.ShapeDtypeStruct((n * TOP_K * ROW_TILES, LANES), F32),
        compiler_params=_cparams("arbitrary"),
    )(blk_expert, blk_valid, row_dst, hx_tiles, row_w, w_gate, w_up, w_down)


def _final_kernel(x_ref, y_ref, g_ref, o_ref):
    o_ref[...] = _rms(_add_moe(x_ref[...], y_ref), g_ref[...])


def _final(x, y, g):
    n = x.shape[0]
    tm = PROJ_TILE
    return pl.pallas_call(
        _final_kernel, grid=(n // tm,),
        in_specs=[pl.BlockSpec((tm, D_MODEL), lambda i: (i, 0)), _moe_out_spec(tm),
                  pl.BlockSpec(g.shape, lambda i: (0, 0))],
        out_specs=pl.BlockSpec((tm, D_MODEL), lambda i: (i, 0)),
        out_shape=jax.ShapeDtypeStruct((n, D_MODEL), F32),
        compiler_params=_cparams("parallel"),
    )(x, y, g)


def _lane_vec(v, width=LANES):
    v = v.reshape(1, -1).astype(F32)
    return jnp.pad(v, ((0, 0), (0, width - v.shape[1])))


def _block_diag(blocks):
    g, c, _ = blocks.shape
    eye = jnp.eye(g, dtype=blocks.dtype)
    return (eye[:, None, :, None] * blocks[:, :, None, :]).reshape(g * c, g * c)


def kernel(x, mem, mix_norm_g, w_in, na_rpb, ret_norm_g, pool_w, pool_scale, gdn_conv_w, gdn_a_log, gdn_dt_bias, gdn_norm_g, w_out, mem_q_norm_g, mem_kv_norm_g, mem_w_q, mem_w_k, mem_w_v, mem_w_o, ffn_norm_g, w_group, b_group, w_router, b_router, w_gate, w_up, w_down, final_norm_g):
    b, s, d = x.shape
    n = b * s
    depth = w_in.shape[0]
    row = lambda v: v.reshape(1, -1).astype(F32)
    ret_tables = _retention_tables(s)
    seg = _block_diag(jnp.full((MIXER_HEADS, HEAD_DIM, HEAD_DIM), 1.0 / HEAD_DIM, F32))
    xs = x.reshape(n, d)
    y_moe = None
    for l in range(depth):
        w_pad = jnp.pad(w_in[l], ((0, 0), (0, P_IN_PAD - w_in.shape[2]))).astype(BF16)
        xs, p_a, p_b, p_c, p_d, p_ab = _norm_proj(xs, y_moe, row(mix_norm_g[l]), w_pad)
        y_a = _neighbourhood_attention(p_a, _na_bias_table(na_rpb[l]), b, s)
        y_b = _retention(p_b, row(ret_norm_g[l]), ret_tables, b, s)
        y_c = _pool(p_c, _block_diag(pool_w[l]).astype(BF16), row(pool_scale[l]), b, s)
        o_f, o_b = _gated_deltanet(p_d, p_ab, gdn_conv_w[l].astype(F32), _lane_vec(gdn_a_log[l]),
                                   _lane_vec(gdn_dt_bias[l]), b, s)
        k_mem, v_mem = _mem_kv(mem, row(mem_kv_norm_g[l]), mem_w_k[l].astype(BF16), mem_w_v[l].astype(BF16))
        w_r = jnp.pad(jnp.concatenate([w_group[l], w_router[l]], axis=1),
                      ((0, 0), (0, LANES - N_GROUPS - N_EXPERTS))).astype(F32)
        b_r = _lane_vec(jnp.concatenate([b_group[l], b_router[l]]))
        xs, hx_tiles, route = _post(xs, y_a, y_b, y_c, o_f, o_b, p_d, seg, row(jnp.tile(gdn_norm_g[l], MIXER_HEADS)),
                                    w_out[l].astype(BF16), row(mem_q_norm_g[l]), mem_w_q[l].astype(BF16), k_mem,
                                    v_mem, mem_w_o[l].astype(BF16), row(ffn_norm_g[l]), w_r, b_r, s)
        y_moe = _moe(hx_tiles, *_dispatch(route), w_gate[l].astype(BF16), w_up[l].astype(BF16),
                     w_down[l].astype(BF16), n)
    return _final(xs, y_moe, row(final_norm_g)).reshape(b, s, d)
```

```python
import functools
import math

import jax
import jax.numpy as jnp
import numpy as np
from jax import lax
from jax.experimental import pallas as pl
from jax.experimental.pallas import tpu as pltpu

F32 = jnp.float32
BF16 = jnp.bfloat16
HIGHEST = lax.Precision.HIGHEST

D_MODEL = 1024
GRID_W = 64
HEAD_DIM = 64
MIXER_WIDTH = 256
MIXER_HEADS = 4
NA_KH = 8
NA_KW = 16
RET_CHUNK = 128
ROPE_BASE = 10000.0
POOL_WINDOWS = (2, 4, 8, 16)
POOL_GROUP = 64
GDN_CHUNK = 64
CONV_K = 4
MEM_HEADS = 4
MEM_HEAD_DIM = 256
N_GROUPS = 4
EXPERTS_PER_GROUP = 8
N_EXPERTS = 32
TOP_K = 2
D_EXPERT = 512
EPS = 1e-6

P_IN_PAD = 3200
V7X_VMEM_LIMIT = 56 * 1024 * 1024
LANES = 128
SUBLANES = 8
ROW_TILES = D_MODEL // LANES
NEG_BIG = -1e30

PROJ_TILE = 512
POOL_TILE = 512
POST_TILE = 512
NA_ROWS_PER_STEP = 8
NA_ROWS_LOCKSTEP = 2
GDN_CHUNKS_PER_STEP = 2
RET_CHUNKS_PER_STEP = 2
MOE_BLOCK = 256
DMA_UNROLL = 8
HALO = 8


def _cparams(*sem):
    return pltpu.CompilerParams(dimension_semantics=sem, vmem_limit_bytes=V7X_VMEM_LIMIT)


def _dot(a, b):
    return jnp.dot(a, b, preferred_element_type=F32)


def _dot_nt(a, b):
    return lax.dot_general(a, b, (((1,), (1,)), ((), ())), preferred_element_type=F32)


def _dot_tn(a, b, precision=None):
    return lax.dot_general(a, b, (((0,), (0,)), ((), ())), preferred_element_type=F32, precision=precision)


def _silu(x):
    return x * (1.0 / (1.0 + jnp.exp(-x)))


def _rms(x, g):
    return x * lax.rsqrt(jnp.mean(x * x, axis=-1, keepdims=True) + EPS) * g


def _from_row_tiles(ref, first, count, stride):
    return jnp.concatenate([ref[pl.ds(first + k, count, stride=stride), :] for k in range(ROW_TILES)], axis=1)


def _to_row_tiles(ref, x):
    count = x.shape[0]
    for k in range(ROW_TILES):
        ref[pl.ds(k, count, stride=ROW_TILES), :] = x[:, k * LANES:(k + 1) * LANES]


def _add_moe(x, y_ref, rt_ref):
    t = x.shape[0]
    stride = TOP_K * ROW_TILES
    for slot in range(TOP_K):
        x = x + rt_ref[:, TOP_K + slot:TOP_K + slot + 1] * _from_row_tiles(y_ref, slot * ROW_TILES, t, stride)
    return x


def _split_bf16(x, parts):
    out = []
    for _ in range(parts):
        p = x.astype(BF16)
        out.append(p)
        x = x - p.astype(F32)
    return out


def _norm_proj_kernel(has_y, *refs):
    if has_y:
        x_ref, y_ref, rt_ref, g_ref, w_ref, xo_ref, pa_ref, pb_ref, pc_ref, pd_ref, pab_ref = refs
        x = _add_moe(x_ref[...], y_ref, rt_ref)
        xo_ref[...] = x
    else:
        x_ref, g_ref, w_ref, pa_ref, pb_ref, pc_ref, pd_ref, pab_ref = refs
        x = x_ref[...]
    h = _rms(x, g_ref[...]).astype(BF16)
    w = MIXER_WIDTH
    pa_ref[...] = _dot(h, w_ref[:, 0:3 * w]).astype(BF16)
    pb_ref[...] = _dot(h, w_ref[:, 3 * w:7 * w])
    pc_ref[...] = _dot(h, w_ref[:, 7 * w:8 * w])
    pd_ref[...] = _dot(h, w_ref[:, 8 * w:12 * w])
    pab_ref[...] = _dot(h, w_ref[:, 12 * w:P_IN_PAD])


def _moe_out_spec(tile):
    return pl.BlockSpec((tile * TOP_K * ROW_TILES, LANES), lambda i: (i, 0))


def _norm_proj(x, moe, g, w_pad):
    n = x.shape[0]
    tm = PROJ_TILE
    w = MIXER_WIDTH
    row = lambda c: pl.BlockSpec((tm, c), lambda i: (i, 0))
    const = lambda a: pl.BlockSpec(a.shape, lambda i: (0,) * a.ndim)
    outs = [jax.ShapeDtypeStruct((n, 3 * w), BF16), jax.ShapeDtypeStruct((n, 4 * w), F32),
            jax.ShapeDtypeStruct((n, w), F32), jax.ShapeDtypeStruct((n, 4 * w), F32),
            jax.ShapeDtypeStruct((n, LANES), F32)]
    out_specs = [row(3 * w), row(4 * w), row(w), row(4 * w), row(LANES)]
    if moe is None:
        ins, in_specs = (x, g, w_pad), [row(D_MODEL), const(g), const(w_pad)]
    else:
        ins = (x, *moe, g, w_pad)
        in_specs = [row(D_MODEL), _moe_out_spec(tm), row(LANES), const(g), const(w_pad)]
        outs = [jax.ShapeDtypeStruct((n, D_MODEL), F32)] + outs
        out_specs = [row(D_MODEL)] + out_specs
    res = pl.pallas_call(
        functools.partial(_norm_proj_kernel, moe is not None),
        grid=(n // tm,), in_specs=in_specs, out_specs=out_specs, out_shape=outs,
        compiler_params=_cparams("parallel"),
    )(*ins)
    return res if moe is not None else [x] + list(res)


def _na_bias_table(rpb):
    qc = np.arange(GRID_W)
    kc = np.arange(GRID_W)
    d_col = np.clip(kc[None, :] - qc[:, None], -(NA_KW - 1), NA_KW - 1) + (NA_KW - 1)
    pick = np.zeros((2 * NA_KW - 1, GRID_W * GRID_W), np.float32)
    pick[d_col.reshape(-1), np.arange(GRID_W * GRID_W)] = 1.0
    win = np.clip(qc - NA_KW // 2, 0, GRID_W - NA_KW)
    ok = (kc[None, :] >= win[:, None]) & (kc[None, :] < win[:, None] + NA_KW)
    cols = jnp.einsum('hrc,cx->hrx', rpb.astype(F32), jnp.asarray(pick), precision=HIGHEST)
    cols = jnp.where(ok[None, None], cols.reshape(MIXER_HEADS, 2 * NA_KH - 1, GRID_W, GRID_W), NEG_BIG)
    per_e = [jnp.transpose(cols[:, NA_KH - 1 - e:2 * NA_KH - 1 - e], (0, 2, 1, 3)) for e in range(NA_KH)]
    return jnp.stack(per_e, axis=1).reshape(MIXER_HEADS, NA_KH, GRID_W, NA_KH * GRID_W)


def _na_kernel(rows, q_ref, k_ref, v_ref, t_ref, o_ref):
    i = pl.program_id(1)
    dh = HEAD_DIM

    def rows_body(it, carry):
        chains = []
        for k in range(NA_ROWS_LOCKSTEP):
            rr = it * NA_ROWS_LOCKSTEP + k
            r = i * NA_ROWS_PER_STEP + rr
            kr0 = jnp.clip(r - NA_KH // 2, 0, rows - NA_KH)
            q = q_ref[pl.ds(pl.multiple_of(rr * GRID_W, GRID_W), GRID_W), :]
            k0 = pl.multiple_of(kr0 * GRID_W, GRID_W)
            kb = k_ref[pl.ds(k0, NA_KH * GRID_W), :]
            vb = v_ref[pl.ds(k0, NA_KH * GRID_W), :]
            for h in range(MIXER_HEADS):
                sl = slice(h * dh, (h + 1) * dh)
                chains.append((q[:, sl], kb[:, sl], vb[:, sl], h, r - kr0))
        s = [_dot_nt(q, kb) * (dh ** -0.5) + t_ref[h, e] for q, kb, _, h, e in chains]
        m = [jnp.max(a, axis=-1, keepdims=True) for a in s]
        p = [jnp.exp(a - b) for a, b in zip(s, m)]
        l = [jnp.sum(a, axis=-1, keepdims=True) for a in p]
        o = [_dot(a.astype(BF16), c[2]) / b for a, b, c in zip(p, l, chains)]
        for k in range(NA_ROWS_LOCKSTEP):
            rr = it * NA_ROWS_LOCKSTEP + k
            row = jnp.concatenate(o[k * MIXER_HEADS:(k + 1) * MIXER_HEADS], axis=-1)
            o_ref[pl.ds(pl.multiple_of(rr * GRID_W, GRID_W), GRID_W), :] = row.astype(BF16)
        return carry

    lax.fori_loop(0, NA_ROWS_PER_STEP // NA_ROWS_LOCKSTEP, rows_body, 0)


def _neighbourhood_attention(p_a, table, b, s):
    rows = s // GRID_W
    assert rows >= NA_KH and rows % NA_ROWS_PER_STEP == 0
    steps = rows // NA_ROWS_PER_STEP
    tq = NA_ROWS_PER_STEP * GRID_W
    w = MIXER_WIDTH
    return pl.pallas_call(
        functools.partial(_na_kernel, rows),
        grid=(b, steps),
        in_specs=[pl.BlockSpec((tq, w), lambda bi, i: (bi * steps + i, 0)),
                  pl.BlockSpec((s, w), lambda bi, i: (bi, 1)),
                  pl.BlockSpec((s, w), lambda bi, i: (bi, 2)),
                  pl.BlockSpec(table.shape, lambda bi, i: (0, 0, 0, 0))],
        out_specs=pl.BlockSpec((tq, w), lambda bi, i: (bi * steps + i, 0)),
        out_shape=jax.ShapeDtypeStruct((b * s, w), BF16),
        compiler_params=_cparams("parallel", "arbitrary"),
    )(p_a, p_a, p_a, table)


def _retention_tables(s):
    h, dh, c = MIXER_HEADS, HEAD_DIM, RET_CHUNK
    half = dh // 2
    inv = ROPE_BASE ** (-jnp.arange(half, dtype=F32) / half)
    ang = jnp.arange(s, dtype=F32)[:, None] * inv[None, :]
    cos, sin = jnp.cos(ang), jnp.sin(ang)
    zero = jnp.zeros_like(sin)
    cos_t = jnp.tile(jnp.concatenate([cos, cos], axis=-1), (1, h))
    sin_lo = jnp.tile(jnp.concatenate([-sin, zero], axis=-1), (1, h))
    sin_hi = jnp.tile(jnp.concatenate([zero, sin], axis=-1), (1, h))
    log_f = np.log1p(-np.exp2(-5.0 - np.arange(h, dtype=np.float64)))
    log_b = log_f[::-1]
    pos = np.arange(c, dtype=np.float64)
    diff = pos[:, None] - pos[None, :]
    dmat = np.where(diff >= 0, np.exp(log_f[:, None, None] * np.maximum(diff, 0.0)), 0.0) \
        + np.where(diff < 0, np.exp(log_b[:, None, None] * np.maximum(-diff, 0.0)), 0.0)
    lanes = lambda t: np.repeat(t.T, dh, axis=1)
    dec = np.stack([lanes(np.exp(log_f[:, None] * (pos + 1.0))),
                    lanes(np.exp(log_f[:, None] * (c - 1.0 - pos))),
                    lanes(np.exp(log_b[:, None] * (c - pos))),
                    lanes(np.exp(log_b[:, None] * pos))])
    chunk_f = [float(np.exp(v * c)) for v in log_f]
    chunk_b = [float(np.exp(v * c)) for v in log_b]
    return cos_t, sin_lo, sin_hi, jnp.asarray(dmat, F32), jnp.asarray(dec, F32), chunk_f, chunk_b


def _rotary(x, cos, sin_lo, sin_hi):
    w = x.shape[-1]
    return x * cos + pltpu.roll(x, w - HEAD_DIM // 2, 1) * sin_lo + pltpu.roll(x, HEAD_DIM // 2, 1) * sin_hi


def _retention_kernel(nb, chunk_f, chunk_b, q_ref, k_ref, v_ref, gate_ref, cos_ref, slo_ref, shi_ref,
                      dmat_ref, dec_ref, ng_ref, o_ref, sf_ref, sb_ref, sball_ref):
    t = pl.program_id(1)
    dh, c, cps = HEAD_DIM, RET_CHUNK, RET_CHUNKS_PER_STEP
    cos, slo, shi = cos_ref[...], slo_ref[...], shi_ref[...]
    kr = _rotary(k_ref[...], cos, slo, shi) * (dh ** -0.5)
    vb = v_ref[...].astype(BF16)
    items = [(ci, h) for ci in range(cps) for h in range(MIXER_HEADS)]
    part = lambda x, ci, h: x[ci * c:(ci + 1) * c, h * dh:(h + 1) * dh]
    split = lambda x: {it: part(x, *it) for it in items}

    @pl.when(t == 0)
    def _():
        sf_ref[...] = jnp.zeros_like(sf_ref)
        sb_ref[...] = jnp.zeros_like(sb_ref)

    @pl.when(t < nb)
    def _():
        blk = nb - 1 - t
        kd, v = split((kr * dec_ref[3]).astype(BF16)), split(vb)
        kv = {it: _dot_tn(kd[it], v[it]) for it in items}
        for h in range(MIXER_HEADS):
            state = sb_ref[h]
            for ci in range(cps - 1, -1, -1):
                sball_ref[blk * cps + ci, h] = state
                state = state * chunk_b[h] + kv[ci, h]
            sb_ref[h] = state

    @pl.when(t >= nb)
    def _():
        blk = t - nb
        qr = _rotary(q_ref[...], cos, slo, shi)
        q, k, v = split(qr.astype(BF16)), split(kr.astype(BF16)), split(vb)
        qf, qbk = split((qr * dec_ref[0]).astype(BF16)), split((qr * dec_ref[2]).astype(BF16))
        kd = split((kr * dec_ref[1]).astype(BF16))
        sc = {it: (_dot_nt(q[it], k[it]) * dmat_ref[it[1]]).astype(BF16) for it in items}
        o = {it: _dot(sc[it], v[it]) for it in items}
        kv = {it: _dot_tn(kd[it], v[it]) for it in items}
        ob = {it: _dot(qbk[it], sball_ref[blk * cps + it[0], it[1]].astype(BF16)) for it in items}
        of = {}
        for h in range(MIXER_HEADS):
            state = sf_ref[h]
            for ci in range(cps):
                of[ci, h] = _dot(qf[ci, h], state.astype(BF16))
                state = state * chunk_f[h] + kv[ci, h]
            sf_ref[h] = state
        o = {it: o[it] + of[it] + ob[it] for it in items}
        mu = {it: jnp.mean(o[it], axis=-1, keepdims=True) for it in items}
        oc = {it: o[it] - mu[it] for it in items}
        var = {it: jnp.mean(oc[it] * oc[it], axis=-1, keepdims=True) for it in items}
        on = {it: oc[it] * lax.rsqrt(var[it] + EPS) for it in items}
        y = jnp.concatenate([jnp.concatenate([on[ci, h] for h in range(MIXER_HEADS)], axis=-1)
                             for ci in range(cps)], axis=0)
        o_ref[...] = (y * ng_ref[...] * _silu(gate_ref[...])).astype(BF16)


def _retention(p_b, norm_g, tables, b, s):
    cos_t, sin_lo, sin_hi, dmat, dec, chunk_f, chunk_b = tables
    cps = RET_CHUNKS_PER_STEP
    tb = RET_CHUNK * cps
    nb = s // tb
    w = MIXER_WIDTH
    dec = jnp.tile(dec, (1, cps, 1))
    block = lambda t: jnp.where(t < nb, nb - 1 - t, t - nb)
    col = lambda j: pl.BlockSpec((tb, w), lambda bi, t: (bi * nb + block(t), j))
    tab = pl.BlockSpec((tb, w), lambda bi, t: (block(t), 0))
    const = lambda a: pl.BlockSpec(a.shape, lambda bi, t: (0,) * a.ndim)
    return pl.pallas_call(
        functools.partial(_retention_kernel, nb, chunk_f, chunk_b),
        grid=(b, 2 * nb),
        in_specs=[col(0), col(1), col(2), col(3), tab, tab, tab, const(dmat), const(dec), const(norm_g)],
        out_specs=pl.BlockSpec((tb, w), lambda bi, t: (bi * nb + jnp.maximum(t - nb, 0), 0)),
        out_shape=jax.ShapeDtypeStruct((b * s, w), BF16),
        scratch_shapes=[pltpu.VMEM((MIXER_HEADS, HEAD_DIM, HEAD_DIM), F32),
                        pltpu.VMEM((MIXER_HEADS, HEAD_DIM, HEAD_DIM), F32),
                        pltpu.VMEM((s // RET_CHUNK, MIXER_HEADS, HEAD_DIM, HEAD_DIM), F32)],
        compiler_params=_cparams("arbitrary", "arbitrary"),
    )(p_b, p_b, p_b, p_b, cos_t, sin_lo, sin_hi, dmat, dec, norm_g)


def _halo_specs(tile, width, n_rows, block_of):
    per = tile // HALO
    last = n_rows // HALO - 1
    main = pl.BlockSpec((tile, width), lambda *g: (block_of(*g), 0))
    prev = pl.BlockSpec((HALO, width), lambda *g: (jnp.maximum(block_of(*g) * per - 1, 0), 0))
    nxt = pl.BlockSpec((HALO, width), lambda *g: (jnp.minimum((block_of(*g) + 1) * per, last), 0))
    return main, prev, nxt


def _fill_padded(pad_ref, main, prev, nxt, first, last):
    t = main.shape[0]
    pad_ref[0:HALO, :] = jnp.where(first, 0.0, prev)
    pad_ref[HALO:HALO + t, :] = main
    pad_ref[HALO + t:2 * HALO + t, :] = jnp.where(last, 0.0, nxt)


def _pool_kernel(s, tiles_per_seq, u_ref, prev_ref, next_ref, w_ref, scale_ref, o_ref, pad_ref):
    t = POOL_TILE
    tseq = pl.program_id(0) % tiles_per_seq
    u = u_ref[...]
    _fill_padded(pad_ref, u, prev_ref[...], next_ref[...], tseq == 0, tseq == tiles_per_seq - 1)
    lane = lax.broadcasted_iota(jnp.int32, (1, MIXER_WIDTH), 1)
    half = jnp.full((1, MIXER_WIDTH), POOL_WINDOWS[0] // 2, jnp.int32)
    for gi in range(1, len(POOL_WINDOWS)):
        half = jnp.where(lane >= gi * POOL_GROUP, POOL_WINDOWS[gi] // 2, half)
    max_half = POOL_WINDOWS[-1] // 2
    acc = jnp.zeros((t, MIXER_WIDTH), F32)
    for d in range(-max_half, max_half):
        inside = (d >= -half) & (d < half)
        acc = acc + jnp.where(inside, pad_ref[HALO + d:HALO + d + t, :], 0.0)
    pos = tseq * t + lax.broadcasted_iota(jnp.int32, (t, 1), 0)
    count = jnp.minimum(pos + half, s) - jnp.maximum(pos - half, 0)
    diff = acc / count.astype(F32) - u
    o_ref[...] = (_dot(diff.astype(BF16), w_ref[...]) * scale_ref[...]).astype(BF16)


def _pool(p_c, w_blockdiag, scale, b, s):
    t = POOL_TILE
    tiles_per_seq = s // t
    main, prev, nxt = _halo_specs(t, MIXER_WIDTH, b * s, lambda i: i)
    const = lambda a: pl.BlockSpec(a.shape, lambda i: (0,) * a.ndim)
    return pl.pallas_call(
        functools.partial(_pool_kernel, s, tiles_per_seq),
        grid=(b * tiles_per_seq,),
        in_specs=[main, prev, nxt, const(w_blockdiag), const(scale)],
        out_specs=pl.BlockSpec((t, MIXER_WIDTH), lambda i: (i, 0)),
        out_shape=jax.ShapeDtypeStruct((b * s, MIXER_WIDTH), BF16),
        scratch_shapes=[pltpu.VMEM((t + 2 * HALO, MIXER_WIDTH), F32)],
        compiler_params=_cparams("parallel"),
    )(p_c, p_c, p_c, w_blockdiag, scale)


def _gdn_prepare(forward, u, ab, alog, dtb, tri_col, tri_row):
    c, dh, w = GDN_CHUNK, HEAD_DIM, MIXER_WIDTH
    t = u.shape[0]
    a_off, b_off = (0, 2 * MIXER_HEADS) if forward else (MIXER_HEADS, 3 * MIXER_HEADS)
    x = ab + dtb
    g_all = -jnp.exp(alog) * (jnp.maximum(x, 0.0) + jnp.log(1.0 + jnp.exp(-jnp.abs(x))))
    beta_all = 1.0 / (1.0 + jnp.exp(-ab))
    g_parts = _split_bf16(g_all, 3)
    gc_col = sum(_dot(tri_col, p) for p in g_parts)
    gc_row = sum(_dot_tn(p, tri_row) for p in g_parts)
    ii = lax.broadcasted_iota(jnp.int32, (c, c), 0)
    jj = lax.broadcasted_iota(jnp.int32, (c, c), 1)
    incl = (ii >= jj) if forward else (ii <= jj)
    strict = (ii > jj) if forward else (ii < jj)
    last = c - 1 if forward else 0
    items = [(ci, h) for ci in range(t // c) for h in range(MIXER_HEADS)]
    rows = lambda ci: slice(ci * c, (ci + 1) * c)
    part = lambda base: [u[rows(ci), base + h * dh:base + (h + 1) * dh] for ci, h in items]
    q, k, v = part(0), part(w), part(2 * w)
    q = [a * lax.rsqrt(jnp.sum(a * a, axis=-1, keepdims=True) + EPS) * (dh ** -0.5) for a in q]
    k = [a * lax.rsqrt(jnp.sum(a * a, axis=-1, keepdims=True) + EPS) for a in k]
    gcc = [gc_col[rows(ci), a_off + h:a_off + h + 1] for ci, h in items]
    gcr = [gc_row[a_off + h:a_off + h + 1, rows(ci)] for ci, h in items]
    beta = [beta_all[rows(ci), b_off + h:b_off + h + 1] for ci, h in items]
    g_last = [a[last:last + 1, :] for a in gcc]
    decay = [jnp.where(incl, jnp.exp(jnp.where(incl, a - b, 0.0)), 0.0) for a, b in zip(gcc, gcr)]
    kq = [_dot_nt(jnp.concatenate([kk, qq], axis=0).astype(BF16), kk.astype(BF16)) for kk, qq in zip(k, q)]
    lmat = [jnp.where(strict, m[:c] * b * d, 0.0) for m, b, d in zip(kq, beta, decay)]
    attn = [(m[c:] * d).astype(BF16) for m, d in zip(kq, decay)]
    xinv = [-a for a in lmat]
    lp = [a.astype(BF16) for a in lmat]
    for _ in range(int(math.log2(c)) - 1):
        lp32 = [_dot(a, a) for a in lp]
        lp = [a.astype(BF16) for a in lp32]
        xinv = [xi + a32 + _dot(xi.astype(BF16), a) for xi, a32, a in zip(xinv, lp32, lp)]
    egc = [jnp.exp(a) for a in gcc]
    rhs = [jnp.concatenate([vv * b, kk * (b * e)], axis=-1) for vv, kk, b, e in zip(v, k, beta, egc)]
    sol = [r + _dot(xi.astype(BF16), r.astype(BF16)) for xi, r in zip(xinv, rhs)]
    wq = [jnp.concatenate([so[:, dh:], qq * e], axis=0).astype(BF16) for so, qq, e in zip(sol, q, egc)]
    u_val = [so[:, :dh] for so in sol]
    k_state = [(kk * jnp.exp(gl - a)).astype(BF16) for kk, gl, a in zip(k, g_last, gcc)]
    e_last = [jnp.exp(gl) for gl in g_last]
    keyed = lambda vals: dict(zip(items, vals))
    return dict(wq=keyed(wq), u_val=keyed(u_val), attn=keyed(attn), k_state=keyed(k_state), e_last=keyed(e_last))


def _gdn_kernel(nb, uf_ref, pf_ref, nf_ref, abf_ref, ub_ref, pb_ref, nb_ref, abb_ref, cw_ref, alog_ref,
                dtb_ref, tril_ref, triu_ref, of_ref, ob_ref, sf_ref, sb_ref, pad_ref):
    n = pl.program_id(1)
    t = uf_ref.shape[0]
    c, dh = GDN_CHUNK, HEAD_DIM
    nch = t // c

    @pl.when(n == 0)
    def _():
        sf_ref[...] = jnp.zeros_like(sf_ref)
        sb_ref[...] = jnp.zeros_like(sb_ref)

    def conv_silu(main_ref, prev_ref, next_ref, blk):
        _fill_padded(pad_ref, main_ref[...], prev_ref[...], next_ref[...], blk == 0, blk == nb - 1)
        acc = jnp.zeros((t, 3 * MIXER_WIDTH), F32)
        for k in range(CONV_K):
            off = HALO + k - CONV_K // 2
            acc = acc + pad_ref[off:off + t, :] * cw_ref[k:k + 1, :]
        return _silu(acc)

    alog, dtb = alog_ref[...], dtb_ref[...]
    tril, triu = tril_ref[...], triu_ref[...]
    u_f = conv_silu(uf_ref, pf_ref, nf_ref, n)
    u_b = conv_silu(ub_ref, pb_ref, nb_ref, nb - 1 - n)
    prep_f = _gdn_prepare(True, u_f, abf_ref[...], alog, dtb, tril, triu)
    prep_b = _gdn_prepare(False, u_b, abb_ref[...], alog, dtb, triu, tril)

    chains = [(prep_f, sf_ref, of_ref, h, True) for h in range(MIXER_HEADS)] + \
             [(prep_b, sb_ref, ob_ref, h, False) for h in range(MIXER_HEADS)]
    state = [s_ref[h] for _, s_ref, _, h, _ in chains]
    for step in range(nch):
        key = [(step if fw else nch - 1 - step, h) for _, _, _, h, fw in chains]
        ws = [_dot(p['wq'][kk], st.astype(BF16)) for (p, *_), kk, st in zip(chains, key, state)]
        v_new = [p['u_val'][kk] - a[:c] for (p, *_), kk, a in zip(chains, key, ws)]
        vb = [a.astype(BF16) for a in v_new]
        out = [a[c:] + _dot(p['attn'][kk], b) for (p, *_), kk, a, b in zip(chains, key, ws, vb)]
        state = [st * p['e_last'][kk] + _dot_tn(p['k_state'][kk], b)
                 for (p, *_), kk, st, b in zip(chains, key, state, vb)]
        for (_, _, o_ref, h, _), (ci, _), o in zip(chains, key, out):
            o_ref[ci * c:(ci + 1) * c, h * dh:(h + 1) * dh] = o
    for (_, s_ref, _, h, _), st in zip(chains, state):
        s_ref[h] = st


def _gated_deltanet(p_d, p_ab, conv_w, alog_vec, dtb_vec, b, s):
    c = GDN_CHUNK
    t = c * GDN_CHUNKS_PER_STEP
    nb = s // t
    w = MIXER_WIDTH
    r = np.arange(t)
    same = (r[:, None] // c) == (r[None, :] // c)
    tril = jnp.asarray(same & (r[:, None] >= r[None, :]), BF16)
    triu = jnp.asarray(same & (r[:, None] <= r[None, :]), BF16)
    fwd = lambda bi, n: bi * nb + n
    bwd = lambda bi, n: bi * nb + nb - 1 - n
    mf, pf, nf = _halo_specs(t, 3 * w, b * s, fwd)
    mb, pb_, nb_ = _halo_specs(t, 3 * w, b * s, bwd)
    abf = pl.BlockSpec((t, LANES), lambda bi, n: (fwd(bi, n), 0))
    abb = pl.BlockSpec((t, LANES), lambda bi, n: (bwd(bi, n), 0))
    const = lambda a: pl.BlockSpec(a.shape, lambda bi, n: (0,) * a.ndim)
    return pl.pallas_call(
        functools.partial(_gdn_kernel, nb),
        grid=(b, nb),
        in_specs=[mf, pf, nf, abf, mb, pb_, nb_, abb, const(conv_w), const(alog_vec), const(dtb_vec),
                  const(tril), const(triu)],
        out_specs=[pl.BlockSpec((t, w), lambda bi, n: (fwd(bi, n), 0)),
                   pl.BlockSpec((t, w), lambda bi, n: (bwd(bi, n), 0))],
        out_shape=[jax.ShapeDtypeStruct((b * s, w), F32), jax.ShapeDtypeStruct((b * s, w), F32)],
        scratch_shapes=[pltpu.VMEM((MIXER_HEADS, HEAD_DIM, HEAD_DIM), F32),
                        pltpu.VMEM((MIXER_HEADS, HEAD_DIM, HEAD_DIM), F32),
                        pltpu.VMEM((t + 2 * HALO, 3 * w), F32)],
        compiler_params=_cparams("arbitrary", "arbitrary"),
    )(p_d, p_d, p_d, p_ab, p_d, p_d, p_d, p_ab, conv_w, alog_vec, dtb_vec, tril, triu)


def _mem_kv_kernel(m_ref, g_ref, wk_ref, wv_ref, k_ref, v_ref):
    h = _rms(m_ref[0], g_ref[...]).astype(BF16)
    k_ref[0] = _dot(h, wk_ref[...]).astype(BF16)
    v_ref[0] = _dot(h, wv_ref[...]).astype(BF16)


def _mem_kv(mem, g, wk, wv):
    b, nm, d = mem.shape
    const = lambda a: pl.BlockSpec(a.shape, lambda i: (0,) * a.ndim)
    blk = pl.BlockSpec((1, nm, d), lambda i: (i, 0, 0))
    return pl.pallas_call(
        _mem_kv_kernel, grid=(b,),
        in_specs=[blk, const(g), const(wk), const(wv)], out_specs=[blk, blk],
        out_shape=[jax.ShapeDtypeStruct((b, nm, d), BF16)] * 2,
        compiler_params=_cparams("parallel"),
    )(mem, g, wk, wv)


def _route_lanes(lg):
    lane_i = lax.broadcasted_iota(jnp.int32, lg.shape, 1)
    lane = lane_i.astype(F32)
    first_at = lambda vals, m: jnp.min(jnp.where(vals == m, lane, float(LANES)), axis=-1, keepdims=True)
    g_mask = lane_i < N_GROUPS
    gl = jnp.where(g_mask, lg, NEG_BIG)
    g_max = jnp.max(gl, axis=-1, keepdims=True)
    g_sum = jnp.sum(jnp.where(g_mask, jnp.exp(gl - g_max), 0.0), axis=-1, keepdims=True)
    g_p = 1.0 / g_sum
    g_idx = first_at(gl, g_max)
    lo = N_GROUPS + EXPERTS_PER_GROUP * g_idx
    e_mask = (lane >= lo) & (lane < lo + EXPERTS_PER_GROUP)
    el = jnp.where(e_mask, lg, NEG_BIG)
    m1 = jnp.max(el, axis=-1, keepdims=True)
    i1 = first_at(el, m1)
    el2 = jnp.where(lane == i1, NEG_BIG, el)
    m2 = jnp.max(el2, axis=-1, keepdims=True)
    i2 = first_at(el2, m2)
    r = jnp.exp(m2 - m1)
    w1 = g_p / (1.0 + r)
    w2 = g_p * r / (1.0 + r)
    out = jnp.where(lane_i == 0, i1 - N_GROUPS, 0.0)
    out = jnp.where(lane_i == 1, i2 - N_GROUPS, out)
    out = jnp.where(lane_i == 2, w1, out)
    out = jnp.where(lane_i == 3, w2, out)
    chosen = jnp.where((lane == i1 - N_GROUPS) | (lane == i2 - N_GROUPS), 1.0, 0.0)
    return out, chosen


def _post_kernel(x_ref, ya_ref, yb_ref, yc_ref, of_ref, ob_ref, gate_ref, seg_ref, gg_ref, wout_ref, gq_ref,
                 wq_ref, k_ref, v_ref, wo_ref, gf_ref, wr_ref, br_ref, xo_ref, hx_ref, rt_ref, hist_ref):
    w = MIXER_WIDTH
    o = of_ref[...] + ob_ref[...]
    ms = sum(_dot(p, seg_ref[...]) for p in _split_bf16(o * o, 2))
    yd = o * lax.rsqrt(ms + EPS) * gg_ref[...] * _silu(gate_ref[...])
    mix = _dot(ya_ref[...], wout_ref[0:w, :]) + _dot(yb_ref[...], wout_ref[w:2 * w, :])
    mix = mix + _dot(yc_ref[...], wout_ref[2 * w:3 * w, :]) + _dot(yd.astype(BF16), wout_ref[3 * w:4 * w, :])
    x1 = x_ref[...] + mix
    q = _dot(_rms(x1, gq_ref[...]).astype(BF16), wq_ref[...]).astype(BF16)
    heads = []
    for h in range(MEM_HEADS):
        sl = slice(h * MEM_HEAD_DIM, (h + 1) * MEM_HEAD_DIM)
        sc = _dot_nt(q[:, sl], k_ref[0, :, sl]) * (MEM_HEAD_DIM ** -0.5)
        p = jnp.exp(sc - jnp.max(sc, axis=-1, keepdims=True))
        l = jnp.sum(p, axis=-1, keepdims=True)
        heads.append((_dot(p.astype(BF16), v_ref[0, :, sl]) / l).astype(BF16))
    x2 = x1 + _dot(jnp.concatenate(heads, axis=-1), wo_ref[...])
    xo_ref[...] = x2
    hx = _rms(x2, gf_ref[...])
    _to_row_tiles(hx_ref, hx)
    h_hi, h_lo = _split_bf16(hx, 2)
    lg = _dot(h_hi, wr_ref[0]) + _dot(h_hi, wr_ref[1]) + _dot(h_lo, wr_ref[0]) + br_ref[...]
    route, chosen = _route_lanes(lg)
    rt_ref[...] = route
    hist_ref[...] = jnp.broadcast_to(jnp.sum(chosen, axis=0, keepdims=True), hist_ref.shape)


def _post(x, y_a, y_b, y_c, o_f, o_b, p_d, seg, gdn_g, w_out, g_q, w_q, k_mem, v_mem, w_o, g_f, w_r, b_r, s):
    n = x.shape[0]
    tm = POST_TILE
    w = MIXER_WIDTH
    per_seq = s // tm
    row = lambda c: pl.BlockSpec((tm, c), lambda i: (i, 0))
    const = lambda a: pl.BlockSpec(a.shape, lambda i: (0,) * a.ndim)
    kv = pl.BlockSpec((1,) + k_mem.shape[1:], lambda i: (i // per_seq, 0, 0))
    return pl.pallas_call(
        _post_kernel, grid=(n // tm,),
        in_specs=[row(D_MODEL), row(w), row(w), row(w), row(w), row(w),
                  pl.BlockSpec((tm, w), lambda i: (i, 3)), const(seg), const(gdn_g), const(w_out), const(g_q),
                  const(w_q), kv, kv, const(w_o), const(g_f), const(w_r), const(b_r)],
        out_specs=[row(D_MODEL), pl.BlockSpec((tm * ROW_TILES, LANES), lambda i: (i, 0)), row(LANES),
                   pl.BlockSpec((SUBLANES, LANES), lambda i: (i, 0))],
        out_shape=[jax.ShapeDtypeStruct((n, D_MODEL), F32), jax.ShapeDtypeStruct((n * ROW_TILES, LANES), F32),
                   jax.ShapeDtypeStruct((n, LANES), F32), jax.ShapeDtypeStruct((n // tm * SUBLANES, LANES), F32)],
        compiler_params=_cparams("parallel"),
    )(x, y_a, y_b, y_c, o_f, o_b, p_d, seg, gdn_g, w_out, g_q, w_q, k_mem, v_mem, w_o, g_f, w_r, b_r)


def _dispatch(route, hist):
    n = route.shape[0]
    a = n * TOP_K
    blk = MOE_BLOCK
    flat_e = route[:, 0:TOP_K].astype(jnp.int32).reshape(a)
    _, order = lax.sort_key_val(flat_e, jnp.arange(a, dtype=jnp.int32))
    counts = jnp.sum(hist.reshape(-1, SUBLANES, LANES)[:, 0, :N_EXPERTS], axis=0).astype(jnp.int32)
    padded = (counts + blk - 1) // blk * blk
    pad_end = jnp.cumsum(padded)
    pad_start = pad_end - padded
    start = jnp.cumsum(counts) - counts
    n_blocks = a // blk + N_EXPERTS
    blk_start = jnp.arange(n_blocks, dtype=jnp.int32) * blk
    blk_expert = jnp.minimum(jnp.sum(pad_end[None, :] <= blk_start[:, None], axis=1), N_EXPERTS - 1).astype(jnp.int32)
    into_expert = blk_start - pad_start[blk_expert]
    blk_valid = jnp.clip(counts[blk_expert] - into_expert, 0, blk).astype(jnp.int32)
    in_blk = jnp.arange(blk, dtype=jnp.int32)[None, :]
    idx = (start[blk_expert] + into_expert)[:, None] + in_blk
    row_dst = jnp.where(in_blk < blk_valid[:, None], order[jnp.clip(idx, 0, a - 1)], 0)
    return blk_expert, blk_valid, row_dst.reshape(-1)


def _moe_kernel(n_blocks, be_ref, bv_ref, dst_ref, hx_hbm, wg_ref, wu_ref, wd_ref, y_hbm,
                xbuf, ybuf, gsem, ssem):
    j = pl.program_id(0)
    blk = MOE_BLOCK
    slot = j % 2
    rt = ROW_TILES
    tile = lambda ref, i: ref.at[pl.ds(pl.multiple_of(i * rt, rt), rt), :]

    def start_gather(jj, sl):
        @pl.when(bv_ref[jj] > 0)
        def _():
            def body(i, carry):
                tok = dst_ref[jj * blk + i] // TOP_K
                pltpu.make_async_copy(tile(hx_hbm, tok), tile(xbuf.at[sl], i), gsem.at[sl]).start()
                return carry
            lax.fori_loop(0, blk, body, 0, unroll=DMA_UNROLL)

    def wait_gather(sl):
        pltpu.make_async_copy(hx_hbm.at[pl.ds(0, blk * rt), :], xbuf.at[sl], gsem.at[sl]).wait()

    def start_scatter():
        def body(i, carry):
            pltpu.make_async_copy(tile(ybuf, i), tile(y_hbm, dst_ref[j * blk + i]), ssem.at[0]).start()
            return carry

        @pl.when(bv_ref[j] == blk)
        def _():
            lax.fori_loop(0, blk, body, 0, unroll=DMA_UNROLL)

        @pl.when(bv_ref[j] < blk)
        def _():
            lax.fori_loop(0, bv_ref[j], body, 0)

    def wait_scatter(jj):
        @pl.when(bv_ref[jj] > 0)
        def _():
            rows = pl.ds(0, bv_ref[jj] * rt)
            pltpu.make_async_copy(ybuf.at[rows, :], y_hbm.at[rows, :], ssem.at[0]).wait()

    @pl.when(j == 0)
    def _():
        start_gather(0, 0)

    @pl.when(j + 1 < n_blocks)
    def _():
        start_gather(j + 1, 1 - slot)

    @pl.when(bv_ref[j] > 0)
    def _():
        wait_gather(slot)
        xs = _from_row_tiles(xbuf.at[slot], 0, blk, rt).astype(BF16)
        g = _dot(xs, wg_ref[0])
        u = _dot(xs, wu_ref[0])
        y = _dot((_silu(g) * u).astype(BF16), wd_ref[0])

        @pl.when(j > 0)
        def _():
            wait_scatter(j - 1)

        _to_row_tiles(ybuf, y)
        start_scatter()

    @pl.when((bv_ref[j] == 0) & (j > 0))
    def _():
        wait_scatter(j - 1)

    @pl.when(j == n_blocks - 1)
    def _():
        wait_scatter(j)


def _moe(hx_tiles, blk_expert, blk_valid, row_dst, w_gate, w_up, w_down, n):
    blk = MOE_BLOCK
    n_blocks = blk_expert.shape[0]
    imap = lambda f: (lambda j, be, bv, dst: f(j, be))
    grid_spec = pltpu.PrefetchScalarGridSpec(
        num_scalar_prefetch=3, grid=(n_blocks,),
        in_specs=[pl.BlockSpec(memory_space=pl.ANY),
                  pl.BlockSpec((1, D_MODEL, D_EXPERT), imap(lambda j, be: (be[j], 0, 0))),
                  pl.BlockSpec((1, D_MODEL, D_EXPERT), imap(lambda j, be: (be[j], 0, 0))),
                  pl.BlockSpec((1, D_EXPERT, D_MODEL), imap(lambda j, be: (be[j], 0, 0)))],
        out_specs=pl.BlockSpec(memory_space=pl.ANY),
        scratch_shapes=[pltpu.VMEM((2, blk * ROW_TILES, LANES), F32), pltpu.VMEM((blk * ROW_TILES, LANES), F32),
                        pltpu.SemaphoreType.DMA((2,)), pltpu.SemaphoreType.DMA((1,))])
    return pl.pallas_call(
        functools.partial(_moe_kernel, n_blocks), grid_spec=grid_spec,
        out_shape=jax.ShapeDtypeStruct((n * TOP_K * ROW_TILES, LANES), F32),
        compiler_params=_cparams("arbitrary"),
    )(blk_expert, blk_valid, row_dst, hx_tiles, w_gate, w_up, w_down)


def _final_kernel(x_ref, y_ref, rt_ref, g_ref, o_ref):
    o_ref[...] = _rms(_add_moe(x_ref[...], y_ref, rt_ref), g_ref[...])


def _final(x, moe, g):
    n = x.shape[0]
    tm = PROJ_TILE
    return pl.pallas_call(
        _final_kernel, grid=(n // tm,),
        in_specs=[pl.BlockSpec((tm, D_MODEL), lambda i: (i, 0)), _moe_out_spec(tm),
                  pl.BlockSpec((tm, LANES), lambda i: (i, 0)), pl.BlockSpec(g.shape, lambda i: (0, 0))],
        out_specs=pl.BlockSpec((tm, D_MODEL), lambda i: (i, 0)),
        out_shape=jax.ShapeDtypeStruct((n, D_MODEL), F32),
        compiler_params=_cparams("parallel"),
    )(x, *moe, g)


def _lane_vec(v, width=LANES):
    v = v.reshape(1, -1).astype(F32)
    return jnp.pad(v, ((0, 0), (0, width - v.shape[1])))


def _block_diag(blocks):
    g, c, _ = blocks.shape
    eye = jnp.eye(g, dtype=blocks.dtype)
    return (eye[:, None, :, None] * blocks[:, :, None, :]).reshape(g * c, g * c)


def kernel(x, mem, mix_norm_g, w_in, na_rpb, ret_norm_g, pool_w, pool_scale, gdn_conv_w, gdn_a_log, gdn_dt_bias, gdn_norm_g, w_out, mem_q_norm_g, mem_kv_norm_g, mem_w_q, mem_w_k, mem_w_v, mem_w_o, ffn_norm_g, w_group, b_group, w_router, b_router, w_gate, w_up, w_down, final_norm_g):
    b, s, d = x.shape
    n = b * s
    depth = w_in.shape[0]
    row = lambda v: v.reshape(1, -1).astype(F32)
    ret_tables = _retention_tables(s)
    seg = _block_diag(jnp.full((MIXER_HEADS, HEAD_DIM, HEAD_DIM), 1.0 / HEAD_DIM, BF16))
    xs = x.reshape(n, d)
    moe = None
    for l in range(depth):
        w_pad = jnp.pad(w_in[l], ((0, 0), (0, P_IN_PAD - w_in.shape[2]))).astype(BF16)
        xs, p_a, p_b, p_c, p_d, p_ab = _norm_proj(xs, moe, row(mix_norm_g[l]), w_pad)
        y_a = _neighbourhood_attention(p_a, _na_bias_table(na_rpb[l]), b, s)
        y_b = _retention(p_b, row(ret_norm_g[l]), ret_tables, b, s)
        y_c = _pool(p_c, _block_diag(pool_w[l]).astype(BF16), row(pool_scale[l]), b, s)
        o_f, o_b = _gated_deltanet(p_d, p_ab, gdn_conv_w[l].astype(F32), _lane_vec(gdn_a_log[l]),
                                   _lane_vec(gdn_dt_bias[l]), b, s)
        k_mem, v_mem = _mem_kv(mem, row(mem_kv_norm_g[l]), mem_w_k[l].astype(BF16), mem_w_v[l].astype(BF16))
        w_r = jnp.pad(jnp.concatenate([w_group[l], w_router[l]], axis=1),
                      ((0, 0), (0, LANES - N_GROUPS - N_EXPERTS))).astype(F32)
        b_r = _lane_vec(jnp.concatenate([b_group[l], b_router[l]]))
        xs, hx_tiles, route, hist = _post(
            xs, y_a, y_b, y_c, o_f, o_b, p_d, seg, row(jnp.tile(gdn_norm_g[l], MIXER_HEADS)), w_out[l].astype(BF16),
            row(mem_q_norm_g[l]), mem_w_q[l].astype(BF16), k_mem, v_mem, mem_w_o[l].astype(BF16),
            row(ffn_norm_g[l]), jnp.stack(_split_bf16(w_r, 2)), b_r, s)
        y_moe = _moe(hx_tiles, *_dispatch(route, hist), w_gate[l].astype(BF16), w_up[l].astype(BF16),
                     w_down[l].astype(BF16), n)
        moe = (y_moe, route)
    return _final(xs, moe, row(final_norm_g)).reshape(b, s, d)
```

```python
import functools
import math

import jax
import jax.numpy as jnp
import numpy as np
from jax import lax
from jax.experimental import pallas as pl
from jax.experimental.pallas import tpu as pltpu

F32 = jnp.float32
BF16 = jnp.bfloat16
HIGHEST = lax.Precision.HIGHEST

D_MODEL = 1024
GRID_W = 64
HEAD_DIM = 64
MIXER_WIDTH = 256
MIXER_HEADS = 4
NA_KH = 8
NA_KW = 16
RET_CHUNK = 128
ROPE_BASE = 10000.0
POOL_WINDOWS = (2, 4, 8, 16)
POOL_GROUP = 64
GDN_CHUNK = 64
CONV_K = 4
MEM_HEADS = 4
MEM_HEAD_DIM = 256
N_GROUPS = 4
EXPERTS_PER_GROUP = 8
N_EXPERTS = 32
TOP_K = 2
D_EXPERT = 512
EPS = 1e-6

P_IN_PAD = 3200
V7X_VMEM_LIMIT = 56 * 1024 * 1024
LANES = 128
SUBLANES = 8
ROW_TILES = D_MODEL // LANES
NEG_BIG = -1e30

PROJ_TILE = 512
POOL_TILE = 512
POST_TILE = 512
NA_ROWS_PER_STEP = 8
NA_ROWS_LOCKSTEP = 2
GDN_CHUNKS_PER_STEP = 2
RET_CHUNKS_PER_STEP = 2
MOE_BLOCK = 256
MOE_PIECES = 8
DMA_UNROLL = 8
HALO = 8


def _cparams(*sem):
    return pltpu.CompilerParams(dimension_semantics=sem, vmem_limit_bytes=V7X_VMEM_LIMIT)


def _dot(a, b):
    return jnp.dot(a, b, preferred_element_type=F32)


def _dot_nt(a, b):
    return lax.dot_general(a, b, (((1,), (1,)), ((), ())), preferred_element_type=F32)


def _dot_tn(a, b, precision=None):
    return lax.dot_general(a, b, (((0,), (0,)), ((), ())), preferred_element_type=F32, precision=precision)


def _silu(x):
    return x * (1.0 / (1.0 + jnp.exp(-x)))


def _rms(x, g):
    return x * lax.rsqrt(jnp.mean(x * x, axis=-1, keepdims=True) + EPS) * g


def _from_row_tiles(ref, first, count, stride):
    return jnp.concatenate([ref[pl.ds(first + k, count, stride=stride), :] for k in range(ROW_TILES)], axis=1)


def _to_row_tiles(ref, x):
    count = x.shape[0]
    for k in range(ROW_TILES):
        ref[pl.ds(k, count, stride=ROW_TILES), :] = x[:, k * LANES:(k + 1) * LANES]


def _add_moe(x, y_ref, rt_ref):
    t = x.shape[0]
    stride = TOP_K * ROW_TILES
    for slot in range(TOP_K):
        x = x + rt_ref[:, TOP_K + slot:TOP_K + slot + 1] * _from_row_tiles(y_ref, slot * ROW_TILES, t, stride)
    return x


def _split_bf16(x, parts):
    out = []
    for _ in range(parts):
        p = x.astype(BF16)
        out.append(p)
        x = x - p.astype(F32)
    return out


def _norm_proj_kernel(has_y, *refs):
    if has_y:
        x_ref, y_ref, rt_ref, g_ref, w_ref, xo_ref, pa_ref, pb_ref, pc_ref, pd_ref, pab_ref = refs
        x = _add_moe(x_ref[...], y_ref, rt_ref)
        xo_ref[...] = x
    else:
        x_ref, g_ref, w_ref, pa_ref, pb_ref, pc_ref, pd_ref, pab_ref = refs
        x = x_ref[...]
    h = _rms(x, g_ref[...]).astype(BF16)
    w = MIXER_WIDTH
    pa_ref[...] = _dot(h, w_ref[:, 0:3 * w]).astype(BF16)
    pb_ref[...] = _dot(h, w_ref[:, 3 * w:7 * w])
    pc_ref[...] = _dot(h, w_ref[:, 7 * w:8 * w])
    pd_ref[...] = _dot(h, w_ref[:, 8 * w:12 * w])
    pab_ref[...] = _dot(h, w_ref[:, 12 * w:P_IN_PAD])


def _moe_out_spec(tile):
    return pl.BlockSpec((tile * TOP_K * ROW_TILES, LANES), lambda i: (i, 0))


def _norm_proj(x, moe, g, w_pad):
    n = x.shape[0]
    tm = PROJ_TILE
    w = MIXER_WIDTH
    row = lambda c: pl.BlockSpec((tm, c), lambda i: (i, 0))
    const = lambda a: pl.BlockSpec(a.shape, lambda i: (0,) * a.ndim)
    outs = [jax.ShapeDtypeStruct((n, 3 * w), BF16), jax.ShapeDtypeStruct((n, 4 * w), F32),
            jax.ShapeDtypeStruct((n, w), F32), jax.ShapeDtypeStruct((n, 4 * w), F32),
            jax.ShapeDtypeStruct((n, LANES), F32)]
    out_specs = [row(3 * w), row(4 * w), row(w), row(4 * w), row(LANES)]
    if moe is None:
        ins, in_specs = (x, g, w_pad), [row(D_MODEL), const(g), const(w_pad)]
    else:
        ins = (x, *moe, g, w_pad)
        in_specs = [row(D_MODEL), _moe_out_spec(tm), row(LANES), const(g), const(w_pad)]
        outs = [jax.ShapeDtypeStruct((n, D_MODEL), F32)] + outs
        out_specs = [row(D_MODEL)] + out_specs
    res = pl.pallas_call(
        functools.partial(_norm_proj_kernel, moe is not None),
        grid=(n // tm,), in_specs=in_specs, out_specs=out_specs, out_shape=outs,
        compiler_params=_cparams("parallel"),
    )(*ins)
    return res if moe is not None else [x] + list(res)


def _na_bias_table(rpb):
    qc = np.arange(GRID_W)
    kc = np.arange(GRID_W)
    d_col = np.clip(kc[None, :] - qc[:, None], -(NA_KW - 1), NA_KW - 1) + (NA_KW - 1)
    pick = np.zeros((2 * NA_KW - 1, GRID_W * GRID_W), np.float32)
    pick[d_col.reshape(-1), np.arange(GRID_W * GRID_W)] = 1.0
    win = np.clip(qc - NA_KW // 2, 0, GRID_W - NA_KW)
    ok = (kc[None, :] >= win[:, None]) & (kc[None, :] < win[:, None] + NA_KW)
    cols = jnp.einsum('hrc,cx->hrx', rpb.astype(F32), jnp.asarray(pick), precision=HIGHEST)
    cols = jnp.where(ok[None, None], cols.reshape(MIXER_HEADS, 2 * NA_KH - 1, GRID_W, GRID_W), NEG_BIG)
    per_e = [jnp.transpose(cols[:, NA_KH - 1 - e:2 * NA_KH - 1 - e], (0, 2, 1, 3)) for e in range(NA_KH)]
    return jnp.stack(per_e, axis=1).reshape(MIXER_HEADS, NA_KH, GRID_W, NA_KH * GRID_W)


def _na_kernel(rows, q_ref, k_ref, v_ref, t_ref, o_ref):
    i = pl.program_id(1)
    dh = HEAD_DIM

    def rows_body(it, carry):
        chains = []
        for k in range(NA_ROWS_LOCKSTEP):
            rr = it * NA_ROWS_LOCKSTEP + k
            r = i * NA_ROWS_PER_STEP + rr
            kr0 = jnp.clip(r - NA_KH // 2, 0, rows - NA_KH)
            q = q_ref[pl.ds(pl.multiple_of(rr * GRID_W, GRID_W), GRID_W), :]
            k0 = pl.multiple_of(kr0 * GRID_W, GRID_W)
            kb = k_ref[pl.ds(k0, NA_KH * GRID_W), :]
            vb = v_ref[pl.ds(k0, NA_KH * GRID_W), :]
            for h in range(MIXER_HEADS):
                sl = slice(h * dh, (h + 1) * dh)
                chains.append((q[:, sl], kb[:, sl], vb[:, sl], h, r - kr0))
        s = [_dot_nt(q, kb) * (dh ** -0.5) + t_ref[h, e] for q, kb, _, h, e in chains]
        m = [jnp.max(a, axis=-1, keepdims=True) for a in s]
        p = [jnp.exp(a - b) for a, b in zip(s, m)]
        l = [jnp.sum(a, axis=-1, keepdims=True) for a in p]
        o = [_dot(a.astype(BF16), c[2]) / b for a, b, c in zip(p, l, chains)]
        for k in range(NA_ROWS_LOCKSTEP):
            rr = it * NA_ROWS_LOCKSTEP + k
            row = jnp.concatenate(o[k * MIXER_HEADS:(k + 1) * MIXER_HEADS], axis=-1)
            o_ref[pl.ds(pl.multiple_of(rr * GRID_W, GRID_W), GRID_W), :] = row.astype(BF16)
        return carry

    lax.fori_loop(0, NA_ROWS_PER_STEP // NA_ROWS_LOCKSTEP, rows_body, 0)


def _neighbourhood_attention(p_a, table, b, s):
    rows = s // GRID_W
    assert rows >= NA_KH and rows % NA_ROWS_PER_STEP == 0
    steps = rows // NA_ROWS_PER_STEP
    tq = NA_ROWS_PER_STEP * GRID_W
    w = MIXER_WIDTH
    return pl.pallas_call(
        functools.partial(_na_kernel, rows),
        grid=(b, steps),
        in_specs=[pl.BlockSpec((tq, w), lambda bi, i: (bi * steps + i, 0)),
                  pl.BlockSpec((s, w), lambda bi, i: (bi, 1)),
                  pl.BlockSpec((s, w), lambda bi, i: (bi, 2)),
                  pl.BlockSpec(table.shape, lambda bi, i: (0, 0, 0, 0))],
        out_specs=pl.BlockSpec((tq, w), lambda bi, i: (bi * steps + i, 0)),
        out_shape=jax.ShapeDtypeStruct((b * s, w), BF16),
        compiler_params=_cparams("parallel", "arbitrary"),
    )(p_a, p_a, p_a, table)


def _retention_tables(s):
    h, dh, c = MIXER_HEADS, HEAD_DIM, RET_CHUNK
    half = dh // 2
    inv = ROPE_BASE ** (-jnp.arange(half, dtype=F32) / half)
    ang = jnp.arange(s, dtype=F32)[:, None] * inv[None, :]
    cos, sin = jnp.cos(ang), jnp.sin(ang)
    zero = jnp.zeros_like(sin)
    cos_t = jnp.tile(jnp.concatenate([cos, cos], axis=-1), (1, h))
    sin_lo = jnp.tile(jnp.concatenate([-sin, zero], axis=-1), (1, h))
    sin_hi = jnp.tile(jnp.concatenate([zero, sin], axis=-1), (1, h))
    log_f = np.log1p(-np.exp2(-5.0 - np.arange(h, dtype=np.float64)))
    log_b = log_f[::-1]
    pos = np.arange(c, dtype=np.float64)
    diff = pos[:, None] - pos[None, :]
    dmat = np.where(diff >= 0, np.exp(log_f[:, None, None] * np.maximum(diff, 0.0)), 0.0) \
        + np.where(diff < 0, np.exp(log_b[:, None, None] * np.maximum(-diff, 0.0)), 0.0)
    lanes = lambda t: np.repeat(t.T, dh, axis=1)
    dec = np.stack([lanes(np.exp(log_f[:, None] * (pos + 1.0))),
                    lanes(np.exp(log_f[:, None] * (c - 1.0 - pos))),
                    lanes(np.exp(log_b[:, None] * (c - pos))),
                    lanes(np.exp(log_b[:, None] * pos))])
    chunk_f = [float(np.exp(v * c)) for v in log_f]
    chunk_b = [float(np.exp(v * c)) for v in log_b]
    return cos_t, sin_lo, sin_hi, jnp.asarray(dmat, F32), jnp.asarray(dec, F32), chunk_f, chunk_b


def _rotary(x, cos, sin_lo, sin_hi):
    w = x.shape[-1]
    return x * cos + pltpu.roll(x, w - HEAD_DIM // 2, 1) * sin_lo + pltpu.roll(x, HEAD_DIM // 2, 1) * sin_hi


def _retention_kernel(nb, chunk_f, chunk_b, q_ref, k_ref, v_ref, gate_ref, cos_ref, slo_ref, shi_ref,
                      dmat_ref, dec_ref, ng_ref, o_ref, sf_ref, sb_ref, sball_ref):
    t = pl.program_id(1)
    dh, c, cps = HEAD_DIM, RET_CHUNK, RET_CHUNKS_PER_STEP
    cos, slo, shi = cos_ref[...], slo_ref[...], shi_ref[...]
    kr = _rotary(k_ref[...], cos, slo, shi) * (dh ** -0.5)
    vb = v_ref[...].astype(BF16)
    items = [(ci, h) for ci in range(cps) for h in range(MIXER_HEADS)]
    part = lambda x, ci, h: x[ci * c:(ci + 1) * c, h * dh:(h + 1) * dh]
    split = lambda x: {it: part(x, *it) for it in items}

    @pl.when(t == 0)
    def _():
        sf_ref[...] = jnp.zeros_like(sf_ref)
        sb_ref[...] = jnp.zeros_like(sb_ref)

    @pl.when(t < nb)
    def _():
        blk = nb - 1 - t
        kd, v = split((kr * dec_ref[3]).astype(BF16)), split(vb)
        kv = {it: _dot_tn(kd[it], v[it]) for it in items}
        for h in range(MIXER_HEADS):
            state = sb_ref[h]
            for ci in range(cps - 1, -1, -1):
                sball_ref[blk * cps + ci, h] = state
                state = state * chunk_b[h] + kv[ci, h]
            sb_ref[h] = state

    @pl.when(t >= nb)
    def _():
        blk = t - nb
        qr = _rotary(q_ref[...], cos, slo, shi)
        q, k, v = split(qr.astype(BF16)), split(kr.astype(BF16)), split(vb)
        qf, qbk = split((qr * dec_ref[0]).astype(BF16)), split((qr * dec_ref[2]).astype(BF16))
        kd = split((kr * dec_ref[1]).astype(BF16))
        sc = {it: (_dot_nt(q[it], k[it]) * dmat_ref[it[1]]).astype(BF16) for it in items}
        o = {it: _dot(sc[it], v[it]) for it in items}
        kv = {it: _dot_tn(kd[it], v[it]) for it in items}
        ob = {it: _dot(qbk[it], sball_ref[blk * cps + it[0], it[1]].astype(BF16)) for it in items}
        of = {}
        for h in range(MIXER_HEADS):
            state = sf_ref[h]
            for ci in range(cps):
                of[ci, h] = _dot(qf[ci, h], state.astype(BF16))
                state = state * chunk_f[h] + kv[ci, h]
            sf_ref[h] = state
        o = {it: o[it] + of[it] + ob[it] for it in items}
        mu = {it: jnp.mean(o[it], axis=-1, keepdims=True) for it in items}
        oc = {it: o[it] - mu[it] for it in items}
        var = {it: jnp.mean(oc[it] * oc[it], axis=-1, keepdims=True) for it in items}
        on = {it: oc[it] * lax.rsqrt(var[it] + EPS) for it in items}
        y = jnp.concatenate([jnp.concatenate([on[ci, h] for h in range(MIXER_HEADS)], axis=-1)
                             for ci in range(cps)], axis=0)
        o_ref[...] = (y * ng_ref[...] * _silu(gate_ref[...])).astype(BF16)


def _retention(p_b, norm_g, tables, b, s):
    cos_t, sin_lo, sin_hi, dmat, dec, chunk_f, chunk_b = tables
    cps = RET_CHUNKS_PER_STEP
    tb = RET_CHUNK * cps
    nb = s // tb
    w = MIXER_WIDTH
    dec = jnp.tile(dec, (1, cps, 1))
    block = lambda t: jnp.where(t < nb, nb - 1 - t, t - nb)
    col = lambda j: pl.BlockSpec((tb, w), lambda bi, t: (bi * nb + block(t), j))
    tab = pl.BlockSpec((tb, w), lambda bi, t: (block(t), 0))
    const = lambda a: pl.BlockSpec(a.shape, lambda bi, t: (0,) * a.ndim)
    return pl.pallas_call(
        functools.partial(_retention_kernel, nb, chunk_f, chunk_b),
        grid=(b, 2 * nb),
        in_specs=[col(0), col(1), col(2), col(3), tab, tab, tab, const(dmat), const(dec), const(norm_g)],
        out_specs=pl.BlockSpec((tb, w), lambda bi, t: (bi * nb + jnp.maximum(t - nb, 0), 0)),
        out_shape=jax.ShapeDtypeStruct((b * s, w), BF16),
        scratch_shapes=[pltpu.VMEM((MIXER_HEADS, HEAD_DIM, HEAD_DIM), F32),
                        pltpu.VMEM((MIXER_HEADS, HEAD_DIM, HEAD_DIM), F32),
                        pltpu.VMEM((s // RET_CHUNK, MIXER_HEADS, HEAD_DIM, HEAD_DIM), F32)],
        compiler_params=_cparams("arbitrary", "arbitrary"),
    )(p_b, p_b, p_b, p_b, cos_t, sin_lo, sin_hi, dmat, dec, norm_g)


def _halo_specs(tile, width, n_rows, block_of):
    per = tile // HALO
    last = n_rows // HALO - 1
    main = pl.BlockSpec((tile, width), lambda *g: (block_of(*g), 0))
    prev = pl.BlockSpec((HALO, width), lambda *g: (jnp.maximum(block_of(*g) * per - 1, 0), 0))
    nxt = pl.BlockSpec((HALO, width), lambda *g: (jnp.minimum((block_of(*g) + 1) * per, last), 0))
    return main, prev, nxt


def _fill_padded(pad_ref, main, prev, nxt, first, last):
    t = main.shape[0]
    pad_ref[0:HALO, :] = jnp.where(first, 0.0, prev)
    pad_ref[HALO:HALO + t, :] = main
    pad_ref[HALO + t:2 * HALO + t, :] = jnp.where(last, 0.0, nxt)


def _pool_kernel(s, tiles_per_seq, u_ref, prev_ref, next_ref, w_ref, scale_ref, o_ref, pad_ref):
    t = POOL_TILE
    tseq = pl.program_id(0) % tiles_per_seq
    u = u_ref[...]
    _fill_padded(pad_ref, u, prev_ref[...], next_ref[...], tseq == 0, tseq == tiles_per_seq - 1)
    lane = lax.broadcasted_iota(jnp.int32, (1, MIXER_WIDTH), 1)
    half = jnp.full((1, MIXER_WIDTH), POOL_WINDOWS[0] // 2, jnp.int32)
    for gi in range(1, len(POOL_WINDOWS)):
        half = jnp.where(lane >= gi * POOL_GROUP, POOL_WINDOWS[gi] // 2, half)
    max_half = POOL_WINDOWS[-1] // 2
    acc = jnp.zeros((t, MIXER_WIDTH), F32)
    for d in range(-max_half, max_half):
        inside = (d >= -half) & (d < half)
        acc = acc + jnp.where(inside, pad_ref[HALO + d:HALO + d + t, :], 0.0)
    pos = tseq * t + lax.broadcasted_iota(jnp.int32, (t, 1), 0)
    count = jnp.minimum(pos + half, s) - jnp.maximum(pos - half, 0)
    diff = acc / count.astype(F32) - u
    o_ref[...] = (_dot(diff.astype(BF16), w_ref[...]) * scale_ref[...]).astype(BF16)


def _pool(p_c, w_blockdiag, scale, b, s):
    t = POOL_TILE
    tiles_per_seq = s // t
    main, prev, nxt = _halo_specs(t, MIXER_WIDTH, b * s, lambda i: i)
    const = lambda a: pl.BlockSpec(a.shape, lambda i: (0,) * a.ndim)
    return pl.pallas_call(
        functools.partial(_pool_kernel, s, tiles_per_seq),
        grid=(b * tiles_per_seq,),
        in_specs=[main, prev, nxt, const(w_blockdiag), const(scale)],
        out_specs=pl.BlockSpec((t, MIXER_WIDTH), lambda i: (i, 0)),
        out_shape=jax.ShapeDtypeStruct((b * s, MIXER_WIDTH), BF16),
        scratch_shapes=[pltpu.VMEM((t + 2 * HALO, MIXER_WIDTH), F32)],
        compiler_params=_cparams("parallel"),
    )(p_c, p_c, p_c, w_blockdiag, scale)


def _head_blockdiag(x):
    lane = lax.broadcasted_iota(jnp.int32, x.shape, 1)
    zero = jnp.zeros_like(x)
    return jnp.concatenate([jnp.where(lane < HEAD_DIM, x, zero), jnp.where(lane >= HEAD_DIM, x, zero)], axis=0)


def _gdn_prepare(forward, u, ab, alog, dtb, tri_col, tri_row, ones_bd):
    c, dh, w = GDN_CHUNK, HEAD_DIM, MIXER_WIDTH
    t = u.shape[0]
    a_off, b_off = (0, 2 * MIXER_HEADS) if forward else (MIXER_HEADS, 3 * MIXER_HEADS)
    x = ab + dtb
    g_all = -jnp.exp(alog) * (jnp.maximum(x, 0.0) + jnp.log(1.0 + jnp.exp(-jnp.abs(x))))
    beta_all = 1.0 / (1.0 + jnp.exp(-ab))
    g_parts = _split_bf16(g_all, 3)
    gc_col = sum(_dot(tri_col, p) for p in g_parts)
    gc_row = sum(_dot_tn(p, tri_row) for p in g_parts)
    ii = lax.broadcasted_iota(jnp.int32, (c, 2 * dh), 0)
    lane = lax.broadcasted_iota(jnp.int32, (c, 2 * dh), 1)
    first = lane < dh
    jj = jnp.where(first, lane, lane - dh)
    incl = (ii >= jj) if forward else (ii <= jj)
    strict = (ii > jj) if forward else (ii < jj)
    last = c - 1 if forward else 0
    items = [(ci, p) for ci in range(t // c) for p in range(MIXER_HEADS // 2)]
    rows = lambda ci: slice(ci * c, (ci + 1) * c)
    part = lambda base: [u[rows(ci), base + 2 * p * dh:base + 2 * (p + 1) * dh] for ci, p in items]
    col_pair = lambda x, off: [jnp.where(first, x[rows(ci), off + 2 * p:off + 2 * p + 1],
                                         x[rows(ci), off + 2 * p + 1:off + 2 * p + 2]) for ci, p in items]
    head_sum = lambda a: sum(_dot(piece, ones_bd) for piece in _split_bf16(a, 2))
    q, k, v = part(0), part(w), part(2 * w)
    q = [a * lax.rsqrt(head_sum(a * a) + EPS) * (dh ** -0.5) for a in q]
    k = [a * lax.rsqrt(head_sum(a * a) + EPS) for a in k]
    gcc = col_pair(gc_col, a_off)
    beta = col_pair(beta_all, b_off)
    gcr = [jnp.concatenate([gc_row[a_off + 2 * p:a_off + 2 * p + 1, rows(ci)],
                            gc_row[a_off + 2 * p + 1:a_off + 2 * p + 2, rows(ci)]], axis=1) for ci, p in items]
    g_last = [a[last:last + 1, :] for a in gcc]
    decay = [jnp.where(incl, jnp.exp(jnp.where(incl, a - b, 0.0)), 0.0) for a, b in zip(gcc, gcr)]
    kq = [_dot_nt(jnp.concatenate([kk, qq], axis=0).astype(BF16), _head_blockdiag(kk.astype(BF16)))
          for kk, qq in zip(k, q)]
    lmat = [jnp.where(strict, m[:c] * b * d, 0.0) for m, b, d in zip(kq, beta, decay)]
    attn = [(m[c:] * d).astype(BF16) for m, d in zip(kq, decay)]
    xinv = [-a for a in lmat]
    lp = [a.astype(BF16) for a in lmat]
    lp_bd = [_head_blockdiag(a) for a in lp]
    for _ in range(int(math.log2(c)) - 1):
        lp32 = [_dot(a, bd) for a, bd in zip(lp, lp_bd)]
        lp = [a.astype(BF16) for a in lp32]
        lp_bd = [_head_blockdiag(a) for a in lp]
        xinv = [xi + a32 + _dot(xi.astype(BF16), bd) for xi, a32, bd in zip(xinv, lp32, lp_bd)]
    egc = [jnp.exp(a) for a in gcc]
    xb = [a.astype(BF16) for a in xinv]
    vbeta = [vv * b for vv, b in zip(v, beta)]
    kbeta = [kk * (b * e) for kk, b, e in zip(k, beta, egc)]
    u_val = [r + _dot(xi, _head_blockdiag(r.astype(BF16))) for xi, r in zip(xb, vbeta)]
    w_dec = [r + _dot(xi, _head_blockdiag(r.astype(BF16))) for xi, r in zip(xb, kbeta)]
    wq = [jnp.concatenate([wd, qq * e], axis=0).astype(BF16) for wd, qq, e in zip(w_dec, q, egc)]
    k_state = [(kk * jnp.exp(gl - a)).astype(BF16) for kk, gl, a in zip(k, g_last, gcc)]
    e_last = [jnp.exp(gl) for gl in g_last]
    keyed = lambda vals: dict(zip(items, vals))
    return dict(wq=keyed(wq), u_val=keyed(u_val), attn=keyed(attn), k_state=keyed(k_state), e_last=keyed(e_last))


def _gdn_kernel(nb, uf_ref, pf_ref, nf_ref, abf_ref, ub_ref, pb_ref, nb_ref, abb_ref, cw_ref, alog_ref,
                dtb_ref, tril_ref, triu_ref, ones_ref, of_ref, ob_ref, sf_ref, sb_ref, pad_ref):
    n = pl.program_id(1)
    t = uf_ref.shape[0]
    c, dh = GDN_CHUNK, HEAD_DIM
    nch = t // c
    pairs = MIXER_HEADS // 2

    @pl.when(n == 0)
    def _():
        sf_ref[...] = jnp.zeros_like(sf_ref)
        sb_ref[...] = jnp.zeros_like(sb_ref)

    def conv_silu(main_ref, prev_ref, next_ref, blk):
        _fill_padded(pad_ref, main_ref[...], prev_ref[...], next_ref[...], blk == 0, blk == nb - 1)
        acc = jnp.zeros((t, 3 * MIXER_WIDTH), F32)
        for k in range(CONV_K):
            off = HALO + k - CONV_K // 2
            acc = acc + pad_ref[off:off + t, :] * cw_ref[k:k + 1, :]
        return _silu(acc)

    alog, dtb = alog_ref[...], dtb_ref[...]
    tril, triu = tril_ref[...], triu_ref[...]
    u_f = conv_silu(uf_ref, pf_ref, nf_ref, n)
    u_b = conv_silu(ub_ref, pb_ref, nb_ref, nb - 1 - n)
    ones_bd = ones_ref[...]
    prep_f = _gdn_prepare(True, u_f, abf_ref[...], alog, dtb, tril, triu, ones_bd)
    prep_b = _gdn_prepare(False, u_b, abb_ref[...], alog, dtb, triu, tril, ones_bd)

    first = lax.broadcasted_iota(jnp.int32, (dh, 2 * dh), 1) < dh
    chains = [(prep_f, sf_ref, of_ref, p, True) for p in range(pairs)] + \
             [(prep_b, sb_ref, ob_ref, p, False) for p in range(pairs)]
    state = [s_ref[p] for _, s_ref, _, p, _ in chains]
    for step in range(nch):
        key = [(step if fw else nch - 1 - step, p) for _, _, _, p, fw in chains]
        ws = [_dot(pr['wq'][kk], _head_blockdiag(st.astype(BF16))) for (pr, *_), kk, st in zip(chains, key, state)]
        v_new = [pr['u_val'][kk] - a[:c] for (pr, *_), kk, a in zip(chains, key, ws)]
        vb = [a.astype(BF16) for a in v_new]
        out = [a[c:] + _dot(pr['attn'][kk], _head_blockdiag(b)) for (pr, *_), kk, a, b in zip(chains, key, ws, vb)]
        cross = [_dot_tn(pr['k_state'][kk], b) for (pr, *_), kk, b in zip(chains, key, vb)]
        state = [st * pr['e_last'][kk] + jnp.where(first, x[:dh], x[dh:])
                 for (pr, *_), kk, st, x in zip(chains, key, state, cross)]
        for (_, _, o_ref, p, _), (ci, _), o in zip(chains, key, out):
            o_ref[ci * c:(ci + 1) * c, 2 * p * dh:2 * (p + 1) * dh] = o
    for (_, s_ref, _, p, _), st in zip(chains, state):
        s_ref[p] = st


def _gated_deltanet(p_d, p_ab, conv_w, alog_vec, dtb_vec, b, s):
    c = GDN_CHUNK
    t = c * GDN_CHUNKS_PER_STEP
    nb = s // t
    w = MIXER_WIDTH
    r = np.arange(t)
    same = (r[:, None] // c) == (r[None, :] // c)
    tril = jnp.asarray(same & (r[:, None] >= r[None, :]), BF16)
    triu = jnp.asarray(same & (r[:, None] <= r[None, :]), BF16)
    ones_bd = _block_diag(jnp.ones((2, HEAD_DIM, HEAD_DIM), BF16))
    fwd = lambda bi, n: bi * nb + n
    bwd = lambda bi, n: bi * nb + nb - 1 - n
    mf, pf, nf = _halo_specs(t, 3 * w, b * s, fwd)
    mb, pb_, nb_ = _halo_specs(t, 3 * w, b * s, bwd)
    abf = pl.BlockSpec((t, LANES), lambda bi, n: (fwd(bi, n), 0))
    abb = pl.BlockSpec((t, LANES), lambda bi, n: (bwd(bi, n), 0))
    const = lambda a: pl.BlockSpec(a.shape, lambda bi, n: (0,) * a.ndim)
    return pl.pallas_call(
        functools.partial(_gdn_kernel, nb),
        grid=(b, nb),
        in_specs=[mf, pf, nf, abf, mb, pb_, nb_, abb, const(conv_w), const(alog_vec), const(dtb_vec),
                  const(tril), const(triu), const(ones_bd)],
        out_specs=[pl.BlockSpec((t, w), lambda bi, n: (fwd(bi, n), 0)),
                   pl.BlockSpec((t, w), lambda bi, n: (bwd(bi, n), 0))],
        out_shape=[jax.ShapeDtypeStruct((b * s, w), F32), jax.ShapeDtypeStruct((b * s, w), F32)],
        scratch_shapes=[pltpu.VMEM((MIXER_HEADS // 2, HEAD_DIM, 2 * HEAD_DIM), F32),
                        pltpu.VMEM((MIXER_HEADS // 2, HEAD_DIM, 2 * HEAD_DIM), F32),
                        pltpu.VMEM((t + 2 * HALO, 3 * w), F32)],
        compiler_params=_cparams("arbitrary", "arbitrary"),
    )(p_d, p_d, p_d, p_ab, p_d, p_d, p_d, p_ab, conv_w, alog_vec, dtb_vec, tril, triu, ones_bd)


def _mem_kv_kernel(m_ref, g_ref, wk_ref, wv_ref, k_ref, v_ref):
    h = _rms(m_ref[0], g_ref[...]).astype(BF16)
    k_ref[0] = _dot(h, wk_ref[...]).astype(BF16)
    v_ref[0] = _dot(h, wv_ref[...]).astype(BF16)


def _mem_kv(mem, g, wk, wv):
    b, nm, d = mem.shape
    const = lambda a: pl.BlockSpec(a.shape, lambda i: (0,) * a.ndim)
    blk = pl.BlockSpec((1, nm, d), lambda i: (i, 0, 0))
    return pl.pallas_call(
        _mem_kv_kernel, grid=(b,),
        in_specs=[blk, const(g), const(wk), const(wv)], out_specs=[blk, blk],
        out_shape=[jax.ShapeDtypeStruct((b, nm, d), BF16)] * 2,
        compiler_params=_cparams("parallel"),
    )(mem, g, wk, wv)


def _route_lanes(lg):
    lane_i = lax.broadcasted_iota(jnp.int32, lg.shape, 1)
    lane = lane_i.astype(F32)
    first_at = lambda vals, m: jnp.min(jnp.where(vals == m, lane, float(LANES)), axis=-1, keepdims=True)
    g_mask = lane_i < N_GROUPS
    gl = jnp.where(g_mask, lg, NEG_BIG)
    g_max = jnp.max(gl, axis=-1, keepdims=True)
    g_sum = jnp.sum(jnp.where(g_mask, jnp.exp(gl - g_max), 0.0), axis=-1, keepdims=True)
    g_p = 1.0 / g_sum
    g_idx = first_at(gl, g_max)
    lo = N_GROUPS + EXPERTS_PER_GROUP * g_idx
    e_mask = (lane >= lo) & (lane < lo + EXPERTS_PER_GROUP)
    el = jnp.where(e_mask, lg, NEG_BIG)
    m1 = jnp.max(el, axis=-1, keepdims=True)
    i1 = first_at(el, m1)
    el2 = jnp.where(lane == i1, NEG_BIG, el)
    m2 = jnp.max(el2, axis=-1, keepdims=True)
    i2 = first_at(el2, m2)
    r = jnp.exp(m2 - m1)
    w1 = g_p / (1.0 + r)
    w2 = g_p * r / (1.0 + r)
    out = jnp.where(lane_i == 0, i1 - N_GROUPS, 0.0)
    out = jnp.where(lane_i == 1, i2 - N_GROUPS, out)
    out = jnp.where(lane_i == 2, w1, out)
    out = jnp.where(lane_i == 3, w2, out)
    chosen = jnp.where((lane == i1 - N_GROUPS) | (lane == i2 - N_GROUPS), 1.0, 0.0)
    return out, chosen


def _post_kernel(x_ref, ya_ref, yb_ref, yc_ref, of_ref, ob_ref, gate_ref, seg_ref, gg_ref, wout_ref, gq_ref,
                 wq_ref, k_ref, v_ref, wo_ref, gf_ref, wr_ref, br_ref, xo_ref, hx_ref, rt_ref, hist_ref):
    w = MIXER_WIDTH
    o = of_ref[...] + ob_ref[...]
    ms = sum(_dot(p, seg_ref[...]) for p in _split_bf16(o * o, 2))
    yd = o * lax.rsqrt(ms + EPS) * gg_ref[...] * _silu(gate_ref[...])
    mix = _dot(ya_ref[...], wout_ref[0:w, :]) + _dot(yb_ref[...], wout_ref[w:2 * w, :])
    mix = mix + _dot(yc_ref[...], wout_ref[2 * w:3 * w, :]) + _dot(yd.astype(BF16), wout_ref[3 * w:4 * w, :])
    x1 = x_ref[...] + mix
    q = _dot(_rms(x1, gq_ref[...]).astype(BF16), wq_ref[...]).astype(BF16)
    heads = []
    for h in range(MEM_HEADS):
        sl = slice(h * MEM_HEAD_DIM, (h + 1) * MEM_HEAD_DIM)
        sc = _dot_nt(q[:, sl], k_ref[0, :, sl]) * (MEM_HEAD_DIM ** -0.5)
        p = jnp.exp(sc - jnp.max(sc, axis=-1, keepdims=True))
        l = jnp.sum(p, axis=-1, keepdims=True)
        heads.append((_dot(p.astype(BF16), v_ref[0, :, sl]) / l).astype(BF16))
    x2 = x1 + _dot(jnp.concatenate(heads, axis=-1), wo_ref[...])
    xo_ref[...] = x2
    hx = _rms(x2, gf_ref[...])
    _to_row_tiles(hx_ref, hx)
    h_hi, h_lo = _split_bf16(hx, 2)
    lg = _dot(h_hi, wr_ref[0]) + _dot(h_hi, wr_ref[1]) + _dot(h_lo, wr_ref[0]) + br_ref[...]
    route, chosen = _route_lanes(lg)
    rt_ref[...] = route
    hist_ref[...] = jnp.broadcast_to(jnp.sum(chosen, axis=0, keepdims=True), hist_ref.shape)


def _post(x, y_a, y_b, y_c, o_f, o_b, p_d, seg, gdn_g, w_out, g_q, w_q, k_mem, v_mem, w_o, g_f, w_r, b_r, s):
    n = x.shape[0]
    tm = POST_TILE
    w = MIXER_WIDTH
    per_seq = s // tm
    row = lambda c: pl.BlockSpec((tm, c), lambda i: (i, 0))
    const = lambda a: pl.BlockSpec(a.shape, lambda i: (0,) * a.ndim)
    kv = pl.BlockSpec((1,) + k_mem.shape[1:], lambda i: (i // per_seq, 0, 0))
    return pl.pallas_call(
        _post_kernel, grid=(n // tm,),
        in_specs=[row(D_MODEL), row(w), row(w), row(w), row(w), row(w),
                  pl.BlockSpec((tm, w), lambda i: (i, 3)), const(seg), const(gdn_g), const(w_out), const(g_q),
                  const(w_q), kv, kv, const(w_o), const(g_f), const(w_r), const(b_r)],
        out_specs=[row(D_MODEL), pl.BlockSpec((tm * ROW_TILES, LANES), lambda i: (i, 0)), row(LANES),
                   pl.BlockSpec((SUBLANES, LANES), lambda i: (i, 0))],
        out_shape=[jax.ShapeDtypeStruct((n, D_MODEL), F32), jax.ShapeDtypeStruct((n * ROW_TILES, LANES), F32),
                   jax.ShapeDtypeStruct((n, LANES), F32), jax.ShapeDtypeStruct((n // tm * SUBLANES, LANES), F32)],
        compiler_params=_cparams("parallel"),
    )(x, y_a, y_b, y_c, o_f, o_b, p_d, seg, gdn_g, w_out, g_q, w_q, k_mem, v_mem, w_o, g_f, w_r, b_r)


def _dispatch(route, hist):
    n = route.shape[0]
    a = n * TOP_K
    blk = MOE_BLOCK
    flat_e = route[:, 0:TOP_K].astype(jnp.int32).reshape(a)
    _, order = lax.sort_key_val(flat_e, jnp.arange(a, dtype=jnp.int32))
    counts = jnp.sum(hist.reshape(-1, SUBLANES, LANES)[:, 0, :N_EXPERTS], axis=0).astype(jnp.int32)
    padded = (counts + blk - 1) // blk * blk
    pad_end = jnp.cumsum(padded)
    pad_start = pad_end - padded
    start = jnp.cumsum(counts) - counts
    n_blocks = a // blk + N_EXPERTS
    blk_start = jnp.arange(n_blocks, dtype=jnp.int32) * blk
    blk_expert = jnp.minimum(jnp.sum(pad_end[None, :] <= blk_start[:, None], axis=1), N_EXPERTS - 1).astype(jnp.int32)
    into_expert = blk_start - pad_start[blk_expert]
    blk_valid = jnp.clip(counts[blk_expert] - into_expert, 0, blk)
    in_blk = jnp.arange(blk, dtype=jnp.int32)[None, :]
    idx = (start[blk_expert] + into_expert)[:, None] + in_blk
    row_dst = jnp.where(in_blk < blk_valid[:, None], order[jnp.clip(idx, 0, a - 1)], a + in_blk)
    return blk_expert, jnp.concatenate([a + in_blk, row_dst], axis=0).reshape(-1)


def _moe_kernel(n_tokens, n_blocks, be_ref, dst_ref, hx_hbm, wg_ref, wu_ref, wd_ref, y_hbm, xbuf, ybuf, gsem, ssem):
    j = pl.program_id(0)
    blk, rt = MOE_BLOCK, ROW_TILES
    slot = j % 2
    other = 1 - slot
    tile = lambda ref, i: ref.at[pl.ds(pl.multiple_of(i * rt, rt), rt), :]
    dst = lambda jj, i: dst_ref[(jj + 1) * blk + i]
    nxt = jnp.minimum(j + 1, n_blocks - 1)

    def gather_row(jj, sl, i):
        tok = jnp.minimum(dst(jj, i) // TOP_K, n_tokens - 1)
        pltpu.make_async_copy(tile(hx_hbm, tok), tile(xbuf.at[sl], i), gsem.at[sl]).start()

    def scatter_row(jj, sl, i):
        pltpu.make_async_copy(tile(ybuf.at[sl], i), tile(y_hbm, dst(jj, i)), ssem.at[sl]).start()

    def wait_gather(sl):
        pltpu.make_async_copy(hx_hbm.at[pl.ds(0, blk * rt), :], xbuf.at[sl], gsem.at[sl]).wait()

    def wait_scatter(sl):
        pltpu.make_async_copy(ybuf.at[sl], y_hbm.at[pl.ds(0, blk * rt), :], ssem.at[sl]).wait()

    def rolled(fn):
        lax.fori_loop(0, blk, lambda i, carry: (fn(i), carry)[1], 0, unroll=DMA_UNROLL)

    @pl.when(j == 0)
    def _():
        ybuf[1] = jnp.zeros((blk * rt, LANES), F32)
        rolled(lambda i: gather_row(0, 0, i))

    @pl.when(j < n_blocks)
    def _():
        wait_gather(slot)

        @pl.when(j >= 1)
        def _():
            wait_scatter(slot)

        per = blk // MOE_PIECES

        def issue_copies(piece):
            for i in range(piece * per, (piece + 1) * per):
                gather_row(nxt, other, i)
                scatter_row(j - 1, other, i)

        xs = _from_row_tiles(xbuf.at[slot], 0, blk, rt).astype(BF16)
        half = D_EXPERT // 2
        acts = []
        for c in range(2):
            cols = slice(c * half, (c + 1) * half)
            g = _dot(xs, wg_ref[0, :, cols].astype(BF16))
            issue_copies(2 * c)
            u = _dot(xs, wu_ref[0, :, cols].astype(BF16))
            issue_copies(2 * c + 1)
            acts.append((_silu(g) * u).astype(BF16))
        h = jnp.concatenate(acts, axis=1)
        quarter = D_MODEL // 4
        for c in range(4):
            y = _dot(h, wd_ref[0, :, c * quarter:(c + 1) * quarter].astype(BF16))
            for k in range(quarter // LANES):
                ybuf[slot, pl.ds(c * (quarter // LANES) + k, blk, stride=rt), :] = y[:, k * LANES:(k + 1) * LANES]
            issue_copies(4 + c)

    @pl.when(j == n_blocks)
    def _():
        wait_gather(slot)
        wait_scatter(slot)
        rolled(lambda i: scatter_row(j - 1, other, i))
        wait_scatter(other)


def _moe(hx_tiles, blk_expert, row_dst, w_gate, w_up, w_down, n):
    blk = MOE_BLOCK
    n_blocks = blk_expert.shape[0]
    expert = lambda j, be, dst: (be[jnp.minimum(j, n_blocks - 1)], 0, 0)
    grid_spec = pltpu.PrefetchScalarGridSpec(
        num_scalar_prefetch=2, grid=(n_blocks + 1,),
        in_specs=[pl.BlockSpec(memory_space=pl.ANY),
                  pl.BlockSpec((1, D_MODEL, D_EXPERT), expert),
                  pl.BlockSpec((1, D_MODEL, D_EXPERT), expert),
                  pl.BlockSpec((1, D_EXPERT, D_MODEL), expert)],
        out_specs=pl.BlockSpec(memory_space=pl.ANY),
        scratch_shapes=[pltpu.VMEM((2, blk * ROW_TILES, LANES), F32), pltpu.VMEM((2, blk * ROW_TILES, LANES), F32),
                        pltpu.SemaphoreType.DMA((2,)), pltpu.SemaphoreType.DMA((2,))])
    return pl.pallas_call(
        functools.partial(_moe_kernel, n, n_blocks), grid_spec=grid_spec,
        out_shape=jax.ShapeDtypeStruct(((n * TOP_K + blk) * ROW_TILES, LANES), F32),
        compiler_params=_cparams("arbitrary"),
    )(blk_expert, row_dst, hx_tiles, w_gate, w_up, w_down)


def _final_kernel(x_ref, y_ref, rt_ref, g_ref, o_ref):
    o_ref[...] = _rms(_add_moe(x_ref[...], y_ref, rt_ref), g_ref[...])


def _final(x, moe, g):
    n = x.shape[0]
    tm = PROJ_TILE
    return pl.pallas_call(
        _final_kernel, grid=(n // tm,),
        in_specs=[pl.BlockSpec((tm, D_MODEL), lambda i: (i, 0)), _moe_out_spec(tm),
                  pl.BlockSpec((tm, LANES), lambda i: (i, 0)), pl.BlockSpec(g.shape, lambda i: (0, 0))],
        out_specs=pl.BlockSpec((tm, D_MODEL), lambda i: (i, 0)),
        out_shape=jax.ShapeDtypeStruct((n, D_MODEL), F32),
        compiler_params=_cparams("parallel"),
    )(x, *moe, g)


def _lane_vec(v, width=LANES):
    v = v.reshape(1, -1).astype(F32)
    return jnp.pad(v, ((0, 0), (0, width - v.shape[1])))


def _block_diag(blocks):
    g, c, _ = blocks.shape
    eye = jnp.eye(g, dtype=blocks.dtype)
    return (eye[:, None, :, None] * blocks[:, :, None, :]).reshape(g * c, g * c)


def kernel(x, mem, mix_norm_g, w_in, na_rpb, ret_norm_g, pool_w, pool_scale, gdn_conv_w, gdn_a_log, gdn_dt_bias, gdn_norm_g, w_out, mem_q_norm_g, mem_kv_norm_g, mem_w_q, mem_w_k, mem_w_v, mem_w_o, ffn_norm_g, w_group, b_group, w_router, b_router, w_gate, w_up, w_down, final_norm_g):
    b, s, d = x.shape
    n = b * s
    depth = w_in.shape[0]
    row = lambda v: v.reshape(1, -1).astype(F32)
    ret_tables = _retention_tables(s)
    seg = _block_diag(jnp.full((MIXER_HEADS, HEAD_DIM, HEAD_DIM), 1.0 / HEAD_DIM, BF16))
    xs = x.reshape(n, d)
    moe = None
    for l in range(depth):
        w_pad = jnp.pad(w_in[l], ((0, 0), (0, P_IN_PAD - w_in.shape[2]))).astype(BF16)
        xs, p_a, p_b, p_c, p_d, p_ab = _norm_proj(xs, moe, row(mix_norm_g[l]), w_pad)
        y_a = _neighbourhood_attention(p_a, _na_bias_table(na_rpb[l]), b, s)
        y_b = _retention(p_b, row(ret_norm_g[l]), ret_tables, b, s)
        y_c = _pool(p_c, _block_diag(pool_w[l]).astype(BF16), row(pool_scale[l]), b, s)
        o_f, o_b = _gated_deltanet(p_d, p_ab, gdn_conv_w[l].astype(F32), _lane_vec(gdn_a_log[l]),
                                   _lane_vec(gdn_dt_bias[l]), b, s)
        k_mem, v_mem = _mem_kv(mem, row(mem_kv_norm_g[l]), mem_w_k[l].astype(BF16), mem_w_v[l].astype(BF16))
        w_r = jnp.pad(jnp.concatenate([w_group[l], w_router[l]], axis=1),
                      ((0, 0), (0, LANES - N_GROUPS - N_EXPERTS))).astype(F32)
        b_r = _lane_vec(jnp.concatenate([b_group[l], b_router[l]]))
        xs, hx_tiles, route, hist = _post(
            xs, y_a, y_b, y_c, o_f, o_b, p_d, seg, row(jnp.tile(gdn_norm_g[l], MIXER_HEADS)), w_out[l].astype(BF16),
            row(mem_q_norm_g[l]), mem_w_q[l].astype(BF16), k_mem, v_mem, mem_w_o[l].astype(BF16),
            row(ffn_norm_g[l]), jnp.stack(_split_bf16(w_r, 2)), b_r, s)
        y_moe = _moe(hx_tiles, *_dispatch(route, hist), w_gate[l], w_up[l], w_down[l], n)
        moe = (y_moe, route)
    return _final(xs, moe, row(final_norm_g)).reshape(b, s, d)
```

```python
import functools
import math

import jax
import jax.numpy as jnp
import numpy as np
from jax import lax
from jax.experimental import pallas as pl
from jax.experimental.pallas import tpu as pltpu

F32 = jnp.float32
BF16 = jnp.bfloat16
HIGHEST = lax.Precision.HIGHEST

D_MODEL = 1024
GRID_W = 64
HEAD_DIM = 64
MIXER_WIDTH = 256
MIXER_HEADS = 4
NA_KH = 8
NA_KW = 16
RET_CHUNK = 128
ROPE_BASE = 10000.0
POOL_WINDOWS = (2, 4, 8, 16)
POOL_GROUP = 64
GDN_CHUNK = 64
CONV_K = 4
MEM_HEADS = 4
MEM_HEAD_DIM = 256
N_GROUPS = 4
EXPERTS_PER_GROUP = 8
N_EXPERTS = 32
TOP_K = 2
D_EXPERT = 512
EPS = 1e-6

P_IN_PAD = 3200
V7X_VMEM_LIMIT = 56 * 1024 * 1024
LANES = 128
SUBLANES = 8
ROW_TILES = D_MODEL // LANES
NEG_BIG = -1e30

PROJ_TILE = 512
POOL_TILE = 512
POST_TILE = 512
NA_ROWS_PER_STEP = 8
NA_ROWS_LOCKSTEP = 2
GDN_CHUNKS_PER_STEP = 2
GDN_PRE_TILE = 512
RET_CHUNKS_PER_STEP = 2
MOE_BLOCK = 256
MOE_PIECES = 8
DMA_UNROLL = 8
HALO = 8


def _cparams(*sem):
    return pltpu.CompilerParams(dimension_semantics=sem, vmem_limit_bytes=V7X_VMEM_LIMIT)


def _dot(a, b):
    return jnp.dot(a, b, preferred_element_type=F32)


def _dot_nt(a, b):
    return lax.dot_general(a, b, (((1,), (1,)), ((), ())), preferred_element_type=F32)


def _dot_tn(a, b, precision=None):
    return lax.dot_general(a, b, (((0,), (0,)), ((), ())), preferred_element_type=F32, precision=precision)


def _silu(x):
    return x * (1.0 / (1.0 + jnp.exp(-x)))


def _rms(x, g):
    return x * lax.rsqrt(jnp.mean(x * x, axis=-1, keepdims=True) + EPS) * g


def _from_row_tiles(ref, first, count, stride):
    return jnp.concatenate([ref[pl.ds(first + k, count, stride=stride), :] for k in range(ROW_TILES)], axis=1)


def _to_row_tiles(ref, x):
    count = x.shape[0]
    for k in range(ROW_TILES):
        ref[pl.ds(k, count, stride=ROW_TILES), :] = x[:, k * LANES:(k + 1) * LANES]


def _add_moe(x, y_ref, rt_ref):
    t = x.shape[0]
    stride = TOP_K * ROW_TILES
    for slot in range(TOP_K):
        x = x + rt_ref[:, TOP_K + slot:TOP_K + slot + 1] * _from_row_tiles(y_ref, slot * ROW_TILES, t, stride)
    return x


def _split_bf16(x, parts):
    out = []
    for _ in range(parts):
        p = x.astype(BF16)
        out.append(p)
        x = x - p.astype(F32)
    return out


def _norm_proj_kernel(has_y, *refs):
    if has_y:
        x_ref, y_ref, rt_ref, g_ref, w_ref, xo_ref, pa_ref, pb_ref, pc_ref, pd_ref, pab_ref = refs
        x = _add_moe(x_ref[...], y_ref, rt_ref)
        xo_ref[...] = x
    else:
        x_ref, g_ref, w_ref, pa_ref, pb_ref, pc_ref, pd_ref, pab_ref = refs
        x = x_ref[...]
    h = _rms(x, g_ref[...]).astype(BF16)
    w = MIXER_WIDTH
    pa_ref[...] = _dot(h, w_ref[:, 0:3 * w]).astype(BF16)
    pb_ref[...] = _dot(h, w_ref[:, 3 * w:7 * w])
    pc_ref[...] = _dot(h, w_ref[:, 7 * w:8 * w])
    pd_ref[...] = _dot(h, w_ref[:, 8 * w:12 * w])
    pab_ref[...] = _dot(h, w_ref[:, 12 * w:P_IN_PAD])


def _moe_out_spec(tile):
    return pl.BlockSpec((tile * TOP_K * ROW_TILES, LANES), lambda i: (i, 0))


def _norm_proj(x, moe, g, w_pad):
    n = x.shape[0]
    tm = PROJ_TILE
    w = MIXER_WIDTH
    row = lambda c: pl.BlockSpec((tm, c), lambda i: (i, 0))
    const = lambda a: pl.BlockSpec(a.shape, lambda i: (0,) * a.ndim)
    outs = [jax.ShapeDtypeStruct((n, 3 * w), BF16), jax.ShapeDtypeStruct((n, 4 * w), F32),
            jax.ShapeDtypeStruct((n, w), F32), jax.ShapeDtypeStruct((n, 4 * w), F32),
            jax.ShapeDtypeStruct((n, LANES), F32)]
    out_specs = [row(3 * w), row(4 * w), row(w), row(4 * w), row(LANES)]
    if moe is None:
        ins, in_specs = (x, g, w_pad), [row(D_MODEL), const(g), const(w_pad)]
    else:
        ins = (x, *moe, g, w_pad)
        in_specs = [row(D_MODEL), _moe_out_spec(tm), row(LANES), const(g), const(w_pad)]
        outs = [jax.ShapeDtypeStruct((n, D_MODEL), F32)] + outs
        out_specs = [row(D_MODEL)] + out_specs
    res = pl.pallas_call(
        functools.partial(_norm_proj_kernel, moe is not None),
        grid=(n // tm,), in_specs=in_specs, out_specs=out_specs, out_shape=outs,
        compiler_params=_cparams("parallel"),
    )(*ins)
    return res if moe is not None else [x] + list(res)


def _na_bias_table(rpb):
    qc = np.arange(GRID_W)
    kc = np.arange(GRID_W)
    d_col = np.clip(kc[None, :] - qc[:, None], -(NA_KW - 1), NA_KW - 1) + (NA_KW - 1)
    pick = np.zeros((2 * NA_KW - 1, GRID_W * GRID_W), np.float32)
    pick[d_col.reshape(-1), np.arange(GRID_W * GRID_W)] = 1.0
    win = np.clip(qc - NA_KW // 2, 0, GRID_W - NA_KW)
    ok = (kc[None, :] >= win[:, None]) & (kc[None, :] < win[:, None] + NA_KW)
    cols = jnp.einsum('hrc,cx->hrx', rpb.astype(F32), jnp.asarray(pick), precision=HIGHEST)
    cols = jnp.where(ok[None, None], cols.reshape(MIXER_HEADS, 2 * NA_KH - 1, GRID_W, GRID_W), NEG_BIG)
    per_e = [jnp.transpose(cols[:, NA_KH - 1 - e:2 * NA_KH - 1 - e], (0, 2, 1, 3)) for e in range(NA_KH)]
    return jnp.stack(per_e, axis=1).reshape(MIXER_HEADS, NA_KH, GRID_W, NA_KH * GRID_W)


def _na_kernel(rows, q_ref, k_ref, v_ref, t_ref, o_ref):
    i = pl.program_id(1)
    dh = HEAD_DIM

    def rows_body(it, carry):
        chains = []
        for k in range(NA_ROWS_LOCKSTEP):
            rr = it * NA_ROWS_LOCKSTEP + k
            r = i * NA_ROWS_PER_STEP + rr
            kr0 = jnp.clip(r - NA_KH // 2, 0, rows - NA_KH)
            q = q_ref[pl.ds(pl.multiple_of(rr * GRID_W, GRID_W), GRID_W), :]
            k0 = pl.multiple_of(kr0 * GRID_W, GRID_W)
            kb = k_ref[pl.ds(k0, NA_KH * GRID_W), :]
            vb = v_ref[pl.ds(k0, NA_KH * GRID_W), :]
            for h in range(MIXER_HEADS):
                sl = slice(h * dh, (h + 1) * dh)
                chains.append((q[:, sl], kb[:, sl], vb[:, sl], h, r - kr0))
        s = [_dot_nt(q, kb) * (dh ** -0.5) + t_ref[h, e] for q, kb, _, h, e in chains]
        m = [jnp.max(a, axis=-1, keepdims=True) for a in s]
        p = [jnp.exp(a - b) for a, b in zip(s, m)]
        l = [jnp.sum(a, axis=-1, keepdims=True) for a in p]
        o = [_dot(a.astype(BF16), c[2]) / b for a, b, c in zip(p, l, chains)]
        for k in range(NA_ROWS_LOCKSTEP):
            rr = it * NA_ROWS_LOCKSTEP + k
            row = jnp.concatenate(o[k * MIXER_HEADS:(k + 1) * MIXER_HEADS], axis=-1)
            o_ref[pl.ds(pl.multiple_of(rr * GRID_W, GRID_W), GRID_W), :] = row.astype(BF16)
        return carry

    lax.fori_loop(0, NA_ROWS_PER_STEP // NA_ROWS_LOCKSTEP, rows_body, 0)


def _neighbourhood_attention(p_a, table, b, s):
    rows = s // GRID_W
    assert rows >= NA_KH and rows % NA_ROWS_PER_STEP == 0
    steps = rows // NA_ROWS_PER_STEP
    tq = NA_ROWS_PER_STEP * GRID_W
    w = MIXER_WIDTH
    return pl.pallas_call(
        functools.partial(_na_kernel, rows),
        grid=(b, steps),
        in_specs=[pl.BlockSpec((tq, w), lambda bi, i: (bi * steps + i, 0)),
                  pl.BlockSpec((s, w), lambda bi, i: (bi, 1)),
                  pl.BlockSpec((s, w), lambda bi, i: (bi, 2)),
                  pl.BlockSpec(table.shape, lambda bi, i: (0, 0, 0, 0))],
        out_specs=pl.BlockSpec((tq, w), lambda bi, i: (bi * steps + i, 0)),
        out_shape=jax.ShapeDtypeStruct((b * s, w), BF16),
        compiler_params=_cparams("parallel", "arbitrary"),
    )(p_a, p_a, p_a, table)


def _retention_tables(s):
    h, dh, c = MIXER_HEADS, HEAD_DIM, RET_CHUNK
    half = dh // 2
    inv = ROPE_BASE ** (-jnp.arange(half, dtype=F32) / half)
    ang = jnp.arange(s, dtype=F32)[:, None] * inv[None, :]
    cos, sin = jnp.cos(ang), jnp.sin(ang)
    zero = jnp.zeros_like(sin)
    cos_t = jnp.tile(jnp.concatenate([cos, cos], axis=-1), (1, h))
    sin_lo = jnp.tile(jnp.concatenate([-sin, zero], axis=-1), (1, h))
    sin_hi = jnp.tile(jnp.concatenate([zero, sin], axis=-1), (1, h))
    log_f = np.log1p(-np.exp2(-5.0 - np.arange(h, dtype=np.float64)))
    log_b = log_f[::-1]
    pos = np.arange(c, dtype=np.float64)
    diff = pos[:, None] - pos[None, :]
    dmat = np.where(diff >= 0, np.exp(log_f[:, None, None] * np.maximum(diff, 0.0)), 0.0) \
        + np.where(diff < 0, np.exp(log_b[:, None, None] * np.maximum(-diff, 0.0)), 0.0)
    lanes = lambda t: np.repeat(t.T, dh, axis=1)
    dec = np.stack([lanes(np.exp(log_f[:, None] * (pos + 1.0))),
                    lanes(np.exp(log_f[:, None] * (c - 1.0 - pos))),
                    lanes(np.exp(log_b[:, None] * (c - pos))),
                    lanes(np.exp(log_b[:, None] * pos))])
    chunk_f = [float(np.exp(v * c)) for v in log_f]
    chunk_b = [float(np.exp(v * c)) for v in log_b]
    return cos_t, sin_lo, sin_hi, jnp.asarray(dmat, F32), jnp.asarray(dec, F32), chunk_f, chunk_b


def _rotary(x, cos, sin_lo, sin_hi):
    w = x.shape[-1]
    return x * cos + pltpu.roll(x, w - HEAD_DIM // 2, 1) * sin_lo + pltpu.roll(x, HEAD_DIM // 2, 1) * sin_hi


def _retention_kernel(nb, chunk_f, chunk_b, q_ref, k_ref, v_ref, gate_ref, cos_ref, slo_ref, shi_ref,
                      dmat_ref, dec_ref, ng_ref, o_ref, sf_ref, sb_ref, sball_ref):
    t = pl.program_id(1)
    dh, c, cps = HEAD_DIM, RET_CHUNK, RET_CHUNKS_PER_STEP
    cos, slo, shi = cos_ref[...], slo_ref[...], shi_ref[...]
    kr = _rotary(k_ref[...], cos, slo, shi) * (dh ** -0.5)
    vb = v_ref[...].astype(BF16)
    items = [(ci, h) for ci in range(cps) for h in range(MIXER_HEADS)]
    part = lambda x, ci, h: x[ci * c:(ci + 1) * c, h * dh:(h + 1) * dh]
    split = lambda x: {it: part(x, *it) for it in items}

    @pl.when(t == 0)
    def _():
        sf_ref[...] = jnp.zeros_like(sf_ref)
        sb_ref[...] = jnp.zeros_like(sb_ref)

    @pl.when(t < nb)
    def _():
        blk = nb - 1 - t
        kd, v = split((kr * dec_ref[3]).astype(BF16)), split(vb)
        kv = {it: _dot_tn(kd[it], v[it]) for it in items}
        for h in range(MIXER_HEADS):
            state = sb_ref[h]
            for ci in range(cps - 1, -1, -1):
                sball_ref[blk * cps + ci, h] = state
                state = state * chunk_b[h] + kv[ci, h]
            sb_ref[h] = state

    @pl.when(t >= nb)
    def _():
        blk = t - nb
        qr = _rotary(q_ref[...], cos, slo, shi)
        q, k, v = split(qr.astype(BF16)), split(kr.astype(BF16)), split(vb)
        qf, qbk = split((qr * dec_ref[0]).astype(BF16)), split((qr * dec_ref[2]).astype(BF16))
        kd = split((kr * dec_ref[1]).astype(BF16))
        sc = {it: (_dot_nt(q[it], k[it]) * dmat_ref[it[1]]).astype(BF16) for it in items}
        o = {it: _dot(sc[it], v[it]) for it in items}
        kv = {it: _dot_tn(kd[it], v[it]) for it in items}
        ob = {it: _dot(qbk[it], sball_ref[blk * cps + it[0], it[1]].astype(BF16)) for it in items}
        of = {}
        for h in range(MIXER_HEADS):
            state = sf_ref[h]
            for ci in range(cps):
                of[ci, h] = _dot(qf[ci, h], state.astype(BF16))
                state = state * chunk_f[h] + kv[ci, h]
            sf_ref[h] = state
        o = {it: o[it] + of[it] + ob[it] for it in items}
        mu = {it: jnp.mean(o[it], axis=-1, keepdims=True) for it in items}
        oc = {it: o[it] - mu[it] for it in items}
        var = {it: jnp.mean(oc[it] * oc[it], axis=-1, keepdims=True) for it in items}
        on = {it: oc[it] * lax.rsqrt(var[it] + EPS) for it in items}
        y = jnp.concatenate([jnp.concatenate([on[ci, h] for h in range(MIXER_HEADS)], axis=-1)
                             for ci in range(cps)], axis=0)
        o_ref[...] = (y * ng_ref[...] * _silu(gate_ref[...])).astype(BF16)


def _retention(p_b, norm_g, tables, b, s):
    cos_t, sin_lo, sin_hi, dmat, dec, chunk_f, chunk_b = tables
    cps = RET_CHUNKS_PER_STEP
    tb = RET_CHUNK * cps
    nb = s // tb
    w = MIXER_WIDTH
    dec = jnp.tile(dec, (1, cps, 1))
    block = lambda t: jnp.where(t < nb, nb - 1 - t, t - nb)
    col = lambda j: pl.BlockSpec((tb, w), lambda bi, t: (bi * nb + block(t), j))
    tab = pl.BlockSpec((tb, w), lambda bi, t: (block(t), 0))
    const = lambda a: pl.BlockSpec(a.shape, lambda bi, t: (0,) * a.ndim)
    return pl.pallas_call(
        functools.partial(_retention_kernel, nb, chunk_f, chunk_b),
        grid=(b, 2 * nb),
        in_specs=[col(0), col(1), col(2), col(3), tab, tab, tab, const(dmat), const(dec), const(norm_g)],
        out_specs=pl.BlockSpec((tb, w), lambda bi, t: (bi * nb + jnp.maximum(t - nb, 0), 0)),
        out_shape=jax.ShapeDtypeStruct((b * s, w), BF16),
        scratch_shapes=[pltpu.VMEM((MIXER_HEADS, HEAD_DIM, HEAD_DIM), F32),
                        pltpu.VMEM((MIXER_HEADS, HEAD_DIM, HEAD_DIM), F32),
                        pltpu.VMEM((s // RET_CHUNK, MIXER_HEADS, HEAD_DIM, HEAD_DIM), F32)],
        compiler_params=_cparams("arbitrary", "arbitrary"),
    )(p_b, p_b, p_b, p_b, cos_t, sin_lo, sin_hi, dmat, dec, norm_g)


def _halo_specs(tile, width, n_rows, block_of):
    per = tile // HALO
    last = n_rows // HALO - 1
    main = pl.BlockSpec((tile, width), lambda *g: (block_of(*g), 0))
    prev = pl.BlockSpec((HALO, width), lambda *g: (jnp.maximum(block_of(*g) * per - 1, 0), 0))
    nxt = pl.BlockSpec((HALO, width), lambda *g: (jnp.minimum((block_of(*g) + 1) * per, last), 0))
    return main, prev, nxt


def _fill_padded(pad_ref, main, prev, nxt, first, last):
    t = main.shape[0]
    pad_ref[0:HALO, :] = jnp.where(first, 0.0, prev)
    pad_ref[HALO:HALO + t, :] = main
    pad_ref[HALO + t:2 * HALO + t, :] = jnp.where(last, 0.0, nxt)


def _pool_kernel(s, tiles_per_seq, u_ref, prev_ref, next_ref, w_ref, scale_ref, o_ref, pad_ref):
    t = POOL_TILE
    tseq = pl.program_id(0) % tiles_per_seq
    u = u_ref[...]
    _fill_padded(pad_ref, u, prev_ref[...], next_ref[...], tseq == 0, tseq == tiles_per_seq - 1)
    lane = lax.broadcasted_iota(jnp.int32, (1, MIXER_WIDTH), 1)
    half = jnp.full((1, MIXER_WIDTH), POOL_WINDOWS[0] // 2, jnp.int32)
    for gi in range(1, len(POOL_WINDOWS)):
        half = jnp.where(lane >= gi * POOL_GROUP, POOL_WINDOWS[gi] // 2, half)
    max_half = POOL_WINDOWS[-1] // 2
    acc = jnp.zeros((t, MIXER_WIDTH), F32)
    for d in range(-max_half, max_half):
        inside = (d >= -half) & (d < half)
        acc = acc + jnp.where(inside, pad_ref[HALO + d:HALO + d + t, :], 0.0)
    pos = tseq * t + lax.broadcasted_iota(jnp.int32, (t, 1), 0)
    count = jnp.minimum(pos + half, s) - jnp.maximum(pos - half, 0)
    diff = acc / count.astype(F32) - u
    o_ref[...] = (_dot(diff.astype(BF16), w_ref[...]) * scale_ref[...]).astype(BF16)


def _pool(p_c, w_blockdiag, scale, b, s):
    t = POOL_TILE
    tiles_per_seq = s // t
    main, prev, nxt = _halo_specs(t, MIXER_WIDTH, b * s, lambda i: i)
    const = lambda a: pl.BlockSpec(a.shape, lambda i: (0,) * a.ndim)
    return pl.pallas_call(
        functools.partial(_pool_kernel, s, tiles_per_seq),
        grid=(b * tiles_per_seq,),
        in_specs=[main, prev, nxt, const(w_blockdiag), const(scale)],
        out_specs=pl.BlockSpec((t, MIXER_WIDTH), lambda i: (i, 0)),
        out_shape=jax.ShapeDtypeStruct((b * s, MIXER_WIDTH), BF16),
        scratch_shapes=[pltpu.VMEM((t + 2 * HALO, MIXER_WIDTH), F32)],
        compiler_params=_cparams("parallel"),
    )(p_c, p_c, p_c, w_blockdiag, scale)


def _head_blockdiag(x):
    lane = lax.broadcasted_iota(jnp.int32, x.shape, 1)
    zero = jnp.zeros_like(x)
    return jnp.concatenate([jnp.where(lane < HEAD_DIM, x, zero), jnp.where(lane >= HEAD_DIM, x, zero)], axis=0)


def _gdn_pre_kernel(tiles_per_seq, x_ref, prev_ref, next_ref, ab_ref, cw_ref, alog_ref, dtb_ref, ones_ref,
                    u_ref, gb_ref, pad_ref):
    t, w = GDN_PRE_TILE, MIXER_WIDTH
    tseq = pl.program_id(0) % tiles_per_seq
    _fill_padded(pad_ref, x_ref[...], prev_ref[...], next_ref[...], tseq == 0, tseq == tiles_per_seq - 1)
    acc = jnp.zeros((t, 3 * w), F32)
    for k in range(CONV_K):
        off = HALO + k - CONV_K // 2
        acc = acc + pad_ref[off:off + t, :] * cw_ref[k:k + 1, :]
    u = _silu(acc)
    head_sum = lambda a: sum(_dot(piece, ones_ref[...]) for piece in _split_bf16(a, 2))
    q, k = u[:, 0:w], u[:, w:2 * w]
    u_ref[:, 0:w] = q * lax.rsqrt(head_sum(q * q) + EPS) * (HEAD_DIM ** -0.5)
    u_ref[:, w:2 * w] = k * lax.rsqrt(head_sum(k * k) + EPS)
    u_ref[:, 2 * w:3 * w] = u[:, 2 * w:3 * w]
    ab = ab_ref[...]
    x = ab + dtb_ref[...]
    g = -jnp.exp(alog_ref[...]) * (jnp.maximum(x, 0.0) + jnp.log(1.0 + jnp.exp(-jnp.abs(x))))
    beta = 1.0 / (1.0 + jnp.exp(-ab))
    lane = lax.broadcasted_iota(jnp.int32, ab.shape, 1)
    gb_ref[...] = jnp.where(lane < 2 * MIXER_HEADS, g, beta)


def _gdn_pre(p_d, p_ab, conv_w, alog_vec, dtb_vec, b, s):
    t = GDN_PRE_TILE
    w = MIXER_WIDTH
    tiles_per_seq = s // t
    ones = _block_diag(jnp.ones((MIXER_HEADS, HEAD_DIM, HEAD_DIM), BF16))
    main, prev, nxt = _halo_specs(t, 3 * w, b * s, lambda i: i)
    const = lambda a: pl.BlockSpec(a.shape, lambda i: (0,) * a.ndim)
    row = lambda c: pl.BlockSpec((t, c), lambda i: (i, 0))
    return pl.pallas_call(
        functools.partial(_gdn_pre_kernel, tiles_per_seq),
        grid=(b * tiles_per_seq,),
        in_specs=[main, prev, nxt, row(LANES), const(conv_w), const(alog_vec), const(dtb_vec), const(ones)],
        out_specs=[row(3 * w), row(LANES)],
        out_shape=[jax.ShapeDtypeStruct((b * s, 3 * w), F32), jax.ShapeDtypeStruct((b * s, LANES), F32)],
        scratch_shapes=[pltpu.VMEM((t + 2 * HALO, 3 * w), F32)],
        compiler_params=_cparams("parallel"),
    )(p_d, p_d, p_d, p_ab, conv_w, alog_vec, dtb_vec, ones)


def _gdn_prepare(forward, u, gb, tri_col, tri_row):
    c, dh, w = GDN_CHUNK, HEAD_DIM, MIXER_WIDTH
    t = u.shape[0]
    a_off, b_off = (0, 2 * MIXER_HEADS) if forward else (MIXER_HEADS, 3 * MIXER_HEADS)
    g_all = beta_all = gb
    g_parts = _split_bf16(g_all, 3)
    gc_col = sum(_dot(tri_col, p) for p in g_parts)
    gc_row = sum(_dot_tn(p, tri_row) for p in g_parts)
    ii = lax.broadcasted_iota(jnp.int32, (c, 2 * dh), 0)
    lane = lax.broadcasted_iota(jnp.int32, (c, 2 * dh), 1)
    first = lane < dh
    jj = jnp.where(first, lane, lane - dh)
    incl = (ii >= jj) if forward else (ii <= jj)
    strict = (ii > jj) if forward else (ii < jj)
    last = c - 1 if forward else 0
    items = [(ci, p) for ci in range(t // c) for p in range(MIXER_HEADS // 2)]
    rows = lambda ci: slice(ci * c, (ci + 1) * c)
    part = lambda base: [u[rows(ci), base + 2 * p * dh:base + 2 * (p + 1) * dh] for ci, p in items]
    col_pair = lambda x, off: [jnp.where(first, x[rows(ci), off + 2 * p:off + 2 * p + 1],
                                         x[rows(ci), off + 2 * p + 1:off + 2 * p + 2]) for ci, p in items]
    q, k, v = part(0), part(w), part(2 * w)
    gcc = col_pair(gc_col, a_off)
    beta = col_pair(beta_all, b_off)
    gcr = [jnp.concatenate([gc_row[a_off + 2 * p:a_off + 2 * p + 1, rows(ci)],
                            gc_row[a_off + 2 * p + 1:a_off + 2 * p + 2, rows(ci)]], axis=1) for ci, p in items]
    g_last = [a[last:last + 1, :] for a in gcc]
    decay = [jnp.where(incl, jnp.exp(jnp.where(incl, a - b, 0.0)), 0.0) for a, b in zip(gcc, gcr)]
    kq = [_dot_nt(jnp.concatenate([kk, qq], axis=0).astype(BF16), _head_blockdiag(kk.astype(BF16)))
          for kk, qq in zip(k, q)]
    lmat = [jnp.where(strict, m[:c] * b * d, 0.0) for m, b, d in zip(kq, beta, decay)]
    attn = [(m[c:] * d).astype(BF16) for m, d in zip(kq, decay)]
    xinv = [-a for a in lmat]
    lp = [a.astype(BF16) for a in lmat]
    lp_bd = [_head_blockdiag(a) for a in lp]
    for _ in range(int(math.log2(c)) - 1):
        lp32 = [_dot(a, bd) for a, bd in zip(lp, lp_bd)]
        lp = [a.astype(BF16) for a in lp32]
        lp_bd = [_head_blockdiag(a) for a in lp]
        xinv = [xi + a32 + _dot(xi.astype(BF16), bd) for xi, a32, bd in zip(xinv, lp32, lp_bd)]
    egc = [jnp.exp(a) for a in gcc]
    xb = [a.astype(BF16) for a in xinv]
    vbeta = [vv * b for vv, b in zip(v, beta)]
    kbeta = [kk * (b * e) for kk, b, e in zip(k, beta, egc)]
    u_val = [r + _dot(xi, _head_blockdiag(r.astype(BF16))) for xi, r in zip(xb, vbeta)]
    w_dec = [r + _dot(xi, _head_blockdiag(r.astype(BF16))) for xi, r in zip(xb, kbeta)]
    wq = [jnp.concatenate([wd, qq * e], axis=0).astype(BF16) for wd, qq, e in zip(w_dec, q, egc)]
    k_state = [(kk * jnp.exp(gl - a)).astype(BF16) for kk, gl, a in zip(k, g_last, gcc)]
    e_last = [jnp.exp(gl) for gl in g_last]
    keyed = lambda vals: dict(zip(items, vals))
    return dict(wq=keyed(wq), u_val=keyed(u_val), attn=keyed(attn), k_state=keyed(k_state), e_last=keyed(e_last))


def _gdn_kernel(uf_ref, gbf_ref, ub_ref, gbb_ref, tril_ref, triu_ref, of_ref, ob_ref, sf_ref, sb_ref):
    n = pl.program_id(1)
    t = uf_ref.shape[0]
    c, dh = GDN_CHUNK, HEAD_DIM
    nch = t // c
    pairs = MIXER_HEADS // 2

    @pl.when(n == 0)
    def _():
        sf_ref[...] = jnp.zeros_like(sf_ref)
        sb_ref[...] = jnp.zeros_like(sb_ref)

    tril, triu = tril_ref[...], triu_ref[...]
    prep_f = _gdn_prepare(True, uf_ref[...], gbf_ref[...], tril, triu)
    prep_b = _gdn_prepare(False, ub_ref[...], gbb_ref[...], triu, tril)

    first = lax.broadcasted_iota(jnp.int32, (dh, 2 * dh), 1) < dh
    chains = [(prep_f, sf_ref, of_ref, p, True) for p in range(pairs)] + \
             [(prep_b, sb_ref, ob_ref, p, False) for p in range(pairs)]
    state = [s_ref[p] for _, s_ref, _, p, _ in chains]
    for step in range(nch):
        key = [(step if fw else nch - 1 - step, p) for _, _, _, p, fw in chains]
        ws = [_dot(pr['wq'][kk], _head_blockdiag(st.astype(BF16))) for (pr, *_), kk, st in zip(chains, key, state)]
        v_new = [pr['u_val'][kk] - a[:c] for (pr, *_), kk, a in zip(chains, key, ws)]
        vb = [a.astype(BF16) for a in v_new]
        out = [a[c:] + _dot(pr['attn'][kk], _head_blockdiag(b)) for (pr, *_), kk, a, b in zip(chains, key, ws, vb)]
        cross = [_dot_tn(pr['k_state'][kk], b) for (pr, *_), kk, b in zip(chains, key, vb)]
        state = [st * pr['e_last'][kk] + jnp.where(first, x[:dh], x[dh:])
                 for (pr, *_), kk, st, x in zip(chains, key, state, cross)]
        for (_, _, o_ref, p, _), (ci, _), o in zip(chains, key, out):
            o_ref[ci * c:(ci + 1) * c, 2 * p * dh:2 * (p + 1) * dh] = o
    for (_, s_ref, _, p, _), st in zip(chains, state):
        s_ref[p] = st


def _gated_deltanet(p_d, p_ab, conv_w, alog_vec, dtb_vec, b, s):
    u, gb = _gdn_pre(p_d, p_ab, conv_w, alog_vec, dtb_vec, b, s)
    c = GDN_CHUNK
    t = c * GDN_CHUNKS_PER_STEP
    nb = s // t
    w = MIXER_WIDTH
    r = np.arange(t)
    same = (r[:, None] // c) == (r[None, :] // c)
    tril = jnp.asarray(same & (r[:, None] >= r[None, :]), BF16)
    triu = jnp.asarray(same & (r[:, None] <= r[None, :]), BF16)
    fwd = lambda bi, n: (bi * nb + n, 0)
    bwd = lambda bi, n: (bi * nb + nb - 1 - n, 0)
    const = lambda a: pl.BlockSpec(a.shape, lambda bi, n: (0,) * a.ndim)
    return pl.pallas_call(
        _gdn_kernel,
        grid=(b, nb),
        in_specs=[pl.BlockSpec((t, 3 * w), fwd), pl.BlockSpec((t, LANES), fwd),
                  pl.BlockSpec((t, 3 * w), bwd), pl.BlockSpec((t, LANES), bwd), const(tril), const(triu)],
        out_specs=[pl.BlockSpec((t, w), fwd), pl.BlockSpec((t, w), bwd)],
        out_shape=[jax.ShapeDtypeStruct((b * s, w), F32), jax.ShapeDtypeStruct((b * s, w), F32)],
        scratch_shapes=[pltpu.VMEM((MIXER_HEADS // 2, HEAD_DIM, 2 * HEAD_DIM), F32),
                        pltpu.VMEM((MIXER_HEADS // 2, HEAD_DIM, 2 * HEAD_DIM), F32)],
        compiler_params=_cparams("arbitrary", "arbitrary"),
    )(u, gb, u, gb, tril, triu)


def _mem_kv_kernel(m_ref, g_ref, wk_ref, wv_ref, k_ref, v_ref):
    h = _rms(m_ref[0], g_ref[...]).astype(BF16)
    k_ref[0] = _dot(h, wk_ref[...]).astype(BF16)
    v_ref[0] = _dot(h, wv_ref[...]).astype(BF16)


def _mem_kv(mem, g, wk, wv):
    b, nm, d = mem.shape
    const = lambda a: pl.BlockSpec(a.shape, lambda i: (0,) * a.ndim)
    blk = pl.BlockSpec((1, nm, d), lambda i: (i, 0, 0))
    return pl.pallas_call(
        _mem_kv_kernel, grid=(b,),
        in_specs=[blk, const(g), const(wk), const(wv)], out_specs=[blk, blk],
        out_shape=[jax.ShapeDtypeStruct((b, nm, d), BF16)] * 2,
        compiler_params=_cparams("parallel"),
    )(mem, g, wk, wv)


def _route_lanes(lg):
    lane_i = lax.broadcasted_iota(jnp.int32, lg.shape, 1)
    lane = lane_i.astype(F32)
    first_at = lambda vals, m: jnp.min(jnp.where(vals == m, lane, float(LANES)), axis=-1, keepdims=True)
    g_mask = lane_i < N_GROUPS
    gl = jnp.where(g_mask, lg, NEG_BIG)
    g_max = jnp.max(gl, axis=-1, keepdims=True)
    g_sum = jnp.sum(jnp.where(g_mask, jnp.exp(gl - g_max), 0.0), axis=-1, keepdims=True)
    g_p = 1.0 / g_sum
    g_idx = first_at(gl, g_max)
    lo = N_GROUPS + EXPERTS_PER_GROUP * g_idx
    e_mask = (lane >= lo) & (lane < lo + EXPERTS_PER_GROUP)
    el = jnp.where(e_mask, lg, NEG_BIG)
    m1 = jnp.max(el, axis=-1, keepdims=True)
    i1 = first_at(el, m1)
    el2 = jnp.where(lane == i1, NEG_BIG, el)
    m2 = jnp.max(el2, axis=-1, keepdims=True)
    i2 = first_at(el2, m2)
    r = jnp.exp(m2 - m1)
    w1 = g_p / (1.0 + r)
    w2 = g_p * r / (1.0 + r)
    out = jnp.where(lane_i == 0, i1 - N_GROUPS, 0.0)
    out = jnp.where(lane_i == 1, i2 - N_GROUPS, out)
    out = jnp.where(lane_i == 2, w1, out)
    out = jnp.where(lane_i == 3, w2, out)
    chosen = jnp.where((lane == i1 - N_GROUPS) | (lane == i2 - N_GROUPS), 1.0, 0.0)
    return out, chosen


def _post_kernel(x_ref, ya_ref, yb_ref, yc_ref, of_ref, ob_ref, gate_ref, seg_ref, gg_ref, wout_ref, gq_ref,
                 wq_ref, k_ref, v_ref, wo_ref, gf_ref, wr_ref, br_ref, xo_ref, hx_ref, rt_ref, hist_ref):
    w = MIXER_WIDTH
    o = of_ref[...] + ob_ref[...]
    ms = sum(_dot(p, seg_ref[...]) for p in _split_bf16(o * o, 2))
    yd = o * lax.rsqrt(ms + EPS) * gg_ref[...] * _silu(gate_ref[...])
    mix = _dot(ya_ref[...], wout_ref[0:w, :]) + _dot(yb_ref[...], wout_ref[w:2 * w, :])
    mix = mix + _dot(yc_ref[...], wout_ref[2 * w:3 * w, :]) + _dot(yd.astype(BF16), wout_ref[3 * w:4 * w, :])
    x1 = x_ref[...] + mix
    q = _dot(_rms(x1, gq_ref[...]).astype(BF16), wq_ref[...]).astype(BF16)
    heads = []
    for h in range(MEM_HEADS):
        sl = slice(h * MEM_HEAD_DIM, (h + 1) * MEM_HEAD_DIM)
        sc = _dot_nt(q[:, sl], k_ref[0, :, sl]) * (MEM_HEAD_DIM ** -0.5)
        p = jnp.exp(sc - jnp.max(sc, axis=-1, keepdims=True))
        l = jnp.sum(p, axis=-1, keepdims=True)
        heads.append((_dot(p.astype(BF16), v_ref[0, :, sl]) / l).astype(BF16))
    x2 = x1 + _dot(jnp.concatenate(heads, axis=-1), wo_ref[...])
    xo_ref[...] = x2
    hx = _rms(x2, gf_ref[...])
    _to_row_tiles(hx_ref, hx)
    h_hi, h_lo = _split_bf16(hx, 2)
    lg = _dot(h_hi, wr_ref[0]) + _dot(h_hi, wr_ref[1]) + _dot(h_lo, wr_ref[0]) + br_ref[...]
    route, chosen = _route_lanes(lg)
    rt_ref[...] = route
    hist_ref[...] = jnp.broadcast_to(jnp.sum(chosen, axis=0, keepdims=True), hist_ref.shape)


def _post(x, y_a, y_b, y_c, o_f, o_b, p_d, seg, gdn_g, w_out, g_q, w_q, k_mem, v_mem, w_o, g_f, w_r, b_r, s):
    n = x.shape[0]
    tm = POST_TILE
    w = MIXER_WIDTH
    per_seq = s // tm
    row = lambda c: pl.BlockSpec((tm, c), lambda i: (i, 0))
    const = lambda a: pl.BlockSpec(a.shape, lambda i: (0,) * a.ndim)
    kv = pl.BlockSpec((1,) + k_mem.shape[1:], lambda i: (i // per_seq, 0, 0))
    return pl.pallas_call(
        _post_kernel, grid=(n // tm,),
        in_specs=[row(D_MODEL), row(w), row(w), row(w), row(w), row(w),
                  pl.BlockSpec((tm, w), lambda i: (i, 3)), const(seg), const(gdn_g), const(w_out), const(g_q),
                  const(w_q), kv, kv, const(w_o), const(g_f), const(w_r), const(b_r)],
        out_specs=[row(D_MODEL), pl.BlockSpec((tm * ROW_TILES, LANES), lambda i: (i, 0)), row(LANES),
                   pl.BlockSpec((SUBLANES, LANES), lambda i: (i, 0))],
        out_shape=[jax.ShapeDtypeStruct((n, D_MODEL), F32), jax.ShapeDtypeStruct((n * ROW_TILES, LANES), F32),
                   jax.ShapeDtypeStruct((n, LANES), F32), jax.ShapeDtypeStruct((n // tm * SUBLANES, LANES), F32)],
        compiler_params=_cparams("parallel"),
    )(x, y_a, y_b, y_c, o_f, o_b, p_d, seg, gdn_g, w_out, g_q, w_q, k_mem, v_mem, w_o, g_f, w_r, b_r)


def _dispatch(route, hist):
    n = route.shape[0]
    a = n * TOP_K
    blk = MOE_BLOCK
    flat_e = route[:, 0:TOP_K].astype(jnp.int32).reshape(a)
    _, order = lax.sort_key_val(flat_e, jnp.arange(a, dtype=jnp.int32))
    counts = jnp.sum(hist.reshape(-1, SUBLANES, LANES)[:, 0, :N_EXPERTS], axis=0).astype(jnp.int32)
    padded = (counts + blk - 1) // blk * blk
    pad_end = jnp.cumsum(padded)
    pad_start = pad_end - padded
    start = jnp.cumsum(counts) - counts
    n_blocks = a // blk + N_EXPERTS
    blk_start = jnp.arange(n_blocks, dtype=jnp.int32) * blk
    blk_expert = jnp.minimum(jnp.sum(pad_end[None, :] <= blk_start[:, None], axis=1), N_EXPERTS - 1).astype(jnp.int32)
    into_expert = blk_start - pad_start[blk_expert]
    blk_valid = jnp.clip(counts[blk_expert] - into_expert, 0, blk)
    in_blk = jnp.arange(blk, dtype=jnp.int32)[None, :]
    idx = (start[blk_expert] + into_expert)[:, None] + in_blk
    row_dst = jnp.where(in_blk < blk_valid[:, None], order[jnp.clip(idx, 0, a - 1)], a + in_blk)
    n_used = (pad_end[-1] // blk).astype(jnp.int32).reshape(1)
    return n_used, blk_expert, jnp.concatenate([a + in_blk, row_dst], axis=0).reshape(-1)


def _moe_kernel(n_tokens, n_blocks, nu_ref, be_ref, dst_ref, hx_hbm, wg_ref, wu_ref, wd_ref, y_hbm,
                xbuf, ybuf, gsem, ssem):
    j = pl.program_id(0)
    blk, rt = MOE_BLOCK, ROW_TILES
    n_used = nu_ref[0]
    slot = j % 2
    other = 1 - slot
    tile = lambda ref, i: ref.at[pl.ds(pl.multiple_of(i * rt, rt), rt), :]
    dst = lambda jj, i: dst_ref[(jj + 1) * blk + i]
    nxt = jnp.minimum(j + 1, n_blocks - 1)

    def gather_row(jj, sl, i):
        tok = jnp.minimum(lax.shift_right_logical(dst(jj, i), TOP_K // 2), n_tokens - 1)
        pltpu.make_async_copy(tile(hx_hbm, tok), tile(xbuf.at[sl], i), gsem.at[sl]).start()

    def scatter_row(jj, sl, i):
        pltpu.make_async_copy(tile(ybuf.at[sl], i), tile(y_hbm, dst(jj, i)), ssem.at[sl]).start()

    def wait_gather(sl):
        pltpu.make_async_copy(hx_hbm.at[pl.ds(0, blk * rt), :], xbuf.at[sl], gsem.at[sl]).wait()

    def wait_scatter(sl):
        pltpu.make_async_copy(ybuf.at[sl], y_hbm.at[pl.ds(0, blk * rt), :], ssem.at[sl]).wait()

    def rolled(fn):
        lax.fori_loop(0, blk, lambda i, carry: (fn(i), carry)[1], 0, unroll=DMA_UNROLL)

    @pl.when(j == 0)
    def _():
        ybuf[1] = jnp.zeros((blk * rt, LANES), F32)
        rolled(lambda i: gather_row(0, 0, i))

    @pl.when(j < n_used)
    def _():
        wait_gather(slot)

        @pl.when(j >= 1)
        def _():
            wait_scatter(slot)

        per = blk // MOE_PIECES

        def issue_copies(piece):
            for i in range(piece * per, (piece + 1) * per):
                gather_row(nxt, other, i)
                scatter_row(j - 1, other, i)

        xs = _from_row_tiles(xbuf.at[slot], 0, blk, rt).astype(BF16)
        half = D_EXPERT // 2
        acts = []
        for c in range(2):
            cols = slice(c * half, (c + 1) * half)
            g = _dot(xs, wg_ref[0, 0, :, cols].astype(BF16))
            issue_copies(2 * c)
            u = _dot(xs, wu_ref[0, 0, :, cols].astype(BF16))
            issue_copies(2 * c + 1)
            acts.append((_silu(g) * u).astype(BF16))
        h = jnp.concatenate(acts, axis=1)
        quarter = D_MODEL // 4
        for c in range(4):
            y = _dot(h, wd_ref[0, 0, :, c * quarter:(c + 1) * quarter].astype(BF16))
            for k in range(quarter // LANES):
                ybuf[slot, pl.ds(c * (quarter // LANES) + k, blk, stride=rt), :] = y[:, k * LANES:(k + 1) * LANES]
            issue_copies(4 + c)

    @pl.when(j == n_used)
    def _():
        wait_gather(slot)
        wait_scatter(slot)
        rolled(lambda i: scatter_row(j - 1, other, i))
        wait_scatter(other)


def _moe(hx_tiles, n_used, blk_expert, row_dst, w_gate, w_up, w_down, layer, n):
    blk = MOE_BLOCK
    n_blocks = blk_expert.shape[0]
    expert = lambda j, nu, be, dst: (layer, be[jnp.minimum(j, n_blocks - 1)], 0, 0)
    grid_spec = pltpu.PrefetchScalarGridSpec(
        num_scalar_prefetch=3, grid=(n_blocks + 1,),
        in_specs=[pl.BlockSpec(memory_space=pl.ANY),
                  pl.BlockSpec((1, 1, D_MODEL, D_EXPERT), expert),
                  pl.BlockSpec((1, 1, D_MODEL, D_EXPERT), expert),
                  pl.BlockSpec((1, 1, D_EXPERT, D_MODEL), expert)],
        out_specs=pl.BlockSpec(memory_space=pl.ANY),
        scratch_shapes=[pltpu.VMEM((2, blk * ROW_TILES, LANES), F32), pltpu.VMEM((2, blk * ROW_TILES, LANES), F32),
                        pltpu.SemaphoreType.DMA((2,)), pltpu.SemaphoreType.DMA((2,))])
    return pl.pallas_call(
        functools.partial(_moe_kernel, n, n_blocks), grid_spec=grid_spec,
        out_shape=jax.ShapeDtypeStruct(((n * TOP_K + blk) * ROW_TILES, LANES), F32),
        compiler_params=_cparams("arbitrary"),
    )(n_used, blk_expert, row_dst, hx_tiles, w_gate, w_up, w_down)


def _final_kernel(x_ref, y_ref, rt_ref, g_ref, o_ref):
    o_ref[...] = _rms(_add_moe(x_ref[...], y_ref, rt_ref), g_ref[...])


def _final(x, moe, g):
    n = x.shape[0]
    tm = PROJ_TILE
    return pl.pallas_call(
        _final_kernel, grid=(n // tm,),
        in_specs=[pl.BlockSpec((tm, D_MODEL), lambda i: (i, 0)), _moe_out_spec(tm),
                  pl.BlockSpec((tm, LANES), lambda i: (i, 0)), pl.BlockSpec(g.shape, lambda i: (0, 0))],
        out_specs=pl.BlockSpec((tm, D_MODEL), lambda i: (i, 0)),
        out_shape=jax.ShapeDtypeStruct((n, D_MODEL), F32),
        compiler_params=_cparams("parallel"),
    )(x, *moe, g)


def _lane_vec(v, width=LANES):
    v = v.reshape(1, -1).astype(F32)
    return jnp.pad(v, ((0, 0), (0, width - v.shape[1])))


def _block_diag(blocks):
    g, c, _ = blocks.shape
    eye = jnp.eye(g, dtype=blocks.dtype)
    return (eye[:, None, :, None] * blocks[:, :, None, :]).reshape(g * c, g * c)


def kernel(x, mem, mix_norm_g, w_in, na_rpb, ret_norm_g, pool_w, pool_scale, gdn_conv_w, gdn_a_log, gdn_dt_bias, gdn_norm_g, w_out, mem_q_norm_g, mem_kv_norm_g, mem_w_q, mem_w_k, mem_w_v, mem_w_o, ffn_norm_g, w_group, b_group, w_router, b_router, w_gate, w_up, w_down, final_norm_g):
    b, s, d = x.shape
    n = b * s
    depth = w_in.shape[0]
    row = lambda v: v.reshape(1, -1).astype(F32)
    ret_tables = _retention_tables(s)
    seg = _block_diag(jnp.full((MIXER_HEADS, HEAD_DIM, HEAD_DIM), 1.0 / HEAD_DIM, BF16))
    xs = x.reshape(n, d)
    moe = None
    for l in range(depth):
        w_pad = jnp.pad(w_in[l], ((0, 0), (0, P_IN_PAD - w_in.shape[2]))).astype(BF16)
        xs, p_a, p_b, p_c, p_d, p_ab = _norm_proj(xs, moe, row(mix_norm_g[l]), w_pad)
        y_a = _neighbourhood_attention(p_a, _na_bias_table(na_rpb[l]), b, s)
        y_b = _retention(p_b, row(ret_norm_g[l]), ret_tables, b, s)
        y_c = _pool(p_c, _block_diag(pool_w[l]).astype(BF16), row(pool_scale[l]), b, s)
        o_f, o_b = _gated_deltanet(p_d, p_ab, gdn_conv_w[l].astype(F32), _lane_vec(gdn_a_log[l]),
                                   _lane_vec(gdn_dt_bias[l]), b, s)
        k_mem, v_mem = _mem_kv(mem, row(mem_kv_norm_g[l]), mem_w_k[l].astype(BF16), mem_w_v[l].astype(BF16))
        w_r = jnp.pad(jnp.concatenate([w_group[l], w_router[l]], axis=1),
                      ((0, 0), (0, LANES - N_GROUPS - N_EXPERTS))).astype(F32)
        b_r = _lane_vec(jnp.concatenate([b_group[l], b_router[l]]))
        xs, hx_tiles, route, hist = _post(
            xs, y_a, y_b, y_c, o_f, o_b, p_d, seg, row(jnp.tile(gdn_norm_g[l], MIXER_HEADS)), w_out[l].astype(BF16),
            row(mem_q_norm_g[l]), mem_w_q[l].astype(BF16), k_mem, v_mem, mem_w_o[l].astype(BF16),
            row(ffn_norm_g[l]), jnp.stack(_split_bf16(w_r, 2)), b_r, s)
        y_moe = _moe(hx_tiles, *_dispatch(route, hist), w_gate, w_up, w_down, l, n)
        moe = (y_moe, route)
    return _final(xs, moe, row(final_norm_g)).reshape(b, s, d)
```

```python
import functools
import math

import jax
import jax.numpy as jnp
import numpy as np
from jax import lax
from jax.experimental import pallas as pl
from jax.experimental.pallas import tpu as pltpu

F32 = jnp.float32
BF16 = jnp.bfloat16
HIGHEST = lax.Precision.HIGHEST

D_MODEL = 1024
GRID_W = 64
HEAD_DIM = 64
MIXER_WIDTH = 256
MIXER_HEADS = 4
NA_KH = 8
NA_KW = 16
RET_CHUNK = 128
ROPE_BASE = 10000.0
POOL_WINDOWS = (2, 4, 8, 16)
POOL_GROUP = 64
GDN_CHUNK = 64
CONV_K = 4
MEM_HEADS = 4
MEM_HEAD_DIM = 256
N_GROUPS = 4
EXPERTS_PER_GROUP = 8
N_EXPERTS = 32
TOP_K = 2
D_EXPERT = 512
EPS = 1e-6

P_IN_PAD = 3200
V7X_VMEM_LIMIT = 56 * 1024 * 1024
LANES = 128
SUBLANES = 8
ROW_TILES = D_MODEL // LANES
NEG_BIG = -1e30

PROJ_TILE = 512
POOL_TILE = 512
POST_TILE = 512
NA_ROWS_PER_STEP = 8
NA_ROWS_LOCKSTEP = 4
GDN_CHUNKS_PER_STEP = 4
GDN_PRE_TILE = 512
RET_CHUNKS_PER_STEP = 4
MOE_BLOCK = 256
MOE_PIECES = 8
DMA_UNROLL = 8
HALO = 8


def _cparams(*sem):
    return pltpu.CompilerParams(dimension_semantics=sem, vmem_limit_bytes=V7X_VMEM_LIMIT)


def _dot(a, b):
    return jnp.dot(a, b, preferred_element_type=F32)


def _dot_nt(a, b):
    return lax.dot_general(a, b, (((1,), (1,)), ((), ())), preferred_element_type=F32)


def _dot_tn(a, b, precision=None):
    return lax.dot_general(a, b, (((0,), (0,)), ((), ())), preferred_element_type=F32, precision=precision)


def _silu(x):
    return x * (1.0 / (1.0 + jnp.exp(-x)))


def _rms(x, g):
    return x * lax.rsqrt(jnp.mean(x * x, axis=-1, keepdims=True) + EPS) * g


def _from_row_tiles(ref, first, count, stride):
    return jnp.concatenate([ref[pl.ds(first + k, count, stride=stride), :] for k in range(ROW_TILES)], axis=1)


def _to_row_tiles(ref, x):
    count = x.shape[0]
    for k in range(ROW_TILES):
        ref[pl.ds(k, count, stride=ROW_TILES), :] = x[:, k * LANES:(k + 1) * LANES]


def _add_moe(x, y_ref, rt_ref):
    t = x.shape[0]
    stride = TOP_K * ROW_TILES
    for slot in range(TOP_K):
        x = x + rt_ref[:, TOP_K + slot:TOP_K + slot + 1] * _from_row_tiles(y_ref, slot * ROW_TILES, t, stride)
    return x


def _split_bf16(x, parts):
    out = []
    for _ in range(parts):
        p = x.astype(BF16)
        out.append(p)
        x = x - p.astype(F32)
    return out


def _norm_proj_kernel(has_y, *refs):
    if has_y:
        x_ref, y_ref, rt_ref, g_ref, w_ref, xo_ref, pa_ref, pb_ref, pc_ref, pd_ref, pab_ref = refs
        x = _add_moe(x_ref[...], y_ref, rt_ref)
        xo_ref[...] = x
    else:
        x_ref, g_ref, w_ref, pa_ref, pb_ref, pc_ref, pd_ref, pab_ref = refs
        x = x_ref[...]
    h = _rms(x, g_ref[...]).astype(BF16)
    w = MIXER_WIDTH
    pa_ref[...] = _dot(h, w_ref[:, 0:3 * w]).astype(BF16)
    pb_ref[...] = _dot(h, w_ref[:, 3 * w:7 * w])
    pc_ref[...] = _dot(h, w_ref[:, 7 * w:8 * w])
    pd_ref[...] = _dot(h, w_ref[:, 8 * w:12 * w])
    pab_ref[...] = _dot(h, w_ref[:, 12 * w:P_IN_PAD])


def _moe_out_spec(tile):
    return pl.BlockSpec((tile * TOP_K * ROW_TILES, LANES), lambda i: (i, 0))


def _norm_proj(x, moe, g, w_pad):
    n = x.shape[0]
    tm = PROJ_TILE
    w = MIXER_WIDTH
    row = lambda c: pl.BlockSpec((tm, c), lambda i: (i, 0))
    const = lambda a: pl.BlockSpec(a.shape, lambda i: (0,) * a.ndim)
    outs = [jax.ShapeDtypeStruct((n, 3 * w), BF16), jax.ShapeDtypeStruct((n, 4 * w), F32),
            jax.ShapeDtypeStruct((n, w), F32), jax.ShapeDtypeStruct((n, 4 * w), F32),
            jax.ShapeDtypeStruct((n, LANES), F32)]
    out_specs = [row(3 * w), row(4 * w), row(w), row(4 * w), row(LANES)]
    if moe is None:
        ins, in_specs = (x, g, w_pad), [row(D_MODEL), const(g), const(w_pad)]
    else:
        ins = (x, *moe, g, w_pad)
        in_specs = [row(D_MODEL), _moe_out_spec(tm), row(LANES), const(g), const(w_pad)]
        outs = [jax.ShapeDtypeStruct((n, D_MODEL), F32)] + outs
        out_specs = [row(D_MODEL)] + out_specs
    res = pl.pallas_call(
        functools.partial(_norm_proj_kernel, moe is not None),
        grid=(n // tm,), in_specs=in_specs, out_specs=out_specs, out_shape=outs,
        compiler_params=_cparams("parallel"),
    )(*ins)
    return res if moe is not None else [x] + list(res)


def _na_bias_table(rpb):
    qc = np.arange(GRID_W)
    kc = np.arange(GRID_W)
    d_col = np.clip(kc[None, :] - qc[:, None], -(NA_KW - 1), NA_KW - 1) + (NA_KW - 1)
    pick = np.zeros((2 * NA_KW - 1, GRID_W * GRID_W), np.float32)
    pick[d_col.reshape(-1), np.arange(GRID_W * GRID_W)] = 1.0
    win = np.clip(qc - NA_KW // 2, 0, GRID_W - NA_KW)
    ok = (kc[None, :] >= win[:, None]) & (kc[None, :] < win[:, None] + NA_KW)
    cols = jnp.einsum('hrc,cx->hrx', rpb.astype(F32), jnp.asarray(pick), precision=HIGHEST)
    cols = jnp.where(ok[None, None], cols.reshape(MIXER_HEADS, 2 * NA_KH - 1, GRID_W, GRID_W), NEG_BIG)
    per_e = [jnp.transpose(cols[:, NA_KH - 1 - e:2 * NA_KH - 1 - e], (0, 2, 1, 3)) for e in range(NA_KH)]
    return jnp.stack(per_e, axis=1).reshape(MIXER_HEADS, NA_KH, GRID_W, NA_KH * GRID_W)


def _na_kernel(rows, q_ref, k_ref, v_ref, t_ref, o_ref):
    i = pl.program_id(1)
    dh = HEAD_DIM

    def rows_body(it, carry):
        chains = []
        for k in range(NA_ROWS_LOCKSTEP):
            rr = it * NA_ROWS_LOCKSTEP + k
            r = i * NA_ROWS_PER_STEP + rr
            kr0 = jnp.clip(r - NA_KH // 2, 0, rows - NA_KH)
            q = q_ref[pl.ds(pl.multiple_of(rr * GRID_W, GRID_W), GRID_W), :]
            k0 = pl.multiple_of(kr0 * GRID_W, GRID_W)
            kb = k_ref[pl.ds(k0, NA_KH * GRID_W), :]
            vb = v_ref[pl.ds(k0, NA_KH * GRID_W), :]
            for h in range(MIXER_HEADS):
                sl = slice(h * dh, (h + 1) * dh)
                chains.append((q[:, sl], kb[:, sl], vb[:, sl], h, r - kr0))
        s = [_dot_nt(q, kb) * (dh ** -0.5) + t_ref[h, e] for q, kb, _, h, e in chains]
        m = [jnp.max(a, axis=-1, keepdims=True) for a in s]
        p = [jnp.exp(a - b) for a, b in zip(s, m)]
        l = [jnp.sum(a, axis=-1, keepdims=True) for a in p]
        o = [_dot(a.astype(BF16), c[2]) / b for a, b, c in zip(p, l, chains)]
        for k in range(NA_ROWS_LOCKSTEP):
            rr = it * NA_ROWS_LOCKSTEP + k
            row = jnp.concatenate(o[k * MIXER_HEADS:(k + 1) * MIXER_HEADS], axis=-1)
            o_ref[pl.ds(pl.multiple_of(rr * GRID_W, GRID_W), GRID_W), :] = row.astype(BF16)
        return carry

    lax.fori_loop(0, NA_ROWS_PER_STEP // NA_ROWS_LOCKSTEP, rows_body, 0)


def _neighbourhood_attention(p_a, table, b, s):
    rows = s // GRID_W
    assert rows >= NA_KH and rows % NA_ROWS_PER_STEP == 0
    steps = rows // NA_ROWS_PER_STEP
    tq = NA_ROWS_PER_STEP * GRID_W
    w = MIXER_WIDTH
    return pl.pallas_call(
        functools.partial(_na_kernel, rows),
        grid=(b, steps),
        in_specs=[pl.BlockSpec((tq, w), lambda bi, i: (bi * steps + i, 0)),
                  pl.BlockSpec((s, w), lambda bi, i: (bi, 1)),
                  pl.BlockSpec((s, w), lambda bi, i: (bi, 2)),
                  pl.BlockSpec(table.shape, lambda bi, i: (0, 0, 0, 0))],
        out_specs=pl.BlockSpec((tq, w), lambda bi, i: (bi * steps + i, 0)),
        out_shape=jax.ShapeDtypeStruct((b * s, w), BF16),
        compiler_params=_cparams("parallel", "arbitrary"),
    )(p_a, p_a, p_a, table)


def _retention_tables(s):
    h, dh, c = MIXER_HEADS, HEAD_DIM, RET_CHUNK
    half = dh // 2
    inv = ROPE_BASE ** (-jnp.arange(half, dtype=F32) / half)
    ang = jnp.arange(s, dtype=F32)[:, None] * inv[None, :]
    cos, sin = jnp.cos(ang), jnp.sin(ang)
    zero = jnp.zeros_like(sin)
    cos_t = jnp.tile(jnp.concatenate([cos, cos], axis=-1), (1, h))
    sin_lo = jnp.tile(jnp.concatenate([-sin, zero], axis=-1), (1, h))
    sin_hi = jnp.tile(jnp.concatenate([zero, sin], axis=-1), (1, h))
    log_f = np.log1p(-np.exp2(-5.0 - np.arange(h, dtype=np.float64)))
    log_b = log_f[::-1]
    pos = np.arange(c, dtype=np.float64)
    diff = pos[:, None] - pos[None, :]
    dmat = np.where(diff >= 0, np.exp(log_f[:, None, None] * np.maximum(diff, 0.0)), 0.0) \
        + np.where(diff < 0, np.exp(log_b[:, None, None] * np.maximum(-diff, 0.0)), 0.0)
    lanes = lambda t: np.repeat(t.T, dh, axis=1)
    dec = np.stack([lanes(np.exp(log_f[:, None] * (pos + 1.0))),
                    lanes(np.exp(log_f[:, None] * (c - 1.0 - pos))),
                    lanes(np.exp(log_b[:, None] * (c - pos))),
                    lanes(np.exp(log_b[:, None] * pos))])
    chunk_f = [float(np.exp(v * c)) for v in log_f]
    chunk_b = [float(np.exp(v * c)) for v in log_b]
    return cos_t, sin_lo, sin_hi, jnp.asarray(dmat, F32), jnp.asarray(dec, F32), chunk_f, chunk_b


def _rotary(x, cos, sin_lo, sin_hi):
    w = x.shape[-1]
    return x * cos + pltpu.roll(x, w - HEAD_DIM // 2, 1) * sin_lo + pltpu.roll(x, HEAD_DIM // 2, 1) * sin_hi


def _retention_kernel(nb, chunk_f, chunk_b, q_ref, k_ref, v_ref, gate_ref, cos_ref, slo_ref, shi_ref,
                      dmat_ref, dec_ref, ng_ref, o_ref, sf_ref, sb_ref, sball_ref):
    t = pl.program_id(1)
    dh, c, cps = HEAD_DIM, RET_CHUNK, RET_CHUNKS_PER_STEP
    cos, slo, shi = cos_ref[...], slo_ref[...], shi_ref[...]
    kr = _rotary(k_ref[...], cos, slo, shi) * (dh ** -0.5)
    vb = v_ref[...].astype(BF16)
    items = [(ci, h) for ci in range(cps) for h in range(MIXER_HEADS)]
    part = lambda x, ci, h: x[ci * c:(ci + 1) * c, h * dh:(h + 1) * dh]
    split = lambda x: {it: part(x, *it) for it in items}

    @pl.when(t == 0)
    def _():
        sf_ref[...] = jnp.zeros_like(sf_ref)
        sb_ref[...] = jnp.zeros_like(sb_ref)

    @pl.when(t < nb)
    def _():
        blk = nb - 1 - t
        kd, v = split((kr * dec_ref[3]).astype(BF16)), split(vb)
        kv = {it: _dot_tn(kd[it], v[it]) for it in items}
        for h in range(MIXER_HEADS):
            state = sb_ref[h]
            for ci in range(cps - 1, -1, -1):
                sball_ref[blk * cps + ci, h] = state
                state = state * chunk_b[h] + kv[ci, h]
            sb_ref[h] = state

    @pl.when(t >= nb)
    def _():
        blk = t - nb
        qr = _rotary(q_ref[...], cos, slo, shi)
        q, k, v = split(qr.astype(BF16)), split(kr.astype(BF16)), split(vb)
        qf, qbk = split((qr * dec_ref[0]).astype(BF16)), split((qr * dec_ref[2]).astype(BF16))
        kd = split((kr * dec_ref[1]).astype(BF16))
        sc = {it: (_dot_nt(q[it], k[it]) * dmat_ref[it[1]]).astype(BF16) for it in items}
        o = {it: _dot(sc[it], v[it]) for it in items}
        kv = {it: _dot_tn(kd[it], v[it]) for it in items}
        ob = {it: _dot(qbk[it], sball_ref[blk * cps + it[0], it[1]].astype(BF16)) for it in items}
        of = {}
        for h in range(MIXER_HEADS):
            state = sf_ref[h]
            for ci in range(cps):
                of[ci, h] = _dot(qf[ci, h], state.astype(BF16))
                state = state * chunk_f[h] + kv[ci, h]
            sf_ref[h] = state
        o = {it: o[it] + of[it] + ob[it] for it in items}
        mu = {it: jnp.mean(o[it], axis=-1, keepdims=True) for it in items}
        oc = {it: o[it] - mu[it] for it in items}
        var = {it: jnp.mean(oc[it] * oc[it], axis=-1, keepdims=True) for it in items}
        on = {it: oc[it] * lax.rsqrt(var[it] + EPS) for it in items}
        y = jnp.concatenate([jnp.concatenate([on[ci, h] for h in range(MIXER_HEADS)], axis=-1)
                             for ci in range(cps)], axis=0)
        o_ref[...] = (y * ng_ref[...] * _silu(gate_ref[...])).astype(BF16)


def _retention(p_b, norm_g, tables, b, s):
    cos_t, sin_lo, sin_hi, dmat, dec, chunk_f, chunk_b = tables
    cps = RET_CHUNKS_PER_STEP
    tb = RET_CHUNK * cps
    nb = s // tb
    w = MIXER_WIDTH
    dec = jnp.tile(dec, (1, cps, 1))
    block = lambda t: jnp.where(t < nb, nb - 1 - t, t - nb)
    col = lambda j: pl.BlockSpec((tb, w), lambda bi, t: (bi * nb + block(t), j))
    tab = pl.BlockSpec((tb, w), lambda bi, t: (block(t), 0))
    const = lambda a: pl.BlockSpec(a.shape, lambda bi, t: (0,) * a.ndim)
    return pl.pallas_call(
        functools.partial(_retention_kernel, nb, chunk_f, chunk_b),
        grid=(b, 2 * nb),
        in_specs=[col(0), col(1), col(2), col(3), tab, tab, tab, const(dmat), const(dec), const(norm_g)],
        out_specs=pl.BlockSpec((tb, w), lambda bi, t: (bi * nb + jnp.maximum(t - nb, 0), 0)),
        out_shape=jax.ShapeDtypeStruct((b * s, w), BF16),
        scratch_shapes=[pltpu.VMEM((MIXER_HEADS, HEAD_DIM, HEAD_DIM), F32),
                        pltpu.VMEM((MIXER_HEADS, HEAD_DIM, HEAD_DIM), F32),
                        pltpu.VMEM((s // RET_CHUNK, MIXER_HEADS, HEAD_DIM, HEAD_DIM), F32)],
        compiler_params=_cparams("arbitrary", "arbitrary"),
    )(p_b, p_b, p_b, p_b, cos_t, sin_lo, sin_hi, dmat, dec, norm_g)


def _halo_specs(tile, width, n_rows, block_of):
    per = tile // HALO
    last = n_rows // HALO - 1
    main = pl.BlockSpec((tile, width), lambda *g: (block_of(*g), 0))
    prev = pl.BlockSpec((HALO, width), lambda *g: (jnp.maximum(block_of(*g) * per - 1, 0), 0))
    nxt = pl.BlockSpec((HALO, width), lambda *g: (jnp.minimum((block_of(*g) + 1) * per, last), 0))
    return main, prev, nxt


def _fill_padded(pad_ref, main, prev, nxt, first, last):
    t = main.shape[0]
    pad_ref[0:HALO, :] = jnp.where(first, 0.0, prev)
    pad_ref[HALO:HALO + t, :] = main
    pad_ref[HALO + t:2 * HALO + t, :] = jnp.where(last, 0.0, nxt)


def _pool_kernel(s, tiles_per_seq, u_ref, prev_ref, next_ref, w_ref, scale_ref, o_ref, pad_ref):
    t = POOL_TILE
    tseq = pl.program_id(0) % tiles_per_seq
    u = u_ref[...]
    _fill_padded(pad_ref, u, prev_ref[...], next_ref[...], tseq == 0, tseq == tiles_per_seq - 1)
    lane = lax.broadcasted_iota(jnp.int32, (1, MIXER_WIDTH), 1)
    half = jnp.full((1, MIXER_WIDTH), POOL_WINDOWS[0] // 2, jnp.int32)
    for gi in range(1, len(POOL_WINDOWS)):
        half = jnp.where(lane >= gi * POOL_GROUP, POOL_WINDOWS[gi] // 2, half)
    max_half = POOL_WINDOWS[-1] // 2
    acc = jnp.zeros((t, MIXER_WIDTH), F32)
    for d in range(-max_half, max_half):
        inside = (d >= -half) & (d < half)
        acc = acc + jnp.where(inside, pad_ref[HALO + d:HALO + d + t, :], 0.0)
    pos = tseq * t + lax.broadcasted_iota(jnp.int32, (t, 1), 0)
    count = jnp.minimum(pos + half, s) - jnp.maximum(pos - half, 0)
    diff = acc / count.astype(F32) - u
    o_ref[...] = (_dot(diff.astype(BF16), w_ref[...]) * scale_ref[...]).astype(BF16)


def _pool(p_c, w_blockdiag, scale, b, s):
    t = POOL_TILE
    tiles_per_seq = s // t
    main, prev, nxt = _halo_specs(t, MIXER_WIDTH, b * s, lambda i: i)
    const = lambda a: pl.BlockSpec(a.shape, lambda i: (0,) * a.ndim)
    return pl.pallas_call(
        functools.partial(_pool_kernel, s, tiles_per_seq),
        grid=(b * tiles_per_seq,),
        in_specs=[main, prev, nxt, const(w_blockdiag), const(scale)],
        out_specs=pl.BlockSpec((t, MIXER_WIDTH), lambda i: (i, 0)),
        out_shape=jax.ShapeDtypeStruct((b * s, MIXER_WIDTH), BF16),
        scratch_shapes=[pltpu.VMEM((t + 2 * HALO, MIXER_WIDTH), F32)],
        compiler_params=_cparams("parallel"),
    )(p_c, p_c, p_c, w_blockdiag, scale)


def _head_blockdiag(x):
    lane = lax.broadcasted_iota(jnp.int32, x.shape, 1)
    zero = jnp.zeros_like(x)
    return jnp.concatenate([jnp.where(lane < HEAD_DIM, x, zero), jnp.where(lane >= HEAD_DIM, x, zero)], axis=0)


def _gdn_pre_kernel(tiles_per_seq, x_ref, prev_ref, next_ref, ab_ref, cw_ref, alog_ref, dtb_ref, ones_ref,
                    u_ref, gb_ref, pad_ref):
    t, w = GDN_PRE_TILE, MIXER_WIDTH
    tseq = pl.program_id(0) % tiles_per_seq
    _fill_padded(pad_ref, x_ref[...], prev_ref[...], next_ref[...], tseq == 0, tseq == tiles_per_seq - 1)
    acc = jnp.zeros((t, 3 * w), F32)
    for k in range(CONV_K):
        off = HALO + k - CONV_K // 2
        acc = acc + pad_ref[off:off + t, :] * cw_ref[k:k + 1, :]
    u = _silu(acc)
    head_sum = lambda a: sum(_dot(piece, ones_ref[...]) for piece in _split_bf16(a, 2))
    q, k = u[:, 0:w], u[:, w:2 * w]
    u_ref[:, 0:w] = q * lax.rsqrt(head_sum(q * q) + EPS) * (HEAD_DIM ** -0.5)
    u_ref[:, w:2 * w] = k * lax.rsqrt(head_sum(k * k) + EPS)
    u_ref[:, 2 * w:3 * w] = u[:, 2 * w:3 * w]
    ab = ab_ref[...]
    x = ab + dtb_ref[...]
    g = -jnp.exp(alog_ref[...]) * (jnp.maximum(x, 0.0) + jnp.log(1.0 + jnp.exp(-jnp.abs(x))))
    beta = 1.0 / (1.0 + jnp.exp(-ab))
    lane = lax.broadcasted_iota(jnp.int32, ab.shape, 1)
    gb_ref[...] = jnp.where(lane < 2 * MIXER_HEADS, g, beta)


def _gdn_pre(p_d, p_ab, conv_w, alog_vec, dtb_vec, b, s):
    t = GDN_PRE_TILE
    w = MIXER_WIDTH
    tiles_per_seq = s // t
    ones = _block_diag(jnp.ones((MIXER_HEADS, HEAD_DIM, HEAD_DIM), BF16))
    main, prev, nxt = _halo_specs(t, 3 * w, b * s, lambda i: i)
    const = lambda a: pl.BlockSpec(a.shape, lambda i: (0,) * a.ndim)
    row = lambda c: pl.BlockSpec((t, c), lambda i: (i, 0))
    return pl.pallas_call(
        functools.partial(_gdn_pre_kernel, tiles_per_seq),
        grid=(b * tiles_per_seq,),
        in_specs=[main, prev, nxt, row(LANES), const(conv_w), const(alog_vec), const(dtb_vec), const(ones)],
        out_specs=[row(3 * w), row(LANES)],
        out_shape=[jax.ShapeDtypeStruct((b * s, 3 * w), F32), jax.ShapeDtypeStruct((b * s, LANES), F32)],
        scratch_shapes=[pltpu.VMEM((t + 2 * HALO, 3 * w), F32)],
        compiler_params=_cparams("parallel"),
    )(p_d, p_d, p_d, p_ab, conv_w, alog_vec, dtb_vec, ones)


def _gdn_prepare(forward, u, gb, tri_col, tri_row):
    c, dh, w = GDN_CHUNK, HEAD_DIM, MIXER_WIDTH
    t = u.shape[0]
    a_off, b_off = (0, 2 * MIXER_HEADS) if forward else (MIXER_HEADS, 3 * MIXER_HEADS)
    g_all = beta_all = gb
    g_parts = _split_bf16(g_all, 3)
    gc_col = sum(_dot(tri_col, p) for p in g_parts)
    gc_row = sum(_dot_tn(p, tri_row) for p in g_parts)
    ii = lax.broadcasted_iota(jnp.int32, (c, 2 * dh), 0)
    lane = lax.broadcasted_iota(jnp.int32, (c, 2 * dh), 1)
    first = lane < dh
    jj = jnp.where(first, lane, lane - dh)
    incl = (ii >= jj) if forward else (ii <= jj)
    strict = (ii > jj) if forward else (ii < jj)
    last = c - 1 if forward else 0
    items = [(ci, p) for ci in range(t // c) for p in range(MIXER_HEADS // 2)]
    rows = lambda ci: slice(ci * c, (ci + 1) * c)
    part = lambda base: [u[rows(ci), base + 2 * p * dh:base + 2 * (p + 1) * dh] for ci, p in items]
    col_pair = lambda x, off: [jnp.where(first, x[rows(ci), off + 2 * p:off + 2 * p + 1],
                                         x[rows(ci), off + 2 * p + 1:off + 2 * p + 2]) for ci, p in items]
    q, k, v = part(0), part(w), part(2 * w)
    gcc = col_pair(gc_col, a_off)
    beta = col_pair(beta_all, b_off)
    gcr = [jnp.concatenate([gc_row[a_off + 2 * p:a_off + 2 * p + 1, rows(ci)],
                            gc_row[a_off + 2 * p + 1:a_off + 2 * p + 2, rows(ci)]], axis=1) for ci, p in items]
    g_last = [a[last:last + 1, :] for a in gcc]
    decay = [jnp.where(incl, jnp.exp(jnp.where(incl, a - b, 0.0)), 0.0) for a, b in zip(gcc, gcr)]
    kq = [_dot_nt(jnp.concatenate([kk, qq], axis=0).astype(BF16), _head_blockdiag(kk.astype(BF16)))
          for kk, qq in zip(k, q)]
    lmat = [jnp.where(strict, m[:c] * b * d, 0.0) for m, b, d in zip(kq, beta, decay)]
    attn = [(m[c:] * d).astype(BF16) for m, d in zip(kq, decay)]
    xinv = [-a for a in lmat]
    lp = [a.astype(BF16) for a in lmat]
    lp_bd = [_head_blockdiag(a) for a in lp]
    for _ in range(int(math.log2(c)) - 1):
        lp32 = [_dot(a, bd) for a, bd in zip(lp, lp_bd)]
        lp = [a.astype(BF16) for a in lp32]
        lp_bd = [_head_blockdiag(a) for a in lp]
        xinv = [xi + a32 + _dot(xi.astype(BF16), bd) for xi, a32, bd in zip(xinv, lp32, lp_bd)]
    egc = [jnp.exp(a) for a in gcc]
    xb = [a.astype(BF16) for a in xinv]
    vbeta = [vv * b for vv, b in zip(v, beta)]
    kbeta = [kk * (b * e) for kk, b, e in zip(k, beta, egc)]
    u_val = [r + _dot(xi, _head_blockdiag(r.astype(BF16))) for xi, r in zip(xb, vbeta)]
    w_dec = [r + _dot(xi, _head_blockdiag(r.astype(BF16))) for xi, r in zip(xb, kbeta)]
    wq = [jnp.concatenate([wd, qq * e], axis=0).astype(BF16) for wd, qq, e in zip(w_dec, q, egc)]
    k_state = [(kk * jnp.exp(gl - a)).astype(BF16) for kk, gl, a in zip(k, g_last, gcc)]
    e_last = [jnp.exp(gl) for gl in g_last]
    keyed = lambda vals: dict(zip(items, vals))
    return dict(wq=keyed(wq), u_val=keyed(u_val), attn=keyed(attn), k_state=keyed(k_state), e_last=keyed(e_last))


def _gdn_kernel(uf_ref, gbf_ref, ub_ref, gbb_ref, tril_ref, triu_ref, of_ref, ob_ref, sf_ref, sb_ref):
    n = pl.program_id(1)
    t = uf_ref.shape[0]
    c, dh = GDN_CHUNK, HEAD_DIM
    nch = t // c
    pairs = MIXER_HEADS // 2

    @pl.when(n == 0)
    def _():
        sf_ref[...] = jnp.zeros_like(sf_ref)
        sb_ref[...] = jnp.zeros_like(sb_ref)

    tril, triu = tril_ref[...], triu_ref[...]
    prep_f = _gdn_prepare(True, uf_ref[...], gbf_ref[...], tril, triu)
    prep_b = _gdn_prepare(False, ub_ref[...], gbb_ref[...], triu, tril)

    first = lax.broadcasted_iota(jnp.int32, (dh, 2 * dh), 1) < dh
    chains = [(prep_f, sf_ref, of_ref, p, True) for p in range(pairs)] + \
             [(prep_b, sb_ref, ob_ref, p, False) for p in range(pairs)]
    state = [s_ref[p] for _, s_ref, _, p, _ in chains]
    for step in range(nch):
        key = [(step if fw else nch - 1 - step, p) for _, _, _, p, fw in chains]
        ws = [_dot(pr['wq'][kk], _head_blockdiag(st.astype(BF16))) for (pr, *_), kk, st in zip(chains, key, state)]
        v_new = [pr['u_val'][kk] - a[:c] for (pr, *_), kk, a in zip(chains, key, ws)]
        vb = [a.astype(BF16) for a in v_new]
        out = [a[c:] + _dot(pr['attn'][kk], _head_blockdiag(b)) for (pr, *_), kk, a, b in zip(chains, key, ws, vb)]
        cross = [_dot_tn(pr['k_state'][kk], b) for (pr, *_), kk, b in zip(chains, key, vb)]
        state = [st * pr['e_last'][kk] + jnp.where(first, x[:dh], x[dh:])
                 for (pr, *_), kk, st, x in zip(chains, key, state, cross)]
        for (_, _, o_ref, p, _), (ci, _), o in zip(chains, key, out):
            o_ref[ci * c:(ci + 1) * c, 2 * p * dh:2 * (p + 1) * dh] = o
    for (_, s_ref, _, p, _), st in zip(chains, state):
        s_ref[p] = st


def _gated_deltanet(p_d, p_ab, conv_w, alog_vec, dtb_vec, b, s):
    u, gb = _gdn_pre(p_d, p_ab, conv_w, alog_vec, dtb_vec, b, s)
    c = GDN_CHUNK
    t = c * GDN_CHUNKS_PER_STEP
    nb = s // t
    w = MIXER_WIDTH
    r = np.arange(t)
    same = (r[:, None] // c) == (r[None, :] // c)
    tril = jnp.asarray(same & (r[:, None] >= r[None, :]), BF16)
    triu = jnp.asarray(same & (r[:, None] <= r[None, :]), BF16)
    fwd = lambda bi, n: (bi * nb + n, 0)
    bwd = lambda bi, n: (bi * nb + nb - 1 - n, 0)
    const = lambda a: pl.BlockSpec(a.shape, lambda bi, n: (0,) * a.ndim)
    return pl.pallas_call(
        _gdn_kernel,
        grid=(b, nb),
        in_specs=[pl.BlockSpec((t, 3 * w), fwd), pl.BlockSpec((t, LANES), fwd),
                  pl.BlockSpec((t, 3 * w), bwd), pl.BlockSpec((t, LANES), bwd), const(tril), const(triu)],
        out_specs=[pl.BlockSpec((t, w), fwd), pl.BlockSpec((t, w), bwd)],
        out_shape=[jax.ShapeDtypeStruct((b * s, w), F32), jax.ShapeDtypeStruct((b * s, w), F32)],
        scratch_shapes=[pltpu.VMEM((MIXER_HEADS // 2, HEAD_DIM, 2 * HEAD_DIM), F32),
                        pltpu.VMEM((MIXER_HEADS // 2, HEAD_DIM, 2 * HEAD_DIM), F32)],
        compiler_params=_cparams("arbitrary", "arbitrary"),
    )(u, gb, u, gb, tril, triu)


def _mem_kv_kernel(m_ref, g_ref, wk_ref, wv_ref, k_ref, v_ref):
    h = _rms(m_ref[0], g_ref[...]).astype(BF16)
    k_ref[0] = _dot(h, wk_ref[...]).astype(BF16)
    v_ref[0] = _dot(h, wv_ref[...]).astype(BF16)


def _mem_kv(mem, g, wk, wv):
    b, nm, d = mem.shape
    const = lambda a: pl.BlockSpec(a.shape, lambda i: (0,) * a.ndim)
    blk = pl.BlockSpec((1, nm, d), lambda i: (i, 0, 0))
    return pl.pallas_call(
        _mem_kv_kernel, grid=(b,),
        in_specs=[blk, const(g), const(wk), const(wv)], out_specs=[blk, blk],
        out_shape=[jax.ShapeDtypeStruct((b, nm, d), BF16)] * 2,
        compiler_params=_cparams("parallel"),
    )(mem, g, wk, wv)


def _route_lanes(lg):
    lane_i = lax.broadcasted_iota(jnp.int32, lg.shape, 1)
    lane = lane_i.astype(F32)
    first_at = lambda vals, m: jnp.min(jnp.where(vals == m, lane, float(LANES)), axis=-1, keepdims=True)
    g_mask = lane_i < N_GROUPS
    gl = jnp.where(g_mask, lg, NEG_BIG)
    g_max = jnp.max(gl, axis=-1, keepdims=True)
    g_sum = jnp.sum(jnp.where(g_mask, jnp.exp(gl - g_max), 0.0), axis=-1, keepdims=True)
    g_p = 1.0 / g_sum
    g_idx = first_at(gl, g_max)
    lo = N_GROUPS + EXPERTS_PER_GROUP * g_idx
    e_mask = (lane >= lo) & (lane < lo + EXPERTS_PER_GROUP)
    el = jnp.where(e_mask, lg, NEG_BIG)
    m1 = jnp.max(el, axis=-1, keepdims=True)
    i1 = first_at(el, m1)
    el2 = jnp.where(lane == i1, NEG_BIG, el)
    m2 = jnp.max(el2, axis=-1, keepdims=True)
    i2 = first_at(el2, m2)
    r = jnp.exp(m2 - m1)
    w1 = g_p / (1.0 + r)
    w2 = g_p * r / (1.0 + r)
    out = jnp.where(lane_i == 0, i1 - N_GROUPS, 0.0)
    out = jnp.where(lane_i == 1, i2 - N_GROUPS, out)
    out = jnp.where(lane_i == 2, w1, out)
    out = jnp.where(lane_i == 3, w2, out)
    chosen = jnp.where((lane == i1 - N_GROUPS) | (lane == i2 - N_GROUPS), 1.0, 0.0)
    return out, chosen


def _post_kernel(x_ref, ya_ref, yb_ref, yc_ref, of_ref, ob_ref, gate_ref, seg_ref, gg_ref, wout_ref, gq_ref,
                 wq_ref, k_ref, v_ref, wo_ref, gf_ref, wr_ref, br_ref, xo_ref, hx_ref, rt_ref, hist_ref):
    w = MIXER_WIDTH
    o = of_ref[...] + ob_ref[...]
    ms = sum(_dot(p, seg_ref[...]) for p in _split_bf16(o * o, 2))
    yd = o * lax.rsqrt(ms + EPS) * gg_ref[...] * _silu(gate_ref[...])
    mix = _dot(ya_ref[...], wout_ref[0:w, :]) + _dot(yb_ref[...], wout_ref[w:2 * w, :])
    mix = mix + _dot(yc_ref[...], wout_ref[2 * w:3 * w, :]) + _dot(yd.astype(BF16), wout_ref[3 * w:4 * w, :])
    x1 = x_ref[...] + mix
    q = _dot(_rms(x1, gq_ref[...]).astype(BF16), wq_ref[...]).astype(BF16)
    heads = []
    for h in range(MEM_HEADS):
        sl = slice(h * MEM_HEAD_DIM, (h + 1) * MEM_HEAD_DIM)
        sc = _dot_nt(q[:, sl], k_ref[0, :, sl]) * (MEM_HEAD_DIM ** -0.5)
        p = jnp.exp(sc - jnp.max(sc, axis=-1, keepdims=True))
        l = jnp.sum(p, axis=-1, keepdims=True)
        heads.append((_dot(p.astype(BF16), v_ref[0, :, sl]) / l).astype(BF16))
    x2 = x1 + _dot(jnp.concatenate(heads, axis=-1), wo_ref[...])
    xo_ref[...] = x2
    hx = _rms(x2, gf_ref[...])
    _to_row_tiles(hx_ref, hx)
    h_hi, h_lo = _split_bf16(hx, 2)
    lg = _dot(h_hi, wr_ref[0]) + _dot(h_hi, wr_ref[1]) + _dot(h_lo, wr_ref[0]) + br_ref[...]
    route, chosen = _route_lanes(lg)
    rt_ref[...] = route
    hist_ref[...] = jnp.broadcast_to(jnp.sum(chosen, axis=0, keepdims=True), hist_ref.shape)


def _post(x, y_a, y_b, y_c, o_f, o_b, p_d, seg, gdn_g, w_out, g_q, w_q, k_mem, v_mem, w_o, g_f, w_r, b_r, s):
    n = x.shape[0]
    tm = POST_TILE
    w = MIXER_WIDTH
    per_seq = s // tm
    row = lambda c: pl.BlockSpec((tm, c), lambda i: (i, 0))
    const = lambda a: pl.BlockSpec(a.shape, lambda i: (0,) * a.ndim)
    kv = pl.BlockSpec((1,) + k_mem.shape[1:], lambda i: (i // per_seq, 0, 0))
    return pl.pallas_call(
        _post_kernel, grid=(n // tm,),
        in_specs=[row(D_MODEL), row(w), row(w), row(w), row(w), row(w),
                  pl.BlockSpec((tm, w), lambda i: (i, 3)), const(seg), const(gdn_g), const(w_out), const(g_q),
                  const(w_q), kv, kv, const(w_o), const(g_f), const(w_r), const(b_r)],
        out_specs=[row(D_MODEL), pl.BlockSpec((tm * ROW_TILES, LANES), lambda i: (i, 0)), row(LANES),
                   pl.BlockSpec((SUBLANES, LANES), lambda i: (i, 0))],
        out_shape=[jax.ShapeDtypeStruct((n, D_MODEL), F32), jax.ShapeDtypeStruct((n * ROW_TILES, LANES), F32),
                   jax.ShapeDtypeStruct((n, LANES), F32), jax.ShapeDtypeStruct((n // tm * SUBLANES, LANES), F32)],
        compiler_params=_cparams("parallel"),
    )(x, y_a, y_b, y_c, o_f, o_b, p_d, seg, gdn_g, w_out, g_q, w_q, k_mem, v_mem, w_o, g_f, w_r, b_r)


def _dispatch(route, hist):
    n = route.shape[0]
    a = n * TOP_K
    blk = MOE_BLOCK
    flat_e = route[:, 0:TOP_K].astype(jnp.int32).reshape(a)
    _, order = lax.sort_key_val(flat_e, jnp.arange(a, dtype=jnp.int32))
    counts = jnp.sum(hist.reshape(-1, SUBLANES, LANES)[:, 0, :N_EXPERTS], axis=0).astype(jnp.int32)
    padded = (counts + blk - 1) // blk * blk
    pad_end = jnp.cumsum(padded)
    pad_start = pad_end - padded
    start = jnp.cumsum(counts) - counts
    n_blocks = a // blk + N_EXPERTS
    blk_start = jnp.arange(n_blocks, dtype=jnp.int32) * blk
    blk_expert = jnp.minimum(jnp.sum(pad_end[None, :] <= blk_start[:, None], axis=1), N_EXPERTS - 1).astype(jnp.int32)
    into_expert = blk_start - pad_start[blk_expert]
    blk_valid = jnp.clip(counts[blk_expert] - into_expert, 0, blk)
    in_blk = jnp.arange(blk, dtype=jnp.int32)[None, :]
    idx = (start[blk_expert] + into_expert)[:, None] + in_blk
    row_dst = jnp.where(in_blk < blk_valid[:, None], order[jnp.clip(idx, 0, a - 1)], a + in_blk)
    n_used = (pad_end[-1] // blk).astype(jnp.int32).reshape(1)
    return n_used, blk_expert, jnp.concatenate([a + in_blk, row_dst], axis=0).reshape(-1)


def _moe_kernel(n_tokens, n_blocks, nu_ref, be_ref, dst_ref, hx_hbm, wg_ref, wu_ref, wd_ref, y_hbm,
                xbuf, ybuf, gsem, ssem):
    j = pl.program_id(0)
    blk, rt = MOE_BLOCK, ROW_TILES
    n_used = nu_ref[0]
    slot = j % 2
    other = 1 - slot
    tile = lambda ref, i: ref.at[pl.ds(pl.multiple_of(i * rt, rt), rt), :]
    dst = lambda jj, i: dst_ref[(jj + 1) * blk + i]
    nxt = jnp.minimum(j + 1, n_blocks - 1)

    def gather_row(jj, sl, i):
        tok = jnp.minimum(lax.shift_right_logical(dst(jj, i), TOP_K // 2), n_tokens - 1)
        pltpu.make_async_copy(tile(hx_hbm, tok), tile(xbuf.at[sl], i), gsem.at[sl]).start()

    def scatter_row(jj, sl, i):
        pltpu.async_copy(tile(ybuf.at[sl], i), tile(y_hbm, dst(jj, i)), ssem.at[sl], priority=1)

    def wait_gather(sl):
        pltpu.make_async_copy(hx_hbm.at[pl.ds(0, blk * rt), :], xbuf.at[sl], gsem.at[sl]).wait()

    def wait_scatter(sl):
        pltpu.make_async_copy(ybuf.at[sl], y_hbm.at[pl.ds(0, blk * rt), :], ssem.at[sl]).wait()

    def rolled(fn):
        lax.fori_loop(0, blk, lambda i, carry: (fn(i), carry)[1], 0, unroll=DMA_UNROLL)

    @pl.when(j == 0)
    def _():
        ybuf[1] = jnp.zeros((blk * rt, LANES), F32)
        rolled(lambda i: gather_row(0, 0, i))

    @pl.when(j < n_used)
    def _():
        wait_gather(slot)

        @pl.when(j >= 1)
        def _():
            wait_scatter(slot)

        per = blk // MOE_PIECES

        def issue_copies(piece):
            for i in range(piece * per, (piece + 1) * per):
                gather_row(nxt, other, i)
                scatter_row(j - 1, other, i)

        xs = _from_row_tiles(xbuf.at[slot], 0, blk, rt).astype(BF16)
        half = D_EXPERT // 2
        acts = []
        for c in range(2):
            cols = slice(c * half, (c + 1) * half)
            g = _dot(xs, wg_ref[0, 0, :, cols].astype(BF16))
            issue_copies(2 * c)
            u = _dot(xs, wu_ref[0, 0, :, cols].astype(BF16))
            issue_copies(2 * c + 1)
            acts.append((_silu(g) * u).astype(BF16))
        h = jnp.concatenate(acts, axis=1)
        quarter = D_MODEL // 4
        for c in range(4):
            y = _dot(h, wd_ref[0, 0, :, c * quarter:(c + 1) * quarter].astype(BF16))
            for k in range(quarter // LANES):
                ybuf[slot, pl.ds(c * (quarter // LANES) + k, blk, stride=rt), :] = y[:, k * LANES:(k + 1) * LANES]
            issue_copies(4 + c)

    @pl.when(j == n_used)
    def _():
        wait_gather(slot)
        wait_scatter(slot)
        rolled(lambda i: scatter_row(j - 1, other, i))
        wait_scatter(other)


def _moe(hx_tiles, n_used, blk_expert, row_dst, w_gate, w_up, w_down, layer, n):
    blk = MOE_BLOCK
    n_blocks = blk_expert.shape[0]
    expert = lambda j, nu, be, dst: (layer, be[jnp.minimum(j, n_blocks - 1)], 0, 0)
    grid_spec = pltpu.PrefetchScalarGridSpec(
        num_scalar_prefetch=3, grid=(n_blocks + 1,),
        in_specs=[pl.BlockSpec(memory_space=pl.ANY),
                  pl.BlockSpec((1, 1, D_MODEL, D_EXPERT), expert),
                  pl.BlockSpec((1, 1, D_MODEL, D_EXPERT), expert),
                  pl.BlockSpec((1, 1, D_EXPERT, D_MODEL), expert)],
        out_specs=pl.BlockSpec(memory_space=pl.ANY),
        scratch_shapes=[pltpu.VMEM((2, blk * ROW_TILES, LANES), F32), pltpu.VMEM((2, blk * ROW_TILES, LANES), F32),
                        pltpu.SemaphoreType.DMA((2,)), pltpu.SemaphoreType.DMA((2,))])
    return pl.pallas_call(
        functools.partial(_moe_kernel, n, n_blocks), grid_spec=grid_spec,
        out_shape=jax.ShapeDtypeStruct(((n * TOP_K + blk) * ROW_TILES, LANES), F32),
        compiler_params=_cparams("arbitrary"),
    )(n_used, blk_expert, row_dst, hx_tiles, w_gate, w_up, w_down)


def _final_kernel(x_ref, y_ref, rt_ref, g_ref, o_ref):
    o_ref[...] = _rms(_add_moe(x_ref[...], y_ref, rt_ref), g_ref[...])


def _final(x, moe, g):
    n = x.shape[0]
    tm = PROJ_TILE
    return pl.pallas_call(
        _final_kernel, grid=(n // tm,),
        in_specs=[pl.BlockSpec((tm, D_MODEL), lambda i: (i, 0)), _moe_out_spec(tm),
                  pl.BlockSpec((tm, LANES), lambda i: (i, 0)), pl.BlockSpec(g.shape, lambda i: (0, 0))],
        out_specs=pl.BlockSpec((tm, D_MODEL), lambda i: (i, 0)),
        out_shape=jax.ShapeDtypeStruct((n, D_MODEL), F32),
        compiler_params=_cparams("parallel"),
    )(x, *moe, g)


def _lane_vec(v, width=LANES):
    v = v.reshape(1, -1).astype(F32)
    return jnp.pad(v, ((0, 0), (0, width - v.shape[1])))


def _block_diag(blocks):
    g, c, _ = blocks.shape
    eye = jnp.eye(g, dtype=blocks.dtype)
    return (eye[:, None, :, None] * blocks[:, :, None, :]).reshape(g * c, g * c)


def kernel(x, mem, mix_norm_g, w_in, na_rpb, ret_norm_g, pool_w, pool_scale, gdn_conv_w, gdn_a_log, gdn_dt_bias, gdn_norm_g, w_out, mem_q_norm_g, mem_kv_norm_g, mem_w_q, mem_w_k, mem_w_v, mem_w_o, ffn_norm_g, w_group, b_group, w_router, b_router, w_gate, w_up, w_down, final_norm_g):
    b, s, d = x.shape
    n = b * s
    depth = w_in.shape[0]
    row = lambda v: v.reshape(1, -1).astype(F32)
    ret_tables = _retention_tables(s)
    seg = _block_diag(jnp.full((MIXER_HEADS, HEAD_DIM, HEAD_DIM), 1.0 / HEAD_DIM, BF16))
    xs = x.reshape(n, d)
    moe = None
    for l in range(depth):
        w_pad = jnp.pad(w_in[l], ((0, 0), (0, P_IN_PAD - w_in.shape[2]))).astype(BF16)
        xs, p_a, p_b, p_c, p_d, p_ab = _norm_proj(xs, moe, row(mix_norm_g[l]), w_pad)
        y_a = _neighbourhood_attention(p_a, _na_bias_table(na_rpb[l]), b, s)
        y_b = _retention(p_b, row(ret_norm_g[l]), ret_tables, b, s)
        y_c = _pool(p_c, _block_diag(pool_w[l]).astype(BF16), row(pool_scale[l]), b, s)
        o_f, o_b = _gated_deltanet(p_d, p_ab, gdn_conv_w[l].astype(F32), _lane_vec(gdn_a_log[l]),
                                   _lane_vec(gdn_dt_bias[l]), b, s)
        k_mem, v_mem = _mem_kv(mem, row(mem_kv_norm_g[l]), mem_w_k[l].astype(BF16), mem_w_v[l].astype(BF16))
        w_r = jnp.pad(jnp.concatenate([w_group[l], w_router[l]], axis=1),
                      ((0, 0), (0, LANES - N_GROUPS - N_EXPERTS))).astype(F32)
        b_r = _lane_vec(jnp.concatenate([b_group[l], b_router[l]]))
        xs, hx_tiles, route, hist = _post(
            xs, y_a, y_b, y_c, o_f, o_b, p_d, seg, row(jnp.tile(gdn_norm_g[l], MIXER_HEADS)), w_out[l].astype(BF16),
            row(mem_q_norm_g[l]), mem_w_q[l].astype(BF16), k_mem, v_mem, mem_w_o[l].astype(BF16),
            row(ffn_norm_g[l]), jnp.stack(_split_bf16(w_r, 2)), b_r, s)
        y_moe = _moe(hx_tiles, *_dispatch(route, hist), w_gate, w_up, w_down, l, n)
        moe = (y_moe, route)
    return _final(xs, moe, row(final_norm_g)).reshape(b, s, d)
```

```python
import functools
import math

import jax
import jax.numpy as jnp
import numpy as np
from jax import lax
from jax.experimental import pallas as pl
from jax.experimental.pallas import tpu as pltpu

F32 = jnp.float32
BF16 = jnp.bfloat16
HIGHEST = lax.Precision.HIGHEST

D_MODEL = 1024
GRID_W = 64
HEAD_DIM = 64
MIXER_WIDTH = 256
MIXER_HEADS = 4
NA_KH = 8
NA_KW = 16
RET_CHUNK = 128
ROPE_BASE = 10000.0
POOL_WINDOWS = (2, 4, 8, 16)
POOL_GROUP = 64
GDN_CHUNK = 64
CONV_K = 4
MEM_HEADS = 4
MEM_HEAD_DIM = 256
N_GROUPS = 4
EXPERTS_PER_GROUP = 8
N_EXPERTS = 32
TOP_K = 2
D_EXPERT = 512
EPS = 1e-6

P_IN_PAD = 3200
V7X_VMEM_LIMIT = 56 * 1024 * 1024
LANES = 128
SUBLANES = 8
ROW_TILES = D_MODEL // LANES
NEG_BIG = -1e30

PROJ_TILE = 512
POOL_TILE = 512
POST_TILE = 512
NA_ROWS_PER_STEP = 8
NA_ROWS_LOCKSTEP = 4
GDN_CHUNKS_PER_STEP = 8
GDN_PRE_TILE = 512
RET_CHUNKS_PER_STEP = 4
MOE_BLOCK = 256
MOE_PIECES = 8
DMA_UNROLL = 8
HALO = 8


def _cparams(*sem):
    return pltpu.CompilerParams(dimension_semantics=sem, vmem_limit_bytes=V7X_VMEM_LIMIT)


def _dot(a, b):
    return jnp.dot(a, b, preferred_element_type=F32)


def _dot_nt(a, b):
    return lax.dot_general(a, b, (((1,), (1,)), ((), ())), preferred_element_type=F32)


def _dot_tn(a, b, precision=None):
    return lax.dot_general(a, b, (((0,), (0,)), ((), ())), preferred_element_type=F32, precision=precision)


def _silu(x):
    return x * (1.0 / (1.0 + jnp.exp(-x)))


def _rms(x, g):
    return x * lax.rsqrt(jnp.mean(x * x, axis=-1, keepdims=True) + EPS) * g


def _from_row_tiles(ref, first, count, stride):
    return jnp.concatenate([ref[pl.ds(first + k, count, stride=stride), :] for k in range(ROW_TILES)], axis=1)


def _to_row_tiles(ref, x):
    count = x.shape[0]
    for k in range(ROW_TILES):
        ref[pl.ds(k, count, stride=ROW_TILES), :] = x[:, k * LANES:(k + 1) * LANES]


def _add_moe(x, y_ref, rt_ref):
    t = x.shape[0]
    stride = TOP_K * ROW_TILES
    for slot in range(TOP_K):
        x = x + rt_ref[:, TOP_K + slot:TOP_K + slot + 1] * _from_row_tiles(y_ref, slot * ROW_TILES, t, stride)
    return x


def _split_bf16(x, parts):
    out = []
    for _ in range(parts):
        p = x.astype(BF16)
        out.append(p)
        x = x - p.astype(F32)
    return out


def _norm_proj_kernel(has_y, *refs):
    if has_y:
        x_ref, y_ref, rt_ref, g_ref, w_ref, xo_ref, pa_ref, pb_ref, pc_ref, pd_ref, pab_ref = refs
        x = _add_moe(x_ref[...], y_ref, rt_ref)
        xo_ref[...] = x
    else:
        x_ref, g_ref, w_ref, pa_ref, pb_ref, pc_ref, pd_ref, pab_ref = refs
        x = x_ref[...]
    h = _rms(x, g_ref[...]).astype(BF16)
    w = MIXER_WIDTH
    pa_ref[...] = _dot(h, w_ref[:, 0:3 * w]).astype(BF16)
    pb_ref[...] = _dot(h, w_ref[:, 3 * w:7 * w])
    pc_ref[...] = _dot(h, w_ref[:, 7 * w:8 * w])
    pd_ref[...] = _dot(h, w_ref[:, 8 * w:12 * w])
    pab_ref[...] = _dot(h, w_ref[:, 12 * w:P_IN_PAD])


def _moe_out_spec(tile):
    return pl.BlockSpec((tile * TOP_K * ROW_TILES, LANES), lambda i: (i, 0))


def _norm_proj(x, moe, g, w_pad):
    n = x.shape[0]
    tm = PROJ_TILE
    w = MIXER_WIDTH
    row = lambda c: pl.BlockSpec((tm, c), lambda i: (i, 0))
    const = lambda a: pl.BlockSpec(a.shape, lambda i: (0,) * a.ndim)
    outs = [jax.ShapeDtypeStruct((n, 3 * w), BF16), jax.ShapeDtypeStruct((n, 4 * w), F32),
            jax.ShapeDtypeStruct((n, w), F32), jax.ShapeDtypeStruct((n, 4 * w), F32),
            jax.ShapeDtypeStruct((n, LANES), F32)]
    out_specs = [row(3 * w), row(4 * w), row(w), row(4 * w), row(LANES)]
    if moe is None:
        ins, in_specs = (x, g, w_pad), [row(D_MODEL), const(g), const(w_pad)]
    else:
        ins = (x, *moe, g, w_pad)
        in_specs = [row(D_MODEL), _moe_out_spec(tm), row(LANES), const(g), const(w_pad)]
        outs = [jax.ShapeDtypeStruct((n, D_MODEL), F32)] + outs
        out_specs = [row(D_MODEL)] + out_specs
    res = pl.pallas_call(
        functools.partial(_norm_proj_kernel, moe is not None),
        grid=(n // tm,), in_specs=in_specs, out_specs=out_specs, out_shape=outs,
        compiler_params=_cparams("parallel"),
    )(*ins)
    return res if moe is not None else [x] + list(res)


def _na_bias_table(rpb):
    qc = np.arange(GRID_W)
    kc = np.arange(GRID_W)
    d_col = np.clip(kc[None, :] - qc[:, None], -(NA_KW - 1), NA_KW - 1) + (NA_KW - 1)
    pick = np.zeros((2 * NA_KW - 1, GRID_W * GRID_W), np.float32)
    pick[d_col.reshape(-1), np.arange(GRID_W * GRID_W)] = 1.0
    win = np.clip(qc - NA_KW // 2, 0, GRID_W - NA_KW)
    ok = (kc[None, :] >= win[:, None]) & (kc[None, :] < win[:, None] + NA_KW)
    cols = jnp.einsum('hrc,cx->hrx', rpb.astype(F32), jnp.asarray(pick), precision=HIGHEST)
    cols = jnp.where(ok[None, None], cols.reshape(MIXER_HEADS, 2 * NA_KH - 1, GRID_W, GRID_W), NEG_BIG)
    per_e = [jnp.transpose(cols[:, NA_KH - 1 - e:2 * NA_KH - 1 - e], (0, 2, 1, 3)) for e in range(NA_KH)]
    return jnp.stack(per_e, axis=1).reshape(MIXER_HEADS, NA_KH, GRID_W, NA_KH * GRID_W)


def _na_kernel(rows, q_ref, k_ref, v_ref, t_ref, o_ref):
    i = pl.program_id(1)
    dh = HEAD_DIM

    def rows_body(it, carry):
        chains = []
        for k in range(NA_ROWS_LOCKSTEP):
            rr = it * NA_ROWS_LOCKSTEP + k
            r = i * NA_ROWS_PER_STEP + rr
            kr0 = jnp.clip(r - NA_KH // 2, 0, rows - NA_KH)
            q = q_ref[pl.ds(pl.multiple_of(rr * GRID_W, GRID_W), GRID_W), :]
            k0 = pl.multiple_of(kr0 * GRID_W, GRID_W)
            kb = k_ref[pl.ds(k0, NA_KH * GRID_W), :]
            vb = v_ref[pl.ds(k0, NA_KH * GRID_W), :]
            for h in range(MIXER_HEADS):
                sl = slice(h * dh, (h + 1) * dh)
                chains.append((q[:, sl], kb[:, sl], vb[:, sl], h, r - kr0))
        s = [_dot_nt(q, kb) * (dh ** -0.5) + t_ref[h, e] for q, kb, _, h, e in chains]
        m = [jnp.max(a, axis=-1, keepdims=True) for a in s]
        p = [jnp.exp(a - b) for a, b in zip(s, m)]
        l = [jnp.sum(a, axis=-1, keepdims=True) for a in p]
        o = [_dot(a.astype(BF16), c[2]) / b for a, b, c in zip(p, l, chains)]
        for k in range(NA_ROWS_LOCKSTEP):
            rr = it * NA_ROWS_LOCKSTEP + k
            row = jnp.concatenate(o[k * MIXER_HEADS:(k + 1) * MIXER_HEADS], axis=-1)
            o_ref[pl.ds(pl.multiple_of(rr * GRID_W, GRID_W), GRID_W), :] = row.astype(BF16)
        return carry

    lax.fori_loop(0, NA_ROWS_PER_STEP // NA_ROWS_LOCKSTEP, rows_body, 0)


def _neighbourhood_attention(p_a, table, b, s):
    rows = s // GRID_W
    assert rows >= NA_KH and rows % NA_ROWS_PER_STEP == 0
    steps = rows // NA_ROWS_PER_STEP
    tq = NA_ROWS_PER_STEP * GRID_W
    w = MIXER_WIDTH
    return pl.pallas_call(
        functools.partial(_na_kernel, rows),
        grid=(b, steps),
        in_specs=[pl.BlockSpec((tq, w), lambda bi, i: (bi * steps + i, 0)),
                  pl.BlockSpec((s, w), lambda bi, i: (bi, 1)),
                  pl.BlockSpec((s, w), lambda bi, i: (bi, 2)),
                  pl.BlockSpec(table.shape, lambda bi, i: (0, 0, 0, 0))],
        out_specs=pl.BlockSpec((tq, w), lambda bi, i: (bi * steps + i, 0)),
        out_shape=jax.ShapeDtypeStruct((b * s, w), BF16),
        compiler_params=_cparams("parallel", "arbitrary"),
    )(p_a, p_a, p_a, table)


def _retention_tables(s):
    h, dh, c = MIXER_HEADS, HEAD_DIM, RET_CHUNK
    half = dh // 2
    inv = ROPE_BASE ** (-jnp.arange(half, dtype=F32) / half)
    ang = jnp.arange(s, dtype=F32)[:, None] * inv[None, :]
    cos, sin = jnp.cos(ang), jnp.sin(ang)
    zero = jnp.zeros_like(sin)
    cos_t = jnp.tile(jnp.concatenate([cos, cos], axis=-1), (1, h))
    sin_lo = jnp.tile(jnp.concatenate([-sin, zero], axis=-1), (1, h))
    sin_hi = jnp.tile(jnp.concatenate([zero, sin], axis=-1), (1, h))
    log_f = np.log1p(-np.exp2(-5.0 - np.arange(h, dtype=np.float64)))
    log_b = log_f[::-1]
    pos = np.arange(c, dtype=np.float64)
    diff = pos[:, None] - pos[None, :]
    dmat = np.where(diff >= 0, np.exp(log_f[:, None, None] * np.maximum(diff, 0.0)), 0.0) \
        + np.where(diff < 0, np.exp(log_b[:, None, None] * np.maximum(-diff, 0.0)), 0.0)
    lanes = lambda t: np.repeat(t.T, dh, axis=1)
    dec = np.stack([lanes(np.exp(log_f[:, None] * (pos + 1.0))),
                    lanes(np.exp(log_f[:, None] * (c - 1.0 - pos))),
                    lanes(np.exp(log_b[:, None] * (c - pos))),
                    lanes(np.exp(log_b[:, None] * pos))])
    chunk_f = [float(np.exp(v * c)) for v in log_f]
    chunk_b = [float(np.exp(v * c)) for v in log_b]
    return cos_t, sin_lo, sin_hi, jnp.asarray(dmat, F32), jnp.asarray(dec, F32), chunk_f, chunk_b


def _rotary(x, cos, sin_lo, sin_hi):
    w = x.shape[-1]
    return x * cos + pltpu.roll(x, w - HEAD_DIM // 2, 1) * sin_lo + pltpu.roll(x, HEAD_DIM // 2, 1) * sin_hi


def _retention_kernel(nb, chunk_f, chunk_b, q_ref, k_ref, v_ref, gate_ref, cos_ref, slo_ref, shi_ref,
                      dmat_ref, dec_ref, ng_ref, o_ref, sf_ref, sb_ref, sball_ref):
    t = pl.program_id(1)
    dh, c, cps = HEAD_DIM, RET_CHUNK, RET_CHUNKS_PER_STEP
    cos, slo, shi = cos_ref[...], slo_ref[...], shi_ref[...]
    kr = _rotary(k_ref[...], cos, slo, shi) * (dh ** -0.5)
    vb = v_ref[...].astype(BF16)
    items = [(ci, h) for ci in range(cps) for h in range(MIXER_HEADS)]
    part = lambda x, ci, h: x[ci * c:(ci + 1) * c, h * dh:(h + 1) * dh]
    split = lambda x: {it: part(x, *it) for it in items}

    @pl.when(t == 0)
    def _():
        sf_ref[...] = jnp.zeros_like(sf_ref)
        sb_ref[...] = jnp.zeros_like(sb_ref)

    @pl.when(t < nb)
    def _():
        blk = nb - 1 - t
        kd, v = split((kr * dec_ref[3]).astype(BF16)), split(vb)
        kv = {it: _dot_tn(kd[it], v[it]) for it in items}
        for h in range(MIXER_HEADS):
            state = sb_ref[h]
            for ci in range(cps - 1, -1, -1):
                sball_ref[blk * cps + ci, h] = state
                state = state * chunk_b[h] + kv[ci, h]
            sb_ref[h] = state

    @pl.when(t >= nb)
    def _():
        blk = t - nb
        qr = _rotary(q_ref[...], cos, slo, shi)
        q, k, v = split(qr.astype(BF16)), split(kr.astype(BF16)), split(vb)
        qf, qbk = split((qr * dec_ref[0]).astype(BF16)), split((qr * dec_ref[2]).astype(BF16))
        kd = split((kr * dec_ref[1]).astype(BF16))
        sc = {it: (_dot_nt(q[it], k[it]) * dmat_ref[it[1]]).astype(BF16) for it in items}
        o = {it: _dot(sc[it], v[it]) for it in items}
        kv = {it: _dot_tn(kd[it], v[it]) for it in items}
        ob = {it: _dot(qbk[it], sball_ref[blk * cps + it[0], it[1]].astype(BF16)) for it in items}
        of = {}
        for h in range(MIXER_HEADS):
            state = sf_ref[h]
            for ci in range(cps):
                of[ci, h] = _dot(qf[ci, h], state.astype(BF16))
                state = state * chunk_f[h] + kv[ci, h]
            sf_ref[h] = state
        o = {it: o[it] + of[it] + ob[it] for it in items}
        mu = {it: jnp.mean(o[it], axis=-1, keepdims=True) for it in items}
        oc = {it: o[it] - mu[it] for it in items}
        var = {it: jnp.mean(oc[it] * oc[it], axis=-1, keepdims=True) for it in items}
        on = {it: oc[it] * lax.rsqrt(var[it] + EPS) for it in items}
        y = jnp.concatenate([jnp.concatenate([on[ci, h] for h in range(MIXER_HEADS)], axis=-1)
                             for ci in range(cps)], axis=0)
        o_ref[...] = (y * ng_ref[...] * _silu(gate_ref[...])).astype(BF16)


def _retention(p_b, norm_g, tables, b, s):
    cos_t, sin_lo, sin_hi, dmat, dec, chunk_f, chunk_b = tables
    cps = RET_CHUNKS_PER_STEP
    tb = RET_CHUNK * cps
    nb = s // tb
    w = MIXER_WIDTH
    dec = jnp.tile(dec, (1, cps, 1))
    block = lambda t: jnp.where(t < nb, nb - 1 - t, t - nb)
    col = lambda j: pl.BlockSpec((tb, w), lambda bi, t: (bi * nb + block(t), j))
    tab = pl.BlockSpec((tb, w), lambda bi, t: (block(t), 0))
    const = lambda a: pl.BlockSpec(a.shape, lambda bi, t: (0,) * a.ndim)
    return pl.pallas_call(
        functools.partial(_retention_kernel, nb, chunk_f, chunk_b),
        grid=(b, 2 * nb),
        in_specs=[col(0), col(1), col(2), col(3), tab, tab, tab, const(dmat), const(dec), const(norm_g)],
        out_specs=pl.BlockSpec((tb, w), lambda bi, t: (bi * nb + jnp.maximum(t - nb, 0), 0)),
        out_shape=jax.ShapeDtypeStruct((b * s, w), BF16),
        scratch_shapes=[pltpu.VMEM((MIXER_HEADS, HEAD_DIM, HEAD_DIM), F32),
                        pltpu.VMEM((MIXER_HEADS, HEAD_DIM, HEAD_DIM), F32),
                        pltpu.VMEM((s // RET_CHUNK, MIXER_HEADS, HEAD_DIM, HEAD_DIM), F32)],
        compiler_params=_cparams("arbitrary", "arbitrary"),
    )(p_b, p_b, p_b, p_b, cos_t, sin_lo, sin_hi, dmat, dec, norm_g)


def _halo_specs(tile, width, n_rows, block_of):
    per = tile // HALO
    last = n_rows // HALO - 1
    main = pl.BlockSpec((tile, width), lambda *g: (block_of(*g), 0))
    prev = pl.BlockSpec((HALO, width), lambda *g: (jnp.maximum(block_of(*g) * per - 1, 0), 0))
    nxt = pl.BlockSpec((HALO, width), lambda *g: (jnp.minimum((block_of(*g) + 1) * per, last), 0))
    return main, prev, nxt


def _fill_padded(pad_ref, main, prev, nxt, first, last):
    t = main.shape[0]
    pad_ref[0:HALO, :] = jnp.where(first, 0.0, prev)
    pad_ref[HALO:HALO + t, :] = main
    pad_ref[HALO + t:2 * HALO + t, :] = jnp.where(last, 0.0, nxt)


def _pool_kernel(s, tiles_per_seq, u_ref, prev_ref, next_ref, w_ref, scale_ref, o_ref, pad_ref):
    t = POOL_TILE
    tseq = pl.program_id(0) % tiles_per_seq
    u = u_ref[...]
    _fill_padded(pad_ref, u, prev_ref[...], next_ref[...], tseq == 0, tseq == tiles_per_seq - 1)
    lane = lax.broadcasted_iota(jnp.int32, (1, MIXER_WIDTH), 1)
    half = jnp.full((1, MIXER_WIDTH), POOL_WINDOWS[0] // 2, jnp.int32)
    for gi in range(1, len(POOL_WINDOWS)):
        half = jnp.where(lane >= gi * POOL_GROUP, POOL_WINDOWS[gi] // 2, half)
    max_half = POOL_WINDOWS[-1] // 2
    acc = jnp.zeros((t, MIXER_WIDTH), F32)
    for d in range(-max_half, max_half):
        inside = (d >= -half) & (d < half)
        acc = acc + jnp.where(inside, pad_ref[HALO + d:HALO + d + t, :], 0.0)
    pos = tseq * t + lax.broadcasted_iota(jnp.int32, (t, 1), 0)
    count = jnp.minimum(pos + half, s) - jnp.maximum(pos - half, 0)
    diff = acc / count.astype(F32) - u
    o_ref[...] = (_dot(diff.astype(BF16), w_ref[...]) * scale_ref[...]).astype(BF16)


def _pool(p_c, w_blockdiag, scale, b, s):
    t = POOL_TILE
    tiles_per_seq = s // t
    main, prev, nxt = _halo_specs(t, MIXER_WIDTH, b * s, lambda i: i)
    const = lambda a: pl.BlockSpec(a.shape, lambda i: (0,) * a.ndim)
    return pl.pallas_call(
        functools.partial(_pool_kernel, s, tiles_per_seq),
        grid=(b * tiles_per_seq,),
        in_specs=[main, prev, nxt, const(w_blockdiag), const(scale)],
        out_specs=pl.BlockSpec((t, MIXER_WIDTH), lambda i: (i, 0)),
        out_shape=jax.ShapeDtypeStruct((b * s, MIXER_WIDTH), BF16),
        scratch_shapes=[pltpu.VMEM((t + 2 * HALO, MIXER_WIDTH), F32)],
        compiler_params=_cparams("parallel"),
    )(p_c, p_c, p_c, w_blockdiag, scale)


def _head_blockdiag(x):
    lane = lax.broadcasted_iota(jnp.int32, x.shape, 1)
    zero = jnp.zeros_like(x)
    return jnp.concatenate([jnp.where(lane < HEAD_DIM, x, zero), jnp.where(lane >= HEAD_DIM, x, zero)], axis=0)


def _gdn_pre_kernel(tiles_per_seq, x_ref, prev_ref, next_ref, ab_ref, cw_ref, alog_ref, dtb_ref, ones_ref,
                    u_ref, gb_ref, pad_ref):
    t, w = GDN_PRE_TILE, MIXER_WIDTH
    tseq = pl.program_id(0) % tiles_per_seq
    _fill_padded(pad_ref, x_ref[...], prev_ref[...], next_ref[...], tseq == 0, tseq == tiles_per_seq - 1)
    acc = jnp.zeros((t, 3 * w), F32)
    for k in range(CONV_K):
        off = HALO + k - CONV_K // 2
        acc = acc + pad_ref[off:off + t, :] * cw_ref[k:k + 1, :]
    u = _silu(acc)
    head_sum = lambda a: sum(_dot(piece, ones_ref[...]) for piece in _split_bf16(a, 2))
    q, k = u[:, 0:w], u[:, w:2 * w]
    u_ref[:, 0:w] = q * lax.rsqrt(head_sum(q * q) + EPS) * (HEAD_DIM ** -0.5)
    u_ref[:, w:2 * w] = k * lax.rsqrt(head_sum(k * k) + EPS)
    u_ref[:, 2 * w:3 * w] = u[:, 2 * w:3 * w]
    ab = ab_ref[...]
    x = ab + dtb_ref[...]
    g = -jnp.exp(alog_ref[...]) * (jnp.maximum(x, 0.0) + jnp.log(1.0 + jnp.exp(-jnp.abs(x))))
    beta = 1.0 / (1.0 + jnp.exp(-ab))
    lane = lax.broadcasted_iota(jnp.int32, ab.shape, 1)
    gb_ref[...] = jnp.where(lane < 2 * MIXER_HEADS, g, beta)


def _gdn_pre(p_d, p_ab, conv_w, alog_vec, dtb_vec, b, s):
    t = GDN_PRE_TILE
    w = MIXER_WIDTH
    tiles_per_seq = s // t
    ones = _block_diag(jnp.ones((MIXER_HEADS, HEAD_DIM, HEAD_DIM), BF16))
    main, prev, nxt = _halo_specs(t, 3 * w, b * s, lambda i: i)
    const = lambda a: pl.BlockSpec(a.shape, lambda i: (0,) * a.ndim)
    row = lambda c: pl.BlockSpec((t, c), lambda i: (i, 0))
    return pl.pallas_call(
        functools.partial(_gdn_pre_kernel, tiles_per_seq),
        grid=(b * tiles_per_seq,),
        in_specs=[main, prev, nxt, row(LANES), const(conv_w), const(alog_vec), const(dtb_vec), const(ones)],
        out_specs=[row(3 * w), row(LANES)],
        out_shape=[jax.ShapeDtypeStruct((b * s, 3 * w), F32), jax.ShapeDtypeStruct((b * s, LANES), F32)],
        scratch_shapes=[pltpu.VMEM((t + 2 * HALO, 3 * w), F32)],
        compiler_params=_cparams("parallel"),
    )(p_d, p_d, p_d, p_ab, conv_w, alog_vec, dtb_vec, ones)


def _gdn_prepare(forward, u, gb, tri_col, tri_row):
    c, dh, w = GDN_CHUNK, HEAD_DIM, MIXER_WIDTH
    t = u.shape[0]
    a_off, b_off = (0, 2 * MIXER_HEADS) if forward else (MIXER_HEADS, 3 * MIXER_HEADS)
    g_all = beta_all = gb
    g_parts = _split_bf16(g_all, 3)
    gc_col = sum(_dot(tri_col, p) for p in g_parts)
    gc_row = sum(_dot_tn(p, tri_row) for p in g_parts)
    ii = lax.broadcasted_iota(jnp.int32, (c, 2 * dh), 0)
    lane = lax.broadcasted_iota(jnp.int32, (c, 2 * dh), 1)
    first = lane < dh
    jj = jnp.where(first, lane, lane - dh)
    incl = (ii >= jj) if forward else (ii <= jj)
    strict = (ii > jj) if forward else (ii < jj)
    last = c - 1 if forward else 0
    items = [(ci, p) for ci in range(t // c) for p in range(MIXER_HEADS // 2)]
    rows = lambda ci: slice(ci * c, (ci + 1) * c)
    part = lambda base: [u[rows(ci), base + 2 * p * dh:base + 2 * (p + 1) * dh] for ci, p in items]
    col_pair = lambda x, off: [jnp.where(first, x[rows(ci), off + 2 * p:off + 2 * p + 1],
                                         x[rows(ci), off + 2 * p + 1:off + 2 * p + 2]) for ci, p in items]
    q, k, v = part(0), part(w), part(2 * w)
    gcc = col_pair(gc_col, a_off)
    beta = col_pair(beta_all, b_off)
    gcr = [jnp.concatenate([gc_row[a_off + 2 * p:a_off + 2 * p + 1, rows(ci)],
                            gc_row[a_off + 2 * p + 1:a_off + 2 * p + 2, rows(ci)]], axis=1) for ci, p in items]
    g_last = [a[last:last + 1, :] for a in gcc]
    decay = [jnp.where(incl, jnp.exp(jnp.where(incl, a - b, 0.0)), 0.0) for a, b in zip(gcc, gcr)]
    kq = [_dot_nt(jnp.concatenate([kk, qq], axis=0).astype(BF16), _head_blockdiag(kk.astype(BF16)))
          for kk, qq in zip(k, q)]
    lmat = [jnp.where(strict, m[:c] * b * d, 0.0) for m, b, d in zip(kq, beta, decay)]
    attn = [(m[c:] * d).astype(BF16) for m, d in zip(kq, decay)]
    xinv = [-a for a in lmat]
    lp = [a.astype(BF16) for a in lmat]
    lp_bd = [_head_blockdiag(a) for a in lp]
    for _ in range(int(math.log2(c)) - 1):
        lp32 = [_dot(a, bd) for a, bd in zip(lp, lp_bd)]
        lp = [a.astype(BF16) for a in lp32]
        lp_bd = [_head_blockdiag(a) for a in lp]
        xinv = [xi + a32 + _dot(xi.astype(BF16), bd) for xi, a32, bd in zip(xinv, lp32, lp_bd)]
    egc = [jnp.exp(a) for a in gcc]
    xb = [a.astype(BF16) for a in xinv]
    vbeta = [vv * b for vv, b in zip(v, beta)]
    kbeta = [kk * (b * e) for kk, b, e in zip(k, beta, egc)]
    u_val = [r + _dot(xi, _head_blockdiag(r.astype(BF16))) for xi, r in zip(xb, vbeta)]
    w_dec = [r + _dot(xi, _head_blockdiag(r.astype(BF16))) for xi, r in zip(xb, kbeta)]
    wq = [jnp.concatenate([wd, qq * e], axis=0).astype(BF16) for wd, qq, e in zip(w_dec, q, egc)]
    k_state = [(kk * jnp.exp(gl - a)).astype(BF16) for kk, gl, a in zip(k, g_last, gcc)]
    e_last = [jnp.exp(gl) for gl in g_last]
    keyed = lambda vals: dict(zip(items, vals))
    return dict(wq=keyed(wq), u_val=keyed(u_val), attn=keyed(attn), k_state=keyed(k_state), e_last=keyed(e_last))


def _gdn_kernel(uf_ref, gbf_ref, ub_ref, gbb_ref, tril_ref, triu_ref, of_ref, ob_ref, sf_ref, sb_ref):
    n = pl.program_id(1)
    t = uf_ref.shape[0]
    c, dh = GDN_CHUNK, HEAD_DIM
    nch = t // c
    pairs = MIXER_HEADS // 2

    @pl.when(n == 0)
    def _():
        sf_ref[...] = jnp.zeros_like(sf_ref)
        sb_ref[...] = jnp.zeros_like(sb_ref)

    tril, triu = tril_ref[...], triu_ref[...]
    prep_f = _gdn_prepare(True, uf_ref[...], gbf_ref[...], tril, triu)
    prep_b = _gdn_prepare(False, ub_ref[...], gbb_ref[...], triu, tril)

    first = lax.broadcasted_iota(jnp.int32, (dh, 2 * dh), 1) < dh
    chains = [(prep_f, sf_ref, of_ref, p, True) for p in range(pairs)] + \
             [(prep_b, sb_ref, ob_ref, p, False) for p in range(pairs)]
    state = [s_ref[p] for _, s_ref, _, p, _ in chains]
    for step in range(nch):
        key = [(step if fw else nch - 1 - step, p) for _, _, _, p, fw in chains]
        ws = [_dot(pr['wq'][kk], _head_blockdiag(st.astype(BF16))) for (pr, *_), kk, st in zip(chains, key, state)]
        v_new = [pr['u_val'][kk] - a[:c] for (pr, *_), kk, a in zip(chains, key, ws)]
        vb = [a.astype(BF16) for a in v_new]
        out = [a[c:] + _dot(pr['attn'][kk], _head_blockdiag(b)) for (pr, *_), kk, a, b in zip(chains, key, ws, vb)]
        cross = [_dot_tn(pr['k_state'][kk], b) for (pr, *_), kk, b in zip(chains, key, vb)]
        state = [st * pr['e_last'][kk] + jnp.where(first, x[:dh], x[dh:])
                 for (pr, *_), kk, st, x in zip(chains, key, state, cross)]
        for (_, _, o_ref, p, _), (ci, _), o in zip(chains, key, out):
            o_ref[ci * c:(ci + 1) * c, 2 * p * dh:2 * (p + 1) * dh] = o
    for (_, s_ref, _, p, _), st in zip(chains, state):
        s_ref[p] = st


def _gated_deltanet(p_d, p_ab, conv_w, alog_vec, dtb_vec, b, s):
    u, gb = _gdn_pre(p_d, p_ab, conv_w, alog_vec, dtb_vec, b, s)
    c = GDN_CHUNK
    t = c * GDN_CHUNKS_PER_STEP
    nb = s // t
    w = MIXER_WIDTH
    r = np.arange(t)
    same = (r[:, None] // c) == (r[None, :] // c)
    tril = jnp.asarray(same & (r[:, None] >= r[None, :]), BF16)
    triu = jnp.asarray(same & (r[:, None] <= r[None, :]), BF16)
    fwd = lambda bi, n: (bi * nb + n, 0)
    bwd = lambda bi, n: (bi * nb + nb - 1 - n, 0)
    const = lambda a: pl.BlockSpec(a.shape, lambda bi, n: (0,) * a.ndim)
    return pl.pallas_call(
        _gdn_kernel,
        grid=(b, nb),
        in_specs=[pl.BlockSpec((t, 3 * w), fwd), pl.BlockSpec((t, LANES), fwd),
                  pl.BlockSpec((t, 3 * w), bwd), pl.BlockSpec((t, LANES), bwd), const(tril), const(triu)],
        out_specs=[pl.BlockSpec((t, w), fwd), pl.BlockSpec((t, w), bwd)],
        out_shape=[jax.ShapeDtypeStruct((b * s, w), F32), jax.ShapeDtypeStruct((b * s, w), F32)],
        scratch_shapes=[pltpu.VMEM((MIXER_HEADS // 2, HEAD_DIM, 2 * HEAD_DIM), F32),
                        pltpu.VMEM((MIXER_HEADS // 2, HEAD_DIM, 2 * HEAD_DIM), F32)],
        compiler_params=_cparams("arbitrary", "arbitrary"),
    )(u, gb, u, gb, tril, triu)


def _mem_kv_kernel(m_ref, g_ref, wk_ref, wv_ref, k_ref, v_ref):
    h = _rms(m_ref[0], g_ref[...]).astype(BF16)
    k_ref[0] = _dot(h, wk_ref[...]).astype(BF16)
    v_ref[0] = _dot(h, wv_ref[...]).astype(BF16)


def _mem_kv(mem, g, wk, wv):
    b, nm, d = mem.shape
    const = lambda a: pl.BlockSpec(a.shape, lambda i: (0,) * a.ndim)
    blk = pl.BlockSpec((1, nm, d), lambda i: (i, 0, 0))
    return pl.pallas_call(
        _mem_kv_kernel, grid=(b,),
        in_specs=[blk, const(g), const(wk), const(wv)], out_specs=[blk, blk],
        out_shape=[jax.ShapeDtypeStruct((b, nm, d), BF16)] * 2,
        compiler_params=_cparams("parallel"),
    )(mem, g, wk, wv)


def _route_lanes(lg):
    lane_i = lax.broadcasted_iota(jnp.int32, lg.shape, 1)
    lane = lane_i.astype(F32)
    first_at = lambda vals, m: jnp.min(jnp.where(vals == m, lane, float(LANES)), axis=-1, keepdims=True)
    g_mask = lane_i < N_GROUPS
    gl = jnp.where(g_mask, lg, NEG_BIG)
    g_max = jnp.max(gl, axis=-1, keepdims=True)
    g_sum = jnp.sum(jnp.where(g_mask, jnp.exp(gl - g_max), 0.0), axis=-1, keepdims=True)
    g_p = 1.0 / g_sum
    g_idx = first_at(gl, g_max)
    lo = N_GROUPS + EXPERTS_PER_GROUP * g_idx
    e_mask = (lane >= lo) & (lane < lo + EXPERTS_PER_GROUP)
    el = jnp.where(e_mask, lg, NEG_BIG)
    m1 = jnp.max(el, axis=-1, keepdims=True)
    i1 = first_at(el, m1)
    el2 = jnp.where(lane == i1, NEG_BIG, el)
    m2 = jnp.max(el2, axis=-1, keepdims=True)
    i2 = first_at(el2, m2)
    r = jnp.exp(m2 - m1)
    w1 = g_p / (1.0 + r)
    w2 = g_p * r / (1.0 + r)
    out = jnp.where(lane_i == 0, i1 - N_GROUPS, 0.0)
    out = jnp.where(lane_i == 1, i2 - N_GROUPS, out)
    out = jnp.where(lane_i == 2, w1, out)
    out = jnp.where(lane_i == 3, w2, out)
    chosen = jnp.where((lane == i1 - N_GROUPS) | (lane == i2 - N_GROUPS), 1.0, 0.0)
    return out, chosen


def _post_kernel(x_ref, ya_ref, yb_ref, yc_ref, of_ref, ob_ref, gate_ref, seg_ref, gg_ref, wout_ref, gq_ref,
                 wq_ref, k_ref, v_ref, wo_ref, gf_ref, wr_ref, br_ref, xo_ref, hx_ref, rt_ref, hist_ref):
    w = MIXER_WIDTH
    o = of_ref[...] + ob_ref[...]
    ms = sum(_dot(p, seg_ref[...]) for p in _split_bf16(o * o, 2))
    yd = o * lax.rsqrt(ms + EPS) * gg_ref[...] * _silu(gate_ref[...])
    mix = _dot(ya_ref[...], wout_ref[0:w, :]) + _dot(yb_ref[...], wout_ref[w:2 * w, :])
    mix = mix + _dot(yc_ref[...], wout_ref[2 * w:3 * w, :]) + _dot(yd.astype(BF16), wout_ref[3 * w:4 * w, :])
    x1 = x_ref[...] + mix
    q = _dot(_rms(x1, gq_ref[...]).astype(BF16), wq_ref[...]).astype(BF16)
    cols = [slice(h * MEM_HEAD_DIM, (h + 1) * MEM_HEAD_DIM) for h in range(MEM_HEADS)]
    sc = [_dot_nt(q[:, sl], k_ref[0, :, sl]) * (MEM_HEAD_DIM ** -0.5) for sl in cols]
    p = [jnp.exp(a - jnp.max(a, axis=-1, keepdims=True)) for a in sc]
    l = [jnp.sum(a, axis=-1, keepdims=True) for a in p]
    heads = [(_dot(a.astype(BF16), v_ref[0, :, sl]) / b).astype(BF16) for a, b, sl in zip(p, l, cols)]
    x2 = x1 + _dot(jnp.concatenate(heads, axis=-1), wo_ref[...])
    xo_ref[...] = x2
    hx = _rms(x2, gf_ref[...])
    _to_row_tiles(hx_ref, hx)
    h_hi, h_lo = _split_bf16(hx, 2)
    lg = _dot(h_hi, wr_ref[0]) + _dot(h_hi, wr_ref[1]) + _dot(h_lo, wr_ref[0]) + br_ref[...]
    route, chosen = _route_lanes(lg)
    rt_ref[...] = route
    hist_ref[...] = jnp.broadcast_to(jnp.sum(chosen, axis=0, keepdims=True), hist_ref.shape)


def _post(x, y_a, y_b, y_c, o_f, o_b, p_d, seg, gdn_g, w_out, g_q, w_q, k_mem, v_mem, w_o, g_f, w_r, b_r, s):
    n = x.shape[0]
    tm = POST_TILE
    w = MIXER_WIDTH
    per_seq = s // tm
    row = lambda c: pl.BlockSpec((tm, c), lambda i: (i, 0))
    const = lambda a: pl.BlockSpec(a.shape, lambda i: (0,) * a.ndim)
    kv = pl.BlockSpec((1,) + k_mem.shape[1:], lambda i: (i // per_seq, 0, 0))
    return pl.pallas_call(
        _post_kernel, grid=(n // tm,),
        in_specs=[row(D_MODEL), row(w), row(w), row(w), row(w), row(w),
                  pl.BlockSpec((tm, w), lambda i: (i, 3)), const(seg), const(gdn_g), const(w_out), const(g_q),
                  const(w_q), kv, kv, const(w_o), const(g_f), const(w_r), const(b_r)],
        out_specs=[row(D_MODEL), pl.BlockSpec((tm * ROW_TILES, LANES), lambda i: (i, 0)), row(LANES),
                   pl.BlockSpec((SUBLANES, LANES), lambda i: (i, 0))],
        out_shape=[jax.ShapeDtypeStruct((n, D_MODEL), F32), jax.ShapeDtypeStruct((n * ROW_TILES, LANES), F32),
                   jax.ShapeDtypeStruct((n, LANES), F32), jax.ShapeDtypeStruct((n // tm * SUBLANES, LANES), F32)],
        compiler_params=_cparams("parallel"),
    )(x, y_a, y_b, y_c, o_f, o_b, p_d, seg, gdn_g, w_out, g_q, w_q, k_mem, v_mem, w_o, g_f, w_r, b_r)


def _dispatch(route, hist):
    n = route.shape[0]
    a = n * TOP_K
    blk = MOE_BLOCK
    flat_e = route[:, 0:TOP_K].astype(jnp.int32).reshape(a)
    _, order = lax.sort_key_val(flat_e, jnp.arange(a, dtype=jnp.int32))
    counts = jnp.sum(hist.reshape(-1, SUBLANES, LANES)[:, 0, :N_EXPERTS], axis=0).astype(jnp.int32)
    padded = (counts + blk - 1) // blk * blk
    pad_end = jnp.cumsum(padded)
    pad_start = pad_end - padded
    start = jnp.cumsum(counts) - counts
    n_blocks = a // blk + N_EXPERTS
    blk_start = jnp.arange(n_blocks, dtype=jnp.int32) * blk
    blk_expert = jnp.minimum(jnp.sum(pad_end[None, :] <= blk_start[:, None], axis=1), N_EXPERTS - 1).astype(jnp.int32)
    into_expert = blk_start - pad_start[blk_expert]
    blk_valid = jnp.clip(counts[blk_expert] - into_expert, 0, blk)
    in_blk = jnp.arange(blk, dtype=jnp.int32)[None, :]
    idx = (start[blk_expert] + into_expert)[:, None] + in_blk
    row_dst = jnp.where(in_blk < blk_valid[:, None], order[jnp.clip(idx, 0, a - 1)], a + in_blk)
    n_used = (pad_end[-1] // blk).astype(jnp.int32).reshape(1)
    return n_used, blk_expert, jnp.concatenate([a + in_blk, row_dst], axis=0).reshape(-1)


def _moe_kernel(n_tokens, n_blocks, nu_ref, be_ref, dst_ref, hx_hbm, wg_ref, wu_ref, wd_ref, y_hbm,
                xbuf, ybuf, gsem, ssem):
    j = pl.program_id(0)
    blk, rt = MOE_BLOCK, ROW_TILES
    n_used = nu_ref[0]
    slot = j % 2
    other = 1 - slot
    tile = lambda ref, i: ref.at[pl.ds(pl.multiple_of(i * rt, rt), rt), :]
    dst = lambda jj, i: dst_ref[(jj + 1) * blk + i]
    nxt = jnp.minimum(j + 1, n_blocks - 1)

    def gather_row(jj, sl, i, queue=0):
        tok = jnp.minimum(lax.shift_right_logical(dst(jj, i), TOP_K // 2), n_tokens - 1)
        pltpu.async_copy(tile(hx_hbm, tok), tile(xbuf.at[sl], i), gsem.at[sl], priority=queue)

    def scatter_row(jj, sl, i):
        pltpu.make_async_copy(tile(ybuf.at[sl], i), tile(y_hbm, dst(jj, i)), ssem.at[sl]).start()

    def wait_gather(sl):
        pltpu.make_async_copy(hx_hbm.at[pl.ds(0, blk * rt), :], xbuf.at[sl], gsem.at[sl]).wait()

    def wait_scatter(sl):
        pltpu.make_async_copy(ybuf.at[sl], y_hbm.at[pl.ds(0, blk * rt), :], ssem.at[sl]).wait()

    def rolled(fn):
        lax.fori_loop(0, blk, lambda i, carry: (fn(i), carry)[1], 0, unroll=DMA_UNROLL)

    @pl.when(j == 0)
    def _():
        ybuf[1] = jnp.zeros((blk * rt, LANES), F32)
        rolled(lambda i: gather_row(0, 0, i))

    @pl.when(j < n_used)
    def _():
        wait_gather(slot)

        @pl.when(j >= 1)
        def _():
            wait_scatter(slot)

        per = blk // MOE_PIECES

        def issue_copies(piece):
            for i in range(piece * per, (piece + 1) * per):
                gather_row(nxt, other, i, queue=i % 2)
                scatter_row(j - 1, other, i)

        xs = _from_row_tiles(xbuf.at[slot], 0, blk, rt).astype(BF16)
        half = D_EXPERT // 2
        acts = []
        for c in range(2):
            cols = slice(c * half, (c + 1) * half)
            g = _dot(xs, wg_ref[0, 0, :, cols].astype(BF16))
            issue_copies(2 * c)
            u = _dot(xs, wu_ref[0, 0, :, cols].astype(BF16))
            issue_copies(2 * c + 1)
            acts.append((_silu(g) * u).astype(BF16))
        h = jnp.concatenate(acts, axis=1)
        quarter = D_MODEL // 4
        for c in range(4):
            y = _dot(h, wd_ref[0, 0, :, c * quarter:(c + 1) * quarter].astype(BF16))
            for k in range(quarter // LANES):
                ybuf[slot, pl.ds(c * (quarter // LANES) + k, blk, stride=rt), :] = y[:, k * LANES:(k + 1) * LANES]
            issue_copies(4 + c)

    @pl.when(j == n_used)
    def _():
        wait_gather(slot)
        wait_scatter(slot)
        rolled(lambda i: scatter_row(j - 1, other, i))
        wait_scatter(other)


def _moe(hx_tiles, n_used, blk_expert, row_dst, w_gate, w_up, w_down, layer, n):
    blk = MOE_BLOCK
    n_blocks = blk_expert.shape[0]
    expert = lambda j, nu, be, dst: (layer, be[jnp.minimum(j, n_blocks - 1)], 0, 0)
    grid_spec = pltpu.PrefetchScalarGridSpec(
        num_scalar_prefetch=3, grid=(n_blocks + 1,),
        in_specs=[pl.BlockSpec(memory_space=pl.ANY),
                  pl.BlockSpec((1, 1, D_MODEL, D_EXPERT), expert),
                  pl.BlockSpec((1, 1, D_MODEL, D_EXPERT), expert),
                  pl.BlockSpec((1, 1, D_EXPERT, D_MODEL), expert)],
        out_specs=pl.BlockSpec(memory_space=pl.ANY),
        scratch_shapes=[pltpu.VMEM((2, blk * ROW_TILES, LANES), F32), pltpu.VMEM((2, blk * ROW_TILES, LANES), F32),
                        pltpu.SemaphoreType.DMA((2,)), pltpu.SemaphoreType.DMA((2,))])
    return pl.pallas_call(
        functools.partial(_moe_kernel, n, n_blocks), grid_spec=grid_spec,
        out_shape=jax.ShapeDtypeStruct(((n * TOP_K + blk) * ROW_TILES, LANES), F32),
        compiler_params=_cparams("arbitrary"),
    )(n_used, blk_expert, row_dst, hx_tiles, w_gate, w_up, w_down)


def _final_kernel(x_ref, y_ref, rt_ref, g_ref, o_ref):
    o_ref[...] = _rms(_add_moe(x_ref[...], y_ref, rt_ref), g_ref[...])


def _final(x, moe, g):
    n = x.shape[0]
    tm = PROJ_TILE
    return pl.pallas_call(
        _final_kernel, grid=(n // tm,),
        in_specs=[pl.BlockSpec((tm, D_MODEL), lambda i: (i, 0)), _moe_out_spec(tm),
                  pl.BlockSpec((tm, LANES), lambda i: (i, 0)), pl.BlockSpec(g.shape, lambda i: (0, 0))],
        out_specs=pl.BlockSpec((tm, D_MODEL), lambda i: (i, 0)),
        out_shape=jax.ShapeDtypeStruct((n, D_MODEL), F32),
        compiler_params=_cparams("parallel"),
    )(x, *moe, g)


def _lane_vec(v, width=LANES):
    v = v.reshape(1, -1).astype(F32)
    return jnp.pad(v, ((0, 0), (0, width - v.shape[1])))


def _block_diag(blocks):
    g, c, _ = blocks.shape
    eye = jnp.eye(g, dtype=blocks.dtype)
    return (eye[:, None, :, None] * blocks[:, :, None, :]).reshape(g * c, g * c)


def kernel(x, mem, mix_norm_g, w_in, na_rpb, ret_norm_g, pool_w, pool_scale, gdn_conv_w, gdn_a_log, gdn_dt_bias, gdn_norm_g, w_out, mem_q_norm_g, mem_kv_norm_g, mem_w_q, mem_w_k, mem_w_v, mem_w_o, ffn_norm_g, w_group, b_group, w_router, b_router, w_gate, w_up, w_down, final_norm_g):
    b, s, d = x.shape
    n = b * s
    depth = w_in.shape[0]
    row = lambda v: v.reshape(1, -1).astype(F32)
    ret_tables = _retention_tables(s)
    seg = _block_diag(jnp.full((MIXER_HEADS, HEAD_DIM, HEAD_DIM), 1.0 / HEAD_DIM, BF16))
    xs = x.reshape(n, d)
    moe = None
    for l in range(depth):
        w_pad = jnp.pad(w_in[l], ((0, 0), (0, P_IN_PAD - w_in.shape[2]))).astype(BF16)
        xs, p_a, p_b, p_c, p_d, p_ab = _norm_proj(xs, moe, row(mix_norm_g[l]), w_pad)
        y_a = _neighbourhood_attention(p_a, _na_bias_table(na_rpb[l]), b, s)
        y_b = _retention(p_b, row(ret_norm_g[l]), ret_tables, b, s)
        y_c = _pool(p_c, _block_diag(pool_w[l]).astype(BF16), row(pool_scale[l]), b, s)
        o_f, o_b = _gated_deltanet(p_d, p_ab, gdn_conv_w[l].astype(F32), _lane_vec(gdn_a_log[l]),
                                   _lane_vec(gdn_dt_bias[l]), b, s)
        k_mem, v_mem = _mem_kv(mem, row(mem_kv_norm_g[l]), mem_w_k[l].astype(BF16), mem_w_v[l].astype(BF16))
        w_r = jnp.pad(jnp.concatenate([w_group[l], w_router[l]], axis=1),
                      ((0, 0), (0, LANES - N_GROUPS - N_EXPERTS))).astype(F32)
        b_r = _lane_vec(jnp.concatenate([b_group[l], b_router[l]]))
        xs, hx_tiles, route, hist = _post(
            xs, y_a, y_b, y_c, o_f, o_b, p_d, seg, row(jnp.tile(gdn_norm_g[l], MIXER_HEADS)), w_out[l].astype(BF16),
            row(mem_q_norm_g[l]), mem_w_q[l].astype(BF16), k_mem, v_mem, mem_w_o[l].astype(BF16),
            row(ffn_norm_g[l]), jnp.stack(_split_bf16(w_r, 2)), b_r, s)
        y_moe = _moe(hx_tiles, *_dispatch(route, hist), w_gate, w_up, w_down, l, n)
        moe = (y_moe, route)
    return _final(xs, moe, row(final_norm_g)).reshape(b, s, d)
```

```python
import functools
import math

import jax
import jax.numpy as jnp
import numpy as np
from jax import lax
from jax.experimental import pallas as pl
from jax.experimental.pallas import tpu as pltpu

F32 = jnp.float32
BF16 = jnp.bfloat16
HIGHEST = lax.Precision.HIGHEST

D_MODEL = 1024
GRID_W = 64
HEAD_DIM = 64
MIXER_WIDTH = 256
MIXER_HEADS = 4
NA_KH = 8
NA_KW = 16
RET_CHUNK = 128
ROPE_BASE = 10000.0
POOL_WINDOWS = (2, 4, 8, 16)
POOL_GROUP = 64
GDN_CHUNK = 64
CONV_K = 4
MEM_HEADS = 4
MEM_HEAD_DIM = 256
N_GROUPS = 4
EXPERTS_PER_GROUP = 8
N_EXPERTS = 32
TOP_K = 2
D_EXPERT = 512
EPS = 1e-6

P_IN_PAD = 3200
V7X_VMEM_LIMIT = 56 * 1024 * 1024
LANES = 128
SUBLANES = 8
ROW_TILES = D_MODEL // LANES
NEG_BIG = -1e30

PROJ_TILE = 512
POOL_TILE = 512
POST_TILE = 512
NA_ROWS_PER_STEP = 8
NA_ROWS_LOCKSTEP = 4
GDN_CHUNKS_PER_STEP = 8
GDN_PRE_TILE = 512
RET_CHUNKS_PER_STEP = 4
PAIRS_PER_GROUP = EXPERTS_PER_GROUP * (EXPERTS_PER_GROUP - 1) // 2
N_CLASSES = N_GROUPS * PAIRS_PER_GROUP
MOE_BLOCK = 128
MOE_PIECES = 8
DMA_UNROLL = 8
HALO = 8


def _cparams(*sem):
    return pltpu.CompilerParams(dimension_semantics=sem, vmem_limit_bytes=V7X_VMEM_LIMIT)


def _dot(a, b):
    return jnp.dot(a, b, preferred_element_type=F32)


def _dot_nt(a, b):
    return lax.dot_general(a, b, (((1,), (1,)), ((), ())), preferred_element_type=F32)


def _dot_tn(a, b, precision=None):
    return lax.dot_general(a, b, (((0,), (0,)), ((), ())), preferred_element_type=F32, precision=precision)


def _silu(x):
    return x * (1.0 / (1.0 + jnp.exp(-x)))


def _rms(x, g):
    return x * lax.rsqrt(jnp.mean(x * x, axis=-1, keepdims=True) + EPS) * g


def _from_row_tiles(ref, first, count, stride):
    return jnp.concatenate([ref[pl.ds(first + k, count, stride=stride), :] for k in range(ROW_TILES)], axis=1)


def _to_row_tiles(ref, x):
    count = x.shape[0]
    for k in range(ROW_TILES):
        ref[pl.ds(k, count, stride=ROW_TILES), :] = x[:, k * LANES:(k + 1) * LANES]


def _add_moe(x, y_ref):
    return x + _from_row_tiles(y_ref, 0, x.shape[0], ROW_TILES)


def _split_bf16(x, parts):
    out = []
    for _ in range(parts):
        p = x.astype(BF16)
        out.append(p)
        x = x - p.astype(F32)
    return out


def _norm_proj_kernel(has_y, *refs):
    if has_y:
        x_ref, y_ref, g_ref, w_ref, xo_ref, pa_ref, pb_ref, pc_ref, pd_ref, pab_ref = refs
        x = _add_moe(x_ref[...], y_ref)
        xo_ref[...] = x
    else:
        x_ref, g_ref, w_ref, pa_ref, pb_ref, pc_ref, pd_ref, pab_ref = refs
        x = x_ref[...]
    h = _rms(x, g_ref[...]).astype(BF16)
    w = MIXER_WIDTH
    pa_ref[...] = _dot(h, w_ref[:, 0:3 * w]).astype(BF16)
    pb_ref[...] = _dot(h, w_ref[:, 3 * w:7 * w])
    pc_ref[...] = _dot(h, w_ref[:, 7 * w:8 * w])
    pd_ref[...] = _dot(h, w_ref[:, 8 * w:12 * w])
    pab_ref[...] = _dot(h, w_ref[:, 12 * w:P_IN_PAD])


def _moe_out_spec(tile):
    return pl.BlockSpec((tile * ROW_TILES, LANES), lambda i: (i, 0))


def _norm_proj(x, moe, g, w_pad):
    n = x.shape[0]
    tm = PROJ_TILE
    w = MIXER_WIDTH
    row = lambda c: pl.BlockSpec((tm, c), lambda i: (i, 0))
    const = lambda a: pl.BlockSpec(a.shape, lambda i: (0,) * a.ndim)
    outs = [jax.ShapeDtypeStruct((n, 3 * w), BF16), jax.ShapeDtypeStruct((n, 4 * w), F32),
            jax.ShapeDtypeStruct((n, w), F32), jax.ShapeDtypeStruct((n, 4 * w), F32),
            jax.ShapeDtypeStruct((n, LANES), F32)]
    out_specs = [row(3 * w), row(4 * w), row(w), row(4 * w), row(LANES)]
    if moe is None:
        ins, in_specs = (x, g, w_pad), [row(D_MODEL), const(g), const(w_pad)]
    else:
        ins = (x, moe, g, w_pad)
        in_specs = [row(D_MODEL), _moe_out_spec(tm), const(g), const(w_pad)]
        outs = [jax.ShapeDtypeStruct((n, D_MODEL), F32)] + outs
        out_specs = [row(D_MODEL)] + out_specs
    res = pl.pallas_call(
        functools.partial(_norm_proj_kernel, moe is not None),
        grid=(n // tm,), in_specs=in_specs, out_specs=out_specs, out_shape=outs,
        compiler_params=_cparams("parallel"),
    )(*ins)
    return res if moe is not None else [x] + list(res)


def _na_bias_table(rpb):
    qc = np.arange(GRID_W)
    kc = np.arange(GRID_W)
    d_col = np.clip(kc[None, :] - qc[:, None], -(NA_KW - 1), NA_KW - 1) + (NA_KW - 1)
    pick = np.zeros((2 * NA_KW - 1, GRID_W * GRID_W), np.float32)
    pick[d_col.reshape(-1), np.arange(GRID_W * GRID_W)] = 1.0
    win = np.clip(qc - NA_KW // 2, 0, GRID_W - NA_KW)
    ok = (kc[None, :] >= win[:, None]) & (kc[None, :] < win[:, None] + NA_KW)
    cols = jnp.einsum('hrc,cx->hrx', rpb.astype(F32), jnp.asarray(pick), precision=HIGHEST)
    cols = jnp.where(ok[None, None], cols.reshape(MIXER_HEADS, 2 * NA_KH - 1, GRID_W, GRID_W), NEG_BIG)
    per_e = [jnp.transpose(cols[:, NA_KH - 1 - e:2 * NA_KH - 1 - e], (0, 2, 1, 3)) for e in range(NA_KH)]
    return jnp.stack(per_e, axis=1).reshape(MIXER_HEADS, NA_KH, GRID_W, NA_KH * GRID_W)


def _na_kernel(rows, q_ref, k_ref, v_ref, t_ref, o_ref):
    i = pl.program_id(1)
    dh = HEAD_DIM

    def rows_body(it, carry):
        chains = []
        for k in range(NA_ROWS_LOCKSTEP):
            rr = it * NA_ROWS_LOCKSTEP + k
            r = i * NA_ROWS_PER_STEP + rr
            kr0 = jnp.clip(r - NA_KH // 2, 0, rows - NA_KH)
            q = q_ref[pl.ds(pl.multiple_of(rr * GRID_W, GRID_W), GRID_W), :]
            k0 = pl.multiple_of(kr0 * GRID_W, GRID_W)
            kb = k_ref[pl.ds(k0, NA_KH * GRID_W), :]
            vb = v_ref[pl.ds(k0, NA_KH * GRID_W), :]
            for h in range(MIXER_HEADS):
                sl = slice(h * dh, (h + 1) * dh)
                chains.append((q[:, sl], kb[:, sl], vb[:, sl], h, r - kr0))
        s = [_dot_nt(q, kb) * (dh ** -0.5) + t_ref[h, e] for q, kb, _, h, e in chains]
        m = [jnp.max(a, axis=-1, keepdims=True) for a in s]
        p = [jnp.exp(a - b) for a, b in zip(s, m)]
        l = [jnp.sum(a, axis=-1, keepdims=True) for a in p]
        o = [_dot(a.astype(BF16), c[2]) / b for a, b, c in zip(p, l, chains)]
        for k in range(NA_ROWS_LOCKSTEP):
            rr = it * NA_ROWS_LOCKSTEP + k
            row = jnp.concatenate(o[k * MIXER_HEADS:(k + 1) * MIXER_HEADS], axis=-1)
            o_ref[pl.ds(pl.multiple_of(rr * GRID_W, GRID_W), GRID_W), :] = row.astype(BF16)
        return carry

    lax.fori_loop(0, NA_ROWS_PER_STEP // NA_ROWS_LOCKSTEP, rows_body, 0)


def _neighbourhood_attention(p_a, table, b, s):
    rows = s // GRID_W
    assert rows >= NA_KH and rows % NA_ROWS_PER_STEP == 0
    steps = rows // NA_ROWS_PER_STEP
    tq = NA_ROWS_PER_STEP * GRID_W
    w = MIXER_WIDTH
    return pl.pallas_call(
        functools.partial(_na_kernel, rows),
        grid=(b, steps),
        in_specs=[pl.BlockSpec((tq, w), lambda bi, i: (bi * steps + i, 0)),
                  pl.BlockSpec((s, w), lambda bi, i: (bi, 1)),
                  pl.BlockSpec((s, w), lambda bi, i: (bi, 2)),
                  pl.BlockSpec(table.shape, lambda bi, i: (0, 0, 0, 0))],
        out_specs=pl.BlockSpec((tq, w), lambda bi, i: (bi * steps + i, 0)),
        out_shape=jax.ShapeDtypeStruct((b * s, w), BF16),
        compiler_params=_cparams("parallel", "arbitrary"),
    )(p_a, p_a, p_a, table)


def _retention_tables(s):
    h, dh, c = MIXER_HEADS, HEAD_DIM, RET_CHUNK
    half = dh // 2
    inv = ROPE_BASE ** (-jnp.arange(half, dtype=F32) / half)
    ang = jnp.arange(s, dtype=F32)[:, None] * inv[None, :]
    cos, sin = jnp.cos(ang), jnp.sin(ang)
    zero = jnp.zeros_like(sin)
    cos_t = jnp.tile(jnp.concatenate([cos, cos], axis=-1), (1, h))
    sin_lo = jnp.tile(jnp.concatenate([-sin, zero], axis=-1), (1, h))
    sin_hi = jnp.tile(jnp.concatenate([zero, sin], axis=-1), (1, h))
    log_f = np.log1p(-np.exp2(-5.0 - np.arange(h, dtype=np.float64)))
    log_b = log_f[::-1]
    pos = np.arange(c, dtype=np.float64)
    diff = pos[:, None] - pos[None, :]
    dmat = np.where(diff >= 0, np.exp(log_f[:, None, None] * np.maximum(diff, 0.0)), 0.0) \
        + np.where(diff < 0, np.exp(log_b[:, None, None] * np.maximum(-diff, 0.0)), 0.0)
    lanes = lambda t: np.repeat(t.T, dh, axis=1)
    dec = np.stack([lanes(np.exp(log_f[:, None] * (pos + 1.0))),
                    lanes(np.exp(log_f[:, None] * (c - 1.0 - pos))),
                    lanes(np.exp(log_b[:, None] * (c - pos))),
                    lanes(np.exp(log_b[:, None] * pos))])
    chunk_f = [float(np.exp(v * c)) for v in log_f]
    chunk_b = [float(np.exp(v * c)) for v in log_b]
    return cos_t, sin_lo, sin_hi, jnp.asarray(dmat, F32), jnp.asarray(dec, F32), chunk_f, chunk_b


def _rotary(x, cos, sin_lo, sin_hi):
    w = x.shape[-1]
    return x * cos + pltpu.roll(x, w - HEAD_DIM // 2, 1) * sin_lo + pltpu.roll(x, HEAD_DIM // 2, 1) * sin_hi


def _retention_kernel(nb, chunk_f, chunk_b, q_ref, k_ref, v_ref, gate_ref, cos_ref, slo_ref, shi_ref,
                      dmat_ref, dec_ref, ng_ref, o_ref, sf_ref, sb_ref, sball_ref):
    t = pl.program_id(1)
    dh, c, cps = HEAD_DIM, RET_CHUNK, RET_CHUNKS_PER_STEP
    cos, slo, shi = cos_ref[...], slo_ref[...], shi_ref[...]
    kr = _rotary(k_ref[...], cos, slo, shi) * (dh ** -0.5)
    vb = v_ref[...].astype(BF16)
    items = [(ci, h) for ci in range(cps) for h in range(MIXER_HEADS)]
    part = lambda x, ci, h: x[ci * c:(ci + 1) * c, h * dh:(h + 1) * dh]
    split = lambda x: {it: part(x, *it) for it in items}

    @pl.when(t == 0)
    def _():
        sf_ref[...] = jnp.zeros_like(sf_ref)
        sb_ref[...] = jnp.zeros_like(sb_ref)

    @pl.when(t < nb)
    def _():
        blk = nb - 1 - t
        kd, v = split((kr * dec_ref[3]).astype(BF16)), split(vb)
        kv = {it: _dot_tn(kd[it], v[it]) for it in items}
        for h in range(MIXER_HEADS):
            state = sb_ref[h]
            for ci in range(cps - 1, -1, -1):
                sball_ref[blk * cps + ci, h] = state
                state = state * chunk_b[h] + kv[ci, h]
            sb_ref[h] = state

    @pl.when(t >= nb)
    def _():
        blk = t - nb
        qr = _rotary(q_ref[...], cos, slo, shi)
        q, k, v = split(qr.astype(BF16)), split(kr.astype(BF16)), split(vb)
        qf, qbk = split((qr * dec_ref[0]).astype(BF16)), split((qr * dec_ref[2]).astype(BF16))
        kd = split((kr * dec_ref[1]).astype(BF16))
        sc = {it: (_dot_nt(q[it], k[it]) * dmat_ref[it[1]]).astype(BF16) for it in items}
        o = {it: _dot(sc[it], v[it]) for it in items}
        kv = {it: _dot_tn(kd[it], v[it]) for it in items}
        ob = {it: _dot(qbk[it], sball_ref[blk * cps + it[0], it[1]].astype(BF16)) for it in items}
        of = {}
        for h in range(MIXER_HEADS):
            state = sf_ref[h]
            for ci in range(cps):
                of[ci, h] = _dot(qf[ci, h], state.astype(BF16))
                state = state * chunk_f[h] + kv[ci, h]
            sf_ref[h] = state
        o = {it: o[it] + of[it] + ob[it] for it in items}
        mu = {it: jnp.mean(o[it], axis=-1, keepdims=True) for it in items}
        oc = {it: o[it] - mu[it] for it in items}
        var = {it: jnp.mean(oc[it] * oc[it], axis=-1, keepdims=True) for it in items}
        on = {it: oc[it] * lax.rsqrt(var[it] + EPS) for it in items}
        y = jnp.concatenate([jnp.concatenate([on[ci, h] for h in range(MIXER_HEADS)], axis=-1)
                             for ci in range(cps)], axis=0)
        o_ref[...] = (y * ng_ref[...] * _silu(gate_ref[...])).astype(BF16)


def _retention(p_b, norm_g, tables, b, s):
    cos_t, sin_lo, sin_hi, dmat, dec, chunk_f, chunk_b = tables
    cps = RET_CHUNKS_PER_STEP
    tb = RET_CHUNK * cps
    nb = s // tb
    w = MIXER_WIDTH
    dec = jnp.tile(dec, (1, cps, 1))
    block = lambda t: jnp.where(t < nb, nb - 1 - t, t - nb)
    col = lambda j: pl.BlockSpec((tb, w), lambda bi, t: (bi * nb + block(t), j))
    tab = pl.BlockSpec((tb, w), lambda bi, t: (block(t), 0))
    const = lambda a: pl.BlockSpec(a.shape, lambda bi, t: (0,) * a.ndim)
    return pl.pallas_call(
        functools.partial(_retention_kernel, nb, chunk_f, chunk_b),
        grid=(b, 2 * nb),
        in_specs=[col(0), col(1), col(2), col(3), tab, tab, tab, const(dmat), const(dec), const(norm_g)],
        out_specs=pl.BlockSpec((tb, w), lambda bi, t: (bi * nb + jnp.maximum(t - nb, 0), 0)),
        out_shape=jax.ShapeDtypeStruct((b * s, w), BF16),
        scratch_shapes=[pltpu.VMEM((MIXER_HEADS, HEAD_DIM, HEAD_DIM), F32),
                        pltpu.VMEM((MIXER_HEADS, HEAD_DIM, HEAD_DIM), F32),
                        pltpu.VMEM((s // RET_CHUNK, MIXER_HEADS, HEAD_DIM, HEAD_DIM), F32)],
        compiler_params=_cparams("arbitrary", "arbitrary"),
    )(p_b, p_b, p_b, p_b, cos_t, sin_lo, sin_hi, dmat, dec, norm_g)


def _halo_specs(tile, width, n_rows, block_of):
    per = tile // HALO
    last = n_rows // HALO - 1
    main = pl.BlockSpec((tile, width), lambda *g: (block_of(*g), 0))
    prev = pl.BlockSpec((HALO, width), lambda *g: (jnp.maximum(block_of(*g) * per - 1, 0), 0))
    nxt = pl.BlockSpec((HALO, width), lambda *g: (jnp.minimum((block_of(*g) + 1) * per, last), 0))
    return main, prev, nxt


def _fill_padded(pad_ref, main, prev, nxt, first, last):
    t = main.shape[0]
    pad_ref[0:HALO, :] = jnp.where(first, 0.0, prev)
    pad_ref[HALO:HALO + t, :] = main
    pad_ref[HALO + t:2 * HALO + t, :] = jnp.where(last, 0.0, nxt)


def _pool_kernel(s, tiles_per_seq, u_ref, prev_ref, next_ref, w_ref, scale_ref, o_ref, pad_ref):
    t = POOL_TILE
    tseq = pl.program_id(0) % tiles_per_seq
    u = u_ref[...]
    _fill_padded(pad_ref, u, prev_ref[...], next_ref[...], tseq == 0, tseq == tiles_per_seq - 1)
    lane = lax.broadcasted_iota(jnp.int32, (1, MIXER_WIDTH), 1)
    half = jnp.full((1, MIXER_WIDTH), POOL_WINDOWS[0] // 2, jnp.int32)
    for gi in range(1, len(POOL_WINDOWS)):
        half = jnp.where(lane >= gi * POOL_GROUP, POOL_WINDOWS[gi] // 2, half)
    max_half = POOL_WINDOWS[-1] // 2
    acc = jnp.zeros((t, MIXER_WIDTH), F32)
    for d in range(-max_half, max_half):
        inside = (d >= -half) & (d < half)
        acc = acc + jnp.where(inside, pad_ref[HALO + d:HALO + d + t, :], 0.0)
    pos = tseq * t + lax.broadcasted_iota(jnp.int32, (t, 1), 0)
    count = jnp.minimum(pos + half, s) - jnp.maximum(pos - half, 0)
    diff = acc / count.astype(F32) - u
    o_ref[...] = (_dot(diff.astype(BF16), w_ref[...]) * scale_ref[...]).astype(BF16)


def _pool(p_c, w_blockdiag, scale, b, s):
    t = POOL_TILE
    tiles_per_seq = s // t
    main, prev, nxt = _halo_specs(t, MIXER_WIDTH, b * s, lambda i: i)
    const = lambda a: pl.BlockSpec(a.shape, lambda i: (0,) * a.ndim)
    return pl.pallas_call(
        functools.partial(_pool_kernel, s, tiles_per_seq),
        grid=(b * tiles_per_seq,),
        in_specs=[main, prev, nxt, const(w_blockdiag), const(scale)],
        out_specs=pl.BlockSpec((t, MIXER_WIDTH), lambda i: (i, 0)),
        out_shape=jax.ShapeDtypeStruct((b * s, MIXER_WIDTH), BF16),
        scratch_shapes=[pltpu.VMEM((t + 2 * HALO, MIXER_WIDTH), F32)],
        compiler_params=_cparams("parallel"),
    )(p_c, p_c, p_c, w_blockdiag, scale)


def _head_blockdiag(x):
    lane = lax.broadcasted_iota(jnp.int32, x.shape, 1)
    zero = jnp.zeros_like(x)
    return jnp.concatenate([jnp.where(lane < HEAD_DIM, x, zero), jnp.where(lane >= HEAD_DIM, x, zero)], axis=0)


def _gdn_pre_kernel(tiles_per_seq, x_ref, prev_ref, next_ref, ab_ref, cw_ref, alog_ref, dtb_ref, ones_ref,
                    u_ref, gb_ref, pad_ref):
    t, w = GDN_PRE_TILE, MIXER_WIDTH
    tseq = pl.program_id(0) % tiles_per_seq
    _fill_padded(pad_ref, x_ref[...], prev_ref[...], next_ref[...], tseq == 0, tseq == tiles_per_seq - 1)
    acc = jnp.zeros((t, 3 * w), F32)
    for k in range(CONV_K):
        off = HALO + k - CONV_K // 2
        acc = acc + pad_ref[off:off + t, :] * cw_ref[k:k + 1, :]
    u = _silu(acc)
    head_sum = lambda a: sum(_dot(piece, ones_ref[...]) for piece in _split_bf16(a, 2))
    q, k = u[:, 0:w], u[:, w:2 * w]
    u_ref[:, 0:w] = q * lax.rsqrt(head_sum(q * q) + EPS) * (HEAD_DIM ** -0.5)
    u_ref[:, w:2 * w] = k * lax.rsqrt(head_sum(k * k) + EPS)
    u_ref[:, 2 * w:3 * w] = u[:, 2 * w:3 * w]
    ab = ab_ref[...]
    x = ab + dtb_ref[...]
    g = -jnp.exp(alog_ref[...]) * (jnp.maximum(x, 0.0) + jnp.log(1.0 + jnp.exp(-jnp.abs(x))))
    beta = 1.0 / (1.0 + jnp.exp(-ab))
    lane = lax.broadcasted_iota(jnp.int32, ab.shape, 1)
    gb_ref[...] = jnp.where(lane < 2 * MIXER_HEADS, g, beta)


def _gdn_pre(p_d, p_ab, conv_w, alog_vec, dtb_vec, b, s):
    t = GDN_PRE_TILE
    w = MIXER_WIDTH
    tiles_per_seq = s // t
    ones = _block_diag(jnp.ones((MIXER_HEADS, HEAD_DIM, HEAD_DIM), BF16))
    main, prev, nxt = _halo_specs(t, 3 * w, b * s, lambda i: i)
    const = lambda a: pl.BlockSpec(a.shape, lambda i: (0,) * a.ndim)
    row = lambda c: pl.BlockSpec((t, c), lambda i: (i, 0))
    return pl.pallas_call(
        functools.partial(_gdn_pre_kernel, tiles_per_seq),
        grid=(b * tiles_per_seq,),
        in_specs=[main, prev, nxt, row(LANES), const(conv_w), const(alog_vec), const(dtb_vec), const(ones)],
        out_specs=[row(3 * w), row(LANES)],
        out_shape=[jax.ShapeDtypeStruct((b * s, 3 * w), F32), jax.ShapeDtypeStruct((b * s, LANES), F32)],
        scratch_shapes=[pltpu.VMEM((t + 2 * HALO, 3 * w), F32)],
        compiler_params=_cparams("parallel"),
    )(p_d, p_d, p_d, p_ab, conv_w, alog_vec, dtb_vec, ones)


def _gdn_prepare(forward, u, gb, tri_col, tri_row):
    c, dh, w = GDN_CHUNK, HEAD_DIM, MIXER_WIDTH
    t = u.shape[0]
    a_off, b_off = (0, 2 * MIXER_HEADS) if forward else (MIXER_HEADS, 3 * MIXER_HEADS)
    g_all = beta_all = gb
    g_parts = _split_bf16(g_all, 3)
    gc_col = sum(_dot(tri_col, p) for p in g_parts)
    gc_row = sum(_dot_tn(p, tri_row) for p in g_parts)
    ii = lax.broadcasted_iota(jnp.int32, (c, 2 * dh), 0)
    lane = lax.broadcasted_iota(jnp.int32, (c, 2 * dh), 1)
    first = lane < dh
    jj = jnp.where(first, lane, lane - dh)
    incl = (ii >= jj) if forward else (ii <= jj)
    strict = (ii > jj) if forward else (ii < jj)
    last = c - 1 if forward else 0
    items = [(ci, p) for ci in range(t // c) for p in range(MIXER_HEADS // 2)]
    rows = lambda ci: slice(ci * c, (ci + 1) * c)
    part = lambda base: [u[rows(ci), base + 2 * p * dh:base + 2 * (p + 1) * dh] for ci, p in items]
    col_pair = lambda x, off: [jnp.where(first, x[rows(ci), off + 2 * p:off + 2 * p + 1],
                                         x[rows(ci), off + 2 * p + 1:off + 2 * p + 2]) for ci, p in items]
    q, k, v = part(0), part(w), part(2 * w)
    gcc = col_pair(gc_col, a_off)
    beta = col_pair(beta_all, b_off)
    gcr = [jnp.concatenate([gc_row[a_off + 2 * p:a_off + 2 * p + 1, rows(ci)],
                            gc_row[a_off + 2 * p + 1:a_off + 2 * p + 2, rows(ci)]], axis=1) for ci, p in items]
    g_last = [a[last:last + 1, :] for a in gcc]
    decay = [jnp.where(incl, jnp.exp(jnp.where(incl, a - b, 0.0)), 0.0) for a, b in zip(gcc, gcr)]
    kq = [_dot_nt(jnp.concatenate([kk, qq], axis=0).astype(BF16), _head_blockdiag(kk.astype(BF16)))
          for kk, qq in zip(k, q)]
    lmat = [jnp.where(strict, m[:c] * b * d, 0.0) for m, b, d in zip(kq, beta, decay)]
    attn = [(m[c:] * d).astype(BF16) for m, d in zip(kq, decay)]
    xinv = [-a for a in lmat]
    lp = [a.astype(BF16) for a in lmat]
    lp_bd = [_head_blockdiag(a) for a in lp]
    for _ in range(int(math.log2(c)) - 1):
        lp32 = [_dot(a, bd) for a, bd in zip(lp, lp_bd)]
        lp = [a.astype(BF16) for a in lp32]
        lp_bd = [_head_blockdiag(a) for a in lp]
        xinv = [xi + a32 + _dot(xi.astype(BF16), bd) for xi, a32, bd in zip(xinv, lp32, lp_bd)]
    egc = [jnp.exp(a) for a in gcc]
    xb = [a.astype(BF16) for a in xinv]
    vbeta = [vv * b for vv, b in zip(v, beta)]
    kbeta = [kk * (b * e) for kk, b, e in zip(k, beta, egc)]
    u_val = [r + _dot(xi, _head_blockdiag(r.astype(BF16))) for xi, r in zip(xb, vbeta)]
    w_dec = [r + _dot(xi, _head_blockdiag(r.astype(BF16))) for xi, r in zip(xb, kbeta)]
    wq = [jnp.concatenate([wd, qq * e], axis=0).astype(BF16) for wd, qq, e in zip(w_dec, q, egc)]
    k_state = [(kk * jnp.exp(gl - a)).astype(BF16) for kk, gl, a in zip(k, g_last, gcc)]
    e_last = [jnp.exp(gl) for gl in g_last]
    keyed = lambda vals: dict(zip(items, vals))
    return dict(wq=keyed(wq), u_val=keyed(u_val), attn=keyed(attn), k_state=keyed(k_state), e_last=keyed(e_last))


def _gdn_kernel(uf_ref, gbf_ref, ub_ref, gbb_ref, tril_ref, triu_ref, of_ref, ob_ref, sf_ref, sb_ref):
    n = pl.program_id(1)
    t = uf_ref.shape[0]
    c, dh = GDN_CHUNK, HEAD_DIM
    nch = t // c
    pairs = MIXER_HEADS // 2

    @pl.when(n == 0)
    def _():
        sf_ref[...] = jnp.zeros_like(sf_ref)
        sb_ref[...] = jnp.zeros_like(sb_ref)

    tril, triu = tril_ref[...], triu_ref[...]
    prep_f = _gdn_prepare(True, uf_ref[...], gbf_ref[...], tril, triu)
    prep_b = _gdn_prepare(False, ub_ref[...], gbb_ref[...], triu, tril)

    first = lax.broadcasted_iota(jnp.int32, (dh, 2 * dh), 1) < dh
    chains = [(prep_f, sf_ref, of_ref, p, True) for p in range(pairs)] + \
             [(prep_b, sb_ref, ob_ref, p, False) for p in range(pairs)]
    state = [s_ref[p] for _, s_ref, _, p, _ in chains]
    for step in range(nch):
        key = [(step if fw else nch - 1 - step, p) for _, _, _, p, fw in chains]
        ws = [_dot(pr['wq'][kk], _head_blockdiag(st.astype(BF16))) for (pr, *_), kk, st in zip(chains, key, state)]
        v_new = [pr['u_val'][kk] - a[:c] for (pr, *_), kk, a in zip(chains, key, ws)]
        vb = [a.astype(BF16) for a in v_new]
        out = [a[c:] + _dot(pr['attn'][kk], _head_blockdiag(b)) for (pr, *_), kk, a, b in zip(chains, key, ws, vb)]
        cross = [_dot_tn(pr['k_state'][kk], b) for (pr, *_), kk, b in zip(chains, key, vb)]
        state = [st * pr['e_last'][kk] + jnp.where(first, x[:dh], x[dh:])
                 for (pr, *_), kk, st, x in zip(chains, key, state, cross)]
        for (_, _, o_ref, p, _), (ci, _), o in zip(chains, key, out):
            o_ref[ci * c:(ci + 1) * c, 2 * p * dh:2 * (p + 1) * dh] = o
    for (_, s_ref, _, p, _), st in zip(chains, state):
        s_ref[p] = st


def _gated_deltanet(p_d, p_ab, conv_w, alog_vec, dtb_vec, b, s):
    u, gb = _gdn_pre(p_d, p_ab, conv_w, alog_vec, dtb_vec, b, s)
    c = GDN_CHUNK
    t = c * GDN_CHUNKS_PER_STEP
    nb = s // t
    w = MIXER_WIDTH
    r = np.arange(t)
    same = (r[:, None] // c) == (r[None, :] // c)
    tril = jnp.asarray(same & (r[:, None] >= r[None, :]), BF16)
    triu = jnp.asarray(same & (r[:, None] <= r[None, :]), BF16)
    fwd = lambda bi, n: (bi * nb + n, 0)
    bwd = lambda bi, n: (bi * nb + nb - 1 - n, 0)
    const = lambda a: pl.BlockSpec(a.shape, lambda bi, n: (0,) * a.ndim)
    return pl.pallas_call(
        _gdn_kernel,
        grid=(b, nb),
        in_specs=[pl.BlockSpec((t, 3 * w), fwd), pl.BlockSpec((t, LANES), fwd),
                  pl.BlockSpec((t, 3 * w), bwd), pl.BlockSpec((t, LANES), bwd), const(tril), const(triu)],
        out_specs=[pl.BlockSpec((t, w), fwd), pl.BlockSpec((t, w), bwd)],
        out_shape=[jax.ShapeDtypeStruct((b * s, w), F32), jax.ShapeDtypeStruct((b * s, w), F32)],
        scratch_shapes=[pltpu.VMEM((MIXER_HEADS // 2, HEAD_DIM, 2 * HEAD_DIM), F32),
                        pltpu.VMEM((MIXER_HEADS // 2, HEAD_DIM, 2 * HEAD_DIM), F32)],
        compiler_params=_cparams("arbitrary", "arbitrary"),
    )(u, gb, u, gb, tril, triu)


def _mem_kv_kernel(m_ref, g_ref, wk_ref, wv_ref, k_ref, v_ref):
    h = _rms(m_ref[0], g_ref[...]).astype(BF16)
    k_ref[0] = _dot(h, wk_ref[...]).astype(BF16)
    v_ref[0] = _dot(h, wv_ref[...]).astype(BF16)


def _mem_kv(mem, g, wk, wv):
    b, nm, d = mem.shape
    const = lambda a: pl.BlockSpec(a.shape, lambda i: (0,) * a.ndim)
    blk = pl.BlockSpec((1, nm, d), lambda i: (i, 0, 0))
    return pl.pallas_call(
        _mem_kv_kernel, grid=(b,),
        in_specs=[blk, const(g), const(wk), const(wv)], out_specs=[blk, blk],
        out_shape=[jax.ShapeDtypeStruct((b, nm, d), BF16)] * 2,
        compiler_params=_cparams("parallel"),
    )(mem, g, wk, wv)


def _route_lanes(lg):
    lane_i = lax.broadcasted_iota(jnp.int32, lg.shape, 1)
    lane = lane_i.astype(F32)
    first_at = lambda vals, m: jnp.min(jnp.where(vals == m, lane, float(LANES)), axis=-1, keepdims=True)
    g_mask = lane_i < N_GROUPS
    gl = jnp.where(g_mask, lg, NEG_BIG)
    g_max = jnp.max(gl, axis=-1, keepdims=True)
    g_sum = jnp.sum(jnp.where(g_mask, jnp.exp(gl - g_max), 0.0), axis=-1, keepdims=True)
    g_p = 1.0 / g_sum
    g_idx = first_at(gl, g_max)
    lo = N_GROUPS + EXPERTS_PER_GROUP * g_idx
    e_mask = (lane >= lo) & (lane < lo + EXPERTS_PER_GROUP)
    el = jnp.where(e_mask, lg, NEG_BIG)
    m1 = jnp.max(el, axis=-1, keepdims=True)
    i1 = first_at(el, m1)
    el2 = jnp.where(lane == i1, NEG_BIG, el)
    m2 = jnp.max(el2, axis=-1, keepdims=True)
    i2 = first_at(el2, m2)
    r = jnp.exp(m2 - m1)
    w1 = g_p / (1.0 + r)
    w2 = g_p * r / (1.0 + r)
    e1, e2 = i1 - N_GROUPS, i2 - N_GROUPS
    e_lo, e_hi = jnp.minimum(e1, e2), jnp.maximum(e1, e2)
    w_lo, w_hi = jnp.where(e1 < e2, w1, w2), jnp.where(e1 < e2, w2, w1)
    a, b = e_lo - EXPERTS_PER_GROUP * g_idx, e_hi - EXPERTS_PER_GROUP * g_idx
    cls = g_idx * PAIRS_PER_GROUP + a * EXPERTS_PER_GROUP - a * (a + 1.0) * 0.5 + (b - a - 1.0)
    out = jnp.where(lane_i == 0, e_lo, 0.0)
    out = jnp.where(lane_i == 1, e_hi, out)
    out = jnp.where(lane_i == 2, w_lo, out)
    out = jnp.where(lane_i == 3, w_hi, out)
    out = jnp.where(lane_i == 4, cls, out)
    return out, jnp.where(lane == cls, 1.0, 0.0)


def _post_kernel(x_ref, ya_ref, yb_ref, yc_ref, of_ref, ob_ref, gate_ref, seg_ref, gg_ref, wout_ref, gq_ref,
                 wq_ref, k_ref, v_ref, wo_ref, gf_ref, wr_ref, br_ref, xo_ref, hx_ref, rt_ref, hist_ref):
    w = MIXER_WIDTH
    o = of_ref[...] + ob_ref[...]
    ms = sum(_dot(p, seg_ref[...]) for p in _split_bf16(o * o, 2))
    yd = o * lax.rsqrt(ms + EPS) * gg_ref[...] * _silu(gate_ref[...])
    mix = _dot(ya_ref[...], wout_ref[0:w, :]) + _dot(yb_ref[...], wout_ref[w:2 * w, :])
    mix = mix + _dot(yc_ref[...], wout_ref[2 * w:3 * w, :]) + _dot(yd.astype(BF16), wout_ref[3 * w:4 * w, :])
    x1 = x_ref[...] + mix
    q = _dot(_rms(x1, gq_ref[...]).astype(BF16), wq_ref[...]).astype(BF16)
    cols = [slice(h * MEM_HEAD_DIM, (h + 1) * MEM_HEAD_DIM) for h in range(MEM_HEADS)]
    sc = [_dot_nt(q[:, sl], k_ref[0, :, sl]) * (MEM_HEAD_DIM ** -0.5) for sl in cols]
    p = [jnp.exp(a - jnp.max(a, axis=-1, keepdims=True)) for a in sc]
    l = [jnp.sum(a, axis=-1, keepdims=True) for a in p]
    heads = [(_dot(a.astype(BF16), v_ref[0, :, sl]) / b).astype(BF16) for a, b, sl in zip(p, l, cols)]
    x2 = x1 + _dot(jnp.concatenate(heads, axis=-1), wo_ref[...])
    xo_ref[...] = x2
    hx = _rms(x2, gf_ref[...])
    _to_row_tiles(hx_ref, hx)
    h_hi, h_lo = _split_bf16(hx, 2)
    lg = _dot(h_hi, wr_ref[0]) + _dot(h_hi, wr_ref[1]) + _dot(h_lo, wr_ref[0]) + br_ref[...]
    route, chosen = _route_lanes(lg)
    rt_ref[...] = route
    hist_ref[...] = jnp.broadcast_to(jnp.sum(chosen, axis=0, keepdims=True), hist_ref.shape)


def _post(x, y_a, y_b, y_c, o_f, o_b, p_d, seg, gdn_g, w_out, g_q, w_q, k_mem, v_mem, w_o, g_f, w_r, b_r, s):
    n = x.shape[0]
    tm = POST_TILE
    w = MIXER_WIDTH
    per_seq = s // tm
    row = lambda c: pl.BlockSpec((tm, c), lambda i: (i, 0))
    const = lambda a: pl.BlockSpec(a.shape, lambda i: (0,) * a.ndim)
    kv = pl.BlockSpec((1,) + k_mem.shape[1:], lambda i: (i // per_seq, 0, 0))
    return pl.pallas_call(
        _post_kernel, grid=(n // tm,),
        in_specs=[row(D_MODEL), row(w), row(w), row(w), row(w), row(w),
                  pl.BlockSpec((tm, w), lambda i: (i, 3)), const(seg), const(gdn_g), const(w_out), const(g_q),
                  const(w_q), kv, kv, const(w_o), const(g_f), const(w_r), const(b_r)],
        out_specs=[row(D_MODEL), pl.BlockSpec((tm * ROW_TILES, LANES), lambda i: (i, 0)), row(LANES),
                   pl.BlockSpec((SUBLANES, LANES), lambda i: (i, 0))],
        out_shape=[jax.ShapeDtypeStruct((n, D_MODEL), F32), jax.ShapeDtypeStruct((n * ROW_TILES, LANES), F32),
                   jax.ShapeDtypeStruct((n, LANES), F32), jax.ShapeDtypeStruct((n // tm * SUBLANES, LANES), F32)],
        compiler_params=_cparams("parallel"),
    )(x, y_a, y_b, y_c, o_f, o_b, p_d, seg, gdn_g, w_out, g_q, w_q, k_mem, v_mem, w_o, g_f, w_r, b_r)


def _pair_class_experts():
    lo, hi = [], []
    for g in range(N_GROUPS):
        for a in range(EXPERTS_PER_GROUP):
            for b in range(a + 1, EXPERTS_PER_GROUP):
                lo.append(g * EXPERTS_PER_GROUP + a)
                hi.append(g * EXPERTS_PER_GROUP + b)
    return np.asarray(lo, np.int32), np.asarray(hi, np.int32)


def _dispatch(route, hist):
    n = route.shape[0]
    blk = MOE_BLOCK
    cls = route[:, 4].astype(jnp.int32)
    _, order = lax.sort_key_val(cls, jnp.arange(n, dtype=jnp.int32))
    counts = jnp.sum(hist.reshape(-1, SUBLANES, LANES)[:, 0, :N_CLASSES], axis=0).astype(jnp.int32)
    padded = (counts + blk - 1) // blk * blk
    pad_end = jnp.cumsum(padded)
    pad_start = pad_end - padded
    start = jnp.cumsum(counts) - counts
    n_blocks = n // blk + N_CLASSES
    blk_start = jnp.arange(n_blocks, dtype=jnp.int32) * blk
    blk_cls = jnp.minimum(jnp.sum(pad_end[None, :] <= blk_start[:, None], axis=1), N_CLASSES - 1).astype(jnp.int32)
    cls_lo, cls_hi = _pair_class_experts()
    into_cls = blk_start - pad_start[blk_cls]
    blk_valid = jnp.clip(counts[blk_cls] - into_cls, 0, blk)
    in_blk = jnp.arange(blk, dtype=jnp.int32)[None, :]
    idx = (start[blk_cls] + into_cls)[:, None] + in_blk
    valid = in_blk < blk_valid[:, None]
    tok = order[jnp.clip(idx, 0, n - 1)]
    row_tok = jnp.where(valid, tok, n + in_blk)
    row_w = jnp.where(valid[..., None], route[:, TOP_K:2 * TOP_K][tok], 0.0).reshape(-1, TOP_K)
    n_used = (pad_end[-1] // blk).astype(jnp.int32).reshape(1)
    return (n_used, jnp.asarray(cls_lo)[blk_cls], jnp.asarray(cls_hi)[blk_cls],
            jnp.concatenate([n + in_blk, row_tok], axis=0).reshape(-1), row_w)


def _moe_kernel(n_tokens, n_blocks, nu_ref, ea_ref, eb_ref, dst_ref, hx_hbm, rw_ref, wga_ref, wua_ref, wda_ref,
                wgb_ref, wub_ref, wdb_ref, y_hbm, xbuf, ybuf, gsem, ssem):
    j = pl.program_id(0)
    blk, rt = MOE_BLOCK, ROW_TILES
    n_used = nu_ref[0]
    slot = j % 2
    other = 1 - slot
    tile = lambda ref, i: ref.at[pl.ds(pl.multiple_of(i * rt, rt), rt), :]
    dst = lambda jj, i: dst_ref[(jj + 1) * blk + i]
    nxt = jnp.minimum(j + 1, n_blocks - 1)

    def gather_row(jj, sl, i):
        tok = jnp.minimum(dst(jj, i), n_tokens - 1)
        pltpu.make_async_copy(tile(hx_hbm, tok), tile(xbuf.at[sl], i), gsem.at[sl]).start()

    def scatter_row(jj, sl, i):
        pltpu.make_async_copy(tile(ybuf.at[sl], i), tile(y_hbm, dst(jj, i)), ssem.at[sl]).start()

    def wait_gather(sl):
        pltpu.make_async_copy(hx_hbm.at[pl.ds(0, blk * rt), :], xbuf.at[sl], gsem.at[sl]).wait()

    def wait_scatter(sl):
        pltpu.make_async_copy(ybuf.at[sl], y_hbm.at[pl.ds(0, blk * rt), :], ssem.at[sl]).wait()

    def rolled(fn):
        lax.fori_loop(0, blk, lambda i, carry: (fn(i), carry)[1], 0, unroll=DMA_UNROLL)

    @pl.when(j == 0)
    def _():
        ybuf[1] = jnp.zeros((blk * rt, LANES), F32)
        rolled(lambda i: gather_row(0, 0, i))

    @pl.when(j < n_used)
    def _():
        wait_gather(slot)

        @pl.when(j >= 1)
        def _():
            wait_scatter(slot)

        per = blk // MOE_PIECES

        def issue_copies(piece):
            for i in range(piece * per, (piece + 1) * per):
                gather_row(nxt, other, i)
                scatter_row(j - 1, other, i)

        xs = _from_row_tiles(xbuf.at[slot], 0, blk, rt).astype(BF16)
        half = D_EXPERT // 2
        hidden = []
        for e, (wg_ref, wu_ref) in enumerate(((wga_ref, wua_ref), (wgb_ref, wub_ref))):
            acts = []
            for c in range(2):
                cols = slice(c * half, (c + 1) * half)
                g = _dot(xs, wg_ref[0, 0, :, cols])
                issue_copies(4 * e + 2 * c)
                u = _dot(xs, wu_ref[0, 0, :, cols])
                issue_copies(4 * e + 2 * c + 1)
                acts.append((_silu(g) * u).astype(BF16))
            hidden.append(jnp.concatenate(acts, axis=1))
        w_a, w_b = rw_ref[:, 0:1], rw_ref[:, 1:2]
        quarter = D_MODEL // 4
        for c in range(4):
            cols = slice(c * quarter, (c + 1) * quarter)
            y = w_a * _dot(hidden[0], wda_ref[0, 0, :, cols]) + w_b * _dot(hidden[1], wdb_ref[0, 0, :, cols])
            for k in range(quarter // LANES):
                ybuf[slot, pl.ds(c * (quarter // LANES) + k, blk, stride=rt), :] = y[:, k * LANES:(k + 1) * LANES]

    @pl.when(j == n_used)
    def _():
        wait_gather(slot)
        wait_scatter(slot)
        rolled(lambda i: scatter_row(j - 1, other, i))
        wait_scatter(other)


def _moe(hx_tiles, n_used, blk_ea, blk_eb, row_tok, row_w, w_gate, w_up, w_down, layer, n):
    blk = MOE_BLOCK
    n_blocks = blk_ea.shape[0]
    block = lambda j: jnp.minimum(j, n_blocks - 1)
    first = lambda j, nu, ea, eb, dst: (layer, ea[block(j)], 0, 0)
    second = lambda j, nu, ea, eb, dst: (layer, eb[block(j)], 0, 0)
    up, down = (1, 1, D_MODEL, D_EXPERT), (1, 1, D_EXPERT, D_MODEL)
    grid_spec = pltpu.PrefetchScalarGridSpec(
        num_scalar_prefetch=4, grid=(n_blocks + 1,),
        in_specs=[pl.BlockSpec(memory_space=pl.ANY),
                  pl.BlockSpec((blk, TOP_K), lambda j, nu, ea, eb, dst: (block(j), 0)),
                  pl.BlockSpec(up, first), pl.BlockSpec(up, first), pl.BlockSpec(down, first),
                  pl.BlockSpec(up, second), pl.BlockSpec(up, second), pl.BlockSpec(down, second)],
        out_specs=pl.BlockSpec(memory_space=pl.ANY),
        scratch_shapes=[pltpu.VMEM((2, blk * ROW_TILES, LANES), F32), pltpu.VMEM((2, blk * ROW_TILES, LANES), F32),
                        pltpu.SemaphoreType.DMA((2,)), pltpu.SemaphoreType.DMA((2,))])
    return pl.pallas_call(
        functools.partial(_moe_kernel, n, n_blocks), grid_spec=grid_spec,
        out_shape=jax.ShapeDtypeStruct(((n + blk) * ROW_TILES, LANES), F32),
        compiler_params=_cparams("arbitrary"),
    )(n_used, blk_ea, blk_eb, row_tok, hx_tiles, row_w, w_gate, w_up, w_down, w_gate, w_up, w_down)


def _final_kernel(x_ref, y_ref, g_ref, o_ref):
    o_ref[...] = _rms(_add_moe(x_ref[...], y_ref), g_ref[...])


def _final(x, moe, g):
    n = x.shape[0]
    tm = PROJ_TILE
    return pl.pallas_call(
        _final_kernel, grid=(n // tm,),
        in_specs=[pl.BlockSpec((tm, D_MODEL), lambda i: (i, 0)), _moe_out_spec(tm),
                  pl.BlockSpec(g.shape, lambda i: (0, 0))],
        out_specs=pl.BlockSpec((tm, D_MODEL), lambda i: (i, 0)),
        out_shape=jax.ShapeDtypeStruct((n, D_MODEL), F32),
        compiler_params=_cparams("parallel"),
    )(x, moe, g)


def _lane_vec(v, width=LANES):
    v = v.reshape(1, -1).astype(F32)
    return jnp.pad(v, ((0, 0), (0, width - v.shape[1])))


def _block_diag(blocks):
    g, c, _ = blocks.shape
    eye = jnp.eye(g, dtype=blocks.dtype)
    return (eye[:, None, :, None] * blocks[:, :, None, :]).reshape(g * c, g * c)


def kernel(x, mem, mix_norm_g, w_in, na_rpb, ret_norm_g, pool_w, pool_scale, gdn_conv_w, gdn_a_log, gdn_dt_bias, gdn_norm_g, w_out, mem_q_norm_g, mem_kv_norm_g, mem_w_q, mem_w_k, mem_w_v, mem_w_o, ffn_norm_g, w_group, b_group, w_router, b_router, w_gate, w_up, w_down, final_norm_g):
    b, s, d = x.shape
    n = b * s
    depth = w_in.shape[0]
    row = lambda v: v.reshape(1, -1).astype(F32)
    ret_tables = _retention_tables(s)
    seg = _block_diag(jnp.full((MIXER_HEADS, HEAD_DIM, HEAD_DIM), 1.0 / HEAD_DIM, BF16))
    wg_bf, wu_bf, wd_bf = w_gate.astype(BF16), w_up.astype(BF16), w_down.astype(BF16)
    xs = x.reshape(n, d)
    moe = None
    for l in range(depth):
        w_pad = jnp.pad(w_in[l], ((0, 0), (0, P_IN_PAD - w_in.shape[2]))).astype(BF16)
        xs, p_a, p_b, p_c, p_d, p_ab = _norm_proj(xs, moe, row(mix_norm_g[l]), w_pad)
        y_a = _neighbourhood_attention(p_a, _na_bias_table(na_rpb[l]), b, s)
        y_b = _retention(p_b, row(ret_norm_g[l]), ret_tables, b, s)
        y_c = _pool(p_c, _block_diag(pool_w[l]).astype(BF16), row(pool_scale[l]), b, s)
        o_f, o_b = _gated_deltanet(p_d, p_ab, gdn_conv_w[l].astype(F32), _lane_vec(gdn_a_log[l]),
                                   _lane_vec(gdn_dt_bias[l]), b, s)
        k_mem, v_mem = _mem_kv(mem, row(mem_kv_norm_g[l]), mem_w_k[l].astype(BF16), mem_w_v[l].astype(BF16))
        w_r = jnp.pad(jnp.concatenate([w_group[l], w_router[l]], axis=1),
                      ((0, 0), (0, LANES - N_GROUPS - N_EXPERTS))).astype(F32)
        b_r = _lane_vec(jnp.concatenate([b_group[l], b_router[l]]))
        xs, hx_tiles, route, hist = _post(
            xs, y_a, y_b, y_c, o_f, o_b, p_d, seg, row(jnp.tile(gdn_norm_g[l], MIXER_HEADS)), w_out[l].astype(BF16),
            row(mem_q_norm_g[l]), mem_w_q[l].astype(BF16), k_mem, v_mem, mem_w_o[l].astype(BF16),
            row(ffn_norm_g[l]), jnp.stack(_split_bf16(w_r, 2)), b_r, s)
        moe = _moe(hx_tiles, *_dispatch(route, hist), wg_bf, wu_bf, wd_bf, l, n)
    return _final(xs, moe, row(final_norm_g)).reshape(b, s, d)
```

```python
import functools
import math

import jax
import jax.numpy as jnp
import numpy as np
from jax import lax
from jax.experimental import pallas as pl
from jax.experimental.pallas import tpu as pltpu

F32 = jnp.float32
BF16 = jnp.bfloat16
HIGHEST = lax.Precision.HIGHEST

D_MODEL = 1024
GRID_W = 64
HEAD_DIM = 64
MIXER_WIDTH = 256
MIXER_HEADS = 4
NA_KH = 8
NA_KW = 16
RET_CHUNK = 128
ROPE_BASE = 10000.0
POOL_WINDOWS = (2, 4, 8, 16)
POOL_GROUP = 64
GDN_CHUNK = 64
CONV_K = 4
MEM_HEADS = 4
MEM_HEAD_DIM = 256
N_GROUPS = 4
EXPERTS_PER_GROUP = 8
N_EXPERTS = 32
TOP_K = 2
D_EXPERT = 512
EPS = 1e-6

P_IN_PAD = 3200
V7X_VMEM_LIMIT = 56 * 1024 * 1024
LANES = 128
SUBLANES = 8
ROW_TILES = D_MODEL // LANES
NEG_BIG = -1e30

PROJ_TILE = 512
POOL_TILE = 512
POST_TILE = 512
NA_ROWS_PER_STEP = 8
NA_ROWS_LOCKSTEP = 4
GDN_CHUNKS_PER_STEP = 8
GDN_PRE_TILE = 512
RET_CHUNKS_PER_STEP = 4
PAIRS_PER_GROUP = EXPERTS_PER_GROUP * (EXPERTS_PER_GROUP - 1) // 2
N_CLASSES = N_GROUPS * PAIRS_PER_GROUP
MOE_BLOCK = 128
MOE_BUFFERS = 3
MOE_PIECES = 8
DMA_UNROLL = 8
HALO = 8


def _cparams(*sem):
    return pltpu.CompilerParams(dimension_semantics=sem, vmem_limit_bytes=V7X_VMEM_LIMIT)


def _dot(a, b):
    return jnp.dot(a, b, preferred_element_type=F32)


def _dot_nt(a, b):
    return lax.dot_general(a, b, (((1,), (1,)), ((), ())), preferred_element_type=F32)


def _dot_tn(a, b, precision=None):
    return lax.dot_general(a, b, (((0,), (0,)), ((), ())), preferred_element_type=F32, precision=precision)


def _silu(x):
    return x * (1.0 / (1.0 + jnp.exp(-x)))


def _rms(x, g):
    return x * lax.rsqrt(jnp.mean(x * x, axis=-1, keepdims=True) + EPS) * g


def _from_row_tiles(ref, first, count, stride):
    return jnp.concatenate([ref[pl.ds(first + k, count, stride=stride), :] for k in range(ROW_TILES)], axis=1)


def _to_row_tiles(ref, x):
    count = x.shape[0]
    for k in range(ROW_TILES):
        ref[pl.ds(k, count, stride=ROW_TILES), :] = x[:, k * LANES:(k + 1) * LANES]


def _add_moe(x, y_ref):
    return x + _from_row_tiles(y_ref, 0, x.shape[0], ROW_TILES)


def _split_bf16(x, parts):
    out = []
    for _ in range(parts):
        p = x.astype(BF16)
        out.append(p)
        x = x - p.astype(F32)
    return out


def _norm_proj_kernel(has_y, *refs):
    if has_y:
        x_ref, y_ref, g_ref, w_ref, xo_ref, pa_ref, pb_ref, pc_ref, pd_ref, pab_ref = refs
        x = _add_moe(x_ref[...], y_ref)
        xo_ref[...] = x
    else:
        x_ref, g_ref, w_ref, pa_ref, pb_ref, pc_ref, pd_ref, pab_ref = refs
        x = x_ref[...]
    h = _rms(x, g_ref[...]).astype(BF16)
    w = MIXER_WIDTH
    pa_ref[...] = _dot(h, w_ref[:, 0:3 * w]).astype(BF16)
    pb_ref[...] = _dot(h, w_ref[:, 3 * w:7 * w])
    pc_ref[...] = _dot(h, w_ref[:, 7 * w:8 * w])
    pd_ref[...] = _dot(h, w_ref[:, 8 * w:12 * w])
    pab_ref[...] = _dot(h, w_ref[:, 12 * w:P_IN_PAD])


def _moe_out_spec(tile):
    return pl.BlockSpec((tile * ROW_TILES, LANES), lambda i: (i, 0))


def _norm_proj(x, moe, g, w_pad):
    n = x.shape[0]
    tm = PROJ_TILE
    w = MIXER_WIDTH
    row = lambda c: pl.BlockSpec((tm, c), lambda i: (i, 0))
    const = lambda a: pl.BlockSpec(a.shape, lambda i: (0,) * a.ndim)
    outs = [jax.ShapeDtypeStruct((n, 3 * w), BF16), jax.ShapeDtypeStruct((n, 4 * w), F32),
            jax.ShapeDtypeStruct((n, w), F32), jax.ShapeDtypeStruct((n, 4 * w), F32),
            jax.ShapeDtypeStruct((n, LANES), F32)]
    out_specs = [row(3 * w), row(4 * w), row(w), row(4 * w), row(LANES)]
    if moe is None:
        ins, in_specs = (x, g, w_pad), [row(D_MODEL), const(g), const(w_pad)]
    else:
        ins = (x, moe, g, w_pad)
        in_specs = [row(D_MODEL), _moe_out_spec(tm), const(g), const(w_pad)]
        outs = [jax.ShapeDtypeStruct((n, D_MODEL), F32)] + outs
        out_specs = [row(D_MODEL)] + out_specs
    res = pl.pallas_call(
        functools.partial(_norm_proj_kernel, moe is not None),
        grid=(n // tm,), in_specs=in_specs, out_specs=out_specs, out_shape=outs,
        compiler_params=_cparams("parallel"),
    )(*ins)
    return res if moe is not None else [x] + list(res)


def _na_bias_table(rpb):
    qc = np.arange(GRID_W)
    kc = np.arange(GRID_W)
    d_col = np.clip(kc[None, :] - qc[:, None], -(NA_KW - 1), NA_KW - 1) + (NA_KW - 1)
    pick = np.zeros((2 * NA_KW - 1, GRID_W * GRID_W), np.float32)
    pick[d_col.reshape(-1), np.arange(GRID_W * GRID_W)] = 1.0
    win = np.clip(qc - NA_KW // 2, 0, GRID_W - NA_KW)
    ok = (kc[None, :] >= win[:, None]) & (kc[None, :] < win[:, None] + NA_KW)
    cols = jnp.einsum('hrc,cx->hrx', rpb.astype(F32), jnp.asarray(pick), precision=HIGHEST)
    cols = jnp.where(ok[None, None], cols.reshape(MIXER_HEADS, 2 * NA_KH - 1, GRID_W, GRID_W), NEG_BIG)
    per_e = [jnp.transpose(cols[:, NA_KH - 1 - e:2 * NA_KH - 1 - e], (0, 2, 1, 3)) for e in range(NA_KH)]
    return jnp.stack(per_e, axis=1).reshape(MIXER_HEADS, NA_KH, GRID_W, NA_KH * GRID_W)


def _na_kernel(rows, q_ref, k_ref, v_ref, t_ref, o_ref):
    i = pl.program_id(1)
    dh = HEAD_DIM

    def rows_body(it, carry):
        chains = []
        for k in range(NA_ROWS_LOCKSTEP):
            rr = it * NA_ROWS_LOCKSTEP + k
            r = i * NA_ROWS_PER_STEP + rr
            kr0 = jnp.clip(r - NA_KH // 2, 0, rows - NA_KH)
            q = q_ref[pl.ds(pl.multiple_of(rr * GRID_W, GRID_W), GRID_W), :]
            k0 = pl.multiple_of(kr0 * GRID_W, GRID_W)
            kb = k_ref[pl.ds(k0, NA_KH * GRID_W), :]
            vb = v_ref[pl.ds(k0, NA_KH * GRID_W), :]
            for h in range(MIXER_HEADS):
                sl = slice(h * dh, (h + 1) * dh)
                chains.append((q[:, sl], kb[:, sl], vb[:, sl], h, r - kr0))
        s = [_dot_nt(q, kb) * (dh ** -0.5) + t_ref[h, e] for q, kb, _, h, e in chains]
        m = [jnp.max(a, axis=-1, keepdims=True) for a in s]
        p = [jnp.exp(a - b) for a, b in zip(s, m)]
        l = [jnp.sum(a, axis=-1, keepdims=True) for a in p]
        o = [_dot(a.astype(BF16), c[2]) / b for a, b, c in zip(p, l, chains)]
        for k in range(NA_ROWS_LOCKSTEP):
            rr = it * NA_ROWS_LOCKSTEP + k
            row = jnp.concatenate(o[k * MIXER_HEADS:(k + 1) * MIXER_HEADS], axis=-1)
            o_ref[pl.ds(pl.multiple_of(rr * GRID_W, GRID_W), GRID_W), :] = row.astype(BF16)
        return carry

    lax.fori_loop(0, NA_ROWS_PER_STEP // NA_ROWS_LOCKSTEP, rows_body, 0)


def _neighbourhood_attention(p_a, table, b, s):
    rows = s // GRID_W
    assert rows >= NA_KH and rows % NA_ROWS_PER_STEP == 0
    steps = rows // NA_ROWS_PER_STEP
    tq = NA_ROWS_PER_STEP * GRID_W
    w = MIXER_WIDTH
    return pl.pallas_call(
        functools.partial(_na_kernel, rows),
        grid=(b, steps),
        in_specs=[pl.BlockSpec((tq, w), lambda bi, i: (bi * steps + i, 0)),
                  pl.BlockSpec((s, w), lambda bi, i: (bi, 1)),
                  pl.BlockSpec((s, w), lambda bi, i: (bi, 2)),
                  pl.BlockSpec(table.shape, lambda bi, i: (0, 0, 0, 0))],
        out_specs=pl.BlockSpec((tq, w), lambda bi, i: (bi * steps + i, 0)),
        out_shape=jax.ShapeDtypeStruct((b * s, w), BF16),
        compiler_params=_cparams("parallel", "arbitrary"),
    )(p_a, p_a, p_a, table)


def _retention_tables(s):
    h, dh, c = MIXER_HEADS, HEAD_DIM, RET_CHUNK
    half = dh // 2
    inv = ROPE_BASE ** (-jnp.arange(half, dtype=F32) / half)
    ang = jnp.arange(s, dtype=F32)[:, None] * inv[None, :]
    cos, sin = jnp.cos(ang), jnp.sin(ang)
    zero = jnp.zeros_like(sin)
    cos_t = jnp.tile(jnp.concatenate([cos, cos], axis=-1), (1, h))
    sin_lo = jnp.tile(jnp.concatenate([-sin, zero], axis=-1), (1, h))
    sin_hi = jnp.tile(jnp.concatenate([zero, sin], axis=-1), (1, h))
    log_f = np.log1p(-np.exp2(-5.0 - np.arange(h, dtype=np.float64)))
    log_b = log_f[::-1]
    pos = np.arange(c, dtype=np.float64)
    diff = pos[:, None] - pos[None, :]
    dmat = np.where(diff >= 0, np.exp(log_f[:, None, None] * np.maximum(diff, 0.0)), 0.0) \
        + np.where(diff < 0, np.exp(log_b[:, None, None] * np.maximum(-diff, 0.0)), 0.0)
    lanes = lambda t: np.repeat(t.T, dh, axis=1)
    dec = np.stack([lanes(np.exp(log_f[:, None] * (pos + 1.0))),
                    lanes(np.exp(log_f[:, None] * (c - 1.0 - pos))),
                    lanes(np.exp(log_b[:, None] * (c - pos))),
                    lanes(np.exp(log_b[:, None] * pos))])
    chunk_f = [float(np.exp(v * c)) for v in log_f]
    chunk_b = [float(np.exp(v * c)) for v in log_b]
    return cos_t, sin_lo, sin_hi, jnp.asarray(dmat, F32), jnp.asarray(dec, F32), chunk_f, chunk_b


def _rotary(x, cos, sin_lo, sin_hi):
    w = x.shape[-1]
    return x * cos + pltpu.roll(x, w - HEAD_DIM // 2, 1) * sin_lo + pltpu.roll(x, HEAD_DIM // 2, 1) * sin_hi


def _retention_kernel(nb, chunk_f, chunk_b, q_ref, k_ref, v_ref, gate_ref, cos_ref, slo_ref, shi_ref,
                      dmat_ref, dec_ref, ng_ref, o_ref, sf_ref, sb_ref, sball_ref):
    t = pl.program_id(1)
    dh, c, cps = HEAD_DIM, RET_CHUNK, RET_CHUNKS_PER_STEP
    cos, slo, shi = cos_ref[...], slo_ref[...], shi_ref[...]
    kr = _rotary(k_ref[...], cos, slo, shi) * (dh ** -0.5)
    vb = v_ref[...].astype(BF16)
    items = [(ci, h) for ci in range(cps) for h in range(MIXER_HEADS)]
    part = lambda x, ci, h: x[ci * c:(ci + 1) * c, h * dh:(h + 1) * dh]
    split = lambda x: {it: part(x, *it) for it in items}

    @pl.when(t == 0)
    def _():
        sf_ref[...] = jnp.zeros_like(sf_ref)
        sb_ref[...] = jnp.zeros_like(sb_ref)

    @pl.when(t < nb)
    def _():
        blk = nb - 1 - t
        kd, v = split((kr * dec_ref[3]).astype(BF16)), split(vb)
        kv = {it: _dot_tn(kd[it], v[it]) for it in items}
        for h in range(MIXER_HEADS):
            state = sb_ref[h]
            for ci in range(cps - 1, -1, -1):
                sball_ref[blk * cps + ci, h] = state
                state = state * chunk_b[h] + kv[ci, h]
            sb_ref[h] = state

    @pl.when(t >= nb)
    def _():
        blk = t - nb
        qr = _rotary(q_ref[...], cos, slo, shi)
        q, k, v = split(qr.astype(BF16)), split(kr.astype(BF16)), split(vb)
        qf, qbk = split((qr * dec_ref[0]).astype(BF16)), split((qr * dec_ref[2]).astype(BF16))
        kd = split((kr * dec_ref[1]).astype(BF16))
        sc = {it: (_dot_nt(q[it], k[it]) * dmat_ref[it[1]]).astype(BF16) for it in items}
        o = {it: _dot(sc[it], v[it]) for it in items}
        kv = {it: _dot_tn(kd[it], v[it]) for it in items}
        ob = {it: _dot(qbk[it], sball_ref[blk * cps + it[0], it[1]].astype(BF16)) for it in items}
        of = {}
        for h in range(MIXER_HEADS):
            state = sf_ref[h]
            for ci in range(cps):
                of[ci, h] = _dot(qf[ci, h], state.astype(BF16))
                state = state * chunk_f[h] + kv[ci, h]
            sf_ref[h] = state
        o = {it: o[it] + of[it] + ob[it] for it in items}
        mu = {it: jnp.mean(o[it], axis=-1, keepdims=True) for it in items}
        oc = {it: o[it] - mu[it] for it in items}
        var = {it: jnp.mean(oc[it] * oc[it], axis=-1, keepdims=True) for it in items}
        on = {it: oc[it] * lax.rsqrt(var[it] + EPS) for it in items}
        y = jnp.concatenate([jnp.concatenate([on[ci, h] for h in range(MIXER_HEADS)], axis=-1)
                             for ci in range(cps)], axis=0)
        o_ref[...] = (y * ng_ref[...] * _silu(gate_ref[...])).astype(BF16)


def _retention(p_b, norm_g, tables, b, s):
    cos_t, sin_lo, sin_hi, dmat, dec, chunk_f, chunk_b = tables
    cps = RET_CHUNKS_PER_STEP
    tb = RET_CHUNK * cps
    nb = s // tb
    w = MIXER_WIDTH
    dec = jnp.tile(dec, (1, cps, 1))
    block = lambda t: jnp.where(t < nb, nb - 1 - t, t - nb)
    col = lambda j: pl.BlockSpec((tb, w), lambda bi, t: (bi * nb + block(t), j))
    tab = pl.BlockSpec((tb, w), lambda bi, t: (block(t), 0))
    const = lambda a: pl.BlockSpec(a.shape, lambda bi, t: (0,) * a.ndim)
    return pl.pallas_call(
        functools.partial(_retention_kernel, nb, chunk_f, chunk_b),
        grid=(b, 2 * nb),
        in_specs=[col(0), col(1), col(2), col(3), tab, tab, tab, const(dmat), const(dec), const(norm_g)],
        out_specs=pl.BlockSpec((tb, w), lambda bi, t: (bi * nb + jnp.maximum(t - nb, 0), 0)),
        out_shape=jax.ShapeDtypeStruct((b * s, w), BF16),
        scratch_shapes=[pltpu.VMEM((MIXER_HEADS, HEAD_DIM, HEAD_DIM), F32),
                        pltpu.VMEM((MIXER_HEADS, HEAD_DIM, HEAD_DIM), F32),
                        pltpu.VMEM((s // RET_CHUNK, MIXER_HEADS, HEAD_DIM, HEAD_DIM), F32)],
        compiler_params=_cparams("arbitrary", "arbitrary"),
    )(p_b, p_b, p_b, p_b, cos_t, sin_lo, sin_hi, dmat, dec, norm_g)


def _halo_specs(tile, width, n_rows, block_of):
    per = tile // HALO
    last = n_rows // HALO - 1
    main = pl.BlockSpec((tile, width), lambda *g: (block_of(*g), 0))
    prev = pl.BlockSpec((HALO, width), lambda *g: (jnp.maximum(block_of(*g) * per - 1, 0), 0))
    nxt = pl.BlockSpec((HALO, width), lambda *g: (jnp.minimum((block_of(*g) + 1) * per, last), 0))
    return main, prev, nxt


def _fill_padded(pad_ref, main, prev, nxt, first, last):
    t = main.shape[0]
    pad_ref[0:HALO, :] = jnp.where(first, 0.0, prev)
    pad_ref[HALO:HALO + t, :] = main
    pad_ref[HALO + t:2 * HALO + t, :] = jnp.where(last, 0.0, nxt)


def _pool_kernel(s, tiles_per_seq, u_ref, prev_ref, next_ref, w_ref, scale_ref, o_ref, pad_ref):
    t = POOL_TILE
    tseq = pl.program_id(0) % tiles_per_seq
    u = u_ref[...]
    _fill_padded(pad_ref, u, prev_ref[...], next_ref[...], tseq == 0, tseq == tiles_per_seq - 1)
    lane = lax.broadcasted_iota(jnp.int32, (1, MIXER_WIDTH), 1)
    half = jnp.full((1, MIXER_WIDTH), POOL_WINDOWS[0] // 2, jnp.int32)
    for gi in range(1, len(POOL_WINDOWS)):
        half = jnp.where(lane >= gi * POOL_GROUP, POOL_WINDOWS[gi] // 2, half)
    max_half = POOL_WINDOWS[-1] // 2
    acc = jnp.zeros((t, MIXER_WIDTH), F32)
    for d in range(-max_half, max_half):
        inside = (d >= -half) & (d < half)
        acc = acc + jnp.where(inside, pad_ref[HALO + d:HALO + d + t, :], 0.0)
    pos = tseq * t + lax.broadcasted_iota(jnp.int32, (t, 1), 0)
    count = jnp.minimum(pos + half, s) - jnp.maximum(pos - half, 0)
    diff = acc / count.astype(F32) - u
    o_ref[...] = (_dot(diff.astype(BF16), w_ref[...]) * scale_ref[...]).astype(BF16)


def _pool(p_c, w_blockdiag, scale, b, s):
    t = POOL_TILE
    tiles_per_seq = s // t
    main, prev, nxt = _halo_specs(t, MIXER_WIDTH, b * s, lambda i: i)
    const = lambda a: pl.BlockSpec(a.shape, lambda i: (0,) * a.ndim)
    return pl.pallas_call(
        functools.partial(_pool_kernel, s, tiles_per_seq),
        grid=(b * tiles_per_seq,),
        in_specs=[main, prev, nxt, const(w_blockdiag), const(scale)],
        out_specs=pl.BlockSpec((t, MIXER_WIDTH), lambda i: (i, 0)),
        out_shape=jax.ShapeDtypeStruct((b * s, MIXER_WIDTH), BF16),
        scratch_shapes=[pltpu.VMEM((t + 2 * HALO, MIXER_WIDTH), F32)],
        compiler_params=_cparams("parallel"),
    )(p_c, p_c, p_c, w_blockdiag, scale)


def _head_blockdiag(x):
    lane = lax.broadcasted_iota(jnp.int32, x.shape, 1)
    zero = jnp.zeros_like(x)
    return jnp.concatenate([jnp.where(lane < HEAD_DIM, x, zero), jnp.where(lane >= HEAD_DIM, x, zero)], axis=0)


def _gdn_pre_kernel(tiles_per_seq, x_ref, prev_ref, next_ref, ab_ref, cw_ref, alog_ref, dtb_ref, ones_ref,
                    u_ref, gb_ref, pad_ref):
    t, w = GDN_PRE_TILE, MIXER_WIDTH
    tseq = pl.program_id(0) % tiles_per_seq
    _fill_padded(pad_ref, x_ref[...], prev_ref[...], next_ref[...], tseq == 0, tseq == tiles_per_seq - 1)
    acc = jnp.zeros((t, 3 * w), F32)
    for k in range(CONV_K):
        off = HALO + k - CONV_K // 2
        acc = acc + pad_ref[off:off + t, :] * cw_ref[k:k + 1, :]
    u = _silu(acc)
    head_sum = lambda a: sum(_dot(piece, ones_ref[...]) for piece in _split_bf16(a, 2))
    q, k = u[:, 0:w], u[:, w:2 * w]
    u_ref[:, 0:w] = q * lax.rsqrt(head_sum(q * q) + EPS) * (HEAD_DIM ** -0.5)
    u_ref[:, w:2 * w] = k * lax.rsqrt(head_sum(k * k) + EPS)
    u_ref[:, 2 * w:3 * w] = u[:, 2 * w:3 * w]
    ab = ab_ref[...]
    x = ab + dtb_ref[...]
    g = -jnp.exp(alog_ref[...]) * (jnp.maximum(x, 0.0) + jnp.log(1.0 + jnp.exp(-jnp.abs(x))))
    beta = 1.0 / (1.0 + jnp.exp(-ab))
    lane = lax.broadcasted_iota(jnp.int32, ab.shape, 1)
    gb_ref[...] = jnp.where(lane < 2 * MIXER_HEADS, g, beta)


def _gdn_pre(p_d, p_ab, conv_w, alog_vec, dtb_vec, b, s):
    t = GDN_PRE_TILE
    w = MIXER_WIDTH
    tiles_per_seq = s // t
    ones = _block_diag(jnp.ones((MIXER_HEADS, HEAD_DIM, HEAD_DIM), BF16))
    main, prev, nxt = _halo_specs(t, 3 * w, b * s, lambda i: i)
    const = lambda a: pl.BlockSpec(a.shape, lambda i: (0,) * a.ndim)
    row = lambda c: pl.BlockSpec((t, c), lambda i: (i, 0))
    return pl.pallas_call(
        functools.partial(_gdn_pre_kernel, tiles_per_seq),
        grid=(b * tiles_per_seq,),
        in_specs=[main, prev, nxt, row(LANES), const(conv_w), const(alog_vec), const(dtb_vec), const(ones)],
        out_specs=[row(3 * w), row(LANES)],
        out_shape=[jax.ShapeDtypeStruct((b * s, 3 * w), F32), jax.ShapeDtypeStruct((b * s, LANES), F32)],
        scratch_shapes=[pltpu.VMEM((t + 2 * HALO, 3 * w), F32)],
        compiler_params=_cparams("parallel"),
    )(p_d, p_d, p_d, p_ab, conv_w, alog_vec, dtb_vec, ones)


def _gdn_prepare(forward, u, gb, tri_col, tri_row):
    c, dh, w = GDN_CHUNK, HEAD_DIM, MIXER_WIDTH
    t = u.shape[0]
    a_off, b_off = (0, 2 * MIXER_HEADS) if forward else (MIXER_HEADS, 3 * MIXER_HEADS)
    g_all = beta_all = gb
    g_parts = _split_bf16(g_all, 3)
    gc_col = sum(_dot(tri_col, p) for p in g_parts)
    gc_row = sum(_dot_tn(p, tri_row) for p in g_parts)
    ii = lax.broadcasted_iota(jnp.int32, (c, 2 * dh), 0)
    lane = lax.broadcasted_iota(jnp.int32, (c, 2 * dh), 1)
    first = lane < dh
    jj = jnp.where(first, lane, lane - dh)
    incl = (ii >= jj) if forward else (ii <= jj)
    strict = (ii > jj) if forward else (ii < jj)
    last = c - 1 if forward else 0
    items = [(ci, p) for ci in range(t // c) for p in range(MIXER_HEADS // 2)]
    rows = lambda ci: slice(ci * c, (ci + 1) * c)
    part = lambda base: [u[rows(ci), base + 2 * p * dh:base + 2 * (p + 1) * dh] for ci, p in items]
    col_pair = lambda x, off: [jnp.where(first, x[rows(ci), off + 2 * p:off + 2 * p + 1],
                                         x[rows(ci), off + 2 * p + 1:off + 2 * p + 2]) for ci, p in items]
    q, k, v = part(0), part(w), part(2 * w)
    gcc = col_pair(gc_col, a_off)
    beta = col_pair(beta_all, b_off)
    gcr = [jnp.concatenate([gc_row[a_off + 2 * p:a_off + 2 * p + 1, rows(ci)],
                            gc_row[a_off + 2 * p + 1:a_off + 2 * p + 2, rows(ci)]], axis=1) for ci, p in items]
    g_last = [a[last:last + 1, :] for a in gcc]
    decay = [jnp.where(incl, jnp.exp(jnp.where(incl, a - b, 0.0)), 0.0) for a, b in zip(gcc, gcr)]
    kq = [_dot_nt(jnp.concatenate([kk, qq], axis=0).astype(BF16), _head_blockdiag(kk.astype(BF16)))
          for kk, qq in zip(k, q)]
    lmat = [jnp.where(strict, m[:c] * b * d, 0.0) for m, b, d in zip(kq, beta, decay)]
    attn = [(m[c:] * d).astype(BF16) for m, d in zip(kq, decay)]
    xinv = [-a for a in lmat]
    lp = [a.astype(BF16) for a in lmat]
    lp_bd = [_head_blockdiag(a) for a in lp]
    for _ in range(int(math.log2(c)) - 1):
        lp32 = [_dot(a, bd) for a, bd in zip(lp, lp_bd)]
        lp = [a.astype(BF16) for a in lp32]
        lp_bd = [_head_blockdiag(a) for a in lp]
        xinv = [xi + a32 + _dot(xi.astype(BF16), bd) for xi, a32, bd in zip(xinv, lp32, lp_bd)]
    egc = [jnp.exp(a) for a in gcc]
    xb = [a.astype(BF16) for a in xinv]
    vbeta = [vv * b for vv, b in zip(v, beta)]
    kbeta = [kk * (b * e) for kk, b, e in zip(k, beta, egc)]
    u_val = [r + _dot(xi, _head_blockdiag(r.astype(BF16))) for xi, r in zip(xb, vbeta)]
    w_dec = [r + _dot(xi, _head_blockdiag(r.astype(BF16))) for xi, r in zip(xb, kbeta)]
    wq = [jnp.concatenate([wd, qq * e], axis=0).astype(BF16) for wd, qq, e in zip(w_dec, q, egc)]
    k_state = [(kk * jnp.exp(gl - a)).astype(BF16) for kk, gl, a in zip(k, g_last, gcc)]
    e_last = [jnp.exp(gl) for gl in g_last]
    keyed = lambda vals: dict(zip(items, vals))
    return dict(wq=keyed(wq), u_val=keyed(u_val), attn=keyed(attn), k_state=keyed(k_state), e_last=keyed(e_last))


def _gdn_kernel(uf_ref, gbf_ref, ub_ref, gbb_ref, tril_ref, triu_ref, of_ref, ob_ref, sf_ref, sb_ref):
    n = pl.program_id(1)
    t = uf_ref.shape[0]
    c, dh = GDN_CHUNK, HEAD_DIM
    nch = t // c
    pairs = MIXER_HEADS // 2

    @pl.when(n == 0)
    def _():
        sf_ref[...] = jnp.zeros_like(sf_ref)
        sb_ref[...] = jnp.zeros_like(sb_ref)

    tril, triu = tril_ref[...], triu_ref[...]
    prep_f = _gdn_prepare(True, uf_ref[...], gbf_ref[...], tril, triu)
    prep_b = _gdn_prepare(False, ub_ref[...], gbb_ref[...], triu, tril)

    first = lax.broadcasted_iota(jnp.int32, (dh, 2 * dh), 1) < dh
    chains = [(prep_f, sf_ref, of_ref, p, True) for p in range(pairs)] + \
             [(prep_b, sb_ref, ob_ref, p, False) for p in range(pairs)]
    state = [s_ref[p] for _, s_ref, _, p, _ in chains]
    for step in range(nch):
        key = [(step if fw else nch - 1 - step, p) for _, _, _, p, fw in chains]
        ws = [_dot(pr['wq'][kk], _head_blockdiag(st.astype(BF16))) for (pr, *_), kk, st in zip(chains, key, state)]
        v_new = [pr['u_val'][kk] - a[:c] for (pr, *_), kk, a in zip(chains, key, ws)]
        vb = [a.astype(BF16) for a in v_new]
        out = [a[c:] + _dot(pr['attn'][kk], _head_blockdiag(b)) for (pr, *_), kk, a, b in zip(chains, key, ws, vb)]
        cross = [_dot_tn(pr['k_state'][kk], b) for (pr, *_), kk, b in zip(chains, key, vb)]
        state = [st * pr['e_last'][kk] + jnp.where(first, x[:dh], x[dh:])
                 for (pr, *_), kk, st, x in zip(chains, key, state, cross)]
        for (_, _, o_ref, p, _), (ci, _), o in zip(chains, key, out):
            o_ref[ci * c:(ci + 1) * c, 2 * p * dh:2 * (p + 1) * dh] = o
    for (_, s_ref, _, p, _), st in zip(chains, state):
        s_ref[p] = st


def _gated_deltanet(p_d, p_ab, conv_w, alog_vec, dtb_vec, b, s):
    u, gb = _gdn_pre(p_d, p_ab, conv_w, alog_vec, dtb_vec, b, s)
    c = GDN_CHUNK
    t = c * GDN_CHUNKS_PER_STEP
    nb = s // t
    w = MIXER_WIDTH
    r = np.arange(t)
    same = (r[:, None] // c) == (r[None, :] // c)
    tril = jnp.asarray(same & (r[:, None] >= r[None, :]), BF16)
    triu = jnp.asarray(same & (r[:, None] <= r[None, :]), BF16)
    fwd = lambda bi, n: (bi * nb + n, 0)
    bwd = lambda bi, n: (bi * nb + nb - 1 - n, 0)
    const = lambda a: pl.BlockSpec(a.shape, lambda bi, n: (0,) * a.ndim)
    return pl.pallas_call(
        _gdn_kernel,
        grid=(b, nb),
        in_specs=[pl.BlockSpec((t, 3 * w), fwd), pl.BlockSpec((t, LANES), fwd),
                  pl.BlockSpec((t, 3 * w), bwd), pl.BlockSpec((t, LANES), bwd), const(tril), const(triu)],
        out_specs=[pl.BlockSpec((t, w), fwd), pl.BlockSpec((t, w), bwd)],
        out_shape=[jax.ShapeDtypeStruct((b * s, w), F32), jax.ShapeDtypeStruct((b * s, w), F32)],
        scratch_shapes=[pltpu.VMEM((MIXER_HEADS // 2, HEAD_DIM, 2 * HEAD_DIM), F32),
                        pltpu.VMEM((MIXER_HEADS // 2, HEAD_DIM, 2 * HEAD_DIM), F32)],
        compiler_params=_cparams("arbitrary", "arbitrary"),
    )(u, gb, u, gb, tril, triu)


def _mem_kv_kernel(m_ref, g_ref, wk_ref, wv_ref, k_ref, v_ref):
    h = _rms(m_ref[0], g_ref[...]).astype(BF16)
    k_ref[0] = _dot(h, wk_ref[...]).astype(BF16)
    v_ref[0] = _dot(h, wv_ref[...]).astype(BF16)


def _mem_kv(mem, g, wk, wv):
    b, nm, d = mem.shape
    const = lambda a: pl.BlockSpec(a.shape, lambda i: (0,) * a.ndim)
    blk = pl.BlockSpec((1, nm, d), lambda i: (i, 0, 0))
    return pl.pallas_call(
        _mem_kv_kernel, grid=(b,),
        in_specs=[blk, const(g), const(wk), const(wv)], out_specs=[blk, blk],
        out_shape=[jax.ShapeDtypeStruct((b, nm, d), BF16)] * 2,
        compiler_params=_cparams("parallel"),
    )(mem, g, wk, wv)


def _route_lanes(lg):
    lane_i = lax.broadcasted_iota(jnp.int32, lg.shape, 1)
    lane = lane_i.astype(F32)
    first_at = lambda vals, m: jnp.min(jnp.where(vals == m, lane, float(LANES)), axis=-1, keepdims=True)
    g_mask = lane_i < N_GROUPS
    gl = jnp.where(g_mask, lg, NEG_BIG)
    g_max = jnp.max(gl, axis=-1, keepdims=True)
    g_sum = jnp.sum(jnp.where(g_mask, jnp.exp(gl - g_max), 0.0), axis=-1, keepdims=True)
    g_p = 1.0 / g_sum
    g_idx = first_at(gl, g_max)
    lo = N_GROUPS + EXPERTS_PER_GROUP * g_idx
    e_mask = (lane >= lo) & (lane < lo + EXPERTS_PER_GROUP)
    el = jnp.where(e_mask, lg, NEG_BIG)
    m1 = jnp.max(el, axis=-1, keepdims=True)
    i1 = first_at(el, m1)
    el2 = jnp.where(lane == i1, NEG_BIG, el)
    m2 = jnp.max(el2, axis=-1, keepdims=True)
    i2 = first_at(el2, m2)
    r = jnp.exp(m2 - m1)
    w1 = g_p / (1.0 + r)
    w2 = g_p * r / (1.0 + r)
    e1, e2 = i1 - N_GROUPS, i2 - N_GROUPS
    e_lo, e_hi = jnp.minimum(e1, e2), jnp.maximum(e1, e2)
    w_lo, w_hi = jnp.where(e1 < e2, w1, w2), jnp.where(e1 < e2, w2, w1)
    a, b = e_lo - EXPERTS_PER_GROUP * g_idx, e_hi - EXPERTS_PER_GROUP * g_idx
    cls = g_idx * PAIRS_PER_GROUP + a * EXPERTS_PER_GROUP - a * (a + 1.0) * 0.5 + (b - a - 1.0)
    out = jnp.where(lane_i == 0, e_lo, 0.0)
    out = jnp.where(lane_i == 1, e_hi, out)
    out = jnp.where(lane_i == 2, w_lo, out)
    out = jnp.where(lane_i == 3, w_hi, out)
    out = jnp.where(lane_i == 4, cls, out)
    return out, jnp.where(lane == cls, 1.0, 0.0)


def _post_kernel(x_ref, ya_ref, yb_ref, yc_ref, of_ref, ob_ref, gate_ref, seg_ref, gg_ref, wout_ref, gq_ref,
                 wq_ref, k_ref, v_ref, wo_ref, gf_ref, wr_ref, br_ref, xo_ref, hx_ref, rt_ref, hist_ref):
    w = MIXER_WIDTH
    o = of_ref[...] + ob_ref[...]
    ms = sum(_dot(p, seg_ref[...]) for p in _split_bf16(o * o, 2))
    yd = o * lax.rsqrt(ms + EPS) * gg_ref[...] * _silu(gate_ref[...])
    mix = _dot(ya_ref[...], wout_ref[0:w, :]) + _dot(yb_ref[...], wout_ref[w:2 * w, :])
    mix = mix + _dot(yc_ref[...], wout_ref[2 * w:3 * w, :]) + _dot(yd.astype(BF16), wout_ref[3 * w:4 * w, :])
    x1 = x_ref[...] + mix
    q = _dot(_rms(x1, gq_ref[...]).astype(BF16), wq_ref[...]).astype(BF16)
    cols = [slice(h * MEM_HEAD_DIM, (h + 1) * MEM_HEAD_DIM) for h in range(MEM_HEADS)]
    sc = [_dot_nt(q[:, sl], k_ref[0, :, sl]) * (MEM_HEAD_DIM ** -0.5) for sl in cols]
    p = [jnp.exp(a - jnp.max(a, axis=-1, keepdims=True)) for a in sc]
    l = [jnp.sum(a, axis=-1, keepdims=True) for a in p]
    heads = [(_dot(a.astype(BF16), v_ref[0, :, sl]) / b).astype(BF16) for a, b, sl in zip(p, l, cols)]
    x2 = x1 + _dot(jnp.concatenate(heads, axis=-1), wo_ref[...])
    xo_ref[...] = x2
    hx = _rms(x2, gf_ref[...])
    _to_row_tiles(hx_ref, hx)
    h_hi, h_lo = _split_bf16(hx, 2)
    lg = _dot(h_hi, wr_ref[0]) + _dot(h_hi, wr_ref[1]) + _dot(h_lo, wr_ref[0]) + br_ref[...]
    route, chosen = _route_lanes(lg)
    rt_ref[...] = route
    hist_ref[...] = jnp.broadcast_to(jnp.sum(chosen, axis=0, keepdims=True), hist_ref.shape)


def _post(x, y_a, y_b, y_c, o_f, o_b, p_d, seg, gdn_g, w_out, g_q, w_q, k_mem, v_mem, w_o, g_f, w_r, b_r, s):
    n = x.shape[0]
    tm = POST_TILE
    w = MIXER_WIDTH
    per_seq = s // tm
    row = lambda c: pl.BlockSpec((tm, c), lambda i: (i, 0))
    const = lambda a: pl.BlockSpec(a.shape, lambda i: (0,) * a.ndim)
    kv = pl.BlockSpec((1,) + k_mem.shape[1:], lambda i: (i // per_seq, 0, 0))
    return pl.pallas_call(
        _post_kernel, grid=(n // tm,),
        in_specs=[row(D_MODEL), row(w), row(w), row(w), row(w), row(w),
                  pl.BlockSpec((tm, w), lambda i: (i, 3)), const(seg), const(gdn_g), const(w_out), const(g_q),
                  const(w_q), kv, kv, const(w_o), const(g_f), const(w_r), const(b_r)],
        out_specs=[row(D_MODEL), pl.BlockSpec((tm * ROW_TILES, LANES), lambda i: (i, 0)), row(LANES),
                   pl.BlockSpec((SUBLANES, LANES), lambda i: (i, 0))],
        out_shape=[jax.ShapeDtypeStruct((n, D_MODEL), F32), jax.ShapeDtypeStruct((n * ROW_TILES, LANES), F32),
                   jax.ShapeDtypeStruct((n, LANES), F32), jax.ShapeDtypeStruct((n // tm * SUBLANES, LANES), F32)],
        compiler_params=_cparams("parallel"),
    )(x, y_a, y_b, y_c, o_f, o_b, p_d, seg, gdn_g, w_out, g_q, w_q, k_mem, v_mem, w_o, g_f, w_r, b_r)


def _pair_class_experts():
    lo, hi = [], []
    for g in range(N_GROUPS):
        for a in range(EXPERTS_PER_GROUP):
            for b in range(a + 1, EXPERTS_PER_GROUP):
                lo.append(g * EXPERTS_PER_GROUP + a)
                hi.append(g * EXPERTS_PER_GROUP + b)
    return np.asarray(lo, np.int32), np.asarray(hi, np.int32)


def _dispatch(route, hist):
    n = route.shape[0]
    blk = MOE_BLOCK
    cls = route[:, 4].astype(jnp.int32)
    _, order, w_lo, w_hi = lax.sort((cls, jnp.arange(n, dtype=jnp.int32), route[:, TOP_K], route[:, TOP_K + 1]),
                                    num_keys=1)
    counts = jnp.sum(hist.reshape(-1, SUBLANES, LANES)[:, 0, :N_CLASSES], axis=0).astype(jnp.int32)
    padded = (counts + blk - 1) // blk * blk
    pad_end = jnp.cumsum(padded)
    pad_start = pad_end - padded
    start = jnp.cumsum(counts) - counts
    n_blocks = n // blk + N_CLASSES
    blk_start = jnp.arange(n_blocks, dtype=jnp.int32) * blk
    blk_cls = jnp.minimum(jnp.sum(pad_end[None, :] <= blk_start[:, None], axis=1), N_CLASSES - 1).astype(jnp.int32)
    cls_lo, cls_hi = _pair_class_experts()
    into_cls = blk_start - pad_start[blk_cls]
    blk_valid = jnp.clip(counts[blk_cls] - into_cls, 0, blk)
    in_blk = jnp.arange(blk, dtype=jnp.int32)[None, :]
    idx = (start[blk_cls] + into_cls)[:, None] + in_blk
    valid = in_blk < blk_valid[:, None]
    idx = jnp.clip(idx, 0, n - 1)
    row_tok = jnp.where(valid, order[idx], n + in_blk)
    row_w = jnp.stack([jnp.where(valid, w_lo[idx], 0.0), jnp.where(valid, w_hi[idx], 0.0)], axis=-1)
    row_w = row_w.reshape(-1, TOP_K)
    n_used = (pad_end[-1] // blk).astype(jnp.int32).reshape(1)
    return (n_used, jnp.asarray(cls_lo)[blk_cls], jnp.asarray(cls_hi)[blk_cls],
            jnp.concatenate([n + in_blk, row_tok], axis=0).reshape(-1), row_w)


def _moe_kernel(n_tokens, n_blocks, nu_ref, ea_ref, eb_ref, dst_ref, hx_hbm, rw_ref, wga_ref, wua_ref, wda_ref,
                wgb_ref, wub_ref, wdb_ref, y_hbm, xbuf, ybuf, gsem, ssem):
    j = pl.program_id(0)
    blk, rt = MOE_BLOCK, ROW_TILES
    n_used = nu_ref[0]
    slot = lax.rem(j, MOE_BUFFERS)
    ahead = lax.rem(j + 2, MOE_BUFFERS)
    tile = lambda ref, i: ref.at[pl.ds(pl.multiple_of(i * rt, rt), rt), :]
    dst = lambda jj, i: dst_ref[(jj + 1) * blk + i]
    clamp = lambda jj: jnp.minimum(jj, n_blocks - 1)

    def gather_row(jj, sl, i):
        tok = jnp.minimum(dst(jj, i), n_tokens - 1)
        pltpu.make_async_copy(tile(hx_hbm, tok), tile(xbuf.at[sl], i), gsem.at[sl]).start()

    def scatter_row(jj, sl, i):
        pltpu.make_async_copy(tile(ybuf.at[sl], i), tile(y_hbm, dst(jj, i)), ssem.at[sl]).start()

    def wait_gather(sl):
        pltpu.make_async_copy(hx_hbm.at[pl.ds(0, blk * rt), :], xbuf.at[sl], gsem.at[sl]).wait()

    def wait_scatter(sl):
        pltpu.make_async_copy(ybuf.at[sl], y_hbm.at[pl.ds(0, blk * rt), :], ssem.at[sl]).wait()

    def rolled(fn):
        lax.fori_loop(0, blk, lambda i, carry: (fn(i), carry)[1], 0, unroll=DMA_UNROLL)

    @pl.when(j == 0)
    def _():
        ybuf[MOE_BUFFERS - 1] = jnp.zeros((blk * rt, LANES), F32)
        rolled(lambda i: gather_row(0, 0, i))
        rolled(lambda i: gather_row(clamp(1), 1, i))

    @pl.when(j < n_used)
    def _():
        wait_gather(slot)

        @pl.when(j >= 2)
        def _():
            wait_scatter(slot)

        per = blk // MOE_PIECES

        def issue_copies(piece):
            for i in range(piece * per, (piece + 1) * per):
                gather_row(clamp(j + 2), ahead, i)
                scatter_row(j - 1, ahead, i)

        xs = _from_row_tiles(xbuf.at[slot], 0, blk, rt).astype(BF16)
        half = D_EXPERT // 2
        hidden = []
        for e, (wg_ref, wu_ref) in enumerate(((wga_ref, wua_ref), (wgb_ref, wub_ref))):
            acts = []
            for c in range(2):
                cols = slice(c * half, (c + 1) * half)
                g = _dot(xs, wg_ref[0, 0, :, cols])
                issue_copies(4 * e + 2 * c)
                u = _dot(xs, wu_ref[0, 0, :, cols])
                issue_copies(4 * e + 2 * c + 1)
                acts.append((_silu(g) * u).astype(BF16))
            hidden.append(jnp.concatenate(acts, axis=1))
        w_a, w_b = rw_ref[:, 0:1], rw_ref[:, 1:2]
        quarter = D_MODEL // 4
        for c in range(4):
            cols = slice(c * quarter, (c + 1) * quarter)
            y = w_a * _dot(hidden[0], wda_ref[0, 0, :, cols]) + w_b * _dot(hidden[1], wdb_ref[0, 0, :, cols])
            for k in range(quarter // LANES):
                ybuf[slot, pl.ds(c * (quarter // LANES) + k, blk, stride=rt), :] = y[:, k * LANES:(k + 1) * LANES]

    @pl.when(j == n_used)
    def _():
        after = lax.rem(j + 1, MOE_BUFFERS)
        wait_gather(slot)
        wait_gather(after)

        @pl.when(j >= 2)
        def _():
            wait_scatter(slot)

        wait_scatter(after)
        rolled(lambda i: scatter_row(j - 1, ahead, i))
        wait_scatter(ahead)


def _moe(hx_tiles, n_used, blk_ea, blk_eb, row_tok, row_w, w_gate, w_up, w_down, layer, n):
    blk = MOE_BLOCK
    n_blocks = blk_ea.shape[0]
    block = lambda j: jnp.minimum(j, n_blocks - 1)
    first = lambda j, nu, ea, eb, dst: (layer, ea[block(j)], 0, 0)
    second = lambda j, nu, ea, eb, dst: (layer, eb[block(j)], 0, 0)
    up, down = (1, 1, D_MODEL, D_EXPERT), (1, 1, D_EXPERT, D_MODEL)
    grid_spec = pltpu.PrefetchScalarGridSpec(
        num_scalar_prefetch=4, grid=(n_blocks + 1,),
        in_specs=[pl.BlockSpec(memory_space=pl.ANY),
                  pl.BlockSpec((blk, TOP_K), lambda j, nu, ea, eb, dst: (block(j), 0)),
                  pl.BlockSpec(up, first), pl.BlockSpec(up, first), pl.BlockSpec(down, first),
                  pl.BlockSpec(up, second), pl.BlockSpec(up, second), pl.BlockSpec(down, second)],
        out_specs=pl.BlockSpec(memory_space=pl.ANY),
        scratch_shapes=[pltpu.VMEM((MOE_BUFFERS, blk * ROW_TILES, LANES), F32),
                        pltpu.VMEM((MOE_BUFFERS, blk * ROW_TILES, LANES), F32),
                        pltpu.SemaphoreType.DMA((MOE_BUFFERS,)), pltpu.SemaphoreType.DMA((MOE_BUFFERS,))])
    return pl.pallas_call(
        functools.partial(_moe_kernel, n, n_blocks), grid_spec=grid_spec,
        out_shape=jax.ShapeDtypeStruct(((n + blk) * ROW_TILES, LANES), F32),
        compiler_params=_cparams("arbitrary"),
    )(n_used, blk_ea, blk_eb, row_tok, hx_tiles, row_w, w_gate, w_up, w_down, w_gate, w_up, w_down)


def _final_kernel(x_ref, y_ref, g_ref, o_ref):
    o_ref[...] = _rms(_add_moe(x_ref[...], y_ref), g_ref[...])


def _final(x, moe, g):
    n = x.shape[0]
    tm = PROJ_TILE
    return pl.pallas_call(
        _final_kernel, grid=(n // tm,),
        in_specs=[pl.BlockSpec((tm, D_MODEL), lambda i: (i, 0)), _moe_out_spec(tm),
                  pl.BlockSpec(g.shape, lambda i: (0, 0))],
        out_specs=pl.BlockSpec((tm, D_MODEL), lambda i: (i, 0)),
        out_shape=jax.ShapeDtypeStruct((n, D_MODEL), F32),
        compiler_params=_cparams("parallel"),
    )(x, moe, g)


def _lane_vec(v, width=LANES):
    v = v.reshape(1, -1).astype(F32)
    return jnp.pad(v, ((0, 0), (0, width - v.shape[1])))


def _block_diag(blocks):
    g, c, _ = blocks.shape
    eye = jnp.eye(g, dtype=blocks.dtype)
    return (eye[:, None, :, None] * blocks[:, :, None, :]).reshape(g * c, g * c)


def kernel(x, mem, mix_norm_g, w_in, na_rpb, ret_norm_g, pool_w, pool_scale, gdn_conv_w, gdn_a_log, gdn_dt_bias, gdn_norm_g, w_out, mem_q_norm_g, mem_kv_norm_g, mem_w_q, mem_w_k, mem_w_v, mem_w_o, ffn_norm_g, w_group, b_group, w_router, b_router, w_gate, w_up, w_down, final_norm_g):
    b, s, d = x.shape
    n = b * s
    depth = w_in.shape[0]
    row = lambda v: v.reshape(1, -1).astype(F32)
    ret_tables = _retention_tables(s)
    seg = _block_diag(jnp.full((MIXER_HEADS, HEAD_DIM, HEAD_DIM), 1.0 / HEAD_DIM, BF16))
    wg_bf, wu_bf, wd_bf = w_gate.astype(BF16), w_up.astype(BF16), w_down.astype(BF16)
    xs = x.reshape(n, d)
    moe = None
    for l in range(depth):
        w_pad = jnp.pad(w_in[l], ((0, 0), (0, P_IN_PAD - w_in.shape[2]))).astype(BF16)
        xs, p_a, p_b, p_c, p_d, p_ab = _norm_proj(xs, moe, row(mix_norm_g[l]), w_pad)
        y_a = _neighbourhood_attention(p_a, _na_bias_table(na_rpb[l]), b, s)
        y_b = _retention(p_b, row(ret_norm_g[l]), ret_tables, b, s)
        y_c = _pool(p_c, _block_diag(pool_w[l]).astype(BF16), row(pool_scale[l]), b, s)
        o_f, o_b = _gated_deltanet(p_d, p_ab, gdn_conv_w[l].astype(F32), _lane_vec(gdn_a_log[l]),
                                   _lane_vec(gdn_dt_bias[l]), b, s)
        k_mem, v_mem = _mem_kv(mem, row(mem_kv_norm_g[l]), mem_w_k[l].astype(BF16), mem_w_v[l].astype(BF16))
        w_r = jnp.pad(jnp.concatenate([w_group[l], w_router[l]], axis=1),
                      ((0, 0), (0, LANES - N_GROUPS - N_EXPERTS))).astype(F32)
        b_r = _lane_vec(jnp.concatenate([b_group[l], b_router[l]]))
        xs, hx_tiles, route, hist = _post(
            xs, y_a, y_b, y_c, o_f, o_b, p_d, seg, row(jnp.tile(gdn_norm_g[l], MIXER_HEADS)), w_out[l].astype(BF16),
            row(mem_q_norm_g[l]), mem_w_q[l].astype(BF16), k_mem, v_mem, mem_w_o[l].astype(BF16),
            row(ffn_norm_g[l]), jnp.stack(_split_bf16(w_r, 2)), b_r, s)
        moe = _moe(hx_tiles, *_dispatch(route, hist), wg_bf, wu_bf, wd_bf, l, n)
    return _final(xs, moe, row(final_norm_g)).reshape(b, s, d)
```

```python
import functools
import math

import jax
import jax.numpy as jnp
import numpy as np
from jax import lax
from jax.experimental import pallas as pl
from jax.experimental.pallas import tpu as pltpu

F32 = jnp.float32
BF16 = jnp.bfloat16
HIGHEST = lax.Precision.HIGHEST

D_MODEL = 1024
GRID_W = 64
HEAD_DIM = 64
MIXER_WIDTH = 256
MIXER_HEADS = 4
NA_KH = 8
NA_KW = 16
RET_CHUNK = 128
ROPE_BASE = 10000.0
POOL_WINDOWS = (2, 4, 8, 16)
POOL_GROUP = 64
GDN_CHUNK = 64
CONV_K = 4
MEM_HEADS = 4
MEM_HEAD_DIM = 256
N_GROUPS = 4
EXPERTS_PER_GROUP = 8
N_EXPERTS = 32
TOP_K = 2
D_EXPERT = 512
EPS = 1e-6

P_IN_PAD = 3200
V7X_VMEM_LIMIT = 56 * 1024 * 1024
LANES = 128
SUBLANES = 8
ROW_TILES = D_MODEL // LANES
NEG_BIG = -1e30

PROJ_TILE = 512
POOL_TILE = 512
POST_TILE = 512
NA_ROWS_PER_STEP = 8
NA_ROWS_LOCKSTEP = 4
GDN_CHUNKS_PER_STEP = 8
GDN_PRE_TILE = 512
RET_CHUNKS_PER_STEP = 8
PAIRS_PER_GROUP = EXPERTS_PER_GROUP * (EXPERTS_PER_GROUP - 1) // 2
N_CLASSES = N_GROUPS * PAIRS_PER_GROUP
MOE_BLOCK = 128
MOE_BUFFERS = 3
MOE_PIECES = 12
DMA_UNROLL = 8
HALO = 8


def _cparams(*sem):
    return pltpu.CompilerParams(dimension_semantics=sem, vmem_limit_bytes=V7X_VMEM_LIMIT)


def _dot(a, b):
    return jnp.dot(a, b, preferred_element_type=F32)


def _dot_nt(a, b):
    return lax.dot_general(a, b, (((1,), (1,)), ((), ())), preferred_element_type=F32)


def _dot_tn(a, b, precision=None):
    return lax.dot_general(a, b, (((0,), (0,)), ((), ())), preferred_element_type=F32, precision=precision)


def _silu(x):
    return x * (1.0 / (1.0 + jnp.exp(-x)))


def _rms(x, g):
    return x * lax.rsqrt(jnp.mean(x * x, axis=-1, keepdims=True) + EPS) * g


def _from_row_tiles(ref, first, count, stride):
    return jnp.concatenate([ref[pl.ds(first + k, count, stride=stride), :] for k in range(ROW_TILES)], axis=1)


def _to_row_tiles(ref, x):
    count = x.shape[0]
    for k in range(ROW_TILES):
        ref[pl.ds(k, count, stride=ROW_TILES), :] = x[:, k * LANES:(k + 1) * LANES]


def _add_moe(x, y_ref):
    return x + _from_row_tiles(y_ref, 0, x.shape[0], ROW_TILES)


def _split_bf16(x, parts):
    out = []
    for _ in range(parts):
        p = x.astype(BF16)
        out.append(p)
        x = x - p.astype(F32)
    return out


def _norm_proj_kernel(has_y, *refs):
    if has_y:
        x_ref, y_ref, g_ref, w_ref, xo_ref, pa_ref, pb_ref, pc_ref, pd_ref, pab_ref = refs
        x = _add_moe(x_ref[...], y_ref)
        xo_ref[...] = x
    else:
        x_ref, g_ref, w_ref, pa_ref, pb_ref, pc_ref, pd_ref, pab_ref = refs
        x = x_ref[...]
    h = _rms(x, g_ref[...]).astype(BF16)
    w = MIXER_WIDTH
    pa_ref[...] = _dot(h, w_ref[:, 0:3 * w]).astype(BF16)
    pb_ref[...] = _dot(h, w_ref[:, 3 * w:7 * w])
    pc_ref[...] = _dot(h, w_ref[:, 7 * w:8 * w])
    pd_ref[...] = _dot(h, w_ref[:, 8 * w:12 * w])
    pab_ref[...] = _dot(h, w_ref[:, 12 * w:P_IN_PAD])


def _moe_out_spec(tile):
    return pl.BlockSpec((tile * ROW_TILES, LANES), lambda i: (i, 0))


def _norm_proj(x, moe, g, w_pad):
    n = x.shape[0]
    tm = PROJ_TILE
    w = MIXER_WIDTH
    row = lambda c: pl.BlockSpec((tm, c), lambda i: (i, 0))
    const = lambda a: pl.BlockSpec(a.shape, lambda i: (0,) * a.ndim)
    outs = [jax.ShapeDtypeStruct((n, 3 * w), BF16), jax.ShapeDtypeStruct((n, 4 * w), F32),
            jax.ShapeDtypeStruct((n, w), F32), jax.ShapeDtypeStruct((n, 4 * w), F32),
            jax.ShapeDtypeStruct((n, LANES), F32)]
    out_specs = [row(3 * w), row(4 * w), row(w), row(4 * w), row(LANES)]
    if moe is None:
        ins, in_specs = (x, g, w_pad), [row(D_MODEL), const(g), const(w_pad)]
    else:
        ins = (x, moe, g, w_pad)
        in_specs = [row(D_MODEL), _moe_out_spec(tm), const(g), const(w_pad)]
        outs = [jax.ShapeDtypeStruct((n, D_MODEL), F32)] + outs
        out_specs = [row(D_MODEL)] + out_specs
    res = pl.pallas_call(
        functools.partial(_norm_proj_kernel, moe is not None),
        grid=(n // tm,), in_specs=in_specs, out_specs=out_specs, out_shape=outs,
        compiler_params=_cparams("parallel"),
    )(*ins)
    return res if moe is not None else [x] + list(res)


def _na_bias_table(rpb):
    qc = np.arange(GRID_W)
    kc = np.arange(GRID_W)
    d_col = np.clip(kc[None, :] - qc[:, None], -(NA_KW - 1), NA_KW - 1) + (NA_KW - 1)
    pick = np.zeros((2 * NA_KW - 1, GRID_W * GRID_W), np.float32)
    pick[d_col.reshape(-1), np.arange(GRID_W * GRID_W)] = 1.0
    win = np.clip(qc - NA_KW // 2, 0, GRID_W - NA_KW)
    ok = (kc[None, :] >= win[:, None]) & (kc[None, :] < win[:, None] + NA_KW)
    cols = jnp.einsum('hrc,cx->hrx', rpb.astype(F32), jnp.asarray(pick), precision=HIGHEST)
    cols = jnp.where(ok[None, None], cols.reshape(MIXER_HEADS, 2 * NA_KH - 1, GRID_W, GRID_W), NEG_BIG)
    per_e = [jnp.transpose(cols[:, NA_KH - 1 - e:2 * NA_KH - 1 - e], (0, 2, 1, 3)) for e in range(NA_KH)]
    return jnp.stack(per_e, axis=1).reshape(MIXER_HEADS, NA_KH, GRID_W, NA_KH * GRID_W)


def _na_kernel(rows, q_ref, k_ref, v_ref, t_ref, o_ref):
    i = pl.program_id(1)
    dh = HEAD_DIM

    def rows_body(it, carry):
        chains = []
        for k in range(NA_ROWS_LOCKSTEP):
            rr = it * NA_ROWS_LOCKSTEP + k
            r = i * NA_ROWS_PER_STEP + rr
            kr0 = jnp.clip(r - NA_KH // 2, 0, rows - NA_KH)
            q = q_ref[pl.ds(pl.multiple_of(rr * GRID_W, GRID_W), GRID_W), :]
            k0 = pl.multiple_of(kr0 * GRID_W, GRID_W)
            kb = k_ref[pl.ds(k0, NA_KH * GRID_W), :]
            vb = v_ref[pl.ds(k0, NA_KH * GRID_W), :]
            for h in range(MIXER_HEADS):
                sl = slice(h * dh, (h + 1) * dh)
                chains.append((q[:, sl], kb[:, sl], vb[:, sl], h, r - kr0))
        s = [_dot_nt(q, kb) * (dh ** -0.5) + t_ref[h, e] for q, kb, _, h, e in chains]
        m = [jnp.max(a, axis=-1, keepdims=True) for a in s]
        p = [jnp.exp(a - b) for a, b in zip(s, m)]
        l = [jnp.sum(a, axis=-1, keepdims=True) for a in p]
        o = [_dot(a.astype(BF16), c[2]) / b for a, b, c in zip(p, l, chains)]
        for k in range(NA_ROWS_LOCKSTEP):
            rr = it * NA_ROWS_LOCKSTEP + k
            row = jnp.concatenate(o[k * MIXER_HEADS:(k + 1) * MIXER_HEADS], axis=-1)
            o_ref[pl.ds(pl.multiple_of(rr * GRID_W, GRID_W), GRID_W), :] = row.astype(BF16)
        return carry

    lax.fori_loop(0, NA_ROWS_PER_STEP // NA_ROWS_LOCKSTEP, rows_body, 0)


def _neighbourhood_attention(p_a, table, b, s):
    rows = s // GRID_W
    assert rows >= NA_KH and rows % NA_ROWS_PER_STEP == 0
    steps = rows // NA_ROWS_PER_STEP
    tq = NA_ROWS_PER_STEP * GRID_W
    w = MIXER_WIDTH
    return pl.pallas_call(
        functools.partial(_na_kernel, rows),
        grid=(b, steps),
        in_specs=[pl.BlockSpec((tq, w), lambda bi, i: (bi * steps + i, 0)),
                  pl.BlockSpec((s, w), lambda bi, i: (bi, 1)),
                  pl.BlockSpec((s, w), lambda bi, i: (bi, 2)),
                  pl.BlockSpec(table.shape, lambda bi, i: (0, 0, 0, 0))],
        out_specs=pl.BlockSpec((tq, w), lambda bi, i: (bi * steps + i, 0)),
        out_shape=jax.ShapeDtypeStruct((b * s, w), BF16),
        compiler_params=_cparams("parallel", "arbitrary"),
    )(p_a, p_a, p_a, table)


def _retention_tables(s):
    h, dh, c = MIXER_HEADS, HEAD_DIM, RET_CHUNK
    half = dh // 2
    inv = ROPE_BASE ** (-jnp.arange(half, dtype=F32) / half)
    ang = jnp.arange(s, dtype=F32)[:, None] * inv[None, :]
    cos, sin = jnp.cos(ang), jnp.sin(ang)
    zero = jnp.zeros_like(sin)
    cos_t = jnp.tile(jnp.concatenate([cos, cos], axis=-1), (1, h))
    sin_lo = jnp.tile(jnp.concatenate([-sin, zero], axis=-1), (1, h))
    sin_hi = jnp.tile(jnp.concatenate([zero, sin], axis=-1), (1, h))
    log_f = np.log1p(-np.exp2(-5.0 - np.arange(h, dtype=np.float64)))
    log_b = log_f[::-1]
    pos = np.arange(c, dtype=np.float64)
    diff = pos[:, None] - pos[None, :]
    dmat = np.where(diff >= 0, np.exp(log_f[:, None, None] * np.maximum(diff, 0.0)), 0.0) \
        + np.where(diff < 0, np.exp(log_b[:, None, None] * np.maximum(-diff, 0.0)), 0.0)
    lanes = lambda t: np.repeat(t.T, dh, axis=1)
    dec = np.stack([lanes(np.exp(log_f[:, None] * (pos + 1.0))),
                    lanes(np.exp(log_f[:, None] * (c - 1.0 - pos))),
                    lanes(np.exp(log_b[:, None] * (c - pos))),
                    lanes(np.exp(log_b[:, None] * pos))])
    chunk_f = [float(np.exp(v * c)) for v in log_f]
    chunk_b = [float(np.exp(v * c)) for v in log_b]
    return cos_t, sin_lo, sin_hi, jnp.asarray(dmat, F32), jnp.asarray(dec, F32), chunk_f, chunk_b


def _rotary(x, cos, sin_lo, sin_hi):
    w = x.shape[-1]
    return x * cos + pltpu.roll(x, w - HEAD_DIM // 2, 1) * sin_lo + pltpu.roll(x, HEAD_DIM // 2, 1) * sin_hi


def _retention_kernel(nb, chunk_f, chunk_b, q_ref, k_ref, v_ref, gate_ref, cos_ref, slo_ref, shi_ref,
                      dmat_ref, dec_ref, ng_ref, o_ref, sf_ref, sb_ref, sball_ref):
    t = pl.program_id(1)
    dh, c, cps = HEAD_DIM, RET_CHUNK, RET_CHUNKS_PER_STEP
    cos, slo, shi = cos_ref[...], slo_ref[...], shi_ref[...]
    kr = _rotary(k_ref[...], cos, slo, shi) * (dh ** -0.5)
    vb = v_ref[...].astype(BF16)
    items = [(ci, h) for ci in range(cps) for h in range(MIXER_HEADS)]
    part = lambda x, ci, h: x[ci * c:(ci + 1) * c, h * dh:(h + 1) * dh]
    split = lambda x: {it: part(x, *it) for it in items}

    @pl.when(t == 0)
    def _():
        sf_ref[...] = jnp.zeros_like(sf_ref)
        sb_ref[...] = jnp.zeros_like(sb_ref)

    @pl.when(t < nb)
    def _():
        blk = nb - 1 - t
        kd, v = split((kr * dec_ref[3]).astype(BF16)), split(vb)
        kv = {it: _dot_tn(kd[it], v[it]) for it in items}
        for h in range(MIXER_HEADS):
            state = sb_ref[h]
            for ci in range(cps - 1, -1, -1):
                sball_ref[blk * cps + ci, h] = state
                state = state * chunk_b[h] + kv[ci, h]
            sb_ref[h] = state

    @pl.when(t >= nb)
    def _():
        blk = t - nb
        qr = _rotary(q_ref[...], cos, slo, shi)
        q, k, v = split(qr.astype(BF16)), split(kr.astype(BF16)), split(vb)
        qf, qbk = split((qr * dec_ref[0]).astype(BF16)), split((qr * dec_ref[2]).astype(BF16))
        kd = split((kr * dec_ref[1]).astype(BF16))
        sc = {it: (_dot_nt(q[it], k[it]) * dmat_ref[it[1]]).astype(BF16) for it in items}
        o = {it: _dot(sc[it], v[it]) for it in items}
        kv = {it: _dot_tn(kd[it], v[it]) for it in items}
        ob = {it: _dot(qbk[it], sball_ref[blk * cps + it[0], it[1]].astype(BF16)) for it in items}
        of = {}
        for h in range(MIXER_HEADS):
            state = sf_ref[h]
            for ci in range(cps):
                of[ci, h] = _dot(qf[ci, h], state.astype(BF16))
                state = state * chunk_f[h] + kv[ci, h]
            sf_ref[h] = state
        o = {it: o[it] + of[it] + ob[it] for it in items}
        mu = {it: jnp.mean(o[it], axis=-1, keepdims=True) for it in items}
        oc = {it: o[it] - mu[it] for it in items}
        var = {it: jnp.mean(oc[it] * oc[it], axis=-1, keepdims=True) for it in items}
        on = {it: oc[it] * lax.rsqrt(var[it] + EPS) for it in items}
        y = jnp.concatenate([jnp.concatenate([on[ci, h] for h in range(MIXER_HEADS)], axis=-1)
                             for ci in range(cps)], axis=0)
        o_ref[...] = (y * ng_ref[...] * _silu(gate_ref[...])).astype(BF16)


def _retention(p_b, norm_g, tables, b, s):
    cos_t, sin_lo, sin_hi, dmat, dec, chunk_f, chunk_b = tables
    cps = RET_CHUNKS_PER_STEP
    tb = RET_CHUNK * cps
    nb = s // tb
    w = MIXER_WIDTH
    dec = jnp.tile(dec, (1, cps, 1))
    block = lambda t: jnp.where(t < nb, nb - 1 - t, t - nb)
    col = lambda j: pl.BlockSpec((tb, w), lambda bi, t: (bi * nb + block(t), j))
    tab = pl.BlockSpec((tb, w), lambda bi, t: (block(t), 0))
    const = lambda a: pl.BlockSpec(a.shape, lambda bi, t: (0,) * a.ndim)
    return pl.pallas_call(
        functools.partial(_retention_kernel, nb, chunk_f, chunk_b),
        grid=(b, 2 * nb),
        in_specs=[col(0), col(1), col(2), col(3), tab, tab, tab, const(dmat), const(dec), const(norm_g)],
        out_specs=pl.BlockSpec((tb, w), lambda bi, t: (bi * nb + jnp.maximum(t - nb, 0), 0)),
        out_shape=jax.ShapeDtypeStruct((b * s, w), BF16),
        scratch_shapes=[pltpu.VMEM((MIXER_HEADS, HEAD_DIM, HEAD_DIM), F32),
                        pltpu.VMEM((MIXER_HEADS, HEAD_DIM, HEAD_DIM), F32),
                        pltpu.VMEM((s // RET_CHUNK, MIXER_HEADS, HEAD_DIM, HEAD_DIM), F32)],
        compiler_params=_cparams("arbitrary", "arbitrary"),
    )(p_b, p_b, p_b, p_b, cos_t, sin_lo, sin_hi, dmat, dec, norm_g)


def _halo_specs(tile, width, n_rows, block_of):
    per = tile // HALO
    last = n_rows // HALO - 1
    main = pl.BlockSpec((tile, width), lambda *g: (block_of(*g), 0))
    prev = pl.BlockSpec((HALO, width), lambda *g: (jnp.maximum(block_of(*g) * per - 1, 0), 0))
    nxt = pl.BlockSpec((HALO, width), lambda *g: (jnp.minimum((block_of(*g) + 1) * per, last), 0))
    return main, prev, nxt


def _fill_padded(pad_ref, main, prev, nxt, first, last):
    t = main.shape[0]
    pad_ref[0:HALO, :] = jnp.where(first, 0.0, prev)
    pad_ref[HALO:HALO + t, :] = main
    pad_ref[HALO + t:2 * HALO + t, :] = jnp.where(last, 0.0, nxt)


def _pool_kernel(s, tiles_per_seq, u_ref, prev_ref, next_ref, w_ref, scale_ref, o_ref, pad_ref):
    t = POOL_TILE
    tseq = pl.program_id(0) % tiles_per_seq
    u = u_ref[...]
    _fill_padded(pad_ref, u, prev_ref[...], next_ref[...], tseq == 0, tseq == tiles_per_seq - 1)
    lane = lax.broadcasted_iota(jnp.int32, (1, MIXER_WIDTH), 1)
    half = jnp.full((1, MIXER_WIDTH), POOL_WINDOWS[0] // 2, jnp.int32)
    for gi in range(1, len(POOL_WINDOWS)):
        half = jnp.where(lane >= gi * POOL_GROUP, POOL_WINDOWS[gi] // 2, half)
    max_half = POOL_WINDOWS[-1] // 2
    acc = jnp.zeros((t, MIXER_WIDTH), F32)
    for d in range(-max_half, max_half):
        inside = (d >= -half) & (d < half)
        acc = acc + jnp.where(inside, pad_ref[HALO + d:HALO + d + t, :], 0.0)
    pos = tseq * t + lax.broadcasted_iota(jnp.int32, (t, 1), 0)
    count = jnp.minimum(pos + half, s) - jnp.maximum(pos - half, 0)
    diff = acc / count.astype(F32) - u
    o_ref[...] = (_dot(diff.astype(BF16), w_ref[...]) * scale_ref[...]).astype(BF16)


def _pool(p_c, w_blockdiag, scale, b, s):
    t = POOL_TILE
    tiles_per_seq = s // t
    main, prev, nxt = _halo_specs(t, MIXER_WIDTH, b * s, lambda i: i)
    const = lambda a: pl.BlockSpec(a.shape, lambda i: (0,) * a.ndim)
    return pl.pallas_call(
        functools.partial(_pool_kernel, s, tiles_per_seq),
        grid=(b * tiles_per_seq,),
        in_specs=[main, prev, nxt, const(w_blockdiag), const(scale)],
        out_specs=pl.BlockSpec((t, MIXER_WIDTH), lambda i: (i, 0)),
        out_shape=jax.ShapeDtypeStruct((b * s, MIXER_WIDTH), BF16),
        scratch_shapes=[pltpu.VMEM((t + 2 * HALO, MIXER_WIDTH), F32)],
        compiler_params=_cparams("parallel"),
    )(p_c, p_c, p_c, w_blockdiag, scale)


def _head_blockdiag(x):
    lane = lax.broadcasted_iota(jnp.int32, x.shape, 1)
    zero = jnp.zeros_like(x)
    return jnp.concatenate([jnp.where(lane < HEAD_DIM, x, zero), jnp.where(lane >= HEAD_DIM, x, zero)], axis=0)


def _gdn_pre_kernel(tiles_per_seq, x_ref, prev_ref, next_ref, ab_ref, cw_ref, alog_ref, dtb_ref, ones_ref,
                    u_ref, gb_ref, pad_ref):
    t, w = GDN_PRE_TILE, MIXER_WIDTH
    tseq = pl.program_id(0) % tiles_per_seq
    _fill_padded(pad_ref, x_ref[...], prev_ref[...], next_ref[...], tseq == 0, tseq == tiles_per_seq - 1)
    acc = jnp.zeros((t, 3 * w), F32)
    for k in range(CONV_K):
        off = HALO + k - CONV_K // 2
        acc = acc + pad_ref[off:off + t, :] * cw_ref[k:k + 1, :]
    u = _silu(acc)
    head_sum = lambda a: sum(_dot(piece, ones_ref[...]) for piece in _split_bf16(a, 2))
    q, k = u[:, 0:w], u[:, w:2 * w]
    u_ref[:, 0:w] = q * lax.rsqrt(head_sum(q * q) + EPS) * (HEAD_DIM ** -0.5)
    u_ref[:, w:2 * w] = k * lax.rsqrt(head_sum(k * k) + EPS)
    u_ref[:, 2 * w:3 * w] = u[:, 2 * w:3 * w]
    ab = ab_ref[...]
    x = ab + dtb_ref[...]
    g = -jnp.exp(alog_ref[...]) * (jnp.maximum(x, 0.0) + jnp.log(1.0 + jnp.exp(-jnp.abs(x))))
    beta = 1.0 / (1.0 + jnp.exp(-ab))
    lane = lax.broadcasted_iota(jnp.int32, ab.shape, 1)
    gb_ref[...] = jnp.where(lane < 2 * MIXER_HEADS, g, beta)


def _gdn_pre(p_d, p_ab, conv_w, alog_vec, dtb_vec, b, s):
    t = GDN_PRE_TILE
    w = MIXER_WIDTH
    tiles_per_seq = s // t
    ones = _block_diag(jnp.ones((MIXER_HEADS, HEAD_DIM, HEAD_DIM), BF16))
    main, prev, nxt = _halo_specs(t, 3 * w, b * s, lambda i: i)
    const = lambda a: pl.BlockSpec(a.shape, lambda i: (0,) * a.ndim)
    row = lambda c: pl.BlockSpec((t, c), lambda i: (i, 0))
    return pl.pallas_call(
        functools.partial(_gdn_pre_kernel, tiles_per_seq),
        grid=(b * tiles_per_seq,),
        in_specs=[main, prev, nxt, row(LANES), const(conv_w), const(alog_vec), const(dtb_vec), const(ones)],
        out_specs=[row(3 * w), row(LANES)],
        out_shape=[jax.ShapeDtypeStruct((b * s, 3 * w), F32), jax.ShapeDtypeStruct((b * s, LANES), F32)],
        scratch_shapes=[pltpu.VMEM((t + 2 * HALO, 3 * w), F32)],
        compiler_params=_cparams("parallel"),
    )(p_d, p_d, p_d, p_ab, conv_w, alog_vec, dtb_vec, ones)


def _gdn_prepare(forward, u, gb, tri_col, tri_row):
    c, dh, w = GDN_CHUNK, HEAD_DIM, MIXER_WIDTH
    t = u.shape[0]
    a_off, b_off = (0, 2 * MIXER_HEADS) if forward else (MIXER_HEADS, 3 * MIXER_HEADS)
    g_all = beta_all = gb
    g_parts = _split_bf16(g_all, 3)
    gc_col = sum(_dot(tri_col, p) for p in g_parts)
    gc_row = sum(_dot_tn(p, tri_row) for p in g_parts)
    ii = lax.broadcasted_iota(jnp.int32, (c, 2 * dh), 0)
    lane = lax.broadcasted_iota(jnp.int32, (c, 2 * dh), 1)
    first = lane < dh
    jj = jnp.where(first, lane, lane - dh)
    incl = (ii >= jj) if forward else (ii <= jj)
    strict = (ii > jj) if forward else (ii < jj)
    last = c - 1 if forward else 0
    items = [(ci, p) for ci in range(t // c) for p in range(MIXER_HEADS // 2)]
    rows = lambda ci: slice(ci * c, (ci + 1) * c)
    part = lambda base: [u[rows(ci), base + 2 * p * dh:base + 2 * (p + 1) * dh] for ci, p in items]
    col_pair = lambda x, off: [jnp.where(first, x[rows(ci), off + 2 * p:off + 2 * p + 1],
                                         x[rows(ci), off + 2 * p + 1:off + 2 * p + 2]) for ci, p in items]
    q, k, v = part(0), part(w), part(2 * w)
    gcc = col_pair(gc_col, a_off)
    beta = col_pair(beta_all, b_off)
    gcr = [jnp.concatenate([gc_row[a_off + 2 * p:a_off + 2 * p + 1, rows(ci)],
                            gc_row[a_off + 2 * p + 1:a_off + 2 * p + 2, rows(ci)]], axis=1) for ci, p in items]
    g_last = [a[last:last + 1, :] for a in gcc]
    decay = [jnp.where(incl, jnp.exp(jnp.where(incl, a - b, 0.0)), 0.0) for a, b in zip(gcc, gcr)]
    kq = [_dot_nt(jnp.concatenate([kk, qq], axis=0).astype(BF16), _head_blockdiag(kk.astype(BF16)))
          for kk, qq in zip(k, q)]
    lmat = [jnp.where(strict, m[:c] * b * d, 0.0) for m, b, d in zip(kq, beta, decay)]
    attn = [(m[c:] * d).astype(BF16) for m, d in zip(kq, decay)]
    xinv = [-a for a in lmat]
    lp = [a.astype(BF16) for a in lmat]
    lp_bd = [_head_blockdiag(a) for a in lp]
    for _ in range(int(math.log2(c)) - 1):
        lp32 = [_dot(a, bd) for a, bd in zip(lp, lp_bd)]
        lp = [a.astype(BF16) for a in lp32]
        lp_bd = [_head_blockdiag(a) for a in lp]
        xinv = [xi + a32 + _dot(xi.astype(BF16), bd) for xi, a32, bd in zip(xinv, lp32, lp_bd)]
    egc = [jnp.exp(a) for a in gcc]
    xb = [a.astype(BF16) for a in xinv]
    vbeta = [vv * b for vv, b in zip(v, beta)]
    kbeta = [kk * (b * e) for kk, b, e in zip(k, beta, egc)]
    u_val = [r + _dot(xi, _head_blockdiag(r.astype(BF16))) for xi, r in zip(xb, vbeta)]
    w_dec = [r + _dot(xi, _head_blockdiag(r.astype(BF16))) for xi, r in zip(xb, kbeta)]
    wq = [jnp.concatenate([wd, qq * e], axis=0).astype(BF16) for wd, qq, e in zip(w_dec, q, egc)]
    k_state = [(kk * jnp.exp(gl - a)).astype(BF16) for kk, gl, a in zip(k, g_last, gcc)]
    e_last = [jnp.exp(gl) for gl in g_last]
    keyed = lambda vals: dict(zip(items, vals))
    return dict(wq=keyed(wq), u_val=keyed(u_val), attn=keyed(attn), k_state=keyed(k_state), e_last=keyed(e_last))


def _gdn_kernel(uf_ref, gbf_ref, ub_ref, gbb_ref, tril_ref, triu_ref, of_ref, ob_ref, sf_ref, sb_ref):
    n = pl.program_id(1)
    t = uf_ref.shape[0]
    c, dh = GDN_CHUNK, HEAD_DIM
    nch = t // c
    pairs = MIXER_HEADS // 2

    @pl.when(n == 0)
    def _():
        sf_ref[...] = jnp.zeros_like(sf_ref)
        sb_ref[...] = jnp.zeros_like(sb_ref)

    tril, triu = tril_ref[...], triu_ref[...]
    prep_f = _gdn_prepare(True, uf_ref[...], gbf_ref[...], tril, triu)
    prep_b = _gdn_prepare(False, ub_ref[...], gbb_ref[...], triu, tril)

    first = lax.broadcasted_iota(jnp.int32, (dh, 2 * dh), 1) < dh
    chains = [(prep_f, sf_ref, of_ref, p, True) for p in range(pairs)] + \
             [(prep_b, sb_ref, ob_ref, p, False) for p in range(pairs)]
    state = [s_ref[p] for _, s_ref, _, p, _ in chains]
    for step in range(nch):
        key = [(step if fw else nch - 1 - step, p) for _, _, _, p, fw in chains]
        ws = [_dot(pr['wq'][kk], _head_blockdiag(st.astype(BF16))) for (pr, *_), kk, st in zip(chains, key, state)]
        v_new = [pr['u_val'][kk] - a[:c] for (pr, *_), kk, a in zip(chains, key, ws)]
        vb = [a.astype(BF16) for a in v_new]
        out = [a[c:] + _dot(pr['attn'][kk], _head_blockdiag(b)) for (pr, *_), kk, a, b in zip(chains, key, ws, vb)]
        cross = [_dot_tn(pr['k_state'][kk], b) for (pr, *_), kk, b in zip(chains, key, vb)]
        state = [st * pr['e_last'][kk] + jnp.where(first, x[:dh], x[dh:])
                 for (pr, *_), kk, st, x in zip(chains, key, state, cross)]
        for (_, _, o_ref, p, _), (ci, _), o in zip(chains, key, out):
            o_ref[ci * c:(ci + 1) * c, 2 * p * dh:2 * (p + 1) * dh] = o
    for (_, s_ref, _, p, _), st in zip(chains, state):
        s_ref[p] = st


def _gated_deltanet(p_d, p_ab, conv_w, alog_vec, dtb_vec, b, s):
    u, gb = _gdn_pre(p_d, p_ab, conv_w, alog_vec, dtb_vec, b, s)
    c = GDN_CHUNK
    t = c * GDN_CHUNKS_PER_STEP
    nb = s // t
    w = MIXER_WIDTH
    r = np.arange(t)
    same = (r[:, None] // c) == (r[None, :] // c)
    tril = jnp.asarray(same & (r[:, None] >= r[None, :]), BF16)
    triu = jnp.asarray(same & (r[:, None] <= r[None, :]), BF16)
    fwd = lambda bi, n: (bi * nb + n, 0)
    bwd = lambda bi, n: (bi * nb + nb - 1 - n, 0)
    const = lambda a: pl.BlockSpec(a.shape, lambda bi, n: (0,) * a.ndim)
    return pl.pallas_call(
        _gdn_kernel,
        grid=(b, nb),
        in_specs=[pl.BlockSpec((t, 3 * w), fwd), pl.BlockSpec((t, LANES), fwd),
                  pl.BlockSpec((t, 3 * w), bwd), pl.BlockSpec((t, LANES), bwd), const(tril), const(triu)],
        out_specs=[pl.BlockSpec((t, w), fwd), pl.BlockSpec((t, w), bwd)],
        out_shape=[jax.ShapeDtypeStruct((b * s, w), F32), jax.ShapeDtypeStruct((b * s, w), F32)],
        scratch_shapes=[pltpu.VMEM((MIXER_HEADS // 2, HEAD_DIM, 2 * HEAD_DIM), F32),
                        pltpu.VMEM((MIXER_HEADS // 2, HEAD_DIM, 2 * HEAD_DIM), F32)],
        compiler_params=_cparams("arbitrary", "arbitrary"),
    )(u, gb, u, gb, tril, triu)


def _mem_kv_kernel(m_ref, g_ref, wk_ref, wv_ref, k_ref, v_ref):
    h = _rms(m_ref[0], g_ref[...]).astype(BF16)
    k_ref[0] = _dot(h, wk_ref[...]).astype(BF16)
    v_ref[0] = _dot(h, wv_ref[...]).astype(BF16)


def _mem_kv(mem, g, wk, wv):
    b, nm, d = mem.shape
    const = lambda a: pl.BlockSpec(a.shape, lambda i: (0,) * a.ndim)
    blk = pl.BlockSpec((1, nm, d), lambda i: (i, 0, 0))
    return pl.pallas_call(
        _mem_kv_kernel, grid=(b,),
        in_specs=[blk, const(g), const(wk), const(wv)], out_specs=[blk, blk],
        out_shape=[jax.ShapeDtypeStruct((b, nm, d), BF16)] * 2,
        compiler_params=_cparams("parallel"),
    )(mem, g, wk, wv)


def _route_lanes(lg):
    lane_i = lax.broadcasted_iota(jnp.int32, lg.shape, 1)
    lane = lane_i.astype(F32)
    first_at = lambda vals, m: jnp.min(jnp.where(vals == m, lane, float(LANES)), axis=-1, keepdims=True)
    g_mask = lane_i < N_GROUPS
    gl = jnp.where(g_mask, lg, NEG_BIG)
    g_max = jnp.max(gl, axis=-1, keepdims=True)
    g_sum = jnp.sum(jnp.where(g_mask, jnp.exp(gl - g_max), 0.0), axis=-1, keepdims=True)
    g_p = 1.0 / g_sum
    g_idx = first_at(gl, g_max)
    lo = N_GROUPS + EXPERTS_PER_GROUP * g_idx
    e_mask = (lane >= lo) & (lane < lo + EXPERTS_PER_GROUP)
    el = jnp.where(e_mask, lg, NEG_BIG)
    m1 = jnp.max(el, axis=-1, keepdims=True)
    i1 = first_at(el, m1)
    el2 = jnp.where(lane == i1, NEG_BIG, el)
    m2 = jnp.max(el2, axis=-1, keepdims=True)
    i2 = first_at(el2, m2)
    r = jnp.exp(m2 - m1)
    w1 = g_p / (1.0 + r)
    w2 = g_p * r / (1.0 + r)
    e1, e2 = i1 - N_GROUPS, i2 - N_GROUPS
    e_lo, e_hi = jnp.minimum(e1, e2), jnp.maximum(e1, e2)
    w_lo, w_hi = jnp.where(e1 < e2, w1, w2), jnp.where(e1 < e2, w2, w1)
    a, b = e_lo - EXPERTS_PER_GROUP * g_idx, e_hi - EXPERTS_PER_GROUP * g_idx
    cls = g_idx * PAIRS_PER_GROUP + a * EXPERTS_PER_GROUP - a * (a + 1.0) * 0.5 + (b - a - 1.0)
    out = jnp.where(lane_i == 0, e_lo, 0.0)
    out = jnp.where(lane_i == 1, e_hi, out)
    out = jnp.where(lane_i == 2, w_lo, out)
    out = jnp.where(lane_i == 3, w_hi, out)
    out = jnp.where(lane_i == 4, cls, out)
    return out, jnp.where(lane == cls, 1.0, 0.0)


def _post_kernel(x_ref, ya_ref, yb_ref, yc_ref, of_ref, ob_ref, gate_ref, seg_ref, gg_ref, wout_ref, gq_ref,
                 wq_ref, k_ref, v_ref, wo_ref, gf_ref, wr_ref, br_ref, xo_ref, hx_ref, rt_ref, hist_ref):
    w = MIXER_WIDTH
    o = of_ref[...] + ob_ref[...]
    ms = sum(_dot(p, seg_ref[...]) for p in _split_bf16(o * o, 2))
    yd = o * lax.rsqrt(ms + EPS) * gg_ref[...] * _silu(gate_ref[...])
    mix = _dot(ya_ref[...], wout_ref[0:w, :]) + _dot(yb_ref[...], wout_ref[w:2 * w, :])
    mix = mix + _dot(yc_ref[...], wout_ref[2 * w:3 * w, :]) + _dot(yd.astype(BF16), wout_ref[3 * w:4 * w, :])
    x1 = x_ref[...] + mix
    q = _dot(_rms(x1, gq_ref[...]).astype(BF16), wq_ref[...]).astype(BF16)
    cols = [slice(h * MEM_HEAD_DIM, (h + 1) * MEM_HEAD_DIM) for h in range(MEM_HEADS)]
    sc = [_dot_nt(q[:, sl], k_ref[0, :, sl]) * (MEM_HEAD_DIM ** -0.5) for sl in cols]
    p = [jnp.exp(a - jnp.max(a, axis=-1, keepdims=True)) for a in sc]
    l = [jnp.sum(a, axis=-1, keepdims=True) for a in p]
    heads = [(_dot(a.astype(BF16), v_ref[0, :, sl]) / b).astype(BF16) for a, b, sl in zip(p, l, cols)]
    x2 = x1 + _dot(jnp.concatenate(heads, axis=-1), wo_ref[...])
    xo_ref[...] = x2
    hx = _rms(x2, gf_ref[...])
    _to_row_tiles(hx_ref, hx)
    h_hi, h_lo = _split_bf16(hx, 2)
    lg = _dot(h_hi, wr_ref[0]) + _dot(h_hi, wr_ref[1]) + _dot(h_lo, wr_ref[0]) + br_ref[...]
    route, chosen = _route_lanes(lg)
    rt_ref[...] = route
    hist_ref[...] = jnp.broadcast_to(jnp.sum(chosen, axis=0, keepdims=True), hist_ref.shape)


def _post(x, y_a, y_b, y_c, o_f, o_b, p_d, seg, gdn_g, w_out, g_q, w_q, k_mem, v_mem, w_o, g_f, w_r, b_r, s):
    n = x.shape[0]
    tm = POST_TILE
    w = MIXER_WIDTH
    per_seq = s // tm
    row = lambda c: pl.BlockSpec((tm, c), lambda i: (i, 0))
    const = lambda a: pl.BlockSpec(a.shape, lambda i: (0,) * a.ndim)
    kv = pl.BlockSpec((1,) + k_mem.shape[1:], lambda i: (i // per_seq, 0, 0))
    return pl.pallas_call(
        _post_kernel, grid=(n // tm,),
        in_specs=[row(D_MODEL), row(w), row(w), row(w), row(w), row(w),
                  pl.BlockSpec((tm, w), lambda i: (i, 3)), const(seg), const(gdn_g), const(w_out), const(g_q),
                  const(w_q), kv, kv, const(w_o), const(g_f), const(w_r), const(b_r)],
        out_specs=[row(D_MODEL), pl.BlockSpec((tm * ROW_TILES, LANES), lambda i: (i, 0)), row(LANES),
                   pl.BlockSpec((SUBLANES, LANES), lambda i: (i, 0))],
        out_shape=[jax.ShapeDtypeStruct((n, D_MODEL), F32), jax.ShapeDtypeStruct((n * ROW_TILES, LANES), F32),
                   jax.ShapeDtypeStruct((n, LANES), F32), jax.ShapeDtypeStruct((n // tm * SUBLANES, LANES), F32)],
        compiler_params=_cparams("parallel"),
    )(x, y_a, y_b, y_c, o_f, o_b, p_d, seg, gdn_g, w_out, g_q, w_q, k_mem, v_mem, w_o, g_f, w_r, b_r)


def _pair_class_experts():
    lo, hi = [], []
    for g in range(N_GROUPS):
        for a in range(EXPERTS_PER_GROUP):
            for b in range(a + 1, EXPERTS_PER_GROUP):
                lo.append(g * EXPERTS_PER_GROUP + a)
                hi.append(g * EXPERTS_PER_GROUP + b)
    return np.asarray(lo, np.int32), np.asarray(hi, np.int32)


def _dispatch(route, hist):
    n = route.shape[0]
    blk = MOE_BLOCK
    cls = route[:, 4].astype(jnp.int32)
    _, order, w_lo, w_hi = lax.sort((cls, jnp.arange(n, dtype=jnp.int32), route[:, TOP_K], route[:, TOP_K + 1]),
                                    num_keys=1)
    counts = jnp.sum(hist.reshape(-1, SUBLANES, LANES)[:, 0, :N_CLASSES], axis=0).astype(jnp.int32)
    padded = (counts + blk - 1) // blk * blk
    pad_end = jnp.cumsum(padded)
    pad_start = pad_end - padded
    start = jnp.cumsum(counts) - counts
    n_blocks = n // blk + N_CLASSES
    blk_start = jnp.arange(n_blocks, dtype=jnp.int32) * blk
    blk_cls = jnp.minimum(jnp.sum(pad_end[None, :] <= blk_start[:, None], axis=1), N_CLASSES - 1).astype(jnp.int32)
    cls_lo, cls_hi = _pair_class_experts()
    into_cls = blk_start - pad_start[blk_cls]
    blk_valid = jnp.clip(counts[blk_cls] - into_cls, 0, blk)
    in_blk = jnp.arange(blk, dtype=jnp.int32)[None, :]
    valid = in_blk < blk_valid[:, None]
    first = start[blk_cls] + into_cls
    rows_of = lambda v: jax.vmap(lambda s: lax.dynamic_slice(jnp.pad(v, (0, blk)), (s,), (blk,)))(first)
    row_tok = jnp.where(valid, rows_of(order), n + in_blk)
    row_w = jnp.stack([jnp.where(valid, rows_of(w_lo), 0.0), jnp.where(valid, rows_of(w_hi), 0.0)], axis=-1)
    row_w = row_w.reshape(-1, TOP_K)
    n_used = (pad_end[-1] // blk).astype(jnp.int32).reshape(1)
    return (n_used, jnp.asarray(cls_lo)[blk_cls], jnp.asarray(cls_hi)[blk_cls],
            jnp.concatenate([n + in_blk, row_tok], axis=0).reshape(-1), row_w)


def _moe_kernel(n_tokens, n_blocks, nu_ref, ea_ref, eb_ref, dst_ref, hx_hbm, rw_ref, wga_ref, wua_ref, wda_ref,
                wgb_ref, wub_ref, wdb_ref, y_hbm, xbuf, ybuf, gsem, ssem):
    j = pl.program_id(0)
    blk, rt = MOE_BLOCK, ROW_TILES
    n_used = nu_ref[0]
    slot = lax.rem(j, MOE_BUFFERS)
    ahead = lax.rem(j + 2, MOE_BUFFERS)
    tile = lambda ref, i: ref.at[pl.ds(pl.multiple_of(i * rt, rt), rt), :]
    dst = lambda jj, i: dst_ref[(jj + 1) * blk + i]
    clamp = lambda jj: jnp.minimum(jj, n_blocks - 1)

    def gather_row(jj, sl, i):
        tok = jnp.minimum(dst(jj, i), n_tokens - 1)
        pltpu.make_async_copy(tile(hx_hbm, tok), tile(xbuf.at[sl], i), gsem.at[sl]).start()

    def scatter_row(jj, sl, i):
        pltpu.make_async_copy(tile(ybuf.at[sl], i), tile(y_hbm, dst(jj, i)), ssem.at[sl]).start()

    def wait_gather(sl):
        pltpu.make_async_copy(hx_hbm.at[pl.ds(0, blk * rt), :], xbuf.at[sl], gsem.at[sl]).wait()

    def wait_scatter(sl):
        pltpu.make_async_copy(ybuf.at[sl], y_hbm.at[pl.ds(0, blk * rt), :], ssem.at[sl]).wait()

    def rolled(fn):
        lax.fori_loop(0, blk, lambda i, carry: (fn(i), carry)[1], 0, unroll=DMA_UNROLL)

    @pl.when(j == 0)
    def _():
        ybuf[MOE_BUFFERS - 1] = jnp.zeros((blk * rt, LANES), F32)
        rolled(lambda i: gather_row(0, 0, i))
        rolled(lambda i: gather_row(clamp(1), 1, i))

    @pl.when(j < n_used)
    def _():
        wait_gather(slot)

        @pl.when(j >= 2)
        def _():
            wait_scatter(slot)

        def issue_copies(piece):
            for i in range(piece * blk // MOE_PIECES, (piece + 1) * blk // MOE_PIECES):
                gather_row(clamp(j + 2), ahead, i)
                scatter_row(j - 1, ahead, i)

        xs = _from_row_tiles(xbuf.at[slot], 0, blk, rt).astype(BF16)
        half = D_EXPERT // 2
        hidden = []
        for e, (wg_ref, wu_ref) in enumerate(((wga_ref, wua_ref), (wgb_ref, wub_ref))):
            acts = []
            for c in range(2):
                cols = slice(c * half, (c + 1) * half)
                g = _dot(xs, wg_ref[0, 0, :, cols])
                issue_copies(4 * e + 2 * c)
                u = _dot(xs, wu_ref[0, 0, :, cols])
                issue_copies(4 * e + 2 * c + 1)
                acts.append((_silu(g) * u).astype(BF16))
            hidden.append(jnp.concatenate(acts, axis=1))
        w_a, w_b = rw_ref[:, 0:1], rw_ref[:, 1:2]
        quarter = D_MODEL // 4
        for c in range(4):
            cols = slice(c * quarter, (c + 1) * quarter)
            y = w_a * _dot(hidden[0], wda_ref[0, 0, :, cols]) + w_b * _dot(hidden[1], wdb_ref[0, 0, :, cols])
            for k in range(quarter // LANES):
                ybuf[slot, pl.ds(c * (quarter // LANES) + k, blk, stride=rt), :] = y[:, k * LANES:(k + 1) * LANES]
            issue_copies(8 + c)

    @pl.when(j == n_used)
    def _():
        after = lax.rem(j + 1, MOE_BUFFERS)
        wait_gather(slot)
        wait_gather(after)

        @pl.when(j >= 2)
        def _():
            wait_scatter(slot)

        wait_scatter(after)
        rolled(lambda i: scatter_row(j - 1, ahead, i))
        wait_scatter(ahead)


def _moe(hx_tiles, n_used, blk_ea, blk_eb, row_tok, row_w, w_gate, w_up, w_down, layer, n):
    blk = MOE_BLOCK
    n_blocks = blk_ea.shape[0]
    block = lambda j: jnp.minimum(j, n_blocks - 1)
    first = lambda j, nu, ea, eb, dst: (layer, ea[block(j)], 0, 0)
    second = lambda j, nu, ea, eb, dst: (layer, eb[block(j)], 0, 0)
    up, down = (1, 1, D_MODEL, D_EXPERT), (1, 1, D_EXPERT, D_MODEL)
    grid_spec = pltpu.PrefetchScalarGridSpec(
        num_scalar_prefetch=4, grid=(n_blocks + 1,),
        in_specs=[pl.BlockSpec(memory_space=pl.ANY),
                  pl.BlockSpec((blk, TOP_K), lambda j, nu, ea, eb, dst: (block(j), 0)),
                  pl.BlockSpec(up, first), pl.BlockSpec(up, first), pl.BlockSpec(down, first),
                  pl.BlockSpec(up, second), pl.BlockSpec(up, second), pl.BlockSpec(down, second)],
        out_specs=pl.BlockSpec(memory_space=pl.ANY),
        scratch_shapes=[pltpu.VMEM((MOE_BUFFERS, blk * ROW_TILES, LANES), F32),
                        pltpu.VMEM((MOE_BUFFERS, blk * ROW_TILES, LANES), F32),
                        pltpu.SemaphoreType.DMA((MOE_BUFFERS,)), pltpu.SemaphoreType.DMA((MOE_BUFFERS,))])
    return pl.pallas_call(
        functools.partial(_moe_kernel, n, n_blocks), grid_spec=grid_spec,
        out_shape=jax.ShapeDtypeStruct(((n + blk) * ROW_TILES, LANES), F32),
        compiler_params=_cparams("arbitrary"),
    )(n_used, blk_ea, blk_eb, row_tok, hx_tiles, row_w, w_gate, w_up, w_down, w_gate, w_up, w_down)


def _final_kernel(x_ref, y_ref, g_ref, o_ref):
    o_ref[...] = _rms(_add_moe(x_ref[...], y_ref), g_ref[...])


def _final(x, moe, g):
    n = x.shape[0]
    tm = PROJ_TILE
    return pl.pallas_call(
        _final_kernel, grid=(n // tm,),
        in_specs=[pl.BlockSpec((tm, D_MODEL), lambda i: (i, 0)), _moe_out_spec(tm),
                  pl.BlockSpec(g.shape, lambda i: (0, 0))],
        out_specs=pl.BlockSpec((tm, D_MODEL), lambda i: (i, 0)),
        out_shape=jax.ShapeDtypeStruct((n, D_MODEL), F32),
        compiler_params=_cparams("parallel"),
    )(x, moe, g)


def _lane_vec(v, width=LANES):
    v = v.reshape(1, -1).astype(F32)
    return jnp.pad(v, ((0, 0), (0, width - v.shape[1])))


def _block_diag(blocks):
    g, c, _ = blocks.shape
    eye = jnp.eye(g, dtype=blocks.dtype)
    return (eye[:, None, :, None] * blocks[:, :, None, :]).reshape(g * c, g * c)


def kernel(x, mem, mix_norm_g, w_in, na_rpb, ret_norm_g, pool_w, pool_scale, gdn_conv_w, gdn_a_log, gdn_dt_bias, gdn_norm_g, w_out, mem_q_norm_g, mem_kv_norm_g, mem_w_q, mem_w_k, mem_w_v, mem_w_o, ffn_norm_g, w_group, b_group, w_router, b_router, w_gate, w_up, w_down, final_norm_g):
    b, s, d = x.shape
    n = b * s
    depth = w_in.shape[0]
    row = lambda v: v.reshape(1, -1).astype(F32)
    ret_tables = _retention_tables(s)
    seg = _block_diag(jnp.full((MIXER_HEADS, HEAD_DIM, HEAD_DIM), 1.0 / HEAD_DIM, BF16))
    wg_bf, wu_bf, wd_bf = w_gate.astype(BF16), w_up.astype(BF16), w_down.astype(BF16)
    xs = x.reshape(n, d)
    moe = None
    for l in range(depth):
        w_pad = jnp.pad(w_in[l], ((0, 0), (0, P_IN_PAD - w_in.shape[2]))).astype(BF16)
        xs, p_a, p_b, p_c, p_d, p_ab = _norm_proj(xs, moe, row(mix_norm_g[l]), w_pad)
        y_a = _neighbourhood_attention(p_a, _na_bias_table(na_rpb[l]), b, s)
        y_b = _retention(p_b, row(ret_norm_g[l]), ret_tables, b, s)
        y_c = _pool(p_c, _block_diag(pool_w[l]).astype(BF16), row(pool_scale[l]), b, s)
        o_f, o_b = _gated_deltanet(p_d, p_ab, gdn_conv_w[l].astype(F32), _lane_vec(gdn_a_log[l]),
                                   _lane_vec(gdn_dt_bias[l]), b, s)
        k_mem, v_mem = _mem_kv(mem, row(mem_kv_norm_g[l]), mem_w_k[l].astype(BF16), mem_w_v[l].astype(BF16))
        w_r = jnp.pad(jnp.concatenate([w_group[l], w_router[l]], axis=1),
                      ((0, 0), (0, LANES - N_GROUPS - N_EXPERTS))).astype(F32)
        b_r = _lane_vec(jnp.concatenate([b_group[l], b_router[l]]))
        xs, hx_tiles, route, hist = _post(
            xs, y_a, y_b, y_c, o_f, o_b, p_d, seg, row(jnp.tile(gdn_norm_g[l], MIXER_HEADS)), w_out[l].astype(BF16),
            row(mem_q_norm_g[l]), mem_w_q[l].astype(BF16), k_mem, v_mem, mem_w_o[l].astype(BF16),
            row(ffn_norm_g[l]), jnp.stack(_split_bf16(w_r, 2)), b_r, s)
        moe = _moe(hx_tiles, *_dispatch(route, hist), wg_bf, wu_bf, wd_bf, l, n)
    return _final(xs, moe, row(final_norm_g)).reshape(b, s, d)
```

```python
import functools
import math

import jax
import jax.numpy as jnp
import numpy as np
from jax import lax
from jax.experimental import pallas as pl
from jax.experimental.pallas import tpu as pltpu

F32 = jnp.float32
BF16 = jnp.bfloat16
HIGHEST = lax.Precision.HIGHEST

D_MODEL = 1024
GRID_W = 64
HEAD_DIM = 64
MIXER_WIDTH = 256
MIXER_HEADS = 4
NA_KH = 8
NA_KW = 16
RET_CHUNK = 128
ROPE_BASE = 10000.0
POOL_WINDOWS = (2, 4, 8, 16)
POOL_GROUP = 64
GDN_CHUNK = 64
CONV_K = 4
MEM_HEADS = 4
MEM_HEAD_DIM = 256
N_GROUPS = 4
EXPERTS_PER_GROUP = 8
N_EXPERTS = 32
TOP_K = 2
D_EXPERT = 512
EPS = 1e-6

P_IN_PAD = 3200
V7X_VMEM_LIMIT = 56 * 1024 * 1024
LANES = 128
SUBLANES = 8
ROW_TILES = D_MODEL // LANES
NEG_BIG = -1e30

PROJ_TILE = 512
POOL_TILE = 512
POST_TILE = 512
NA_ROWS_PER_STEP = 8
NA_ROWS_LOCKSTEP = 4
GDN_CHUNKS_PER_STEP = 8
GDN_PRE_TILE = 512
RET_CHUNKS_PER_STEP = 8
PAIRS_PER_GROUP = EXPERTS_PER_GROUP * (EXPERTS_PER_GROUP - 1) // 2
N_CLASSES = N_GROUPS * PAIRS_PER_GROUP
MOE_BLOCK = 128
MOE_BUFFERS = 3
MOE_PIECES = 12
DMA_UNROLL = 8
HALO = 8


def _cparams(*sem):
    return pltpu.CompilerParams(dimension_semantics=sem, vmem_limit_bytes=V7X_VMEM_LIMIT)


def _dot(a, b):
    return jnp.dot(a, b, preferred_element_type=F32)


def _dot_nt(a, b):
    return lax.dot_general(a, b, (((1,), (1,)), ((), ())), preferred_element_type=F32)


def _dot_tn(a, b, precision=None):
    return lax.dot_general(a, b, (((0,), (0,)), ((), ())), preferred_element_type=F32, precision=precision)


def _silu(x):
    return x * (1.0 / (1.0 + jnp.exp(-x)))


def _rms(x, g):
    return x * lax.rsqrt(jnp.mean(x * x, axis=-1, keepdims=True) + EPS) * g


def _from_row_tiles(ref, first, count, stride):
    return jnp.concatenate([ref[pl.ds(first + k, count, stride=stride), :] for k in range(ROW_TILES)], axis=1)


def _to_row_tiles(ref, x):
    count = x.shape[0]
    for k in range(ROW_TILES):
        ref[pl.ds(k, count, stride=ROW_TILES), :] = x[:, k * LANES:(k + 1) * LANES]


def _add_moe(x, y_ref):
    return x + _from_row_tiles(y_ref, 0, x.shape[0], ROW_TILES)


def _split_bf16(x, parts):
    out = []
    for _ in range(parts):
        p = x.astype(BF16)
        out.append(p)
        x = x - p.astype(F32)
    return out


def _norm_proj_kernel(has_y, *refs):
    if has_y:
        x_ref, y_ref, g_ref, w_ref, xo_ref, pa_ref, pb_ref, pc_ref, pd_ref, pab_ref = refs
        x = _add_moe(x_ref[...], y_ref)
        xo_ref[...] = x
    else:
        x_ref, g_ref, w_ref, pa_ref, pb_ref, pc_ref, pd_ref, pab_ref = refs
        x = x_ref[...]
    h = _rms(x, g_ref[...]).astype(BF16)
    w = MIXER_WIDTH
    pa_ref[...] = _dot(h, w_ref[:, 0:3 * w]).astype(BF16)
    pb_ref[...] = _dot(h, w_ref[:, 3 * w:7 * w])
    pc_ref[...] = _dot(h, w_ref[:, 7 * w:8 * w])
    pd_ref[...] = _dot(h, w_ref[:, 8 * w:12 * w])
    pab_ref[...] = _dot(h, w_ref[:, 12 * w:P_IN_PAD])


def _moe_out_spec(tile):
    return pl.BlockSpec((tile * ROW_TILES, LANES), lambda i: (i, 0))


def _norm_proj(x, moe, g, w_pad):
    n = x.shape[0]
    tm = PROJ_TILE
    w = MIXER_WIDTH
    row = lambda c: pl.BlockSpec((tm, c), lambda i: (i, 0))
    const = lambda a: pl.BlockSpec(a.shape, lambda i: (0,) * a.ndim)
    outs = [jax.ShapeDtypeStruct((n, 3 * w), BF16), jax.ShapeDtypeStruct((n, 4 * w), F32),
            jax.ShapeDtypeStruct((n, w), F32), jax.ShapeDtypeStruct((n, 4 * w), F32),
            jax.ShapeDtypeStruct((n, LANES), F32)]
    out_specs = [row(3 * w), row(4 * w), row(w), row(4 * w), row(LANES)]
    if moe is None:
        ins, in_specs = (x, g, w_pad), [row(D_MODEL), const(g), const(w_pad)]
    else:
        ins = (x, moe, g, w_pad)
        in_specs = [row(D_MODEL), _moe_out_spec(tm), const(g), const(w_pad)]
        outs = [jax.ShapeDtypeStruct((n, D_MODEL), F32)] + outs
        out_specs = [row(D_MODEL)] + out_specs
    res = pl.pallas_call(
        functools.partial(_norm_proj_kernel, moe is not None),
        grid=(n // tm,), in_specs=in_specs, out_specs=out_specs, out_shape=outs,
        compiler_params=_cparams("parallel"),
    )(*ins)
    return res if moe is not None else [x] + list(res)


def _na_bias_table(rpb):
    qc = np.arange(GRID_W)
    kc = np.arange(GRID_W)
    d_col = np.clip(kc[None, :] - qc[:, None], -(NA_KW - 1), NA_KW - 1) + (NA_KW - 1)
    pick = np.zeros((2 * NA_KW - 1, GRID_W * GRID_W), np.float32)
    pick[d_col.reshape(-1), np.arange(GRID_W * GRID_W)] = 1.0
    win = np.clip(qc - NA_KW // 2, 0, GRID_W - NA_KW)
    ok = (kc[None, :] >= win[:, None]) & (kc[None, :] < win[:, None] + NA_KW)
    cols = jnp.einsum('hrc,cx->hrx', rpb.astype(F32), jnp.asarray(pick), precision=HIGHEST)
    cols = jnp.where(ok[None, None], cols.reshape(MIXER_HEADS, 2 * NA_KH - 1, GRID_W, GRID_W), NEG_BIG)
    per_e = [jnp.transpose(cols[:, NA_KH - 1 - e:2 * NA_KH - 1 - e], (0, 2, 1, 3)) for e in range(NA_KH)]
    return jnp.stack(per_e, axis=1).reshape(MIXER_HEADS, NA_KH, GRID_W, NA_KH * GRID_W)


def _na_kernel(rows, q_ref, k_ref, v_ref, t_ref, o_ref):
    i = pl.program_id(1)
    dh = HEAD_DIM

    def rows_body(it, carry):
        chains = []
        for k in range(NA_ROWS_LOCKSTEP):
            rr = it * NA_ROWS_LOCKSTEP + k
            r = i * NA_ROWS_PER_STEP + rr
            kr0 = jnp.clip(r - NA_KH // 2, 0, rows - NA_KH)
            q = q_ref[pl.ds(pl.multiple_of(rr * GRID_W, GRID_W), GRID_W), :]
            k0 = pl.multiple_of(kr0 * GRID_W, GRID_W)
            kb = k_ref[pl.ds(k0, NA_KH * GRID_W), :]
            vb = v_ref[pl.ds(k0, NA_KH * GRID_W), :]
            for h in range(MIXER_HEADS):
                sl = slice(h * dh, (h + 1) * dh)
                chains.append((q[:, sl], kb[:, sl], vb[:, sl], h, r - kr0))
        s = [_dot_nt(q, kb) * (dh ** -0.5) + t_ref[h, e] for q, kb, _, h, e in chains]
        m = [jnp.max(a, axis=-1, keepdims=True) for a in s]
        p = [jnp.exp(a - b) for a, b in zip(s, m)]
        l = [jnp.sum(a, axis=-1, keepdims=True) for a in p]
        o = [_dot(a.astype(BF16), c[2]) / b for a, b, c in zip(p, l, chains)]
        for k in range(NA_ROWS_LOCKSTEP):
            rr = it * NA_ROWS_LOCKSTEP + k
            row = jnp.concatenate(o[k * MIXER_HEADS:(k + 1) * MIXER_HEADS], axis=-1)
            o_ref[pl.ds(pl.multiple_of(rr * GRID_W, GRID_W), GRID_W), :] = row.astype(BF16)
        return carry

    lax.fori_loop(0, NA_ROWS_PER_STEP // NA_ROWS_LOCKSTEP, rows_body, 0)


def _neighbourhood_attention(p_a, table, b, s):
    rows = s // GRID_W
    assert rows >= NA_KH and rows % NA_ROWS_PER_STEP == 0
    steps = rows // NA_ROWS_PER_STEP
    tq = NA_ROWS_PER_STEP * GRID_W
    w = MIXER_WIDTH
    return pl.pallas_call(
        functools.partial(_na_kernel, rows),
        grid=(b, steps),
        in_specs=[pl.BlockSpec((tq, w), lambda bi, i: (bi * steps + i, 0)),
                  pl.BlockSpec((s, w), lambda bi, i: (bi, 1)),
                  pl.BlockSpec((s, w), lambda bi, i: (bi, 2)),
                  pl.BlockSpec(table.shape, lambda bi, i: (0, 0, 0, 0))],
        out_specs=pl.BlockSpec((tq, w), lambda bi, i: (bi * steps + i, 0)),
        out_shape=jax.ShapeDtypeStruct((b * s, w), BF16),
        compiler_params=_cparams("parallel", "arbitrary"),
    )(p_a, p_a, p_a, table)


def _retention_tables(s):
    h, dh, c = MIXER_HEADS, HEAD_DIM, RET_CHUNK
    half = dh // 2
    inv = ROPE_BASE ** (-jnp.arange(half, dtype=F32) / half)
    ang = jnp.arange(s, dtype=F32)[:, None] * inv[None, :]
    cos, sin = jnp.cos(ang), jnp.sin(ang)
    zero = jnp.zeros_like(sin)
    cos_t = jnp.tile(jnp.concatenate([cos, cos], axis=-1), (1, h))
    sin_lo = jnp.tile(jnp.concatenate([-sin, zero], axis=-1), (1, h))
    sin_hi = jnp.tile(jnp.concatenate([zero, sin], axis=-1), (1, h))
    log_f = np.log1p(-np.exp2(-5.0 - np.arange(h, dtype=np.float64)))
    log_b = log_f[::-1]
    pos = np.arange(c, dtype=np.float64)
    diff = pos[:, None] - pos[None, :]
    dmat = np.where(diff >= 0, np.exp(log_f[:, None, None] * np.maximum(diff, 0.0)), 0.0) \
        + np.where(diff < 0, np.exp(log_b[:, None, None] * np.maximum(-diff, 0.0)), 0.0)
    lanes = lambda t: np.repeat(t.T, dh, axis=1)
    dec = np.stack([lanes(np.exp(log_f[:, None] * (pos + 1.0))),
                    lanes(np.exp(log_f[:, None] * (c - 1.0 - pos))),
                    lanes(np.exp(log_b[:, None] * (c - pos))),
                    lanes(np.exp(log_b[:, None] * pos))])
    chunk_f = [float(np.exp(v * c)) for v in log_f]
    chunk_b = [float(np.exp(v * c)) for v in log_b]
    return cos_t, sin_lo, sin_hi, jnp.asarray(dmat, F32), jnp.asarray(dec, F32), chunk_f, chunk_b


def _rotary(x, cos, sin_lo, sin_hi):
    w = x.shape[-1]
    return x * cos + pltpu.roll(x, w - HEAD_DIM // 2, 1) * sin_lo + pltpu.roll(x, HEAD_DIM // 2, 1) * sin_hi


def _retention_kernel(nb, chunk_f, chunk_b, q_ref, k_ref, v_ref, gate_ref, cos_ref, slo_ref, shi_ref,
                      dmat_ref, dec_ref, ng_ref, o_ref, sf_ref, sb_ref, sball_ref):
    t = pl.program_id(1)
    dh, c, cps = HEAD_DIM, RET_CHUNK, RET_CHUNKS_PER_STEP
    cos, slo, shi = cos_ref[...], slo_ref[...], shi_ref[...]
    kr = _rotary(k_ref[...], cos, slo, shi) * (dh ** -0.5)
    vb = v_ref[...].astype(BF16)
    items = [(ci, h) for ci in range(cps) for h in range(MIXER_HEADS)]
    part = lambda x, ci, h: x[ci * c:(ci + 1) * c, h * dh:(h + 1) * dh]
    split = lambda x: {it: part(x, *it) for it in items}

    @pl.when(t == 0)
    def _():
        sf_ref[...] = jnp.zeros_like(sf_ref)
        sb_ref[...] = jnp.zeros_like(sb_ref)

    @pl.when(t < nb)
    def _():
        blk = nb - 1 - t
        kd, v = split((kr * dec_ref[3]).astype(BF16)), split(vb)
        kv = {it: _dot_tn(kd[it], v[it]) for it in items}
        for h in range(MIXER_HEADS):
            state = sb_ref[h]
            for ci in range(cps - 1, -1, -1):
                sball_ref[blk * cps + ci, h] = state
                state = state * chunk_b[h] + kv[ci, h]
            sb_ref[h] = state

    @pl.when(t >= nb)
    def _():
        blk = t - nb
        qr = _rotary(q_ref[...], cos, slo, shi)
        q, k, v = split(qr.astype(BF16)), split(kr.astype(BF16)), split(vb)
        qf, qbk = split((qr * dec_ref[0]).astype(BF16)), split((qr * dec_ref[2]).astype(BF16))
        kd = split((kr * dec_ref[1]).astype(BF16))
        sc = {it: (_dot_nt(q[it], k[it]) * dmat_ref[it[1]]).astype(BF16) for it in items}
        o = {it: _dot(sc[it], v[it]) for it in items}
        kv = {it: _dot_tn(kd[it], v[it]) for it in items}
        ob = {it: _dot(qbk[it], sball_ref[blk * cps + it[0], it[1]].astype(BF16)) for it in items}
        of = {}
        for h in range(MIXER_HEADS):
            state = sf_ref[h]
            for ci in range(cps):
                of[ci, h] = _dot(qf[ci, h], state.astype(BF16))
                state = state * chunk_f[h] + kv[ci, h]
            sf_ref[h] = state
        o = {it: o[it] + of[it] + ob[it] for it in items}
        mu = {it: jnp.mean(o[it], axis=-1, keepdims=True) for it in items}
        oc = {it: o[it] - mu[it] for it in items}
        var = {it: jnp.mean(oc[it] * oc[it], axis=-1, keepdims=True) for it in items}
        on = {it: oc[it] * lax.rsqrt(var[it] + EPS) for it in items}
        y = jnp.concatenate([jnp.concatenate([on[ci, h] for h in range(MIXER_HEADS)], axis=-1)
                             for ci in range(cps)], axis=0)
        o_ref[...] = (y * ng_ref[...] * _silu(gate_ref[...])).astype(BF16)


def _retention(p_b, norm_g, tables, b, s):
    cos_t, sin_lo, sin_hi, dmat, dec, chunk_f, chunk_b = tables
    cps = RET_CHUNKS_PER_STEP
    tb = RET_CHUNK * cps
    nb = s // tb
    w = MIXER_WIDTH
    dec = jnp.tile(dec, (1, cps, 1))
    block = lambda t: jnp.where(t < nb, nb - 1 - t, t - nb)
    col = lambda j: pl.BlockSpec((tb, w), lambda bi, t: (bi * nb + block(t), j))
    tab = pl.BlockSpec((tb, w), lambda bi, t: (block(t), 0))
    const = lambda a: pl.BlockSpec(a.shape, lambda bi, t: (0,) * a.ndim)
    return pl.pallas_call(
        functools.partial(_retention_kernel, nb, chunk_f, chunk_b),
        grid=(b, 2 * nb),
        in_specs=[col(0), col(1), col(2), col(3), tab, tab, tab, const(dmat), const(dec), const(norm_g)],
        out_specs=pl.BlockSpec((tb, w), lambda bi, t: (bi * nb + jnp.maximum(t - nb, 0), 0)),
        out_shape=jax.ShapeDtypeStruct((b * s, w), BF16),
        scratch_shapes=[pltpu.VMEM((MIXER_HEADS, HEAD_DIM, HEAD_DIM), F32),
                        pltpu.VMEM((MIXER_HEADS, HEAD_DIM, HEAD_DIM), F32),
                        pltpu.VMEM((s // RET_CHUNK, MIXER_HEADS, HEAD_DIM, HEAD_DIM), F32)],
        compiler_params=_cparams("arbitrary", "arbitrary"),
    )(p_b, p_b, p_b, p_b, cos_t, sin_lo, sin_hi, dmat, dec, norm_g)


def _halo_specs(tile, width, n_rows, block_of):
    per = tile // HALO
    last = n_rows // HALO - 1
    main = pl.BlockSpec((tile, width), lambda *g: (block_of(*g), 0))
    prev = pl.BlockSpec((HALO, width), lambda *g: (jnp.maximum(block_of(*g) * per - 1, 0), 0))
    nxt = pl.BlockSpec((HALO, width), lambda *g: (jnp.minimum((block_of(*g) + 1) * per, last), 0))
    return main, prev, nxt


def _fill_padded(pad_ref, main, prev, nxt, first, last):
    t = main.shape[0]
    pad_ref[0:HALO, :] = jnp.where(first, 0.0, prev)
    pad_ref[HALO:HALO + t, :] = main
    pad_ref[HALO + t:2 * HALO + t, :] = jnp.where(last, 0.0, nxt)


def _pool_kernel(s, tiles_per_seq, u_ref, prev_ref, next_ref, w_ref, scale_ref, o_ref, pad_ref):
    t = POOL_TILE
    tseq = pl.program_id(0) % tiles_per_seq
    u = u_ref[...]
    _fill_padded(pad_ref, u, prev_ref[...], next_ref[...], tseq == 0, tseq == tiles_per_seq - 1)
    lane = lax.broadcasted_iota(jnp.int32, (1, MIXER_WIDTH), 1)
    half = jnp.full((1, MIXER_WIDTH), POOL_WINDOWS[0] // 2, jnp.int32)
    for gi in range(1, len(POOL_WINDOWS)):
        half = jnp.where(lane >= gi * POOL_GROUP, POOL_WINDOWS[gi] // 2, half)
    max_half = POOL_WINDOWS[-1] // 2
    acc = jnp.zeros((t, MIXER_WIDTH), F32)
    for d in range(-max_half, max_half):
        inside = (d >= -half) & (d < half)
        acc = acc + jnp.where(inside, pad_ref[HALO + d:HALO + d + t, :], 0.0)
    pos = tseq * t + lax.broadcasted_iota(jnp.int32, (t, 1), 0)
    count = jnp.minimum(pos + half, s) - jnp.maximum(pos - half, 0)
    diff = acc / count.astype(F32) - u
    o_ref[...] = (_dot(diff.astype(BF16), w_ref[...]) * scale_ref[...]).astype(BF16)


def _pool(p_c, w_blockdiag, scale, b, s):
    t = POOL_TILE
    tiles_per_seq = s // t
    main, prev, nxt = _halo_specs(t, MIXER_WIDTH, b * s, lambda i: i)
    const = lambda a: pl.BlockSpec(a.shape, lambda i: (0,) * a.ndim)
    return pl.pallas_call(
        functools.partial(_pool_kernel, s, tiles_per_seq),
        grid=(b * tiles_per_seq,),
        in_specs=[main, prev, nxt, const(w_blockdiag), const(scale)],
        out_specs=pl.BlockSpec((t, MIXER_WIDTH), lambda i: (i, 0)),
        out_shape=jax.ShapeDtypeStruct((b * s, MIXER_WIDTH), BF16),
        scratch_shapes=[pltpu.VMEM((t + 2 * HALO, MIXER_WIDTH), F32)],
        compiler_params=_cparams("parallel"),
    )(p_c, p_c, p_c, w_blockdiag, scale)


def _head_blockdiag(x):
    lane = lax.broadcasted_iota(jnp.int32, x.shape, 1)
    zero = jnp.zeros_like(x)
    return jnp.concatenate([jnp.where(lane < HEAD_DIM, x, zero), jnp.where(lane >= HEAD_DIM, x, zero)], axis=0)


def _gdn_pre_kernel(tiles_per_seq, x_ref, prev_ref, next_ref, ab_ref, cw_ref, alog_ref, dtb_ref, ones_ref,
                    u_ref, gb_ref, pad_ref):
    t, w = GDN_PRE_TILE, MIXER_WIDTH
    tseq = pl.program_id(0) % tiles_per_seq
    _fill_padded(pad_ref, x_ref[...], prev_ref[...], next_ref[...], tseq == 0, tseq == tiles_per_seq - 1)
    acc = jnp.zeros((t, 3 * w), F32)
    for k in range(CONV_K):
        off = HALO + k - CONV_K // 2
        acc = acc + pad_ref[off:off + t, :] * cw_ref[k:k + 1, :]
    u = _silu(acc)
    head_sum = lambda a: sum(_dot(piece, ones_ref[...]) for piece in _split_bf16(a, 2))
    q, k = u[:, 0:w], u[:, w:2 * w]
    u_ref[:, 0:w] = q * lax.rsqrt(head_sum(q * q) + EPS) * (HEAD_DIM ** -0.5)
    u_ref[:, w:2 * w] = k * lax.rsqrt(head_sum(k * k) + EPS)
    u_ref[:, 2 * w:3 * w] = u[:, 2 * w:3 * w]
    ab = ab_ref[...]
    x = ab + dtb_ref[...]
    g = -jnp.exp(alog_ref[...]) * (jnp.maximum(x, 0.0) + jnp.log(1.0 + jnp.exp(-jnp.abs(x))))
    beta = 1.0 / (1.0 + jnp.exp(-ab))
    lane = lax.broadcasted_iota(jnp.int32, ab.shape, 1)
    gb_ref[...] = jnp.where(lane < 2 * MIXER_HEADS, g, beta)


def _gdn_pre(p_d, p_ab, conv_w, alog_vec, dtb_vec, b, s):
    t = GDN_PRE_TILE
    w = MIXER_WIDTH
    tiles_per_seq = s // t
    ones = _block_diag(jnp.ones((MIXER_HEADS, HEAD_DIM, HEAD_DIM), BF16))
    main, prev, nxt = _halo_specs(t, 3 * w, b * s, lambda i: i)
    const = lambda a: pl.BlockSpec(a.shape, lambda i: (0,) * a.ndim)
    row = lambda c: pl.BlockSpec((t, c), lambda i: (i, 0))
    return pl.pallas_call(
        functools.partial(_gdn_pre_kernel, tiles_per_seq),
        grid=(b * tiles_per_seq,),
        in_specs=[main, prev, nxt, row(LANES), const(conv_w), const(alog_vec), const(dtb_vec), const(ones)],
        out_specs=[row(3 * w), row(LANES)],
        out_shape=[jax.ShapeDtypeStruct((b * s, 3 * w), F32), jax.ShapeDtypeStruct((b * s, LANES), F32)],
        scratch_shapes=[pltpu.VMEM((t + 2 * HALO, 3 * w), F32)],
        compiler_params=_cparams("parallel"),
    )(p_d, p_d, p_d, p_ab, conv_w, alog_vec, dtb_vec, ones)


def _gdn_prepare(forward, u, gb, tri_col, tri_row):
    c, dh, w = GDN_CHUNK, HEAD_DIM, MIXER_WIDTH
    t = u.shape[0]
    a_off, b_off = (0, 2 * MIXER_HEADS) if forward else (MIXER_HEADS, 3 * MIXER_HEADS)
    g_all = beta_all = gb
    g_parts = _split_bf16(g_all, 3)
    gc_col = sum(_dot(tri_col, p) for p in g_parts)
    gc_row = sum(_dot_tn(p, tri_row) for p in g_parts)
    ii = lax.broadcasted_iota(jnp.int32, (c, 2 * dh), 0)
    lane = lax.broadcasted_iota(jnp.int32, (c, 2 * dh), 1)
    first = lane < dh
    jj = jnp.where(first, lane, lane - dh)
    incl = (ii >= jj) if forward else (ii <= jj)
    strict = (ii > jj) if forward else (ii < jj)
    last = c - 1 if forward else 0
    items = [(ci, p) for ci in range(t // c) for p in range(MIXER_HEADS // 2)]
    rows = lambda ci: slice(ci * c, (ci + 1) * c)
    part = lambda base: [u[rows(ci), base + 2 * p * dh:base + 2 * (p + 1) * dh] for ci, p in items]
    col_pair = lambda x, off: [jnp.where(first, x[rows(ci), off + 2 * p:off + 2 * p + 1],
                                         x[rows(ci), off + 2 * p + 1:off + 2 * p + 2]) for ci, p in items]
    q, k, v = part(0), part(w), part(2 * w)
    gcc = col_pair(gc_col, a_off)
    beta = col_pair(beta_all, b_off)
    gcr = [jnp.concatenate([gc_row[a_off + 2 * p:a_off + 2 * p + 1, rows(ci)],
                            gc_row[a_off + 2 * p + 1:a_off + 2 * p + 2, rows(ci)]], axis=1) for ci, p in items]
    g_last = [a[last:last + 1, :] for a in gcc]
    decay = [jnp.where(incl, jnp.exp(jnp.where(incl, a - b, 0.0)), 0.0) for a, b in zip(gcc, gcr)]
    kq = [_dot_nt(jnp.concatenate([kk, qq], axis=0).astype(BF16), _head_blockdiag(kk.astype(BF16)))
          for kk, qq in zip(k, q)]
    lmat = [jnp.where(strict, m[:c] * b * d, 0.0) for m, b, d in zip(kq, beta, decay)]
    attn = [(m[c:] * d).astype(BF16) for m, d in zip(kq, decay)]
    xinv = [-a for a in lmat]
    lp = [a.astype(BF16) for a in lmat]
    lp_bd = [_head_blockdiag(a) for a in lp]
    for _ in range(int(math.log2(c)) - 1):
        lp32 = [_dot(a, bd) for a, bd in zip(lp, lp_bd)]
        lp = [a.astype(BF16) for a in lp32]
        lp_bd = [_head_blockdiag(a) for a in lp]
        xinv = [xi + a32 + _dot(xi.astype(BF16), bd) for xi, a32, bd in zip(xinv, lp32, lp_bd)]
    egc = [jnp.exp(a) for a in gcc]
    xb = [a.astype(BF16) for a in xinv]
    vbeta = [vv * b for vv, b in zip(v, beta)]
    kbeta = [kk * (b * e) for kk, b, e in zip(k, beta, egc)]
    u_val = [r + _dot(xi, _head_blockdiag(r.astype(BF16))) for xi, r in zip(xb, vbeta)]
    w_dec = [r + _dot(xi, _head_blockdiag(r.astype(BF16))) for xi, r in zip(xb, kbeta)]
    wq = [jnp.concatenate([wd, qq * e], axis=0).astype(BF16) for wd, qq, e in zip(w_dec, q, egc)]
    k_state = [(kk * jnp.exp(gl - a)).astype(BF16) for kk, gl, a in zip(k, g_last, gcc)]
    e_last = [jnp.exp(gl) for gl in g_last]
    keyed = lambda vals: dict(zip(items, vals))
    return dict(wq=keyed(wq), u_val=keyed(u_val), attn=keyed(attn), k_state=keyed(k_state), e_last=keyed(e_last))


def _gdn_kernel(uf_ref, gbf_ref, ub_ref, gbb_ref, tril_ref, triu_ref, of_ref, ob_ref, sf_ref, sb_ref):
    n = pl.program_id(1)
    t = uf_ref.shape[0]
    c, dh = GDN_CHUNK, HEAD_DIM
    nch = t // c
    pairs = MIXER_HEADS // 2

    @pl.when(n == 0)
    def _():
        sf_ref[...] = jnp.zeros_like(sf_ref)
        sb_ref[...] = jnp.zeros_like(sb_ref)

    tril, triu = tril_ref[...], triu_ref[...]
    prep_f = _gdn_prepare(True, uf_ref[...], gbf_ref[...], tril, triu)
    prep_b = _gdn_prepare(False, ub_ref[...], gbb_ref[...], triu, tril)

    first = lax.broadcasted_iota(jnp.int32, (dh, 2 * dh), 1) < dh
    chains = [(prep_f, sf_ref, of_ref, p, True) for p in range(pairs)] + \
             [(prep_b, sb_ref, ob_ref, p, False) for p in range(pairs)]
    state = [s_ref[p] for _, s_ref, _, p, _ in chains]
    for step in range(nch):
        key = [(step if fw else nch - 1 - step, p) for _, _, _, p, fw in chains]
        ws = [_dot(pr['wq'][kk], _head_blockdiag(st.astype(BF16))) for (pr, *_), kk, st in zip(chains, key, state)]
        v_new = [pr['u_val'][kk] - a[:c] for (pr, *_), kk, a in zip(chains, key, ws)]
        vb = [a.astype(BF16) for a in v_new]
        out = [a[c:] + _dot(pr['attn'][kk], _head_blockdiag(b)) for (pr, *_), kk, a, b in zip(chains, key, ws, vb)]
        cross = [_dot_tn(pr['k_state'][kk], b) for (pr, *_), kk, b in zip(chains, key, vb)]
        state = [st * pr['e_last'][kk] + jnp.where(first, x[:dh], x[dh:])
                 for (pr, *_), kk, st, x in zip(chains, key, state, cross)]
        for (_, _, o_ref, p, _), (ci, _), o in zip(chains, key, out):
            o_ref[ci * c:(ci + 1) * c, 2 * p * dh:2 * (p + 1) * dh] = o
    for (_, s_ref, _, p, _), st in zip(chains, state):
        s_ref[p] = st


def _gated_deltanet(p_d, p_ab, conv_w, alog_vec, dtb_vec, b, s):
    u, gb = _gdn_pre(p_d, p_ab, conv_w, alog_vec, dtb_vec, b, s)
    c = GDN_CHUNK
    t = c * GDN_CHUNKS_PER_STEP
    nb = s // t
    w = MIXER_WIDTH
    r = np.arange(t)
    same = (r[:, None] // c) == (r[None, :] // c)
    tril = jnp.asarray(same & (r[:, None] >= r[None, :]), BF16)
    triu = jnp.asarray(same & (r[:, None] <= r[None, :]), BF16)
    fwd = lambda bi, n: (bi * nb + n, 0)
    bwd = lambda bi, n: (bi * nb + nb - 1 - n, 0)
    const = lambda a: pl.BlockSpec(a.shape, lambda bi, n: (0,) * a.ndim)
    return pl.pallas_call(
        _gdn_kernel,
        grid=(b, nb),
        in_specs=[pl.BlockSpec((t, 3 * w), fwd), pl.BlockSpec((t, LANES), fwd),
                  pl.BlockSpec((t, 3 * w), bwd), pl.BlockSpec((t, LANES), bwd), const(tril), const(triu)],
        out_specs=[pl.BlockSpec((t, w), fwd), pl.BlockSpec((t, w), bwd)],
        out_shape=[jax.ShapeDtypeStruct((b * s, w), F32), jax.ShapeDtypeStruct((b * s, w), F32)],
        scratch_shapes=[pltpu.VMEM((MIXER_HEADS // 2, HEAD_DIM, 2 * HEAD_DIM), F32),
                        pltpu.VMEM((MIXER_HEADS // 2, HEAD_DIM, 2 * HEAD_DIM), F32)],
        compiler_params=_cparams("arbitrary", "arbitrary"),
    )(u, gb, u, gb, tril, triu)


def _mem_kv_kernel(m_ref, g_ref, wk_ref, wv_ref, k_ref, v_ref):
    h = _rms(m_ref[0], g_ref[...]).astype(BF16)
    k_ref[0] = _dot(h, wk_ref[...]).astype(BF16)
    v_ref[0] = _dot(h, wv_ref[...]).astype(BF16)


def _mem_kv(mem, g, wk, wv):
    b, nm, d = mem.shape
    const = lambda a: pl.BlockSpec(a.shape, lambda i: (0,) * a.ndim)
    blk = pl.BlockSpec((1, nm, d), lambda i: (i, 0, 0))
    return pl.pallas_call(
        _mem_kv_kernel, grid=(b,),
        in_specs=[blk, const(g), const(wk), const(wv)], out_specs=[blk, blk],
        out_shape=[jax.ShapeDtypeStruct((b, nm, d), BF16)] * 2,
        compiler_params=_cparams("parallel"),
    )(mem, g, wk, wv)


def _route_lanes(lg):
    lane_i = lax.broadcasted_iota(jnp.int32, lg.shape, 1)
    lane = lane_i.astype(F32)
    first_at = lambda vals, m: jnp.min(jnp.where(vals == m, lane, float(LANES)), axis=-1, keepdims=True)
    g_mask = lane_i < N_GROUPS
    gl = jnp.where(g_mask, lg, NEG_BIG)
    g_max = jnp.max(gl, axis=-1, keepdims=True)
    g_sum = jnp.sum(jnp.where(g_mask, jnp.exp(gl - g_max), 0.0), axis=-1, keepdims=True)
    g_p = 1.0 / g_sum
    g_idx = first_at(gl, g_max)
    lo = N_GROUPS + EXPERTS_PER_GROUP * g_idx
    e_mask = (lane >= lo) & (lane < lo + EXPERTS_PER_GROUP)
    el = jnp.where(e_mask, lg, NEG_BIG)
    m1 = jnp.max(el, axis=-1, keepdims=True)
    i1 = first_at(el, m1)
    el2 = jnp.where(lane == i1, NEG_BIG, el)
    m2 = jnp.max(el2, axis=-1, keepdims=True)
    i2 = first_at(el2, m2)
    r = jnp.exp(m2 - m1)
    w1 = g_p / (1.0 + r)
    w2 = g_p * r / (1.0 + r)
    e1, e2 = i1 - N_GROUPS, i2 - N_GROUPS
    e_lo, e_hi = jnp.minimum(e1, e2), jnp.maximum(e1, e2)
    w_lo, w_hi = jnp.where(e1 < e2, w1, w2), jnp.where(e1 < e2, w2, w1)
    a, b = e_lo - EXPERTS_PER_GROUP * g_idx, e_hi - EXPERTS_PER_GROUP * g_idx
    cls = g_idx * PAIRS_PER_GROUP + a * EXPERTS_PER_GROUP - a * (a + 1.0) * 0.5 + (b - a - 1.0)
    out = jnp.where(lane_i == 0, e_lo, 0.0)
    out = jnp.where(lane_i == 1, e_hi, out)
    out = jnp.where(lane_i == 2, w_lo, out)
    out = jnp.where(lane_i == 3, w_hi, out)
    out = jnp.where(lane_i == 4, cls, out)
    return out, jnp.where(lane == cls, 1.0, 0.0)


def _post_kernel(x_ref, ya_ref, yb_ref, yc_ref, of_ref, ob_ref, gate_ref, seg_ref, gg_ref, wout_ref, gq_ref,
                 wq_ref, k_ref, v_ref, wo_ref, gf_ref, wr_ref, br_ref, xo_ref, hx_ref, rt_ref, hist_ref):
    w = MIXER_WIDTH
    o = of_ref[...] + ob_ref[...]
    ms = sum(_dot(p, seg_ref[...]) for p in _split_bf16(o * o, 2))
    yd = o * lax.rsqrt(ms + EPS) * gg_ref[...] * _silu(gate_ref[...])
    mix = _dot(ya_ref[...], wout_ref[0:w, :]) + _dot(yb_ref[...], wout_ref[w:2 * w, :])
    mix = mix + _dot(yc_ref[...], wout_ref[2 * w:3 * w, :]) + _dot(yd.astype(BF16), wout_ref[3 * w:4 * w, :])
    x1 = x_ref[...] + mix
    q = _dot(_rms(x1, gq_ref[...]).astype(BF16), wq_ref[...]).astype(BF16)
    cols = [slice(h * MEM_HEAD_DIM, (h + 1) * MEM_HEAD_DIM) for h in range(MEM_HEADS)]
    sc = [_dot_nt(q[:, sl], k_ref[0, :, sl]) * (MEM_HEAD_DIM ** -0.5) for sl in cols]
    p = [jnp.exp(a - jnp.max(a, axis=-1, keepdims=True)) for a in sc]
    l = [jnp.sum(a, axis=-1, keepdims=True) for a in p]
    heads = [(_dot(a.astype(BF16), v_ref[0, :, sl]) / b).astype(BF16) for a, b, sl in zip(p, l, cols)]
    x2 = x1 + _dot(jnp.concatenate(heads, axis=-1), wo_ref[...])
    xo_ref[...] = x2
    hx = _rms(x2, gf_ref[...])
    _to_row_tiles(hx_ref, hx)
    h_hi, h_lo = _split_bf16(hx, 2)
    lg = _dot(h_hi, wr_ref[0]) + _dot(h_hi, wr_ref[1]) + _dot(h_lo, wr_ref[0]) + br_ref[...]
    route, chosen = _route_lanes(lg)
    rt_ref[...] = route
    hist_ref[...] = jnp.broadcast_to(jnp.sum(chosen, axis=0, keepdims=True), hist_ref.shape)


def _post(x, y_a, y_b, y_c, o_f, o_b, p_d, seg, gdn_g, w_out, g_q, w_q, k_mem, v_mem, w_o, g_f, w_r, b_r, s):
    n = x.shape[0]
    tm = POST_TILE
    w = MIXER_WIDTH
    per_seq = s // tm
    row = lambda c: pl.BlockSpec((tm, c), lambda i: (i, 0))
    const = lambda a: pl.BlockSpec(a.shape, lambda i: (0,) * a.ndim)
    kv = pl.BlockSpec((1,) + k_mem.shape[1:], lambda i: (i // per_seq, 0, 0))
    return pl.pallas_call(
        _post_kernel, grid=(n // tm,),
        in_specs=[row(D_MODEL), row(w), row(w), row(w), row(w), row(w),
                  pl.BlockSpec((tm, w), lambda i: (i, 3)), const(seg), const(gdn_g), const(w_out), const(g_q),
                  const(w_q), kv, kv, const(w_o), const(g_f), const(w_r), const(b_r)],
        out_specs=[row(D_MODEL), pl.BlockSpec((tm * ROW_TILES, LANES), lambda i: (i, 0)), row(LANES),
                   pl.BlockSpec((SUBLANES, LANES), lambda i: (i, 0))],
        out_shape=[jax.ShapeDtypeStruct((n, D_MODEL), F32), jax.ShapeDtypeStruct((n * ROW_TILES, LANES), F32),
                   jax.ShapeDtypeStruct((n, LANES), F32), jax.ShapeDtypeStruct((n // tm * SUBLANES, LANES), F32)],
        compiler_params=_cparams("parallel"),
    )(x, y_a, y_b, y_c, o_f, o_b, p_d, seg, gdn_g, w_out, g_q, w_q, k_mem, v_mem, w_o, g_f, w_r, b_r)


def _pair_class_experts():
    lo, hi = [], []
    for g in range(N_GROUPS):
        for a in range(EXPERTS_PER_GROUP):
            for b in range(a + 1, EXPERTS_PER_GROUP):
                lo.append(g * EXPERTS_PER_GROUP + a)
                hi.append(g * EXPERTS_PER_GROUP + b)
    return np.asarray(lo, np.int32), np.asarray(hi, np.int32)


def _dispatch(route, hist):
    n = route.shape[0]
    blk = MOE_BLOCK
    counts = jnp.sum(hist.reshape(-1, SUBLANES, LANES)[:, 0, :N_CLASSES], axis=0).astype(jnp.int32)
    padded = (counts + blk - 1) // blk * blk
    pad_end = jnp.cumsum(padded)
    n_blocks = n // blk + N_CLASSES
    blk_start = jnp.arange(n_blocks, dtype=jnp.int32) * blk
    blk_cls = jnp.minimum(jnp.sum(pad_end[None, :] <= blk_start[:, None], axis=1), N_CLASSES - 1).astype(jnp.int32)
    cls_lo, cls_hi = _pair_class_experts()
    in_blk = jnp.arange(blk, dtype=jnp.int32)[None, :]
    pad_key = jnp.where(in_blk < (padded - counts)[:, None], jnp.arange(N_CLASSES, dtype=jnp.int32)[:, None], N_CLASSES)
    keys = jnp.concatenate([route[:, 4].astype(jnp.int32), pad_key.reshape(-1)])
    toks = jnp.concatenate([jnp.arange(n, dtype=jnp.int32), jnp.tile(n + in_blk[0], N_CLASSES)])
    zeros = jnp.zeros((N_CLASSES * blk,), F32)
    w_lo = jnp.concatenate([route[:, TOP_K], zeros])
    w_hi = jnp.concatenate([route[:, TOP_K + 1], zeros])
    _, row_tok, w_lo, w_hi = lax.sort((keys, toks, w_lo, w_hi), num_keys=1)
    n_used = (pad_end[-1] // blk).astype(jnp.int32).reshape(1)
    return (n_used, jnp.asarray(cls_lo)[blk_cls], jnp.asarray(cls_hi)[blk_cls],
            jnp.concatenate([n + in_blk[0], row_tok]), jnp.stack([w_lo, w_hi], axis=-1))


def _moe_kernel(n_tokens, n_blocks, nu_ref, ea_ref, eb_ref, dst_ref, hx_hbm, rw_ref, wga_ref, wua_ref, wda_ref,
                wgb_ref, wub_ref, wdb_ref, y_hbm, xbuf, ybuf, gsem, ssem):
    j = pl.program_id(0)
    blk, rt = MOE_BLOCK, ROW_TILES
    n_used = nu_ref[0]
    slot = lax.rem(j, MOE_BUFFERS)
    ahead = lax.rem(j + 2, MOE_BUFFERS)
    tile = lambda ref, i: ref.at[pl.ds(pl.multiple_of(i * rt, rt), rt), :]
    dst = lambda jj, i: dst_ref[(jj + 1) * blk + i]
    clamp = lambda jj: jnp.minimum(jj, n_blocks - 1)

    def gather_row(jj, sl, i):
        tok = jnp.minimum(dst(jj, i), n_tokens - 1)
        pltpu.make_async_copy(tile(hx_hbm, tok), tile(xbuf.at[sl], i), gsem.at[sl]).start()

    def scatter_row(jj, sl, i):
        pltpu.make_async_copy(tile(ybuf.at[sl], i), tile(y_hbm, dst(jj, i)), ssem.at[sl]).start()

    def wait_gather(sl):
        pltpu.make_async_copy(hx_hbm.at[pl.ds(0, blk * rt), :], xbuf.at[sl], gsem.at[sl]).wait()

    def wait_scatter(sl):
        pltpu.make_async_copy(ybuf.at[sl], y_hbm.at[pl.ds(0, blk * rt), :], ssem.at[sl]).wait()

    def rolled(fn):
        lax.fori_loop(0, blk, lambda i, carry: (fn(i), carry)[1], 0, unroll=DMA_UNROLL)

    @pl.when(j == 0)
    def _():
        ybuf[MOE_BUFFERS - 1] = jnp.zeros((blk * rt, LANES), F32)
        rolled(lambda i: gather_row(0, 0, i))
        rolled(lambda i: gather_row(clamp(1), 1, i))

    @pl.when(j < n_used)
    def _():
        wait_gather(slot)

        @pl.when(j >= 2)
        def _():
            wait_scatter(slot)

        def issue_copies(piece):
            for i in range(piece * blk // MOE_PIECES, (piece + 1) * blk // MOE_PIECES):
                gather_row(clamp(j + 2), ahead, i)
                scatter_row(j - 1, ahead, i)

        xs = _from_row_tiles(xbuf.at[slot], 0, blk, rt).astype(BF16)
        half = D_EXPERT // 2
        hidden = []
        for e, (wg_ref, wu_ref) in enumerate(((wga_ref, wua_ref), (wgb_ref, wub_ref))):
            acts = []
            for c in range(2):
                cols = slice(c * half, (c + 1) * half)
                g = _dot(xs, wg_ref[0, 0, :, cols])
                issue_copies(4 * e + 2 * c)
                u = _dot(xs, wu_ref[0, 0, :, cols])
                issue_copies(4 * e + 2 * c + 1)
                acts.append((_silu(g) * u).astype(BF16))
            hidden.append(jnp.concatenate(acts, axis=1))
        w_a, w_b = rw_ref[:, 0:1], rw_ref[:, 1:2]
        quarter = D_MODEL // 4
        for c in range(4):
            cols = slice(c * quarter, (c + 1) * quarter)
            y = w_a * _dot(hidden[0], wda_ref[0, 0, :, cols]) + w_b * _dot(hidden[1], wdb_ref[0, 0, :, cols])
            for k in range(quarter // LANES):
                ybuf[slot, pl.ds(c * (quarter // LANES) + k, blk, stride=rt), :] = y[:, k * LANES:(k + 1) * LANES]
            issue_copies(8 + c)

    @pl.when(j == n_used)
    def _():
        after = lax.rem(j + 1, MOE_BUFFERS)
        wait_gather(slot)
        wait_gather(after)

        @pl.when(j >= 2)
        def _():
            wait_scatter(slot)

        wait_scatter(after)
        rolled(lambda i: scatter_row(j - 1, ahead, i))
        wait_scatter(ahead)


def _moe(hx_tiles, n_used, blk_ea, blk_eb, row_tok, row_w, w_gate, w_up, w_down, layer, n):
    blk = MOE_BLOCK
    n_blocks = blk_ea.shape[0]
    block = lambda j: jnp.minimum(j, n_blocks - 1)
    first = lambda j, nu, ea, eb, dst: (layer, ea[block(j)], 0, 0)
    second = lambda j, nu, ea, eb, dst: (layer, eb[block(j)], 0, 0)
    up, down = (1, 1, D_MODEL, D_EXPERT), (1, 1, D_EXPERT, D_MODEL)
    weights = pl.BlockSpec
    grid_spec = pltpu.PrefetchScalarGridSpec(
        num_scalar_prefetch=4, grid=(n_blocks + 1,),
        in_specs=[pl.BlockSpec(memory_space=pl.ANY),
                  pl.BlockSpec((blk, TOP_K), lambda j, nu, ea, eb, dst: (block(j), 0)),
                  weights(up, first), weights(up, first), weights(down, first),
                  weights(up, second), weights(up, second), weights(down, second)],
        out_specs=pl.BlockSpec(memory_space=pl.ANY),
        scratch_shapes=[pltpu.VMEM((MOE_BUFFERS, blk * ROW_TILES, LANES), F32),
                        pltpu.VMEM((MOE_BUFFERS, blk * ROW_TILES, LANES), F32),
                        pltpu.SemaphoreType.DMA((MOE_BUFFERS,)), pltpu.SemaphoreType.DMA((MOE_BUFFERS,))])
    return pl.pallas_call(
        functools.partial(_moe_kernel, n, n_blocks), grid_spec=grid_spec,
        out_shape=jax.ShapeDtypeStruct(((n + blk) * ROW_TILES, LANES), F32),
        compiler_params=_cparams("arbitrary"),
    )(n_used, blk_ea, blk_eb, row_tok, hx_tiles, row_w, w_gate, w_up, w_down, w_gate, w_up, w_down)


def _final_kernel(x_ref, y_ref, g_ref, o_ref):
    o_ref[...] = _rms(_add_moe(x_ref[...], y_ref), g_ref[...])


def _final(x, moe, g):
    n = x.shape[0]
    tm = PROJ_TILE
    return pl.pallas_call(
        _final_kernel, grid=(n // tm,),
        in_specs=[pl.BlockSpec((tm, D_MODEL), lambda i: (i, 0)), _moe_out_spec(tm),
                  pl.BlockSpec(g.shape, lambda i: (0, 0))],
        out_specs=pl.BlockSpec((tm, D_MODEL), lambda i: (i, 0)),
        out_shape=jax.ShapeDtypeStruct((n, D_MODEL), F32),
        compiler_params=_cparams("parallel"),
    )(x, moe, g)


def _lane_vec(v, width=LANES):
    v = v.reshape(1, -1).astype(F32)
    return jnp.pad(v, ((0, 0), (0, width - v.shape[1])))


def _block_diag(blocks):
    g, c, _ = blocks.shape
    eye = jnp.eye(g, dtype=blocks.dtype)
    return (eye[:, None, :, None] * blocks[:, :, None, :]).reshape(g * c, g * c)


def kernel(x, mem, mix_norm_g, w_in, na_rpb, ret_norm_g, pool_w, pool_scale, gdn_conv_w, gdn_a_log, gdn_dt_bias, gdn_norm_g, w_out, mem_q_norm_g, mem_kv_norm_g, mem_w_q, mem_w_k, mem_w_v, mem_w_o, ffn_norm_g, w_group, b_group, w_router, b_router, w_gate, w_up, w_down, final_norm_g):
    b, s, d = x.shape
    n = b * s
    depth = w_in.shape[0]
    row = lambda v: v.reshape(1, -1).astype(F32)
    ret_tables = _retention_tables(s)
    seg = _block_diag(jnp.full((MIXER_HEADS, HEAD_DIM, HEAD_DIM), 1.0 / HEAD_DIM, BF16))
    wg_bf, wu_bf, wd_bf = w_gate.astype(BF16), w_up.astype(BF16), w_down.astype(BF16)
    xs = x.reshape(n, d)
    moe = None
    for l in range(depth):
        w_pad = jnp.pad(w_in[l], ((0, 0), (0, P_IN_PAD - w_in.shape[2]))).astype(BF16)
        xs, p_a, p_b, p_c, p_d, p_ab = _norm_proj(xs, moe, row(mix_norm_g[l]), w_pad)
        y_a = _neighbourhood_attention(p_a, _na_bias_table(na_rpb[l]), b, s)
        y_b = _retention(p_b, row(ret_norm_g[l]), ret_tables, b, s)
        y_c = _pool(p_c, _block_diag(pool_w[l]).astype(BF16), row(pool_scale[l]), b, s)
        o_f, o_b = _gated_deltanet(p_d, p_ab, gdn_conv_w[l].astype(F32), _lane_vec(gdn_a_log[l]),
                                   _lane_vec(gdn_dt_bias[l]), b, s)
        k_mem, v_mem = _mem_kv(mem, row(mem_kv_norm_g[l]), mem_w_k[l].astype(BF16), mem_w_v[l].astype(BF16))
        w_r = jnp.pad(jnp.concatenate([w_group[l], w_router[l]], axis=1),
                      ((0, 0), (0, LANES - N_GROUPS - N_EXPERTS))).astype(F32)
        b_r = _lane_vec(jnp.concatenate([b_group[l], b_router[l]]))
        xs, hx_tiles, route, hist = _post(
            xs, y_a, y_b, y_c, o_f, o_b, p_d, seg, row(jnp.tile(gdn_norm_g[l], MIXER_HEADS)), w_out[l].astype(BF16),
            row(mem_q_norm_g[l]), mem_w_q[l].astype(BF16), k_mem, v_mem, mem_w_o[l].astype(BF16),
            row(ffn_norm_g[l]), jnp.stack(_split_bf16(w_r, 2)), b_r, s)
        moe = _moe(hx_tiles, *_dispatch(route, hist), wg_bf, wu_bf, wd_bf, l, n)
    return _final(xs, moe, row(final_norm_g)).reshape(b, s, d)
```

```python
import functools
import math

import jax
import jax.numpy as jnp
import numpy as np
from jax import lax
from jax.experimental import pallas as pl
from jax.experimental.pallas import tpu as pltpu

F32 = jnp.float32
BF16 = jnp.bfloat16
HIGHEST = lax.Precision.HIGHEST

D_MODEL = 1024
GRID_W = 64
HEAD_DIM = 64
MIXER_WIDTH = 256
MIXER_HEADS = 4
NA_KH = 8
NA_KW = 16
RET_CHUNK = 128
ROPE_BASE = 10000.0
POOL_WINDOWS = (2, 4, 8, 16)
POOL_GROUP = 64
GDN_CHUNK = 64
CONV_K = 4
MEM_HEADS = 4
MEM_HEAD_DIM = 256
N_GROUPS = 4
EXPERTS_PER_GROUP = 8
N_EXPERTS = 32
TOP_K = 2
D_EXPERT = 512
EPS = 1e-6

P_IN_PAD = 3200
V7X_VMEM_LIMIT = 56 * 1024 * 1024
LANES = 128
SUBLANES = 8
ROW_TILES = D_MODEL // LANES
NEG_BIG = -1e30

PROJ_TILE = 512
POOL_TILE = 512
POST_TILE = 512
NA_ROWS_PER_STEP = 8
NA_ROWS_LOCKSTEP = 4
GDN_CHUNKS_PER_STEP = 16
GDN_PRE_TILE = 512
RET_CHUNKS_PER_STEP = 8
PAIRS_PER_GROUP = EXPERTS_PER_GROUP * (EXPERTS_PER_GROUP - 1) // 2
N_CLASSES = N_GROUPS * PAIRS_PER_GROUP
MOE_BLOCK = 128
MOE_BUFFERS = 3
MOE_PIECES = 12
DMA_UNROLL = 8
HALO = 8


def _cparams(*sem):
    return pltpu.CompilerParams(dimension_semantics=sem, vmem_limit_bytes=V7X_VMEM_LIMIT)


def _dot(a, b):
    return jnp.dot(a, b, preferred_element_type=F32)


def _dot_nt(a, b):
    return lax.dot_general(a, b, (((1,), (1,)), ((), ())), preferred_element_type=F32)


def _dot_tn(a, b, precision=None):
    return lax.dot_general(a, b, (((0,), (0,)), ((), ())), preferred_element_type=F32, precision=precision)


def _silu(x):
    return x * (1.0 / (1.0 + jnp.exp(-x)))


def _rms(x, g):
    return x * lax.rsqrt(jnp.mean(x * x, axis=-1, keepdims=True) + EPS) * g


def _from_row_tiles(ref, first, count, stride):
    return jnp.concatenate([ref[pl.ds(first + k, count, stride=stride), :] for k in range(ROW_TILES)], axis=1)


def _to_row_tiles(ref, x):
    count = x.shape[0]
    for k in range(ROW_TILES):
        ref[pl.ds(k, count, stride=ROW_TILES), :] = x[:, k * LANES:(k + 1) * LANES]


def _add_moe(x, y_ref):
    return x + _from_row_tiles(y_ref, 0, x.shape[0], ROW_TILES)


def _split_bf16(x, parts):
    out = []
    for _ in range(parts):
        p = x.astype(BF16)
        out.append(p)
        x = x - p.astype(F32)
    return out


def _norm_proj_kernel(has_y, *refs):
    if has_y:
        x_ref, y_ref, g_ref, w_ref, xo_ref, pa_ref, pb_ref, pc_ref, pd_ref, pab_ref = refs
        x = _add_moe(x_ref[...], y_ref)
        xo_ref[...] = x
    else:
        x_ref, g_ref, w_ref, pa_ref, pb_ref, pc_ref, pd_ref, pab_ref = refs
        x = x_ref[...]
    h = _rms(x, g_ref[...]).astype(BF16)
    w = MIXER_WIDTH
    pa_ref[...] = _dot(h, w_ref[:, 0:3 * w]).astype(BF16)
    pb_ref[...] = _dot(h, w_ref[:, 3 * w:7 * w])
    pc_ref[...] = _dot(h, w_ref[:, 7 * w:8 * w])
    pd_ref[...] = _dot(h, w_ref[:, 8 * w:12 * w])
    pab_ref[...] = _dot(h, w_ref[:, 12 * w:P_IN_PAD])


def _moe_out_spec(tile):
    return pl.BlockSpec((tile * ROW_TILES, LANES), lambda i: (i, 0))


def _norm_proj(x, moe, g, w_pad):
    n = x.shape[0]
    tm = PROJ_TILE
    w = MIXER_WIDTH
    row = lambda c: pl.BlockSpec((tm, c), lambda i: (i, 0))
    const = lambda a: pl.BlockSpec(a.shape, lambda i: (0,) * a.ndim)
    outs = [jax.ShapeDtypeStruct((n, 3 * w), BF16), jax.ShapeDtypeStruct((n, 4 * w), F32),
            jax.ShapeDtypeStruct((n, w), F32), jax.ShapeDtypeStruct((n, 4 * w), F32),
            jax.ShapeDtypeStruct((n, LANES), F32)]
    out_specs = [row(3 * w), row(4 * w), row(w), row(4 * w), row(LANES)]
    if moe is None:
        ins, in_specs = (x, g, w_pad), [row(D_MODEL), const(g), const(w_pad)]
    else:
        ins = (x, moe, g, w_pad)
        in_specs = [row(D_MODEL), _moe_out_spec(tm), const(g), const(w_pad)]
        outs = [jax.ShapeDtypeStruct((n, D_MODEL), F32)] + outs
        out_specs = [row(D_MODEL)] + out_specs
    res = pl.pallas_call(
        functools.partial(_norm_proj_kernel, moe is not None),
        grid=(n // tm,), in_specs=in_specs, out_specs=out_specs, out_shape=outs,
        compiler_params=_cparams("parallel"),
    )(*ins)
    return res if moe is not None else [x] + list(res)


def _na_bias_table(rpb):
    qc = np.arange(GRID_W)
    kc = np.arange(GRID_W)
    d_col = np.clip(kc[None, :] - qc[:, None], -(NA_KW - 1), NA_KW - 1) + (NA_KW - 1)
    pick = np.zeros((2 * NA_KW - 1, GRID_W * GRID_W), np.float32)
    pick[d_col.reshape(-1), np.arange(GRID_W * GRID_W)] = 1.0
    win = np.clip(qc - NA_KW // 2, 0, GRID_W - NA_KW)
    ok = (kc[None, :] >= win[:, None]) & (kc[None, :] < win[:, None] + NA_KW)
    cols = jnp.einsum('hrc,cx->hrx', rpb.astype(F32), jnp.asarray(pick), precision=HIGHEST)
    cols = jnp.where(ok[None, None], cols.reshape(MIXER_HEADS, 2 * NA_KH - 1, GRID_W, GRID_W), NEG_BIG)
    per_e = [jnp.transpose(cols[:, NA_KH - 1 - e:2 * NA_KH - 1 - e], (0, 2, 1, 3)) for e in range(NA_KH)]
    return jnp.stack(per_e, axis=1).reshape(MIXER_HEADS, NA_KH, GRID_W, NA_KH * GRID_W)


def _na_kernel(rows, q_ref, k_ref, v_ref, t_ref, o_ref):
    i = pl.program_id(1)
    dh = HEAD_DIM

    def rows_body(it, carry):
        chains = []
        for k in range(NA_ROWS_LOCKSTEP):
            rr = it * NA_ROWS_LOCKSTEP + k
            r = i * NA_ROWS_PER_STEP + rr
            kr0 = jnp.clip(r - NA_KH // 2, 0, rows - NA_KH)
            q = q_ref[pl.ds(pl.multiple_of(rr * GRID_W, GRID_W), GRID_W), :]
            k0 = pl.multiple_of(kr0 * GRID_W, GRID_W)
            kb = k_ref[pl.ds(k0, NA_KH * GRID_W), :]
            vb = v_ref[pl.ds(k0, NA_KH * GRID_W), :]
            for h in range(MIXER_HEADS):
                sl = slice(h * dh, (h + 1) * dh)
                chains.append((q[:, sl], kb[:, sl], vb[:, sl], h, r - kr0))
        s = [_dot_nt(q, kb) * (dh ** -0.5) + t_ref[h, e] for q, kb, _, h, e in chains]
        m = [jnp.max(a, axis=-1, keepdims=True) for a in s]
        p = [jnp.exp(a - b) for a, b in zip(s, m)]
        l = [jnp.sum(a, axis=-1, keepdims=True) for a in p]
        o = [_dot(a.astype(BF16), c[2]) / b for a, b, c in zip(p, l, chains)]
        for k in range(NA_ROWS_LOCKSTEP):
            rr = it * NA_ROWS_LOCKSTEP + k
            row = jnp.concatenate(o[k * MIXER_HEADS:(k + 1) * MIXER_HEADS], axis=-1)
            o_ref[pl.ds(pl.multiple_of(rr * GRID_W, GRID_W), GRID_W), :] = row.astype(BF16)
        return carry

    lax.fori_loop(0, NA_ROWS_PER_STEP // NA_ROWS_LOCKSTEP, rows_body, 0)


def _neighbourhood_attention(p_a, table, b, s):
    rows = s // GRID_W
    assert rows >= NA_KH and rows % NA_ROWS_PER_STEP == 0
    steps = rows // NA_ROWS_PER_STEP
    tq = NA_ROWS_PER_STEP * GRID_W
    w = MIXER_WIDTH
    return pl.pallas_call(
        functools.partial(_na_kernel, rows),
        grid=(b, steps),
        in_specs=[pl.BlockSpec((tq, w), lambda bi, i: (bi * steps + i, 0)),
                  pl.BlockSpec((s, w), lambda bi, i: (bi, 1)),
                  pl.BlockSpec((s, w), lambda bi, i: (bi, 2)),
                  pl.BlockSpec(table.shape, lambda bi, i: (0, 0, 0, 0))],
        out_specs=pl.BlockSpec((tq, w), lambda bi, i: (bi * steps + i, 0)),
        out_shape=jax.ShapeDtypeStruct((b * s, w), BF16),
        compiler_params=_cparams("parallel", "arbitrary"),
    )(p_a, p_a, p_a, table)


def _retention_tables(s):
    h, dh, c = MIXER_HEADS, HEAD_DIM, RET_CHUNK
    half = dh // 2
    inv = ROPE_BASE ** (-jnp.arange(half, dtype=F32) / half)
    ang = jnp.arange(s, dtype=F32)[:, None] * inv[None, :]
    cos, sin = jnp.cos(ang), jnp.sin(ang)
    zero = jnp.zeros_like(sin)
    cos_t = jnp.tile(jnp.concatenate([cos, cos], axis=-1), (1, h))
    sin_lo = jnp.tile(jnp.concatenate([-sin, zero], axis=-1), (1, h))
    sin_hi = jnp.tile(jnp.concatenate([zero, sin], axis=-1), (1, h))
    log_f = np.log1p(-np.exp2(-5.0 - np.arange(h, dtype=np.float64)))
    log_b = log_f[::-1]
    pos = np.arange(c, dtype=np.float64)
    diff = pos[:, None] - pos[None, :]
    dmat = np.where(diff >= 0, np.exp(log_f[:, None, None] * np.maximum(diff, 0.0)), 0.0) \
        + np.where(diff < 0, np.exp(log_b[:, None, None] * np.maximum(-diff, 0.0)), 0.0)
    lanes = lambda t: np.repeat(t.T, dh, axis=1)
    dec = np.stack([lanes(np.exp(log_f[:, None] * (pos + 1.0))),
                    lanes(np.exp(log_f[:, None] * (c - 1.0 - pos))),
                    lanes(np.exp(log_b[:, None] * (c - pos))),
                    lanes(np.exp(log_b[:, None] * pos))])
    chunk_f = [float(np.exp(v * c)) for v in log_f]
    chunk_b = [float(np.exp(v * c)) for v in log_b]
    return cos_t, sin_lo, sin_hi, jnp.asarray(dmat, F32), jnp.asarray(dec, F32), chunk_f, chunk_b


def _rotary(x, cos, sin_lo, sin_hi):
    w = x.shape[-1]
    return x * cos + pltpu.roll(x, w - HEAD_DIM // 2, 1) * sin_lo + pltpu.roll(x, HEAD_DIM // 2, 1) * sin_hi


def _retention_kernel(nb, chunk_f, chunk_b, q_ref, k_ref, v_ref, gate_ref, cos_ref, slo_ref, shi_ref,
                      dmat_ref, dec_ref, ng_ref, o_ref, sf_ref, sb_ref, sball_ref):
    t = pl.program_id(1)
    dh, c, cps = HEAD_DIM, RET_CHUNK, RET_CHUNKS_PER_STEP
    cos, slo, shi = cos_ref[...], slo_ref[...], shi_ref[...]
    kr = _rotary(k_ref[...], cos, slo, shi) * (dh ** -0.5)
    vb = v_ref[...].astype(BF16)
    items = [(ci, h) for ci in range(cps) for h in range(MIXER_HEADS)]
    part = lambda x, ci, h: x[ci * c:(ci + 1) * c, h * dh:(h + 1) * dh]
    split = lambda x: {it: part(x, *it) for it in items}

    @pl.when(t == 0)
    def _():
        sf_ref[...] = jnp.zeros_like(sf_ref)
        sb_ref[...] = jnp.zeros_like(sb_ref)

    @pl.when(t < nb)
    def _():
        blk = nb - 1 - t
        kd, v = split((kr * dec_ref[3]).astype(BF16)), split(vb)
        kv = {it: _dot_tn(kd[it], v[it]) for it in items}
        for h in range(MIXER_HEADS):
            state = sb_ref[h]
            for ci in range(cps - 1, -1, -1):
                sball_ref[blk * cps + ci, h] = state
                state = state * chunk_b[h] + kv[ci, h]
            sb_ref[h] = state

    @pl.when(t >= nb)
    def _():
        blk = t - nb
        qr = _rotary(q_ref[...], cos, slo, shi)
        q, k, v = split(qr.astype(BF16)), split(kr.astype(BF16)), split(vb)
        qf, qbk = split((qr * dec_ref[0]).astype(BF16)), split((qr * dec_ref[2]).astype(BF16))
        kd = split((kr * dec_ref[1]).astype(BF16))
        sc = {it: (_dot_nt(q[it], k[it]) * dmat_ref[it[1]]).astype(BF16) for it in items}
        o = {it: _dot(sc[it], v[it]) for it in items}
        kv = {it: _dot_tn(kd[it], v[it]) for it in items}
        ob = {it: _dot(qbk[it], sball_ref[blk * cps + it[0], it[1]].astype(BF16)) for it in items}
        of = {}
        for h in range(MIXER_HEADS):
            state = sf_ref[h]
            for ci in range(cps):
                of[ci, h] = _dot(qf[ci, h], state.astype(BF16))
                state = state * chunk_f[h] + kv[ci, h]
            sf_ref[h] = state
        o = {it: o[it] + of[it] + ob[it] for it in items}
        mu = {it: jnp.mean(o[it], axis=-1, keepdims=True) for it in items}
        oc = {it: o[it] - mu[it] for it in items}
        var = {it: jnp.mean(oc[it] * oc[it], axis=-1, keepdims=True) for it in items}
        on = {it: oc[it] * lax.rsqrt(var[it] + EPS) for it in items}
        y = jnp.concatenate([jnp.concatenate([on[ci, h] for h in range(MIXER_HEADS)], axis=-1)
                             for ci in range(cps)], axis=0)
        o_ref[...] = (y * ng_ref[...] * _silu(gate_ref[...])).astype(BF16)


def _retention(p_b, norm_g, tables, b, s):
    cos_t, sin_lo, sin_hi, dmat, dec, chunk_f, chunk_b = tables
    cps = RET_CHUNKS_PER_STEP
    tb = RET_CHUNK * cps
    nb = s // tb
    w = MIXER_WIDTH
    dec = jnp.tile(dec, (1, cps, 1))
    block = lambda t: jnp.where(t < nb, nb - 1 - t, t - nb)
    col = lambda j: pl.BlockSpec((tb, w), lambda bi, t: (bi * nb + block(t), j))
    tab = pl.BlockSpec((tb, w), lambda bi, t: (block(t), 0))
    const = lambda a: pl.BlockSpec(a.shape, lambda bi, t: (0,) * a.ndim)
    return pl.pallas_call(
        functools.partial(_retention_kernel, nb, chunk_f, chunk_b),
        grid=(b, 2 * nb),
        in_specs=[col(0), col(1), col(2), col(3), tab, tab, tab, const(dmat), const(dec), const(norm_g)],
        out_specs=pl.BlockSpec((tb, w), lambda bi, t: (bi * nb + jnp.maximum(t - nb, 0), 0)),
        out_shape=jax.ShapeDtypeStruct((b * s, w), BF16),
        scratch_shapes=[pltpu.VMEM((MIXER_HEADS, HEAD_DIM, HEAD_DIM), F32),
                        pltpu.VMEM((MIXER_HEADS, HEAD_DIM, HEAD_DIM), F32),
                        pltpu.VMEM((s // RET_CHUNK, MIXER_HEADS, HEAD_DIM, HEAD_DIM), F32)],
        compiler_params=_cparams("arbitrary", "arbitrary"),
    )(p_b, p_b, p_b, p_b, cos_t, sin_lo, sin_hi, dmat, dec, norm_g)


def _halo_specs(tile, width, n_rows, block_of):
    per = tile // HALO
    last = n_rows // HALO - 1
    main = pl.BlockSpec((tile, width), lambda *g: (block_of(*g), 0))
    prev = pl.BlockSpec((HALO, width), lambda *g: (jnp.maximum(block_of(*g) * per - 1, 0), 0))
    nxt = pl.BlockSpec((HALO, width), lambda *g: (jnp.minimum((block_of(*g) + 1) * per, last), 0))
    return main, prev, nxt


def _fill_padded(pad_ref, main, prev, nxt, first, last):
    t = main.shape[0]
    pad_ref[0:HALO, :] = jnp.where(first, 0.0, prev)
    pad_ref[HALO:HALO + t, :] = main
    pad_ref[HALO + t:2 * HALO + t, :] = jnp.where(last, 0.0, nxt)


def _pool_kernel(s, tiles_per_seq, u_ref, prev_ref, next_ref, w_ref, scale_ref, o_ref, pad_ref):
    t = POOL_TILE
    tseq = pl.program_id(0) % tiles_per_seq
    u = u_ref[...]
    _fill_padded(pad_ref, u, prev_ref[...], next_ref[...], tseq == 0, tseq == tiles_per_seq - 1)
    lane = lax.broadcasted_iota(jnp.int32, (1, MIXER_WIDTH), 1)
    half = jnp.full((1, MIXER_WIDTH), POOL_WINDOWS[0] // 2, jnp.int32)
    for gi in range(1, len(POOL_WINDOWS)):
        half = jnp.where(lane >= gi * POOL_GROUP, POOL_WINDOWS[gi] // 2, half)
    max_half = POOL_WINDOWS[-1] // 2
    acc = jnp.zeros((t, MIXER_WIDTH), F32)
    for d in range(-max_half, max_half):
        inside = (d >= -half) & (d < half)
        acc = acc + jnp.where(inside, pad_ref[HALO + d:HALO + d + t, :], 0.0)
    pos = tseq * t + lax.broadcasted_iota(jnp.int32, (t, 1), 0)
    count = jnp.minimum(pos + half, s) - jnp.maximum(pos - half, 0)
    diff = acc / count.astype(F32) - u
    o_ref[...] = (_dot(diff.astype(BF16), w_ref[...]) * scale_ref[...]).astype(BF16)


def _pool(p_c, w_blockdiag, scale, b, s):
    t = POOL_TILE
    tiles_per_seq = s // t
    main, prev, nxt = _halo_specs(t, MIXER_WIDTH, b * s, lambda i: i)
    const = lambda a: pl.BlockSpec(a.shape, lambda i: (0,) * a.ndim)
    return pl.pallas_call(
        functools.partial(_pool_kernel, s, tiles_per_seq),
        grid=(b * tiles_per_seq,),
        in_specs=[main, prev, nxt, const(w_blockdiag), const(scale)],
        out_specs=pl.BlockSpec((t, MIXER_WIDTH), lambda i: (i, 0)),
        out_shape=jax.ShapeDtypeStruct((b * s, MIXER_WIDTH), BF16),
        scratch_shapes=[pltpu.VMEM((t + 2 * HALO, MIXER_WIDTH), F32)],
        compiler_params=_cparams("parallel"),
    )(p_c, p_c, p_c, w_blockdiag, scale)


def _head_blockdiag(x):
    lane = lax.broadcasted_iota(jnp.int32, x.shape, 1)
    zero = jnp.zeros_like(x)
    return jnp.concatenate([jnp.where(lane < HEAD_DIM, x, zero), jnp.where(lane >= HEAD_DIM, x, zero)], axis=0)


def _gdn_pre_kernel(tiles_per_seq, x_ref, prev_ref, next_ref, ab_ref, cw_ref, alog_ref, dtb_ref, ones_ref,
                    u_ref, gb_ref, pad_ref):
    t, w = GDN_PRE_TILE, MIXER_WIDTH
    tseq = pl.program_id(0) % tiles_per_seq
    _fill_padded(pad_ref, x_ref[...], prev_ref[...], next_ref[...], tseq == 0, tseq == tiles_per_seq - 1)
    acc = jnp.zeros((t, 3 * w), F32)
    for k in range(CONV_K):
        off = HALO + k - CONV_K // 2
        acc = acc + pad_ref[off:off + t, :] * cw_ref[k:k + 1, :]
    u = _silu(acc)
    head_sum = lambda a: sum(_dot(piece, ones_ref[...]) for piece in _split_bf16(a, 2))
    q, k = u[:, 0:w], u[:, w:2 * w]
    u_ref[:, 0:w] = q * lax.rsqrt(head_sum(q * q) + EPS) * (HEAD_DIM ** -0.5)
    u_ref[:, w:2 * w] = k * lax.rsqrt(head_sum(k * k) + EPS)
    u_ref[:, 2 * w:3 * w] = u[:, 2 * w:3 * w]
    ab = ab_ref[...]
    x = ab + dtb_ref[...]
    g = -jnp.exp(alog_ref[...]) * (jnp.maximum(x, 0.0) + jnp.log(1.0 + jnp.exp(-jnp.abs(x))))
    beta = 1.0 / (1.0 + jnp.exp(-ab))
    lane = lax.broadcasted_iota(jnp.int32, ab.shape, 1)
    gb_ref[...] = jnp.where(lane < 2 * MIXER_HEADS, g, beta)


def _gdn_pre(p_d, p_ab, conv_w, alog_vec, dtb_vec, b, s):
    t = GDN_PRE_TILE
    w = MIXER_WIDTH
    tiles_per_seq = s // t
    ones = _block_diag(jnp.ones((MIXER_HEADS, HEAD_DIM, HEAD_DIM), BF16))
    main, prev, nxt = _halo_specs(t, 3 * w, b * s, lambda i: i)
    const = lambda a: pl.BlockSpec(a.shape, lambda i: (0,) * a.ndim)
    row = lambda c: pl.BlockSpec((t, c), lambda i: (i, 0))
    return pl.pallas_call(
        functools.partial(_gdn_pre_kernel, tiles_per_seq),
        grid=(b * tiles_per_seq,),
        in_specs=[main, prev, nxt, row(LANES), const(conv_w), const(alog_vec), const(dtb_vec), const(ones)],
        out_specs=[row(3 * w), row(LANES)],
        out_shape=[jax.ShapeDtypeStruct((b * s, 3 * w), F32), jax.ShapeDtypeStruct((b * s, LANES), F32)],
        scratch_shapes=[pltpu.VMEM((t + 2 * HALO, 3 * w), F32)],
        compiler_params=_cparams("parallel"),
    )(p_d, p_d, p_d, p_ab, conv_w, alog_vec, dtb_vec, ones)


def _gdn_prepare(forward, u, gb, tri_col, tri_row):
    c, dh, w = GDN_CHUNK, HEAD_DIM, MIXER_WIDTH
    t = u.shape[0]
    a_off, b_off = (0, 2 * MIXER_HEADS) if forward else (MIXER_HEADS, 3 * MIXER_HEADS)
    g_all = beta_all = gb
    g_parts = _split_bf16(g_all, 3)
    gc_col = sum(_dot(tri_col, p) for p in g_parts)
    gc_row = sum(_dot_tn(p, tri_row) for p in g_parts)
    ii = lax.broadcasted_iota(jnp.int32, (c, 2 * dh), 0)
    lane = lax.broadcasted_iota(jnp.int32, (c, 2 * dh), 1)
    first = lane < dh
    jj = jnp.where(first, lane, lane - dh)
    incl = (ii >= jj) if forward else (ii <= jj)
    strict = (ii > jj) if forward else (ii < jj)
    last = c - 1 if forward else 0
    items = [(ci, p) for ci in range(t // c) for p in range(MIXER_HEADS // 2)]
    rows = lambda ci: slice(ci * c, (ci + 1) * c)
    part = lambda base: [u[rows(ci), base + 2 * p * dh:base + 2 * (p + 1) * dh] for ci, p in items]
    col_pair = lambda x, off: [jnp.where(first, x[rows(ci), off + 2 * p:off + 2 * p + 1],
                                         x[rows(ci), off + 2 * p + 1:off + 2 * p + 2]) for ci, p in items]
    q, k, v = part(0), part(w), part(2 * w)
    gcc = col_pair(gc_col, a_off)
    beta = col_pair(beta_all, b_off)
    gcr = [jnp.concatenate([gc_row[a_off + 2 * p:a_off + 2 * p + 1, rows(ci)],
                            gc_row[a_off + 2 * p + 1:a_off + 2 * p + 2, rows(ci)]], axis=1) for ci, p in items]
    g_last = [a[last:last + 1, :] for a in gcc]
    decay = [jnp.where(incl, jnp.exp(jnp.where(incl, a - b, 0.0)), 0.0) for a, b in zip(gcc, gcr)]
    kq = [_dot_nt(jnp.concatenate([kk, qq], axis=0).astype(BF16), _head_blockdiag(kk.astype(BF16)))
          for kk, qq in zip(k, q)]
    lmat = [jnp.where(strict, m[:c] * b * d, 0.0) for m, b, d in zip(kq, beta, decay)]
    attn = [(m[c:] * d).astype(BF16) for m, d in zip(kq, decay)]
    xinv = [-a for a in lmat]
    lp = [a.astype(BF16) for a in lmat]
    lp_bd = [_head_blockdiag(a) for a in lp]
    for _ in range(int(math.log2(c)) - 1):
        lp32 = [_dot(a, bd) for a, bd in zip(lp, lp_bd)]
        lp = [a.astype(BF16) for a in lp32]
        lp_bd = [_head_blockdiag(a) for a in lp]
        xinv = [xi + a32 + _dot(xi.astype(BF16), bd) for xi, a32, bd in zip(xinv, lp32, lp_bd)]
    egc = [jnp.exp(a) for a in gcc]
    xb = [a.astype(BF16) for a in xinv]
    vbeta = [vv * b for vv, b in zip(v, beta)]
    kbeta = [kk * (b * e) for kk, b, e in zip(k, beta, egc)]
    u_val = [r + _dot(xi, _head_blockdiag(r.astype(BF16))) for xi, r in zip(xb, vbeta)]
    w_dec = [r + _dot(xi, _head_blockdiag(r.astype(BF16))) for xi, r in zip(xb, kbeta)]
    wq = [jnp.concatenate([wd, qq * e], axis=0).astype(BF16) for wd, qq, e in zip(w_dec, q, egc)]
    k_state = [(kk * jnp.exp(gl - a)).astype(BF16) for kk, gl, a in zip(k, g_last, gcc)]
    e_last = [jnp.exp(gl) for gl in g_last]
    keyed = lambda vals: dict(zip(items, vals))
    return dict(wq=keyed(wq), u_val=keyed(u_val), attn=keyed(attn), k_state=keyed(k_state), e_last=keyed(e_last))


def _gdn_kernel(uf_ref, gbf_ref, ub_ref, gbb_ref, tril_ref, triu_ref, of_ref, ob_ref, sf_ref, sb_ref):
    n = pl.program_id(1)
    t = uf_ref.shape[0]
    c, dh = GDN_CHUNK, HEAD_DIM
    nch = t // c
    pairs = MIXER_HEADS // 2

    @pl.when(n == 0)
    def _():
        sf_ref[...] = jnp.zeros_like(sf_ref)
        sb_ref[...] = jnp.zeros_like(sb_ref)

    tril, triu = tril_ref[...], triu_ref[...]
    prep_f = _gdn_prepare(True, uf_ref[...], gbf_ref[...], tril, triu)
    prep_b = _gdn_prepare(False, ub_ref[...], gbb_ref[...], triu, tril)

    first = lax.broadcasted_iota(jnp.int32, (dh, 2 * dh), 1) < dh
    chains = [(prep_f, sf_ref, of_ref, p, True) for p in range(pairs)] + \
             [(prep_b, sb_ref, ob_ref, p, False) for p in range(pairs)]
    state = [s_ref[p] for _, s_ref, _, p, _ in chains]
    for step in range(nch):
        key = [(step if fw else nch - 1 - step, p) for _, _, _, p, fw in chains]
        ws = [_dot(pr['wq'][kk], _head_blockdiag(st.astype(BF16))) for (pr, *_), kk, st in zip(chains, key, state)]
        v_new = [pr['u_val'][kk] - a[:c] for (pr, *_), kk, a in zip(chains, key, ws)]
        vb = [a.astype(BF16) for a in v_new]
        out = [a[c:] + _dot(pr['attn'][kk], _head_blockdiag(b)) for (pr, *_), kk, a, b in zip(chains, key, ws, vb)]
        cross = [_dot_tn(pr['k_state'][kk], b) for (pr, *_), kk, b in zip(chains, key, vb)]
        state = [st * pr['e_last'][kk] + jnp.where(first, x[:dh], x[dh:])
                 for (pr, *_), kk, st, x in zip(chains, key, state, cross)]
        for (_, _, o_ref, p, _), (ci, _), o in zip(chains, key, out):
            o_ref[ci * c:(ci + 1) * c, 2 * p * dh:2 * (p + 1) * dh] = o
    for (_, s_ref, _, p, _), st in zip(chains, state):
        s_ref[p] = st


def _gated_deltanet(p_d, p_ab, conv_w, alog_vec, dtb_vec, b, s):
    u, gb = _gdn_pre(p_d, p_ab, conv_w, alog_vec, dtb_vec, b, s)
    c = GDN_CHUNK
    t = c * GDN_CHUNKS_PER_STEP
    nb = s // t
    w = MIXER_WIDTH
    r = np.arange(t)
    same = (r[:, None] // c) == (r[None, :] // c)
    tril = jnp.asarray(same & (r[:, None] >= r[None, :]), BF16)
    triu = jnp.asarray(same & (r[:, None] <= r[None, :]), BF16)
    fwd = lambda bi, n: (bi * nb + n, 0)
    bwd = lambda bi, n: (bi * nb + nb - 1 - n, 0)
    const = lambda a: pl.BlockSpec(a.shape, lambda bi, n: (0,) * a.ndim)
    return pl.pallas_call(
        _gdn_kernel,
        grid=(b, nb),
        in_specs=[pl.BlockSpec((t, 3 * w), fwd), pl.BlockSpec((t, LANES), fwd),
                  pl.BlockSpec((t, 3 * w), bwd), pl.BlockSpec((t, LANES), bwd), const(tril), const(triu)],
        out_specs=[pl.BlockSpec((t, w), fwd), pl.BlockSpec((t, w), bwd)],
        out_shape=[jax.ShapeDtypeStruct((b * s, w), F32), jax.ShapeDtypeStruct((b * s, w), F32)],
        scratch_shapes=[pltpu.VMEM((MIXER_HEADS // 2, HEAD_DIM, 2 * HEAD_DIM), F32),
                        pltpu.VMEM((MIXER_HEADS // 2, HEAD_DIM, 2 * HEAD_DIM), F32)],
        compiler_params=_cparams("arbitrary", "arbitrary"),
    )(u, gb, u, gb, tril, triu)


def _mem_kv_kernel(m_ref, g_ref, wk_ref, wv_ref, k_ref, v_ref):
    h = _rms(m_ref[0], g_ref[...]).astype(BF16)
    k_ref[0] = _dot(h, wk_ref[...]).astype(BF16)
    v_ref[0] = _dot(h, wv_ref[...]).astype(BF16)


def _mem_kv(mem, g, wk, wv):
    b, nm, d = mem.shape
    const = lambda a: pl.BlockSpec(a.shape, lambda i: (0,) * a.ndim)
    blk = pl.BlockSpec((1, nm, d), lambda i: (i, 0, 0))
    return pl.pallas_call(
        _mem_kv_kernel, grid=(b,),
        in_specs=[blk, const(g), const(wk), const(wv)], out_specs=[blk, blk],
        out_shape=[jax.ShapeDtypeStruct((b, nm, d), BF16)] * 2,
        compiler_params=_cparams("parallel"),
    )(mem, g, wk, wv)


def _route_lanes(lg):
    lane_i = lax.broadcasted_iota(jnp.int32, lg.shape, 1)
    lane = lane_i.astype(F32)
    first_at = lambda vals, m: jnp.min(jnp.where(vals == m, lane, float(LANES)), axis=-1, keepdims=True)
    g_mask = lane_i < N_GROUPS
    gl = jnp.where(g_mask, lg, NEG_BIG)
    g_max = jnp.max(gl, axis=-1, keepdims=True)
    g_sum = jnp.sum(jnp.where(g_mask, jnp.exp(gl - g_max), 0.0), axis=-1, keepdims=True)
    g_p = 1.0 / g_sum
    g_idx = first_at(gl, g_max)
    lo = N_GROUPS + EXPERTS_PER_GROUP * g_idx
    e_mask = (lane >= lo) & (lane < lo + EXPERTS_PER_GROUP)
    el = jnp.where(e_mask, lg, NEG_BIG)
    m1 = jnp.max(el, axis=-1, keepdims=True)
    i1 = first_at(el, m1)
    el2 = jnp.where(lane == i1, NEG_BIG, el)
    m2 = jnp.max(el2, axis=-1, keepdims=True)
    i2 = first_at(el2, m2)
    r = jnp.exp(m2 - m1)
    w1 = g_p / (1.0 + r)
    w2 = g_p * r / (1.0 + r)
    e1, e2 = i1 - N_GROUPS, i2 - N_GROUPS
    e_lo, e_hi = jnp.minimum(e1, e2), jnp.maximum(e1, e2)
    w_lo, w_hi = jnp.where(e1 < e2, w1, w2), jnp.where(e1 < e2, w2, w1)
    a, b = e_lo - EXPERTS_PER_GROUP * g_idx, e_hi - EXPERTS_PER_GROUP * g_idx
    cls = g_idx * PAIRS_PER_GROUP + a * EXPERTS_PER_GROUP - a * (a + 1.0) * 0.5 + (b - a - 1.0)
    out = jnp.where(lane_i == 0, e_lo, 0.0)
    out = jnp.where(lane_i == 1, e_hi, out)
    out = jnp.where(lane_i == 2, w_lo, out)
    out = jnp.where(lane_i == 3, w_hi, out)
    out = jnp.where(lane_i == 4, cls, out)
    return out, jnp.where(lane == cls, 1.0, 0.0)


def _post_kernel(x_ref, ya_ref, yb_ref, yc_ref, of_ref, ob_ref, gate_ref, seg_ref, gg_ref, wout_ref, gq_ref,
                 wq_ref, k_ref, v_ref, wo_ref, gf_ref, wr_ref, br_ref, xo_ref, hx_ref, rt_ref, hist_ref):
    w = MIXER_WIDTH
    o = of_ref[...] + ob_ref[...]
    ms = sum(_dot(p, seg_ref[...]) for p in _split_bf16(o * o, 2))
    yd = o * lax.rsqrt(ms + EPS) * gg_ref[...] * _silu(gate_ref[...])
    mix = _dot(ya_ref[...], wout_ref[0:w, :]) + _dot(yb_ref[...], wout_ref[w:2 * w, :])
    mix = mix + _dot(yc_ref[...], wout_ref[2 * w:3 * w, :]) + _dot(yd.astype(BF16), wout_ref[3 * w:4 * w, :])
    x1 = x_ref[...] + mix
    q = _dot(_rms(x1, gq_ref[...]).astype(BF16), wq_ref[...]).astype(BF16)
    cols = [slice(h * MEM_HEAD_DIM, (h + 1) * MEM_HEAD_DIM) for h in range(MEM_HEADS)]
    sc = [_dot_nt(q[:, sl], k_ref[0, :, sl]) * (MEM_HEAD_DIM ** -0.5) for sl in cols]
    p = [jnp.exp(a - jnp.max(a, axis=-1, keepdims=True)) for a in sc]
    l = [jnp.sum(a, axis=-1, keepdims=True) for a in p]
    heads = [(_dot(a.astype(BF16), v_ref[0, :, sl]) / b).astype(BF16) for a, b, sl in zip(p, l, cols)]
    x2 = x1 + _dot(jnp.concatenate(heads, axis=-1), wo_ref[...])
    xo_ref[...] = x2
    hx = _rms(x2, gf_ref[...])
    _to_row_tiles(hx_ref, hx)
    h_hi, h_lo = _split_bf16(hx, 2)
    lg = _dot(h_hi, wr_ref[0]) + _dot(h_hi, wr_ref[1]) + _dot(h_lo, wr_ref[0]) + br_ref[...]
    route, chosen = _route_lanes(lg)
    rt_ref[...] = route
    hist_ref[...] = jnp.broadcast_to(jnp.sum(chosen, axis=0, keepdims=True), hist_ref.shape)


def _post(x, y_a, y_b, y_c, o_f, o_b, p_d, seg, gdn_g, w_out, g_q, w_q, k_mem, v_mem, w_o, g_f, w_r, b_r, s):
    n = x.shape[0]
    tm = POST_TILE
    w = MIXER_WIDTH
    per_seq = s // tm
    row = lambda c: pl.BlockSpec((tm, c), lambda i: (i, 0))
    const = lambda a: pl.BlockSpec(a.shape, lambda i: (0,) * a.ndim)
    kv = pl.BlockSpec((1,) + k_mem.shape[1:], lambda i: (i // per_seq, 0, 0))
    return pl.pallas_call(
        _post_kernel, grid=(n // tm,),
        in_specs=[row(D_MODEL), row(w), row(w), row(w), row(w), row(w),
                  pl.BlockSpec((tm, w), lambda i: (i, 3)), const(seg), const(gdn_g), const(w_out), const(g_q),
                  const(w_q), kv, kv, const(w_o), const(g_f), const(w_r), const(b_r)],
        out_specs=[row(D_MODEL), pl.BlockSpec((tm * ROW_TILES, LANES), lambda i: (i, 0)), row(LANES),
                   pl.BlockSpec((SUBLANES, LANES), lambda i: (i, 0))],
        out_shape=[jax.ShapeDtypeStruct((n, D_MODEL), F32), jax.ShapeDtypeStruct((n * ROW_TILES, LANES), F32),
                   jax.ShapeDtypeStruct((n, LANES), F32), jax.ShapeDtypeStruct((n // tm * SUBLANES, LANES), F32)],
        compiler_params=_cparams("parallel"),
    )(x, y_a, y_b, y_c, o_f, o_b, p_d, seg, gdn_g, w_out, g_q, w_q, k_mem, v_mem, w_o, g_f, w_r, b_r)


def _pair_class_experts():
    lo, hi = [], []
    for g in range(N_GROUPS):
        for a in range(EXPERTS_PER_GROUP):
            for b in range(a + 1, EXPERTS_PER_GROUP):
                lo.append(g * EXPERTS_PER_GROUP + a)
                hi.append(g * EXPERTS_PER_GROUP + b)
    return np.asarray(lo, np.int32), np.asarray(hi, np.int32)


def _dispatch(route, hist):
    n = route.shape[0]
    blk = MOE_BLOCK
    counts = jnp.sum(hist.reshape(-1, SUBLANES, LANES)[:, 0, :N_CLASSES], axis=0).astype(jnp.int32)
    padded = (counts + blk - 1) // blk * blk
    pad_end = jnp.cumsum(padded)
    n_blocks = n // blk + N_CLASSES
    blk_start = jnp.arange(n_blocks, dtype=jnp.int32) * blk
    blk_cls = jnp.minimum(jnp.sum(pad_end[None, :] <= blk_start[:, None], axis=1), N_CLASSES - 1).astype(jnp.int32)
    cls_lo, cls_hi = _pair_class_experts()
    in_blk = jnp.arange(blk, dtype=jnp.int32)[None, :]
    pad_key = jnp.where(in_blk < (padded - counts)[:, None], jnp.arange(N_CLASSES, dtype=jnp.int32)[:, None], N_CLASSES)
    keys = jnp.concatenate([route[:, 4].astype(jnp.int32), pad_key.reshape(-1)])
    toks = jnp.concatenate([jnp.arange(n, dtype=jnp.int32), jnp.tile(n + in_blk[0], N_CLASSES)])
    zeros = jnp.zeros((N_CLASSES * blk,), F32)
    w_lo = jnp.concatenate([route[:, TOP_K], zeros])
    w_hi = jnp.concatenate([route[:, TOP_K + 1], zeros])
    _, row_tok, w_lo, w_hi = lax.sort((keys, toks, w_lo, w_hi), num_keys=1)
    n_used = (pad_end[-1] // blk).astype(jnp.int32).reshape(1)
    return (n_used, jnp.asarray(cls_lo)[blk_cls], jnp.asarray(cls_hi)[blk_cls],
            jnp.concatenate([n + in_blk[0], row_tok]), jnp.stack([w_lo, w_hi], axis=-1))


def _moe_kernel(n_tokens, n_blocks, nu_ref, ea_ref, eb_ref, dst_ref, hx_hbm, rw_ref, wga_ref, wua_ref, wda_ref,
                wgb_ref, wub_ref, wdb_ref, y_hbm, xbuf, ybuf, gsem, ssem):
    j = pl.program_id(0)
    blk, rt = MOE_BLOCK, ROW_TILES
    n_used = nu_ref[0]
    slot = lax.rem(j, MOE_BUFFERS)
    ahead = lax.rem(j + 2, MOE_BUFFERS)
    tile = lambda ref, i: ref.at[pl.ds(pl.multiple_of(i * rt, rt), rt), :]
    dst = lambda jj, i: dst_ref[(jj + 1) * blk + i]
    clamp = lambda jj: jnp.minimum(jj, n_blocks - 1)

    def gather_row(jj, sl, i):
        tok = jnp.minimum(dst(jj, i), n_tokens - 1)
        pltpu.make_async_copy(tile(hx_hbm, tok), tile(xbuf.at[sl], i), gsem.at[sl]).start()

    def scatter_row(jj, sl, i):
        pltpu.make_async_copy(tile(ybuf.at[sl], i), tile(y_hbm, dst(jj, i)), ssem.at[sl]).start()

    def wait_gather(sl):
        pltpu.make_async_copy(hx_hbm.at[pl.ds(0, blk * rt), :], xbuf.at[sl], gsem.at[sl]).wait()

    def wait_scatter(sl):
        pltpu.make_async_copy(ybuf.at[sl], y_hbm.at[pl.ds(0, blk * rt), :], ssem.at[sl]).wait()

    def rolled(fn):
        lax.fori_loop(0, blk, lambda i, carry: (fn(i), carry)[1], 0, unroll=DMA_UNROLL)

    @pl.when(j == 0)
    def _():
        ybuf[MOE_BUFFERS - 1] = jnp.zeros((blk * rt, LANES), F32)
        rolled(lambda i: gather_row(0, 0, i))
        rolled(lambda i: gather_row(clamp(1), 1, i))

    @pl.when(j < n_used)
    def _():
        wait_gather(slot)

        @pl.when(j >= 2)
        def _():
            wait_scatter(slot)

        def issue_copies(piece):
            for i in range(piece * blk // MOE_PIECES, (piece + 1) * blk // MOE_PIECES):
                gather_row(clamp(j + 2), ahead, i)
                scatter_row(j - 1, ahead, i)

        xs = _from_row_tiles(xbuf.at[slot], 0, blk, rt).astype(BF16)
        half = D_EXPERT // 2
        hidden = []
        for e, (wg_ref, wu_ref) in enumerate(((wga_ref, wua_ref), (wgb_ref, wub_ref))):
            acts = []
            for c in range(2):
                cols = slice(c * half, (c + 1) * half)
                g = _dot(xs, wg_ref[0, 0, :, cols])
                issue_copies(4 * e + 2 * c)
                u = _dot(xs, wu_ref[0, 0, :, cols])
                issue_copies(4 * e + 2 * c + 1)
                acts.append((_silu(g) * u).astype(BF16))
            hidden.append(jnp.concatenate(acts, axis=1))
        w_a, w_b = rw_ref[:, 0:1], rw_ref[:, 1:2]
        quarter = D_MODEL // 4
        for c in range(4):
            cols = slice(c * quarter, (c + 1) * quarter)
            y = w_a * _dot(hidden[0], wda_ref[0, 0, :, cols]) + w_b * _dot(hidden[1], wdb_ref[0, 0, :, cols])
            for k in range(quarter // LANES):
                ybuf[slot, pl.ds(c * (quarter // LANES) + k, blk, stride=rt), :] = y[:, k * LANES:(k + 1) * LANES]
            issue_copies(8 + c)

    @pl.when(j == n_used)
    def _():
        after = lax.rem(j + 1, MOE_BUFFERS)
        wait_gather(slot)
        wait_gather(after)

        @pl.when(j >= 2)
        def _():
            wait_scatter(slot)

        wait_scatter(after)
        rolled(lambda i: scatter_row(j - 1, ahead, i))
        wait_scatter(ahead)


def _moe(hx_tiles, n_used, blk_ea, blk_eb, row_tok, row_w, w_gate, w_up, w_down, layer, n):
    blk = MOE_BLOCK
    n_blocks = blk_ea.shape[0]
    block = lambda j: jnp.minimum(j, n_blocks - 1)
    first = lambda j, nu, ea, eb, dst: (layer, ea[block(j)], 0, 0)
    second = lambda j, nu, ea, eb, dst: (layer, eb[block(j)], 0, 0)
    up, down = (1, 1, D_MODEL, D_EXPERT), (1, 1, D_EXPERT, D_MODEL)
    weights = pl.BlockSpec
    grid_spec = pltpu.PrefetchScalarGridSpec(
        num_scalar_prefetch=4, grid=(n_blocks + 1,),
        in_specs=[pl.BlockSpec(memory_space=pl.ANY),
                  pl.BlockSpec((blk, TOP_K), lambda j, nu, ea, eb, dst: (block(j), 0)),
                  weights(up, first), weights(up, first), weights(down, first),
                  weights(up, second), weights(up, second), weights(down, second)],
        out_specs=pl.BlockSpec(memory_space=pl.ANY),
        scratch_shapes=[pltpu.VMEM((MOE_BUFFERS, blk * ROW_TILES, LANES), F32),
                        pltpu.VMEM((MOE_BUFFERS, blk * ROW_TILES, LANES), F32),
                        pltpu.SemaphoreType.DMA((MOE_BUFFERS,)), pltpu.SemaphoreType.DMA((MOE_BUFFERS,))])
    return pl.pallas_call(
        functools.partial(_moe_kernel, n, n_blocks), grid_spec=grid_spec,
        out_shape=jax.ShapeDtypeStruct(((n + blk) * ROW_TILES, LANES), F32),
        compiler_params=_cparams("arbitrary"),
    )(n_used, blk_ea, blk_eb, row_tok, hx_tiles, row_w, w_gate, w_up, w_down, w_gate, w_up, w_down)


def _final_kernel(x_ref, y_ref, g_ref, o_ref):
    o_ref[...] = _rms(_add_moe(x_ref[...], y_ref), g_ref[...])


def _final(x, moe, g):
    n = x.shape[0]
    tm = PROJ_TILE
    return pl.pallas_call(
        _final_kernel, grid=(n // tm,),
        in_specs=[pl.BlockSpec((tm, D_MODEL), lambda i: (i, 0)), _moe_out_spec(tm),
                  pl.BlockSpec(g.shape, lambda i: (0, 0))],
        out_specs=pl.BlockSpec((tm, D_MODEL), lambda i: (i, 0)),
        out_shape=jax.ShapeDtypeStruct((n, D_MODEL), F32),
        compiler_params=_cparams("parallel"),
    )(x, moe, g)


def _lane_vec(v, width=LANES):
    v = v.reshape(1, -1).astype(F32)
    return jnp.pad(v, ((0, 0), (0, width - v.shape[1])))


def _block_diag(blocks):
    g, c, _ = blocks.shape
    eye = jnp.eye(g, dtype=blocks.dtype)
    return (eye[:, None, :, None] * blocks[:, :, None, :]).reshape(g * c, g * c)


def kernel(x, mem, mix_norm_g, w_in, na_rpb, ret_norm_g, pool_w, pool_scale, gdn_conv_w, gdn_a_log, gdn_dt_bias, gdn_norm_g, w_out, mem_q_norm_g, mem_kv_norm_g, mem_w_q, mem_w_k, mem_w_v, mem_w_o, ffn_norm_g, w_group, b_group, w_router, b_router, w_gate, w_up, w_down, final_norm_g):
    b, s, d = x.shape
    n = b * s
    depth = w_in.shape[0]
    row = lambda v: v.reshape(1, -1).astype(F32)
    ret_tables = _retention_tables(s)
    seg = _block_diag(jnp.full((MIXER_HEADS, HEAD_DIM, HEAD_DIM), 1.0 / HEAD_DIM, BF16))
    wg_bf, wu_bf, wd_bf = w_gate.astype(BF16), w_up.astype(BF16), w_down.astype(BF16)
    xs = x.reshape(n, d)
    moe = None
    for l in range(depth):
        w_pad = jnp.pad(w_in[l], ((0, 0), (0, P_IN_PAD - w_in.shape[2]))).astype(BF16)
        xs, p_a, p_b, p_c, p_d, p_ab = _norm_proj(xs, moe, row(mix_norm_g[l]), w_pad)
        y_a = _neighbourhood_attention(p_a, _na_bias_table(na_rpb[l]), b, s)
        y_b = _retention(p_b, row(ret_norm_g[l]), ret_tables, b, s)
        y_c = _pool(p_c, _block_diag(pool_w[l]).astype(BF16), row(pool_scale[l]), b, s)
        o_f, o_b = _gated_deltanet(p_d, p_ab, gdn_conv_w[l].astype(F32), _lane_vec(gdn_a_log[l]),
                                   _lane_vec(gdn_dt_bias[l]), b, s)
        k_mem, v_mem = _mem_kv(mem, row(mem_kv_norm_g[l]), mem_w_k[l].astype(BF16), mem_w_v[l].astype(BF16))
        w_r = jnp.pad(jnp.concatenate([w_group[l], w_router[l]], axis=1),
                      ((0, 0), (0, LANES - N_GROUPS - N_EXPERTS))).astype(F32)
        b_r = _lane_vec(jnp.concatenate([b_group[l], b_router[l]]))
        xs, hx_tiles, route, hist = _post(
            xs, y_a, y_b, y_c, o_f, o_b, p_d, seg, row(jnp.tile(gdn_norm_g[l], MIXER_HEADS)), w_out[l].astype(BF16),
            row(mem_q_norm_g[l]), mem_w_q[l].astype(BF16), k_mem, v_mem, mem_w_o[l].astype(BF16),
            row(ffn_norm_g[l]), jnp.stack(_split_bf16(w_r, 2)), b_r, s)
        moe = _moe(hx_tiles, *_dispatch(route, hist), wg_bf, wu_bf, wd_bf, l, n)
    return _final(xs, moe, row(final_norm_g)).reshape(b, s, d)
```

```python
import functools
import math

import jax
import jax.numpy as jnp
import numpy as np
from jax import lax
from jax.experimental import pallas as pl
from jax.experimental.pallas import tpu as pltpu

F32 = jnp.float32
BF16 = jnp.bfloat16
HIGHEST = lax.Precision.HIGHEST

D_MODEL = 1024
GRID_W = 64
HEAD_DIM = 64
MIXER_WIDTH = 256
MIXER_HEADS = 4
NA_KH = 8
NA_KW = 16
RET_CHUNK = 128
ROPE_BASE = 10000.0
POOL_WINDOWS = (2, 4, 8, 16)
POOL_GROUP = 64
GDN_CHUNK = 64
CONV_K = 4
MEM_HEADS = 4
MEM_HEAD_DIM = 256
N_GROUPS = 4
EXPERTS_PER_GROUP = 8
N_EXPERTS = 32
TOP_K = 2
D_EXPERT = 512
EPS = 1e-6

P_IN_PAD = 3200
V7X_VMEM_LIMIT = 56 * 1024 * 1024
LANES = 128
SUBLANES = 8
ROW_TILES = D_MODEL // LANES
NEG_BIG = -1e30

PROJ_TILE = 512
POOL_TILE = 512
POST_TILE = 512
NA_ROWS_PER_STEP = 8
NA_ROWS_LOCKSTEP = 4
GDN_CHUNKS_PER_STEP = 8
GDN_PRE_TILE = 512
RET_CHUNKS_PER_STEP = 8
PAIRS_PER_GROUP = EXPERTS_PER_GROUP * (EXPERTS_PER_GROUP - 1) // 2
N_CLASSES = N_GROUPS * PAIRS_PER_GROUP
MOE_BLOCK = 128
MOE_BUFFERS = 3
MOE_PIECES = 12
DMA_UNROLL = 8
HALO = 8


def _cparams(*sem):
    return pltpu.CompilerParams(dimension_semantics=sem, vmem_limit_bytes=V7X_VMEM_LIMIT)


def _dot(a, b):
    return jnp.dot(a, b, preferred_element_type=F32)


def _dot_nt(a, b):
    return lax.dot_general(a, b, (((1,), (1,)), ((), ())), preferred_element_type=F32)


def _dot_tn(a, b, precision=None):
    return lax.dot_general(a, b, (((0,), (0,)), ((), ())), preferred_element_type=F32, precision=precision)


def _silu(x):
    return x * (1.0 / (1.0 + jnp.exp(-x)))


def _rms(x, g):
    return x * lax.rsqrt(jnp.mean(x * x, axis=-1, keepdims=True) + EPS) * g


def _from_row_tiles(ref, first, count, stride):
    return jnp.concatenate([ref[pl.ds(first + k, count, stride=stride), :] for k in range(ROW_TILES)], axis=1)


def _to_row_tiles(ref, x):
    count = x.shape[0]
    for k in range(ROW_TILES):
        ref[pl.ds(k, count, stride=ROW_TILES), :] = x[:, k * LANES:(k + 1) * LANES]


def _add_moe(x, y_ref):
    return x + _from_row_tiles(y_ref, 0, x.shape[0], ROW_TILES)


def _split_bf16(x, parts):
    out = []
    for _ in range(parts):
        p = x.astype(BF16)
        out.append(p)
        x = x - p.astype(F32)
    return out


def _norm_proj_kernel(has_y, *refs):
    if has_y:
        x_ref, y_ref, g_ref, w_ref, xo_ref, pa_ref, pb_ref, pc_ref, pd_ref, pab_ref = refs
        x = _add_moe(x_ref[...], y_ref)
        xo_ref[...] = x
    else:
        x_ref, g_ref, w_ref, pa_ref, pb_ref, pc_ref, pd_ref, pab_ref = refs
        x = x_ref[...]
    h = _rms(x, g_ref[...]).astype(BF16)
    w = MIXER_WIDTH
    pa_ref[...] = _dot(h, w_ref[:, 0:3 * w]).astype(BF16)
    pb_ref[...] = _dot(h, w_ref[:, 3 * w:7 * w])
    pc_ref[...] = _dot(h, w_ref[:, 7 * w:8 * w])
    pd_ref[...] = _dot(h, w_ref[:, 8 * w:12 * w])
    pab_ref[...] = _dot(h, w_ref[:, 12 * w:P_IN_PAD])


def _moe_out_spec(tile):
    return pl.BlockSpec((tile * ROW_TILES, LANES), lambda i: (i, 0))


def _norm_proj(x, moe, g, w_pad):
    n = x.shape[0]
    tm = PROJ_TILE
    w = MIXER_WIDTH
    row = lambda c: pl.BlockSpec((tm, c), lambda i: (i, 0))
    const = lambda a: pl.BlockSpec(a.shape, lambda i: (0,) * a.ndim)
    outs = [jax.ShapeDtypeStruct((n, 3 * w), BF16), jax.ShapeDtypeStruct((n, 4 * w), F32),
            jax.ShapeDtypeStruct((n, w), F32), jax.ShapeDtypeStruct((n, 4 * w), F32),
            jax.ShapeDtypeStruct((n, LANES), F32)]
    out_specs = [row(3 * w), row(4 * w), row(w), row(4 * w), row(LANES)]
    if moe is None:
        ins, in_specs = (x, g, w_pad), [row(D_MODEL), const(g), const(w_pad)]
    else:
        ins = (x, moe, g, w_pad)
        in_specs = [row(D_MODEL), _moe_out_spec(tm), const(g), const(w_pad)]
        outs = [jax.ShapeDtypeStruct((n, D_MODEL), F32)] + outs
        out_specs = [row(D_MODEL)] + out_specs
    res = pl.pallas_call(
        functools.partial(_norm_proj_kernel, moe is not None),
        grid=(n // tm,), in_specs=in_specs, out_specs=out_specs, out_shape=outs,
        compiler_params=_cparams("parallel"),
    )(*ins)
    return res if moe is not None else [x] + list(res)


def _na_bias_table(rpb):
    qc = np.arange(GRID_W)
    kc = np.arange(GRID_W)
    d_col = np.clip(kc[None, :] - qc[:, None], -(NA_KW - 1), NA_KW - 1) + (NA_KW - 1)
    pick = np.zeros((2 * NA_KW - 1, GRID_W * GRID_W), np.float32)
    pick[d_col.reshape(-1), np.arange(GRID_W * GRID_W)] = 1.0
    win = np.clip(qc - NA_KW // 2, 0, GRID_W - NA_KW)
    ok = (kc[None, :] >= win[:, None]) & (kc[None, :] < win[:, None] + NA_KW)
    cols = jnp.einsum('hrc,cx->hrx', rpb.astype(F32), jnp.asarray(pick), precision=HIGHEST)
    cols = jnp.where(ok[None, None], cols.reshape(MIXER_HEADS, 2 * NA_KH - 1, GRID_W, GRID_W), NEG_BIG)
    per_e = [jnp.transpose(cols[:, NA_KH - 1 - e:2 * NA_KH - 1 - e], (0, 2, 1, 3)) for e in range(NA_KH)]
    return jnp.stack(per_e, axis=1).reshape(MIXER_HEADS, NA_KH, GRID_W, NA_KH * GRID_W)


def _na_kernel(rows, q_ref, k_ref, v_ref, t_ref, o_ref):
    i = pl.program_id(1)
    dh = HEAD_DIM

    def rows_body(it, carry):
        chains = []
        for k in range(NA_ROWS_LOCKSTEP):
            rr = it * NA_ROWS_LOCKSTEP + k
            r = i * NA_ROWS_PER_STEP + rr
            kr0 = jnp.clip(r - NA_KH // 2, 0, rows - NA_KH)
            q = q_ref[pl.ds(pl.multiple_of(rr * GRID_W, GRID_W), GRID_W), :]
            k0 = pl.multiple_of(kr0 * GRID_W, GRID_W)
            kb = k_ref[pl.ds(k0, NA_KH * GRID_W), :]
            vb = v_ref[pl.ds(k0, NA_KH * GRID_W), :]
            for h in range(MIXER_HEADS):
                sl = slice(h * dh, (h + 1) * dh)
                chains.append((q[:, sl], kb[:, sl], vb[:, sl], h, r - kr0))
        s = [_dot_nt(q, kb) * (dh ** -0.5) + t_ref[h, e] for q, kb, _, h, e in chains]
        m = [jnp.max(a, axis=-1, keepdims=True) for a in s]
        p = [jnp.exp(a - b) for a, b in zip(s, m)]
        l = [jnp.sum(a, axis=-1, keepdims=True) for a in p]
        o = [_dot(a.astype(BF16), c[2]) / b for a, b, c in zip(p, l, chains)]
        for k in range(NA_ROWS_LOCKSTEP):
            rr = it * NA_ROWS_LOCKSTEP + k
            row = jnp.concatenate(o[k * MIXER_HEADS:(k + 1) * MIXER_HEADS], axis=-1)
            o_ref[pl.ds(pl.multiple_of(rr * GRID_W, GRID_W), GRID_W), :] = row.astype(BF16)
        return carry

    lax.fori_loop(0, NA_ROWS_PER_STEP // NA_ROWS_LOCKSTEP, rows_body, 0)


def _neighbourhood_attention(p_a, table, b, s):
    rows = s // GRID_W
    assert rows >= NA_KH and rows % NA_ROWS_PER_STEP == 0
    steps = rows // NA_ROWS_PER_STEP
    tq = NA_ROWS_PER_STEP * GRID_W
    w = MIXER_WIDTH
    return pl.pallas_call(
        functools.partial(_na_kernel, rows),
        grid=(b, steps),
        in_specs=[pl.BlockSpec((tq, w), lambda bi, i: (bi * steps + i, 0)),
                  pl.BlockSpec((s, w), lambda bi, i: (bi, 1)),
                  pl.BlockSpec((s, w), lambda bi, i: (bi, 2)),
                  pl.BlockSpec(table.shape, lambda bi, i: (0, 0, 0, 0))],
        out_specs=pl.BlockSpec((tq, w), lambda bi, i: (bi * steps + i, 0)),
        out_shape=jax.ShapeDtypeStruct((b * s, w), BF16),
        compiler_params=_cparams("parallel", "arbitrary"),
    )(p_a, p_a, p_a, table)


def _retention_tables(s):
    h, dh, c = MIXER_HEADS, HEAD_DIM, RET_CHUNK
    half = dh // 2
    inv = ROPE_BASE ** (-jnp.arange(half, dtype=F32) / half)
    ang = jnp.arange(s, dtype=F32)[:, None] * inv[None, :]
    cos, sin = jnp.cos(ang), jnp.sin(ang)
    zero = jnp.zeros_like(sin)
    cos_t = jnp.tile(jnp.concatenate([cos, cos], axis=-1), (1, h))
    sin_lo = jnp.tile(jnp.concatenate([-sin, zero], axis=-1), (1, h))
    sin_hi = jnp.tile(jnp.concatenate([zero, sin], axis=-1), (1, h))
    log_f = np.log1p(-np.exp2(-5.0 - np.arange(h, dtype=np.float64)))
    log_b = log_f[::-1]
    pos = np.arange(c, dtype=np.float64)
    diff = pos[:, None] - pos[None, :]
    dmat = np.where(diff >= 0, np.exp(log_f[:, None, None] * np.maximum(diff, 0.0)), 0.0) \
        + np.where(diff < 0, np.exp(log_b[:, None, None] * np.maximum(-diff, 0.0)), 0.0)
    lanes = lambda t: np.repeat(t.T, dh, axis=1)
    dec = np.stack([lanes(np.exp(log_f[:, None] * (pos + 1.0))),
                    lanes(np.exp(log_f[:, None] * (c - 1.0 - pos))),
                    lanes(np.exp(log_b[:, None] * (c - pos))),
                    lanes(np.exp(log_b[:, None] * pos))])
    chunk_f = [float(np.exp(v * c)) for v in log_f]
    chunk_b = [float(np.exp(v * c)) for v in log_b]
    return cos_t, sin_lo, sin_hi, jnp.asarray(dmat, F32), jnp.asarray(dec, F32), chunk_f, chunk_b


def _rotary(x, cos, sin_lo, sin_hi):
    w = x.shape[-1]
    return x * cos + pltpu.roll(x, w - HEAD_DIM // 2, 1) * sin_lo + pltpu.roll(x, HEAD_DIM // 2, 1) * sin_hi


def _retention_kernel(nb, chunk_f, chunk_b, q_ref, k_ref, v_ref, gate_ref, cos_ref, slo_ref, shi_ref,
                      dmat_ref, dec_ref, ng_ref, o_ref, sf_ref, sb_ref, sball_ref):
    t = pl.program_id(1)
    dh, c, cps = HEAD_DIM, RET_CHUNK, RET_CHUNKS_PER_STEP
    cos, slo, shi = cos_ref[...], slo_ref[...], shi_ref[...]
    kr = _rotary(k_ref[...], cos, slo, shi) * (dh ** -0.5)
    vb = v_ref[...].astype(BF16)
    items = [(ci, h) for ci in range(cps) for h in range(MIXER_HEADS)]
    part = lambda x, ci, h: x[ci * c:(ci + 1) * c, h * dh:(h + 1) * dh]
    split = lambda x: {it: part(x, *it) for it in items}

    @pl.when(t == 0)
    def _():
        sf_ref[...] = jnp.zeros_like(sf_ref)
        sb_ref[...] = jnp.zeros_like(sb_ref)

    @pl.when(t < nb)
    def _():
        blk = nb - 1 - t
        kd, v = split((kr * dec_ref[3]).astype(BF16)), split(vb)
        kv = {it: _dot_tn(kd[it], v[it]) for it in items}
        for h in range(MIXER_HEADS):
            state = sb_ref[h]
            for ci in range(cps - 1, -1, -1):
                sball_ref[blk * cps + ci, h] = state
                state = state * chunk_b[h] + kv[ci, h]
            sb_ref[h] = state

    @pl.when(t >= nb)
    def _():
        blk = t - nb
        qr = _rotary(q_ref[...], cos, slo, shi)
        q, k, v = split(qr.astype(BF16)), split(kr.astype(BF16)), split(vb)
        qf, qbk = split((qr * dec_ref[0]).astype(BF16)), split((qr * dec_ref[2]).astype(BF16))
        kd = split((kr * dec_ref[1]).astype(BF16))
        sc = {it: (_dot_nt(q[it], k[it]) * dmat_ref[it[1]]).astype(BF16) for it in items}
        o = {it: _dot(sc[it], v[it]) for it in items}
        kv = {it: _dot_tn(kd[it], v[it]) for it in items}
        ob = {it: _dot(qbk[it], sball_ref[blk * cps + it[0], it[1]].astype(BF16)) for it in items}
        of = {}
        for h in range(MIXER_HEADS):
            state = sf_ref[h]
            for ci in range(cps):
                of[ci, h] = _dot(qf[ci, h], state.astype(BF16))
                state = state * chunk_f[h] + kv[ci, h]
            sf_ref[h] = state
        o = {it: o[it] + of[it] + ob[it] for it in items}
        mu = {it: jnp.mean(o[it], axis=-1, keepdims=True) for it in items}
        oc = {it: o[it] - mu[it] for it in items}
        var = {it: jnp.mean(oc[it] * oc[it], axis=-1, keepdims=True) for it in items}
        on = {it: oc[it] * lax.rsqrt(var[it] + EPS) for it in items}
        y = jnp.concatenate([jnp.concatenate([on[ci, h] for h in range(MIXER_HEADS)], axis=-1)
                             for ci in range(cps)], axis=0)
        o_ref[...] = (y * ng_ref[...] * _silu(gate_ref[...])).astype(BF16)


def _retention(p_b, norm_g, tables, b, s):
    cos_t, sin_lo, sin_hi, dmat, dec, chunk_f, chunk_b = tables
    cps = RET_CHUNKS_PER_STEP
    tb = RET_CHUNK * cps
    nb = s // tb
    w = MIXER_WIDTH
    dec = jnp.tile(dec, (1, cps, 1))
    block = lambda t: jnp.where(t < nb, nb - 1 - t, t - nb)
    col = lambda j: pl.BlockSpec((tb, w), lambda bi, t: (bi * nb + block(t), j))
    tab = pl.BlockSpec((tb, w), lambda bi, t: (block(t), 0))
    const = lambda a: pl.BlockSpec(a.shape, lambda bi, t: (0,) * a.ndim)
    return pl.pallas_call(
        functools.partial(_retention_kernel, nb, chunk_f, chunk_b),
        grid=(b, 2 * nb),
        in_specs=[col(0), col(1), col(2), col(3), tab, tab, tab, const(dmat), const(dec), const(norm_g)],
        out_specs=pl.BlockSpec((tb, w), lambda bi, t: (bi * nb + jnp.maximum(t - nb, 0), 0)),
        out_shape=jax.ShapeDtypeStruct((b * s, w), BF16),
        scratch_shapes=[pltpu.VMEM((MIXER_HEADS, HEAD_DIM, HEAD_DIM), F32),
                        pltpu.VMEM((MIXER_HEADS, HEAD_DIM, HEAD_DIM), F32),
                        pltpu.VMEM((s // RET_CHUNK, MIXER_HEADS, HEAD_DIM, HEAD_DIM), F32)],
        compiler_params=_cparams("arbitrary", "arbitrary"),
    )(p_b, p_b, p_b, p_b, cos_t, sin_lo, sin_hi, dmat, dec, norm_g)


def _halo_specs(tile, width, n_rows, block_of):
    per = tile // HALO
    last = n_rows // HALO - 1
    main = pl.BlockSpec((tile, width), lambda *g: (block_of(*g), 0))
    prev = pl.BlockSpec((HALO, width), lambda *g: (jnp.maximum(block_of(*g) * per - 1, 0), 0))
    nxt = pl.BlockSpec((HALO, width), lambda *g: (jnp.minimum((block_of(*g) + 1) * per, last), 0))
    return main, prev, nxt


def _fill_padded(pad_ref, main, prev, nxt, first, last):
    t = main.shape[0]
    pad_ref[0:HALO, :] = jnp.where(first, 0.0, prev)
    pad_ref[HALO:HALO + t, :] = main
    pad_ref[HALO + t:2 * HALO + t, :] = jnp.where(last, 0.0, nxt)


def _pool_kernel(s, tiles_per_seq, u_ref, prev_ref, next_ref, w_ref, scale_ref, o_ref, pad_ref):
    t = POOL_TILE
    tseq = pl.program_id(0) % tiles_per_seq
    u = u_ref[...]
    _fill_padded(pad_ref, u, prev_ref[...], next_ref[...], tseq == 0, tseq == tiles_per_seq - 1)
    lane = lax.broadcasted_iota(jnp.int32, (1, MIXER_WIDTH), 1)
    half = jnp.full((1, MIXER_WIDTH), POOL_WINDOWS[0] // 2, jnp.int32)
    for gi in range(1, len(POOL_WINDOWS)):
        half = jnp.where(lane >= gi * POOL_GROUP, POOL_WINDOWS[gi] // 2, half)
    max_half = POOL_WINDOWS[-1] // 2
    acc = jnp.zeros((t, MIXER_WIDTH), F32)
    for d in range(-max_half, max_half):
        inside = (d >= -half) & (d < half)
        acc = acc + jnp.where(inside, pad_ref[HALO + d:HALO + d + t, :], 0.0)
    pos = tseq * t + lax.broadcasted_iota(jnp.int32, (t, 1), 0)
    count = jnp.minimum(pos + half, s) - jnp.maximum(pos - half, 0)
    diff = acc / count.astype(F32) - u
    o_ref[...] = (_dot(diff.astype(BF16), w_ref[...]) * scale_ref[...]).astype(BF16)


def _pool(p_c, w_blockdiag, scale, b, s):
    t = POOL_TILE
    tiles_per_seq = s // t
    main, prev, nxt = _halo_specs(t, MIXER_WIDTH, b * s, lambda i: i)
    const = lambda a: pl.BlockSpec(a.shape, lambda i: (0,) * a.ndim)
    return pl.pallas_call(
        functools.partial(_pool_kernel, s, tiles_per_seq),
        grid=(b * tiles_per_seq,),
        in_specs=[main, prev, nxt, const(w_blockdiag), const(scale)],
        out_specs=pl.BlockSpec((t, MIXER_WIDTH), lambda i: (i, 0)),
        out_shape=jax.ShapeDtypeStruct((b * s, MIXER_WIDTH), BF16),
        scratch_shapes=[pltpu.VMEM((t + 2 * HALO, MIXER_WIDTH), F32)],
        compiler_params=_cparams("parallel"),
    )(p_c, p_c, p_c, w_blockdiag, scale)


def _head_blockdiag(x):
    lane = lax.broadcasted_iota(jnp.int32, x.shape, 1)
    zero = jnp.zeros_like(x)
    return jnp.concatenate([jnp.where(lane < HEAD_DIM, x, zero), jnp.where(lane >= HEAD_DIM, x, zero)], axis=0)


def _gdn_pre_kernel(tiles_per_seq, x_ref, prev_ref, next_ref, ab_ref, cw_ref, alog_ref, dtb_ref, ones_ref,
                    u_ref, gb_ref, pad_ref):
    t, w = GDN_PRE_TILE, MIXER_WIDTH
    tseq = pl.program_id(0) % tiles_per_seq
    _fill_padded(pad_ref, x_ref[...], prev_ref[...], next_ref[...], tseq == 0, tseq == tiles_per_seq - 1)
    acc = jnp.zeros((t, 3 * w), F32)
    for k in range(CONV_K):
        off = HALO + k - CONV_K // 2
        acc = acc + pad_ref[off:off + t, :] * cw_ref[k:k + 1, :]
    u = _silu(acc)
    head_sum = lambda a: sum(_dot(piece, ones_ref[...]) for piece in _split_bf16(a, 2))
    q, k = u[:, 0:w], u[:, w:2 * w]
    u_ref[:, 0:w] = q * lax.rsqrt(head_sum(q * q) + EPS) * (HEAD_DIM ** -0.5)
    u_ref[:, w:2 * w] = k * lax.rsqrt(head_sum(k * k) + EPS)
    u_ref[:, 2 * w:3 * w] = u[:, 2 * w:3 * w]
    ab = ab_ref[...]
    x = ab + dtb_ref[...]
    g = -jnp.exp(alog_ref[...]) * (jnp.maximum(x, 0.0) + jnp.log(1.0 + jnp.exp(-jnp.abs(x))))
    beta = 1.0 / (1.0 + jnp.exp(-ab))
    lane = lax.broadcasted_iota(jnp.int32, ab.shape, 1)
    gb_ref[...] = jnp.where(lane < 2 * MIXER_HEADS, g, beta)


def _gdn_pre(p_d, p_ab, conv_w, alog_vec, dtb_vec, b, s):
    t = GDN_PRE_TILE
    w = MIXER_WIDTH
    tiles_per_seq = s // t
    ones = _block_diag(jnp.ones((MIXER_HEADS, HEAD_DIM, HEAD_DIM), BF16))
    main, prev, nxt = _halo_specs(t, 3 * w, b * s, lambda i: i)
    const = lambda a: pl.BlockSpec(a.shape, lambda i: (0,) * a.ndim)
    row = lambda c: pl.BlockSpec((t, c), lambda i: (i, 0))
    return pl.pallas_call(
        functools.partial(_gdn_pre_kernel, tiles_per_seq),
        grid=(b * tiles_per_seq,),
        in_specs=[main, prev, nxt, row(LANES), const(conv_w), const(alog_vec), const(dtb_vec), const(ones)],
        out_specs=[row(3 * w), row(LANES)],
        out_shape=[jax.ShapeDtypeStruct((b * s, 3 * w), F32), jax.ShapeDtypeStruct((b * s, LANES), F32)],
        scratch_shapes=[pltpu.VMEM((t + 2 * HALO, 3 * w), F32)],
        compiler_params=_cparams("parallel"),
    )(p_d, p_d, p_d, p_ab, conv_w, alog_vec, dtb_vec, ones)


def _gdn_prepare(forward, u, gb, tri_col, tri_row):
    c, dh, w = GDN_CHUNK, HEAD_DIM, MIXER_WIDTH
    t = u.shape[0]
    a_off, b_off = (0, 2 * MIXER_HEADS) if forward else (MIXER_HEADS, 3 * MIXER_HEADS)
    g_all = beta_all = gb
    g_parts = _split_bf16(g_all, 3)
    gc_col = sum(_dot(tri_col, p) for p in g_parts)
    gc_row = sum(_dot_tn(p, tri_row) for p in g_parts)
    ii = lax.broadcasted_iota(jnp.int32, (c, 2 * dh), 0)
    lane = lax.broadcasted_iota(jnp.int32, (c, 2 * dh), 1)
    first = lane < dh
    jj = jnp.where(first, lane, lane - dh)
    incl = (ii >= jj) if forward else (ii <= jj)
    strict = (ii > jj) if forward else (ii < jj)
    last = c - 1 if forward else 0
    items = [(ci, p) for ci in range(t // c) for p in range(MIXER_HEADS // 2)]
    rows = lambda ci: slice(ci * c, (ci + 1) * c)
    part = lambda base: [u[rows(ci), base + 2 * p * dh:base + 2 * (p + 1) * dh] for ci, p in items]
    col_pair = lambda x, off: [jnp.where(first, x[rows(ci), off + 2 * p:off + 2 * p + 1],
                                         x[rows(ci), off + 2 * p + 1:off + 2 * p + 2]) for ci, p in items]
    q, k, v = part(0), part(w), part(2 * w)
    gcc = col_pair(gc_col, a_off)
    beta = col_pair(beta_all, b_off)
    gcr = [jnp.concatenate([gc_row[a_off + 2 * p:a_off + 2 * p + 1, rows(ci)],
                            gc_row[a_off + 2 * p + 1:a_off + 2 * p + 2, rows(ci)]], axis=1) for ci, p in items]
    g_last = [a[last:last + 1, :] for a in gcc]
    decay = [jnp.where(incl, jnp.exp(jnp.where(incl, a - b, 0.0)), 0.0) for a, b in zip(gcc, gcr)]
    kq = [_dot_nt(jnp.concatenate([kk, qq], axis=0).astype(BF16), _head_blockdiag(kk.astype(BF16)))
          for kk, qq in zip(k, q)]
    lmat = [jnp.where(strict, m[:c] * b * d, 0.0) for m, b, d in zip(kq, beta, decay)]
    attn = [(m[c:] * d).astype(BF16) for m, d in zip(kq, decay)]
    xinv = [-a for a in lmat]
    lp = [a.astype(BF16) for a in lmat]
    lp_bd = [_head_blockdiag(a) for a in lp]
    for _ in range(int(math.log2(c)) - 1):
        lp32 = [_dot(a, bd) for a, bd in zip(lp, lp_bd)]
        lp = [a.astype(BF16) for a in lp32]
        lp_bd = [_head_blockdiag(a) for a in lp]
        xinv = [xi + a32 + _dot(xi.astype(BF16), bd) for xi, a32, bd in zip(xinv, lp32, lp_bd)]
    egc = [jnp.exp(a) for a in gcc]
    xb = [a.astype(BF16) for a in xinv]
    vbeta = [vv * b for vv, b in zip(v, beta)]
    kbeta = [kk * (b * e) for kk, b, e in zip(k, beta, egc)]
    u_val = [r + _dot(xi, _head_blockdiag(r.astype(BF16))) for xi, r in zip(xb, vbeta)]
    w_dec = [r + _dot(xi, _head_blockdiag(r.astype(BF16))) for xi, r in zip(xb, kbeta)]
    wq = [jnp.concatenate([wd, qq * e], axis=0).astype(BF16) for wd, qq, e in zip(w_dec, q, egc)]
    k_state = [(kk * jnp.exp(gl - a)).astype(BF16) for kk, gl, a in zip(k, g_last, gcc)]
    e_last = [jnp.exp(gl) for gl in g_last]
    keyed = lambda vals: dict(zip(items, vals))
    return dict(wq=keyed(wq), u_val=keyed(u_val), attn=keyed(attn), k_state=keyed(k_state), e_last=keyed(e_last))


def _gdn_kernel(uf_ref, gbf_ref, ub_ref, gbb_ref, tril_ref, triu_ref, of_ref, ob_ref, sf_ref, sb_ref):
    n = pl.program_id(1)
    t = uf_ref.shape[0]
    c, dh = GDN_CHUNK, HEAD_DIM
    nch = t // c
    pairs = MIXER_HEADS // 2

    @pl.when(n == 0)
    def _():
        sf_ref[...] = jnp.zeros_like(sf_ref)
        sb_ref[...] = jnp.zeros_like(sb_ref)

    tril, triu = tril_ref[...], triu_ref[...]
    prep_f = _gdn_prepare(True, uf_ref[...], gbf_ref[...], tril, triu)
    prep_b = _gdn_prepare(False, ub_ref[...], gbb_ref[...], triu, tril)

    first = lax.broadcasted_iota(jnp.int32, (dh, 2 * dh), 1) < dh
    chains = [(prep_f, sf_ref, of_ref, p, True) for p in range(pairs)] + \
             [(prep_b, sb_ref, ob_ref, p, False) for p in range(pairs)]
    state = [s_ref[p] for _, s_ref, _, p, _ in chains]
    for step in range(nch):
        key = [(step if fw else nch - 1 - step, p) for _, _, _, p, fw in chains]
        ws = [_dot(pr['wq'][kk], _head_blockdiag(st.astype(BF16))) for (pr, *_), kk, st in zip(chains, key, state)]
        v_new = [pr['u_val'][kk] - a[:c] for (pr, *_), kk, a in zip(chains, key, ws)]
        vb = [a.astype(BF16) for a in v_new]
        out = [a[c:] + _dot(pr['attn'][kk], _head_blockdiag(b)) for (pr, *_), kk, a, b in zip(chains, key, ws, vb)]
        cross = [_dot_tn(pr['k_state'][kk], b) for (pr, *_), kk, b in zip(chains, key, vb)]
        state = [st * pr['e_last'][kk] + jnp.where(first, x[:dh], x[dh:])
                 for (pr, *_), kk, st, x in zip(chains, key, state, cross)]
        for (_, _, o_ref, p, _), (ci, _), o in zip(chains, key, out):
            o_ref[ci * c:(ci + 1) * c, 2 * p * dh:2 * (p + 1) * dh] = o
    for (_, s_ref, _, p, _), st in zip(chains, state):
        s_ref[p] = st


def _gated_deltanet(p_d, p_ab, conv_w, alog_vec, dtb_vec, b, s):
    u, gb = _gdn_pre(p_d, p_ab, conv_w, alog_vec, dtb_vec, b, s)
    c = GDN_CHUNK
    t = c * GDN_CHUNKS_PER_STEP
    nb = s // t
    w = MIXER_WIDTH
    r = np.arange(t)
    same = (r[:, None] // c) == (r[None, :] // c)
    tril = jnp.asarray(same & (r[:, None] >= r[None, :]), BF16)
    triu = jnp.asarray(same & (r[:, None] <= r[None, :]), BF16)
    fwd = lambda bi, n: (bi * nb + n, 0)
    bwd = lambda bi, n: (bi * nb + nb - 1 - n, 0)
    const = lambda a: pl.BlockSpec(a.shape, lambda bi, n: (0,) * a.ndim)
    return pl.pallas_call(
        _gdn_kernel,
        grid=(b, nb),
        in_specs=[pl.BlockSpec((t, 3 * w), fwd), pl.BlockSpec((t, LANES), fwd),
                  pl.BlockSpec((t, 3 * w), bwd), pl.BlockSpec((t, LANES), bwd), const(tril), const(triu)],
        out_specs=[pl.BlockSpec((t, w), fwd), pl.BlockSpec((t, w), bwd)],
        out_shape=[jax.ShapeDtypeStruct((b * s, w), F32), jax.ShapeDtypeStruct((b * s, w), F32)],
        scratch_shapes=[pltpu.VMEM((MIXER_HEADS // 2, HEAD_DIM, 2 * HEAD_DIM), F32),
                        pltpu.VMEM((MIXER_HEADS // 2, HEAD_DIM, 2 * HEAD_DIM), F32)],
        compiler_params=_cparams("arbitrary", "arbitrary"),
    )(u, gb, u, gb, tril, triu)


def _mem_kv_kernel(m_ref, g_ref, wk_ref, wv_ref, k_ref, v_ref):
    h = _rms(m_ref[0], g_ref[...]).astype(BF16)
    k_ref[0] = _dot(h, wk_ref[...]).astype(BF16)
    v_ref[0] = _dot(h, wv_ref[...]).astype(BF16)


def _mem_kv(mem, g, wk, wv):
    b, nm, d = mem.shape
    const = lambda a: pl.BlockSpec(a.shape, lambda i: (0,) * a.ndim)
    blk = pl.BlockSpec((1, nm, d), lambda i: (i, 0, 0))
    return pl.pallas_call(
        _mem_kv_kernel, grid=(b,),
        in_specs=[blk, const(g), const(wk), const(wv)], out_specs=[blk, blk],
        out_shape=[jax.ShapeDtypeStruct((b, nm, d), BF16)] * 2,
        compiler_params=_cparams("parallel"),
    )(mem, g, wk, wv)


def _route_lanes(lg):
    lane_i = lax.broadcasted_iota(jnp.int32, lg.shape, 1)
    lane = lane_i.astype(F32)
    first_at = lambda vals, m: jnp.min(jnp.where(vals == m, lane, float(LANES)), axis=-1, keepdims=True)
    g_mask = lane_i < N_GROUPS
    gl = jnp.where(g_mask, lg, NEG_BIG)
    g_max = jnp.max(gl, axis=-1, keepdims=True)
    g_sum = jnp.sum(jnp.where(g_mask, jnp.exp(gl - g_max), 0.0), axis=-1, keepdims=True)
    g_p = 1.0 / g_sum
    g_idx = first_at(gl, g_max)
    lo = N_GROUPS + EXPERTS_PER_GROUP * g_idx
    e_mask = (lane >= lo) & (lane < lo + EXPERTS_PER_GROUP)
    el = jnp.where(e_mask, lg, NEG_BIG)
    m1 = jnp.max(el, axis=-1, keepdims=True)
    i1 = first_at(el, m1)
    el2 = jnp.where(lane == i1, NEG_BIG, el)
    m2 = jnp.max(el2, axis=-1, keepdims=True)
    i2 = first_at(el2, m2)
    r = jnp.exp(m2 - m1)
    w1 = g_p / (1.0 + r)
    w2 = g_p * r / (1.0 + r)
    e1, e2 = i1 - N_GROUPS, i2 - N_GROUPS
    e_lo, e_hi = jnp.minimum(e1, e2), jnp.maximum(e1, e2)
    w_lo, w_hi = jnp.where(e1 < e2, w1, w2), jnp.where(e1 < e2, w2, w1)
    a, b = e_lo - EXPERTS_PER_GROUP * g_idx, e_hi - EXPERTS_PER_GROUP * g_idx
    cls = g_idx * PAIRS_PER_GROUP + a * EXPERTS_PER_GROUP - a * (a + 1.0) * 0.5 + (b - a - 1.0)
    out = jnp.where(lane_i == 0, e_lo, 0.0)
    out = jnp.where(lane_i == 1, e_hi, out)
    out = jnp.where(lane_i == 2, w_lo, out)
    out = jnp.where(lane_i == 3, w_hi, out)
    out = jnp.where(lane_i == 4, cls, out)
    return out, jnp.where(lane == cls, 1.0, 0.0)


def _post_kernel(x_ref, ya_ref, yb_ref, yc_ref, of_ref, ob_ref, gate_ref, seg_ref, gg_ref, wout_ref, gq_ref,
                 wq_ref, k_ref, v_ref, wo_ref, gf_ref, wr_ref, br_ref, xo_ref, hx_ref, rt_ref, hist_ref):
    w = MIXER_WIDTH
    o = of_ref[...] + ob_ref[...]
    ms = _dot((o * o).astype(BF16), seg_ref[...])
    yd = o * lax.rsqrt(ms + EPS) * gg_ref[...] * _silu(gate_ref[...])
    mix = _dot(ya_ref[...], wout_ref[0:w, :]) + _dot(yb_ref[...], wout_ref[w:2 * w, :])
    mix = mix + _dot(yc_ref[...], wout_ref[2 * w:3 * w, :]) + _dot(yd.astype(BF16), wout_ref[3 * w:4 * w, :])
    x1 = x_ref[...] + mix
    q = _dot(_rms(x1, gq_ref[...]).astype(BF16), wq_ref[...]).astype(BF16)
    cols = [slice(h * MEM_HEAD_DIM, (h + 1) * MEM_HEAD_DIM) for h in range(MEM_HEADS)]
    sc = [_dot_nt(q[:, sl], k_ref[0, :, sl]) * (MEM_HEAD_DIM ** -0.5) for sl in cols]
    p = [jnp.exp(a - jnp.max(a, axis=-1, keepdims=True)) for a in sc]
    l = [jnp.sum(a, axis=-1, keepdims=True) for a in p]
    heads = [(_dot(a.astype(BF16), v_ref[0, :, sl]) / b).astype(BF16) for a, b, sl in zip(p, l, cols)]
    x2 = x1 + _dot(jnp.concatenate(heads, axis=-1), wo_ref[...])
    xo_ref[...] = x2
    hx = _rms(x2, gf_ref[...])
    _to_row_tiles(hx_ref, hx)
    lg = _dot(hx.astype(BF16), wr_ref[...]) + br_ref[...]
    route, chosen = _route_lanes(lg)
    rt_ref[...] = route
    hist_ref[...] = jnp.broadcast_to(jnp.sum(chosen, axis=0, keepdims=True), hist_ref.shape)


def _post(x, y_a, y_b, y_c, o_f, o_b, p_d, seg, gdn_g, w_out, g_q, w_q, k_mem, v_mem, w_o, g_f, w_r, b_r, s):
    n = x.shape[0]
    tm = POST_TILE
    w = MIXER_WIDTH
    per_seq = s // tm
    row = lambda c: pl.BlockSpec((tm, c), lambda i: (i, 0))
    const = lambda a: pl.BlockSpec(a.shape, lambda i: (0,) * a.ndim)
    kv = pl.BlockSpec((1,) + k_mem.shape[1:], lambda i: (i // per_seq, 0, 0))
    return pl.pallas_call(
        _post_kernel, grid=(n // tm,),
        in_specs=[row(D_MODEL), row(w), row(w), row(w), row(w), row(w),
                  pl.BlockSpec((tm, w), lambda i: (i, 3)), const(seg), const(gdn_g), const(w_out), const(g_q),
                  const(w_q), kv, kv, const(w_o), const(g_f), const(w_r), const(b_r)],
        out_specs=[row(D_MODEL), pl.BlockSpec((tm * ROW_TILES, LANES), lambda i: (i, 0)), row(LANES),
                   pl.BlockSpec((SUBLANES, LANES), lambda i: (i, 0))],
        out_shape=[jax.ShapeDtypeStruct((n, D_MODEL), F32), jax.ShapeDtypeStruct((n * ROW_TILES, LANES), F32),
                   jax.ShapeDtypeStruct((n, LANES), F32), jax.ShapeDtypeStruct((n // tm * SUBLANES, LANES), F32)],
        compiler_params=_cparams("parallel"),
    )(x, y_a, y_b, y_c, o_f, o_b, p_d, seg, gdn_g, w_out, g_q, w_q, k_mem, v_mem, w_o, g_f, w_r, b_r)


def _pair_class_experts():
    lo, hi = [], []
    for g in range(N_GROUPS):
        for a in range(EXPERTS_PER_GROUP):
            for b in range(a + 1, EXPERTS_PER_GROUP):
                lo.append(g * EXPERTS_PER_GROUP + a)
                hi.append(g * EXPERTS_PER_GROUP + b)
    return np.asarray(lo, np.int32), np.asarray(hi, np.int32)


def _dispatch(route, hist):
    n = route.shape[0]
    blk = MOE_BLOCK
    counts = jnp.sum(hist.reshape(-1, SUBLANES, LANES)[:, 0, :N_CLASSES], axis=0).astype(jnp.int32)
    padded = (counts + blk - 1) // blk * blk
    pad_end = jnp.cumsum(padded)
    n_blocks = n // blk + N_CLASSES
    blk_start = jnp.arange(n_blocks, dtype=jnp.int32) * blk
    blk_cls = jnp.minimum(jnp.sum(pad_end[None, :] <= blk_start[:, None], axis=1), N_CLASSES - 1).astype(jnp.int32)
    cls_lo, cls_hi = _pair_class_experts()
    in_blk = jnp.arange(blk, dtype=jnp.int32)[None, :]
    pad_key = jnp.where(in_blk < (padded - counts)[:, None], jnp.arange(N_CLASSES, dtype=jnp.int32)[:, None], N_CLASSES)
    keys = jnp.concatenate([route[:, 4].astype(jnp.int32), pad_key.reshape(-1)])
    toks = jnp.concatenate([jnp.arange(n, dtype=jnp.int32), jnp.tile(n + in_blk[0], N_CLASSES)])
    zeros = jnp.zeros((N_CLASSES * blk,), F32)
    w_lo = jnp.concatenate([route[:, TOP_K], zeros])
    w_hi = jnp.concatenate([route[:, TOP_K + 1], zeros])
    _, row_tok, w_lo, w_hi = lax.sort((keys, toks, w_lo, w_hi), num_keys=1)
    n_used = (pad_end[-1] // blk).astype(jnp.int32).reshape(1)
    return (n_used, jnp.asarray(cls_lo)[blk_cls], jnp.asarray(cls_hi)[blk_cls],
            jnp.concatenate([n + in_blk[0], row_tok]), jnp.stack([w_lo, w_hi], axis=-1))


def _moe_kernel(n_tokens, n_blocks, nu_ref, ea_ref, eb_ref, dst_ref, hx_hbm, rw_ref, wga_ref, wua_ref, wda_ref,
                wgb_ref, wub_ref, wdb_ref, y_hbm, xbuf, ybuf, gsem, ssem):
    j = pl.program_id(0)
    blk, rt = MOE_BLOCK, ROW_TILES
    n_used = nu_ref[0]
    slot = lax.rem(j, MOE_BUFFERS)
    ahead = lax.rem(j + 2, MOE_BUFFERS)
    tile = lambda ref, i: ref.at[pl.ds(pl.multiple_of(i * rt, rt), rt), :]
    dst = lambda jj, i: dst_ref[(jj + 1) * blk + i]
    clamp = lambda jj: jnp.minimum(jj, n_blocks - 1)

    def gather_row(jj, sl, i):
        tok = jnp.minimum(dst(jj, i), n_tokens - 1)
        pltpu.make_async_copy(tile(hx_hbm, tok), tile(xbuf.at[sl], i), gsem.at[sl]).start()

    def scatter_row(jj, sl, i):
        pltpu.make_async_copy(tile(ybuf.at[sl], i), tile(y_hbm, dst(jj, i)), ssem.at[sl]).start()

    def wait_gather(sl):
        pltpu.make_async_copy(hx_hbm.at[pl.ds(0, blk * rt), :], xbuf.at[sl], gsem.at[sl]).wait()

    def wait_scatter(sl):
        pltpu.make_async_copy(ybuf.at[sl], y_hbm.at[pl.ds(0, blk * rt), :], ssem.at[sl]).wait()

    def rolled(fn):
        lax.fori_loop(0, blk, lambda i, carry: (fn(i), carry)[1], 0, unroll=DMA_UNROLL)

    @pl.when(j == 0)
    def _():
        ybuf[MOE_BUFFERS - 1] = jnp.zeros((blk * rt, LANES), F32)
        rolled(lambda i: gather_row(0, 0, i))
        rolled(lambda i: gather_row(clamp(1), 1, i))

    @pl.when(j < n_used)
    def _():
        wait_gather(slot)

        @pl.when(j >= 2)
        def _():
            wait_scatter(slot)

        def issue_copies(piece):
            for i in range(piece * blk // MOE_PIECES, (piece + 1) * blk // MOE_PIECES):
                gather_row(clamp(j + 2), ahead, i)
                scatter_row(j - 1, ahead, i)

        xs = _from_row_tiles(xbuf.at[slot], 0, blk, rt).astype(BF16)
        half = D_EXPERT // 2
        hidden = []
        for e, (wg_ref, wu_ref) in enumerate(((wga_ref, wua_ref), (wgb_ref, wub_ref))):
            acts = []
            for c in range(2):
                cols = slice(c * half, (c + 1) * half)
                g = _dot(xs, wg_ref[0, 0, :, cols])
                issue_copies(4 * e + 2 * c)
                u = _dot(xs, wu_ref[0, 0, :, cols])
                issue_copies(4 * e + 2 * c + 1)
                acts.append((_silu(g) * u).astype(BF16))
            hidden.append(jnp.concatenate(acts, axis=1))
        w_a, w_b = rw_ref[:, 0:1], rw_ref[:, 1:2]
        quarter = D_MODEL // 4
        for c in range(4):
            cols = slice(c * quarter, (c + 1) * quarter)
            y = w_a * _dot(hidden[0], wda_ref[0, 0, :, cols]) + w_b * _dot(hidden[1], wdb_ref[0, 0, :, cols])
            for k in range(quarter // LANES):
                ybuf[slot, pl.ds(c * (quarter // LANES) + k, blk, stride=rt), :] = y[:, k * LANES:(k + 1) * LANES]
            issue_copies(8 + c)

    @pl.when(j == n_used)
    def _():
        after = lax.rem(j + 1, MOE_BUFFERS)
        wait_gather(slot)
        wait_gather(after)

        @pl.when(j >= 2)
        def _():
            wait_scatter(slot)

        wait_scatter(after)
        rolled(lambda i: scatter_row(j - 1, ahead, i))
        wait_scatter(ahead)


def _moe(hx_tiles, n_used, blk_ea, blk_eb, row_tok, row_w, w_gate, w_up, w_down, layer, n):
    blk = MOE_BLOCK
    n_blocks = blk_ea.shape[0]
    block = lambda j: jnp.minimum(j, n_blocks - 1)
    first = lambda j, nu, ea, eb, dst: (layer, ea[block(j)], 0, 0)
    second = lambda j, nu, ea, eb, dst: (layer, eb[block(j)], 0, 0)
    up, down = (1, 1, D_MODEL, D_EXPERT), (1, 1, D_EXPERT, D_MODEL)
    weights = pl.BlockSpec
    grid_spec = pltpu.PrefetchScalarGridSpec(
        num_scalar_prefetch=4, grid=(n_blocks + 1,),
        in_specs=[pl.BlockSpec(memory_space=pl.ANY),
                  pl.BlockSpec((blk, TOP_K), lambda j, nu, ea, eb, dst: (block(j), 0)),
                  weights(up, first), weights(up, first), weights(down, first),
                  weights(up, second), weights(up, second), weights(down, second)],
        out_specs=pl.BlockSpec(memory_space=pl.ANY),
        scratch_shapes=[pltpu.VMEM((MOE_BUFFERS, blk * ROW_TILES, LANES), F32),
                        pltpu.VMEM((MOE_BUFFERS, blk * ROW_TILES, LANES), F32),
                        pltpu.SemaphoreType.DMA((MOE_BUFFERS,)), pltpu.SemaphoreType.DMA((MOE_BUFFERS,))])
    return pl.pallas_call(
        functools.partial(_moe_kernel, n, n_blocks), grid_spec=grid_spec,
        out_shape=jax.ShapeDtypeStruct(((n + blk) * ROW_TILES, LANES), F32),
        compiler_params=_cparams("arbitrary"),
    )(n_used, blk_ea, blk_eb, row_tok, hx_tiles, row_w, w_gate, w_up, w_down, w_gate, w_up, w_down)


def _final_kernel(x_ref, y_ref, g_ref, o_ref):
    o_ref[...] = _rms(_add_moe(x_ref[...], y_ref), g_ref[...])


def _final(x, moe, g):
    n = x.shape[0]
    tm = PROJ_TILE
    return pl.pallas_call(
        _final_kernel, grid=(n // tm,),
        in_specs=[pl.BlockSpec((tm, D_MODEL), lambda i: (i, 0)), _moe_out_spec(tm),
                  pl.BlockSpec(g.shape, lambda i: (0, 0))],
        out_specs=pl.BlockSpec((tm, D_MODEL), lambda i: (i, 0)),
        out_shape=jax.ShapeDtypeStruct((n, D_MODEL), F32),
        compiler_params=_cparams("parallel"),
    )(x, moe, g)


def _lane_vec(v, width=LANES):
    v = v.reshape(1, -1).astype(F32)
    return jnp.pad(v, ((0, 0), (0, width - v.shape[1])))


def _block_diag(blocks):
    g, c, _ = blocks.shape
    eye = jnp.eye(g, dtype=blocks.dtype)
    return (eye[:, None, :, None] * blocks[:, :, None, :]).reshape(g * c, g * c)


def kernel(x, mem, mix_norm_g, w_in, na_rpb, ret_norm_g, pool_w, pool_scale, gdn_conv_w, gdn_a_log, gdn_dt_bias, gdn_norm_g, w_out, mem_q_norm_g, mem_kv_norm_g, mem_w_q, mem_w_k, mem_w_v, mem_w_o, ffn_norm_g, w_group, b_group, w_router, b_router, w_gate, w_up, w_down, final_norm_g):
    b, s, d = x.shape
    n = b * s
    depth = w_in.shape[0]
    row = lambda v: v.reshape(1, -1).astype(F32)
    ret_tables = _retention_tables(s)
    seg = _block_diag(jnp.full((MIXER_HEADS, HEAD_DIM, HEAD_DIM), 1.0 / HEAD_DIM, BF16))
    wg_bf, wu_bf, wd_bf = w_gate.astype(BF16), w_up.astype(BF16), w_down.astype(BF16)
    xs = x.reshape(n, d)
    moe = None
    for l in range(depth):
        w_pad = jnp.pad(w_in[l], ((0, 0), (0, P_IN_PAD - w_in.shape[2]))).astype(BF16)
        xs, p_a, p_b, p_c, p_d, p_ab = _norm_proj(xs, moe, row(mix_norm_g[l]), w_pad)
        y_a = _neighbourhood_attention(p_a, _na_bias_table(na_rpb[l]), b, s)
        y_b = _retention(p_b, row(ret_norm_g[l]), ret_tables, b, s)
        y_c = _pool(p_c, _block_diag(pool_w[l]).astype(BF16), row(pool_scale[l]), b, s)
        o_f, o_b = _gated_deltanet(p_d, p_ab, gdn_conv_w[l].astype(F32), _lane_vec(gdn_a_log[l]),
                                   _lane_vec(gdn_dt_bias[l]), b, s)
        k_mem, v_mem = _mem_kv(mem, row(mem_kv_norm_g[l]), mem_w_k[l].astype(BF16), mem_w_v[l].astype(BF16))
        w_r = jnp.pad(jnp.concatenate([w_group[l], w_router[l]], axis=1),
                      ((0, 0), (0, LANES - N_GROUPS - N_EXPERTS))).astype(F32)
        b_r = _lane_vec(jnp.concatenate([b_group[l], b_router[l]]))
        xs, hx_tiles, route, hist = _post(
            xs, y_a, y_b, y_c, o_f, o_b, p_d, seg, row(jnp.tile(gdn_norm_g[l], MIXER_HEADS)), w_out[l].astype(BF16),
            row(mem_q_norm_g[l]), mem_w_q[l].astype(BF16), k_mem, v_mem, mem_w_o[l].astype(BF16),
            row(ffn_norm_g[l]), w_r.astype(BF16), b_r, s)
        moe = _moe(hx_tiles, *_dispatch(route, hist), wg_bf, wu_bf, wd_bf, l, n)
    return _final(xs, moe, row(final_norm_g)).reshape(b, s, d)
```

```python
import functools
import math

import jax
import jax.numpy as jnp
import numpy as np
from jax import lax
from jax.experimental import pallas as pl
from jax.experimental.pallas import tpu as pltpu

F32 = jnp.float32
BF16 = jnp.bfloat16
HIGHEST = lax.Precision.HIGHEST

D_MODEL = 1024
GRID_W = 64
HEAD_DIM = 64
MIXER_WIDTH = 256
MIXER_HEADS = 4
NA_KH = 8
NA_KW = 16
RET_CHUNK = 128
ROPE_BASE = 10000.0
POOL_WINDOWS = (2, 4, 8, 16)
POOL_GROUP = 64
GDN_CHUNK = 64
CONV_K = 4
MEM_HEADS = 4
MEM_HEAD_DIM = 256
N_GROUPS = 4
EXPERTS_PER_GROUP = 8
N_EXPERTS = 32
TOP_K = 2
D_EXPERT = 512
EPS = 1e-6

P_IN_PAD = 3200
V7X_VMEM_LIMIT = 56 * 1024 * 1024
LANES = 128
SUBLANES = 8
ROW_TILES = D_MODEL // LANES
NEG_BIG = -1e30

PROJ_TILE = 512
POOL_TILE = 512
POST_TILE = 512
NA_ROWS_PER_STEP = 8
NA_ROWS_LOCKSTEP = 4
GDN_CHUNKS_PER_STEP = 8
GDN_PRE_TILE = 512
RET_CHUNKS_PER_STEP = 8
PAIRS_PER_GROUP = EXPERTS_PER_GROUP * (EXPERTS_PER_GROUP - 1) // 2
N_CLASSES = N_GROUPS * PAIRS_PER_GROUP
MOE_BLOCK = 128
MOE_BUFFERS = 3
MOE_PIECES = 12
DMA_UNROLL = 8
HALO = 8


def _cparams(*sem):
    return pltpu.CompilerParams(dimension_semantics=sem, vmem_limit_bytes=V7X_VMEM_LIMIT)


def _dot(a, b):
    return jnp.dot(a, b, preferred_element_type=F32)


def _dot_nt(a, b):
    return lax.dot_general(a, b, (((1,), (1,)), ((), ())), preferred_element_type=F32)


def _dot_tn(a, b, precision=None):
    return lax.dot_general(a, b, (((0,), (0,)), ((), ())), preferred_element_type=F32, precision=precision)


def _silu(x):
    return x * (1.0 / (1.0 + jnp.exp(-x)))


def _rms(x, g):
    return x * lax.rsqrt(jnp.mean(x * x, axis=-1, keepdims=True) + EPS) * g


def _from_row_tiles(ref, first, count, stride):
    return jnp.concatenate([ref[pl.ds(first + k, count, stride=stride), :] for k in range(ROW_TILES)], axis=1)


def _to_row_tiles(ref, x):
    count = x.shape[0]
    for k in range(ROW_TILES):
        ref[pl.ds(k, count, stride=ROW_TILES), :] = x[:, k * LANES:(k + 1) * LANES]


def _add_moe(x, y_ref):
    return x + _from_row_tiles(y_ref, 0, x.shape[0], ROW_TILES)


def _split_bf16(x, parts):
    out = []
    for _ in range(parts):
        p = x.astype(BF16)
        out.append(p)
        x = x - p.astype(F32)
    return out


def _norm_proj_kernel(has_y, *refs):
    if has_y:
        x_ref, y_ref, g_ref, w_ref, xo_ref, pa_ref, pb_ref, pc_ref, pd_ref, pab_ref = refs
        x = _add_moe(x_ref[...], y_ref)
        xo_ref[...] = x
    else:
        x_ref, g_ref, w_ref, pa_ref, pb_ref, pc_ref, pd_ref, pab_ref = refs
        x = x_ref[...]
    h = _rms(x, g_ref[...]).astype(BF16)
    w = MIXER_WIDTH
    pa_ref[...] = _dot(h, w_ref[:, 0:3 * w]).astype(BF16)
    pb_ref[...] = _dot(h, w_ref[:, 3 * w:7 * w])
    pc_ref[...] = _dot(h, w_ref[:, 7 * w:8 * w])
    pd_ref[...] = _dot(h, w_ref[:, 8 * w:12 * w])
    pab_ref[...] = _dot(h, w_ref[:, 12 * w:P_IN_PAD])


def _moe_out_spec(tile):
    return pl.BlockSpec((tile * ROW_TILES, LANES), lambda i: (i, 0))


def _norm_proj(x, moe, g, w_pad):
    n = x.shape[0]
    tm = PROJ_TILE
    w = MIXER_WIDTH
    row = lambda c: pl.BlockSpec((tm, c), lambda i: (i, 0))
    const = lambda a: pl.BlockSpec(a.shape, lambda i: (0,) * a.ndim)
    outs = [jax.ShapeDtypeStruct((n, 3 * w), BF16), jax.ShapeDtypeStruct((n, 4 * w), F32),
            jax.ShapeDtypeStruct((n, w), F32), jax.ShapeDtypeStruct((n, 4 * w), F32),
            jax.ShapeDtypeStruct((n, LANES), F32)]
    out_specs = [row(3 * w), row(4 * w), row(w), row(4 * w), row(LANES)]
    if moe is None:
        ins, in_specs = (x, g, w_pad), [row(D_MODEL), const(g), const(w_pad)]
    else:
        ins = (x, moe, g, w_pad)
        in_specs = [row(D_MODEL), _moe_out_spec(tm), const(g), const(w_pad)]
        outs = [jax.ShapeDtypeStruct((n, D_MODEL), F32)] + outs
        out_specs = [row(D_MODEL)] + out_specs
    res = pl.pallas_call(
        functools.partial(_norm_proj_kernel, moe is not None),
        grid=(n // tm,), in_specs=in_specs, out_specs=out_specs, out_shape=outs,
        compiler_params=_cparams("parallel"),
    )(*ins)
    return res if moe is not None else [x] + list(res)


def _na_bias_table(rpb):
    qc = np.arange(GRID_W)
    kc = np.arange(GRID_W)
    d_col = np.clip(kc[None, :] - qc[:, None], -(NA_KW - 1), NA_KW - 1) + (NA_KW - 1)
    pick = np.zeros((2 * NA_KW - 1, GRID_W * GRID_W), np.float32)
    pick[d_col.reshape(-1), np.arange(GRID_W * GRID_W)] = 1.0
    win = np.clip(qc - NA_KW // 2, 0, GRID_W - NA_KW)
    ok = (kc[None, :] >= win[:, None]) & (kc[None, :] < win[:, None] + NA_KW)
    cols = jnp.einsum('hrc,cx->hrx', rpb.astype(F32), jnp.asarray(pick), precision=HIGHEST)
    cols = jnp.where(ok[None, None], cols.reshape(MIXER_HEADS, 2 * NA_KH - 1, GRID_W, GRID_W), NEG_BIG)
    per_e = [jnp.transpose(cols[:, NA_KH - 1 - e:2 * NA_KH - 1 - e], (0, 2, 1, 3)) for e in range(NA_KH)]
    return jnp.stack(per_e, axis=1).reshape(MIXER_HEADS, NA_KH, GRID_W, NA_KH * GRID_W)


def _na_kernel(rows, q_ref, k_ref, v_ref, t_ref, o_ref):
    i = pl.program_id(1)
    dh = HEAD_DIM

    def rows_body(it, carry):
        chains = []
        for k in range(NA_ROWS_LOCKSTEP):
            rr = it * NA_ROWS_LOCKSTEP + k
            r = i * NA_ROWS_PER_STEP + rr
            kr0 = jnp.clip(r - NA_KH // 2, 0, rows - NA_KH)
            q = q_ref[pl.ds(pl.multiple_of(rr * GRID_W, GRID_W), GRID_W), :]
            k0 = pl.multiple_of(kr0 * GRID_W, GRID_W)
            kb = k_ref[pl.ds(k0, NA_KH * GRID_W), :]
            vb = v_ref[pl.ds(k0, NA_KH * GRID_W), :]
            for h in range(MIXER_HEADS):
                sl = slice(h * dh, (h + 1) * dh)
                chains.append((q[:, sl], kb[:, sl], vb[:, sl], h, r - kr0))
        s = [_dot_nt(q, kb) * (dh ** -0.5) + t_ref[h, e] for q, kb, _, h, e in chains]
        m = [jnp.max(a, axis=-1, keepdims=True) for a in s]
        p = [jnp.exp(a - b) for a, b in zip(s, m)]
        l = [jnp.sum(a, axis=-1, keepdims=True) for a in p]
        o = [_dot(a.astype(BF16), c[2]) / b for a, b, c in zip(p, l, chains)]
        for k in range(NA_ROWS_LOCKSTEP):
            rr = it * NA_ROWS_LOCKSTEP + k
            row = jnp.concatenate(o[k * MIXER_HEADS:(k + 1) * MIXER_HEADS], axis=-1)
            o_ref[pl.ds(pl.multiple_of(rr * GRID_W, GRID_W), GRID_W), :] = row.astype(BF16)
        return carry

    lax.fori_loop(0, NA_ROWS_PER_STEP // NA_ROWS_LOCKSTEP, rows_body, 0)


def _neighbourhood_attention(p_a, table, b, s):
    rows = s // GRID_W
    assert rows >= NA_KH and rows % NA_ROWS_PER_STEP == 0
    steps = rows // NA_ROWS_PER_STEP
    tq = NA_ROWS_PER_STEP * GRID_W
    w = MIXER_WIDTH
    return pl.pallas_call(
        functools.partial(_na_kernel, rows),
        grid=(b, steps),
        in_specs=[pl.BlockSpec((tq, w), lambda bi, i: (bi * steps + i, 0)),
                  pl.BlockSpec((s, w), lambda bi, i: (bi, 1)),
                  pl.BlockSpec((s, w), lambda bi, i: (bi, 2)),
                  pl.BlockSpec(table.shape, lambda bi, i: (0, 0, 0, 0))],
        out_specs=pl.BlockSpec((tq, w), lambda bi, i: (bi * steps + i, 0)),
        out_shape=jax.ShapeDtypeStruct((b * s, w), BF16),
        compiler_params=_cparams("parallel", "arbitrary"),
    )(p_a, p_a, p_a, table)


def _retention_tables(s):
    h, dh, c = MIXER_HEADS, HEAD_DIM, RET_CHUNK
    half = dh // 2
    inv = ROPE_BASE ** (-jnp.arange(half, dtype=F32) / half)
    ang = jnp.arange(s, dtype=F32)[:, None] * inv[None, :]
    cos, sin = jnp.cos(ang), jnp.sin(ang)
    zero = jnp.zeros_like(sin)
    cos_t = jnp.tile(jnp.concatenate([cos, cos], axis=-1), (1, h))
    sin_lo = jnp.tile(jnp.concatenate([-sin, zero], axis=-1), (1, h))
    sin_hi = jnp.tile(jnp.concatenate([zero, sin], axis=-1), (1, h))
    log_f = np.log1p(-np.exp2(-5.0 - np.arange(h, dtype=np.float64)))
    log_b = log_f[::-1]
    pos = np.arange(c, dtype=np.float64)
    diff = pos[:, None] - pos[None, :]
    dmat = np.where(diff >= 0, np.exp(log_f[:, None, None] * np.maximum(diff, 0.0)), 0.0) \
        + np.where(diff < 0, np.exp(log_b[:, None, None] * np.maximum(-diff, 0.0)), 0.0)
    lanes = lambda t: np.repeat(t.T, dh, axis=1)
    dec = np.stack([lanes(np.exp(log_f[:, None] * (pos + 1.0))),
                    lanes(np.exp(log_f[:, None] * (c - 1.0 - pos))),
                    lanes(np.exp(log_b[:, None] * (c - pos))),
                    lanes(np.exp(log_b[:, None] * pos))])
    chunk_f = [float(np.exp(v * c)) for v in log_f]
    chunk_b = [float(np.exp(v * c)) for v in log_b]
    return cos_t, sin_lo, sin_hi, jnp.asarray(dmat, F32), jnp.asarray(dec, F32), chunk_f, chunk_b


def _rotary(x, cos, sin_lo, sin_hi):
    w = x.shape[-1]
    return x * cos + pltpu.roll(x, w - HEAD_DIM // 2, 1) * sin_lo + pltpu.roll(x, HEAD_DIM // 2, 1) * sin_hi


def _retention_kernel(nb, chunk_f, chunk_b, q_ref, k_ref, v_ref, gate_ref, cos_ref, slo_ref, shi_ref,
                      dmat_ref, dec_ref, ng_ref, o_ref, sf_ref, sb_ref, sball_ref):
    t = pl.program_id(1)
    dh, c, cps = HEAD_DIM, RET_CHUNK, RET_CHUNKS_PER_STEP
    cos, slo, shi = cos_ref[...], slo_ref[...], shi_ref[...]
    kr = _rotary(k_ref[...], cos, slo, shi) * (dh ** -0.5)
    vb = v_ref[...].astype(BF16)
    items = [(ci, h) for ci in range(cps) for h in range(MIXER_HEADS)]
    part = lambda x, ci, h: x[ci * c:(ci + 1) * c, h * dh:(h + 1) * dh]
    split = lambda x: {it: part(x, *it) for it in items}

    @pl.when(t == 0)
    def _():
        sf_ref[...] = jnp.zeros_like(sf_ref)
        sb_ref[...] = jnp.zeros_like(sb_ref)

    @pl.when(t < nb)
    def _():
        blk = nb - 1 - t
        kd, v = split((kr * dec_ref[3]).astype(BF16)), split(vb)
        kv = {it: _dot_tn(kd[it], v[it]) for it in items}
        for h in range(MIXER_HEADS):
            state = sb_ref[h]
            for ci in range(cps - 1, -1, -1):
                sball_ref[blk * cps + ci, h] = state
                state = state * chunk_b[h] + kv[ci, h]
            sb_ref[h] = state

    @pl.when(t >= nb)
    def _():
        blk = t - nb
        qr = _rotary(q_ref[...], cos, slo, shi)
        q, k, v = split(qr.astype(BF16)), split(kr.astype(BF16)), split(vb)
        qf, qbk = split((qr * dec_ref[0]).astype(BF16)), split((qr * dec_ref[2]).astype(BF16))
        kd = split((kr * dec_ref[1]).astype(BF16))
        sc = {it: (_dot_nt(q[it], k[it]) * dmat_ref[it[1]]).astype(BF16) for it in items}
        o = {it: _dot(sc[it], v[it]) for it in items}
        kv = {it: _dot_tn(kd[it], v[it]) for it in items}
        ob = {it: _dot(qbk[it], sball_ref[blk * cps + it[0], it[1]].astype(BF16)) for it in items}
        of = {}
        for h in range(MIXER_HEADS):
            state = sf_ref[h]
            for ci in range(cps):
                of[ci, h] = _dot(qf[ci, h], state.astype(BF16))
                state = state * chunk_f[h] + kv[ci, h]
            sf_ref[h] = state
        o = {it: o[it] + of[it] + ob[it] for it in items}
        mu = {it: jnp.mean(o[it], axis=-1, keepdims=True) for it in items}
        oc = {it: o[it] - mu[it] for it in items}
        var = {it: jnp.mean(oc[it] * oc[it], axis=-1, keepdims=True) for it in items}
        on = {it: oc[it] * lax.rsqrt(var[it] + EPS) for it in items}
        y = jnp.concatenate([jnp.concatenate([on[ci, h] for h in range(MIXER_HEADS)], axis=-1)
                             for ci in range(cps)], axis=0)
        o_ref[...] = (y * ng_ref[...] * _silu(gate_ref[...])).astype(BF16)


def _retention(p_b, norm_g, tables, b, s):
    cos_t, sin_lo, sin_hi, dmat, dec, chunk_f, chunk_b = tables
    cps = RET_CHUNKS_PER_STEP
    tb = RET_CHUNK * cps
    nb = s // tb
    w = MIXER_WIDTH
    dec = jnp.tile(dec, (1, cps, 1))
    block = lambda t: jnp.where(t < nb, nb - 1 - t, t - nb)
    col = lambda j: pl.BlockSpec((tb, w), lambda bi, t: (bi * nb + block(t), j))
    tab = pl.BlockSpec((tb, w), lambda bi, t: (block(t), 0))
    const = lambda a: pl.BlockSpec(a.shape, lambda bi, t: (0,) * a.ndim)
    return pl.pallas_call(
        functools.partial(_retention_kernel, nb, chunk_f, chunk_b),
        grid=(b, 2 * nb),
        in_specs=[col(0), col(1), col(2), col(3), tab, tab, tab, const(dmat), const(dec), const(norm_g)],
        out_specs=pl.BlockSpec((tb, w), lambda bi, t: (bi * nb + jnp.maximum(t - nb, 0), 0)),
        out_shape=jax.ShapeDtypeStruct((b * s, w), BF16),
        scratch_shapes=[pltpu.VMEM((MIXER_HEADS, HEAD_DIM, HEAD_DIM), F32),
                        pltpu.VMEM((MIXER_HEADS, HEAD_DIM, HEAD_DIM), F32),
                        pltpu.VMEM((s // RET_CHUNK, MIXER_HEADS, HEAD_DIM, HEAD_DIM), F32)],
        compiler_params=_cparams("arbitrary", "arbitrary"),
    )(p_b, p_b, p_b, p_b, cos_t, sin_lo, sin_hi, dmat, dec, norm_g)


def _halo_specs(tile, width, n_rows, block_of):
    per = tile // HALO
    last = n_rows // HALO - 1
    main = pl.BlockSpec((tile, width), lambda *g: (block_of(*g), 0))
    prev = pl.BlockSpec((HALO, width), lambda *g: (jnp.maximum(block_of(*g) * per - 1, 0), 0))
    nxt = pl.BlockSpec((HALO, width), lambda *g: (jnp.minimum((block_of(*g) + 1) * per, last), 0))
    return main, prev, nxt


def _fill_padded(pad_ref, main, prev, nxt, first, last):
    t = main.shape[0]
    pad_ref[0:HALO, :] = jnp.where(first, 0.0, prev)
    pad_ref[HALO:HALO + t, :] = main
    pad_ref[HALO + t:2 * HALO + t, :] = jnp.where(last, 0.0, nxt)


def _pool_kernel(s, tiles_per_seq, u_ref, prev_ref, next_ref, w_ref, scale_ref, o_ref, pad_ref):
    t = POOL_TILE
    tseq = pl.program_id(0) % tiles_per_seq
    u = u_ref[...]
    _fill_padded(pad_ref, u, prev_ref[...], next_ref[...], tseq == 0, tseq == tiles_per_seq - 1)
    lane = lax.broadcasted_iota(jnp.int32, (1, MIXER_WIDTH), 1)
    half = jnp.full((1, MIXER_WIDTH), POOL_WINDOWS[0] // 2, jnp.int32)
    for gi in range(1, len(POOL_WINDOWS)):
        half = jnp.where(lane >= gi * POOL_GROUP, POOL_WINDOWS[gi] // 2, half)
    max_half = POOL_WINDOWS[-1] // 2
    acc = jnp.zeros((t, MIXER_WIDTH), F32)
    for d in range(-max_half, max_half):
        inside = (d >= -half) & (d < half)
        acc = acc + jnp.where(inside, pad_ref[HALO + d:HALO + d + t, :], 0.0)
    pos = tseq * t + lax.broadcasted_iota(jnp.int32, (t, 1), 0)
    count = jnp.minimum(pos + half, s) - jnp.maximum(pos - half, 0)
    diff = acc / count.astype(F32) - u
    o_ref[...] = (_dot(diff.astype(BF16), w_ref[...]) * scale_ref[...]).astype(BF16)


def _pool(p_c, w_blockdiag, scale, b, s):
    t = POOL_TILE
    tiles_per_seq = s // t
    main, prev, nxt = _halo_specs(t, MIXER_WIDTH, b * s, lambda i: i)
    const = lambda a: pl.BlockSpec(a.shape, lambda i: (0,) * a.ndim)
    return pl.pallas_call(
        functools.partial(_pool_kernel, s, tiles_per_seq),
        grid=(b * tiles_per_seq,),
        in_specs=[main, prev, nxt, const(w_blockdiag), const(scale)],
        out_specs=pl.BlockSpec((t, MIXER_WIDTH), lambda i: (i, 0)),
        out_shape=jax.ShapeDtypeStruct((b * s, MIXER_WIDTH), BF16),
        scratch_shapes=[pltpu.VMEM((t + 2 * HALO, MIXER_WIDTH), F32)],
        compiler_params=_cparams("parallel"),
    )(p_c, p_c, p_c, w_blockdiag, scale)


def _head_blockdiag(x):
    lane = lax.broadcasted_iota(jnp.int32, x.shape, 1)
    zero = jnp.zeros_like(x)
    return jnp.concatenate([jnp.where(lane < HEAD_DIM, x, zero), jnp.where(lane >= HEAD_DIM, x, zero)], axis=0)


def _gdn_pre_kernel(tiles_per_seq, x_ref, prev_ref, next_ref, ab_ref, cw_ref, alog_ref, dtb_ref, ones_ref,
                    u_ref, gb_ref, pad_ref):
    t, w = GDN_PRE_TILE, MIXER_WIDTH
    tseq = pl.program_id(0) % tiles_per_seq
    _fill_padded(pad_ref, x_ref[...], prev_ref[...], next_ref[...], tseq == 0, tseq == tiles_per_seq - 1)
    acc = jnp.zeros((t, 3 * w), F32)
    for k in range(CONV_K):
        off = HALO + k - CONV_K // 2
        acc = acc + pad_ref[off:off + t, :] * cw_ref[k:k + 1, :]
    u = _silu(acc)
    head_sum = lambda a: sum(_dot(piece, ones_ref[...]) for piece in _split_bf16(a, 2))
    q, k = u[:, 0:w], u[:, w:2 * w]
    u_ref[:, 0:w] = q * lax.rsqrt(head_sum(q * q) + EPS) * (HEAD_DIM ** -0.5)
    u_ref[:, w:2 * w] = k * lax.rsqrt(head_sum(k * k) + EPS)
    u_ref[:, 2 * w:3 * w] = u[:, 2 * w:3 * w]
    ab = ab_ref[...]
    x = ab + dtb_ref[...]
    g = -jnp.exp(alog_ref[...]) * (jnp.maximum(x, 0.0) + jnp.log(1.0 + jnp.exp(-jnp.abs(x))))
    beta = 1.0 / (1.0 + jnp.exp(-ab))
    lane = lax.broadcasted_iota(jnp.int32, ab.shape, 1)
    gb_ref[...] = jnp.where(lane < 2 * MIXER_HEADS, g, beta)


def _gdn_pre(p_d, p_ab, conv_w, alog_vec, dtb_vec, b, s):
    t = GDN_PRE_TILE
    w = MIXER_WIDTH
    tiles_per_seq = s // t
    ones = _block_diag(jnp.ones((MIXER_HEADS, HEAD_DIM, HEAD_DIM), BF16))
    main, prev, nxt = _halo_specs(t, 3 * w, b * s, lambda i: i)
    const = lambda a: pl.BlockSpec(a.shape, lambda i: (0,) * a.ndim)
    row = lambda c: pl.BlockSpec((t, c), lambda i: (i, 0))
    return pl.pallas_call(
        functools.partial(_gdn_pre_kernel, tiles_per_seq),
        grid=(b * tiles_per_seq,),
        in_specs=[main, prev, nxt, row(LANES), const(conv_w), const(alog_vec), const(dtb_vec), const(ones)],
        out_specs=[row(3 * w), row(LANES)],
        out_shape=[jax.ShapeDtypeStruct((b * s, 3 * w), F32), jax.ShapeDtypeStruct((b * s, LANES), F32)],
        scratch_shapes=[pltpu.VMEM((t + 2 * HALO, 3 * w), F32)],
        compiler_params=_cparams("parallel"),
    )(p_d, p_d, p_d, p_ab, conv_w, alog_vec, dtb_vec, ones)


def _gdn_prepare(forward, u, gb, tri_col, tri_row):
    c, dh, w = GDN_CHUNK, HEAD_DIM, MIXER_WIDTH
    t = u.shape[0]
    a_off, b_off = (0, 2 * MIXER_HEADS) if forward else (MIXER_HEADS, 3 * MIXER_HEADS)
    g_all = beta_all = gb
    g_parts = _split_bf16(g_all, 3)
    chunk_rows = [slice(ci * c, (ci + 1) * c) for ci in range(t // c)]
    gc_col = jnp.concatenate([sum(_dot(tri_col, p[r]) for p in g_parts) for r in chunk_rows], axis=0)
    gc_row = jnp.concatenate([sum(_dot_tn(p[r], tri_row) for p in g_parts) for r in chunk_rows], axis=1)
    ii = lax.broadcasted_iota(jnp.int32, (c, 2 * dh), 0)
    lane = lax.broadcasted_iota(jnp.int32, (c, 2 * dh), 1)
    first = lane < dh
    jj = jnp.where(first, lane, lane - dh)
    incl = (ii >= jj) if forward else (ii <= jj)
    strict = (ii > jj) if forward else (ii < jj)
    last = c - 1 if forward else 0
    items = [(ci, p) for ci in range(t // c) for p in range(MIXER_HEADS // 2)]
    rows = lambda ci: slice(ci * c, (ci + 1) * c)
    part = lambda base: [u[rows(ci), base + 2 * p * dh:base + 2 * (p + 1) * dh] for ci, p in items]
    col_pair = lambda x, off: [jnp.where(first, x[rows(ci), off + 2 * p:off + 2 * p + 1],
                                         x[rows(ci), off + 2 * p + 1:off + 2 * p + 2]) for ci, p in items]
    q, k, v = part(0), part(w), part(2 * w)
    gcc = col_pair(gc_col, a_off)
    beta = col_pair(beta_all, b_off)
    gcr = [jnp.concatenate([gc_row[a_off + 2 * p:a_off + 2 * p + 1, rows(ci)],
                            gc_row[a_off + 2 * p + 1:a_off + 2 * p + 2, rows(ci)]], axis=1) for ci, p in items]
    g_last = [a[last:last + 1, :] for a in gcc]
    decay = [jnp.where(incl, jnp.exp(jnp.where(incl, a - b, 0.0)), 0.0) for a, b in zip(gcc, gcr)]
    kq = [_dot_nt(jnp.concatenate([kk, qq], axis=0).astype(BF16), _head_blockdiag(kk.astype(BF16)))
          for kk, qq in zip(k, q)]
    lmat = [jnp.where(strict, m[:c] * b * d, 0.0) for m, b, d in zip(kq, beta, decay)]
    attn = [(m[c:] * d).astype(BF16) for m, d in zip(kq, decay)]
    xinv = [-a for a in lmat]
    lp = [a.astype(BF16) for a in lmat]
    lp_bd = [_head_blockdiag(a) for a in lp]
    for _ in range(int(math.log2(c)) - 1):
        lp32 = [_dot(a, bd) for a, bd in zip(lp, lp_bd)]
        lp = [a.astype(BF16) for a in lp32]
        lp_bd = [_head_blockdiag(a) for a in lp]
        xinv = [xi + a32 + _dot(xi.astype(BF16), bd) for xi, a32, bd in zip(xinv, lp32, lp_bd)]
    egc = [jnp.exp(a) for a in gcc]
    xb = [a.astype(BF16) for a in xinv]
    vbeta = [vv * b for vv, b in zip(v, beta)]
    kbeta = [kk * (b * e) for kk, b, e in zip(k, beta, egc)]
    u_val = [r + _dot(xi, _head_blockdiag(r.astype(BF16))) for xi, r in zip(xb, vbeta)]
    w_dec = [r + _dot(xi, _head_blockdiag(r.astype(BF16))) for xi, r in zip(xb, kbeta)]
    wq = [jnp.concatenate([wd, qq * e], axis=0).astype(BF16) for wd, qq, e in zip(w_dec, q, egc)]
    k_state = [(kk * jnp.exp(gl - a)).astype(BF16) for kk, gl, a in zip(k, g_last, gcc)]
    e_last = [jnp.exp(gl) for gl in g_last]
    keyed = lambda vals: dict(zip(items, vals))
    return dict(wq=keyed(wq), u_val=keyed(u_val), attn=keyed(attn), k_state=keyed(k_state), e_last=keyed(e_last))


def _gdn_kernel(uf_ref, gbf_ref, ub_ref, gbb_ref, tril_ref, triu_ref, of_ref, ob_ref, sf_ref, sb_ref):
    n = pl.program_id(1)
    t = uf_ref.shape[0]
    c, dh = GDN_CHUNK, HEAD_DIM
    nch = t // c
    pairs = MIXER_HEADS // 2

    @pl.when(n == 0)
    def _():
        sf_ref[...] = jnp.zeros_like(sf_ref)
        sb_ref[...] = jnp.zeros_like(sb_ref)

    tril, triu = tril_ref[...], triu_ref[...]
    prep_f = _gdn_prepare(True, uf_ref[...], gbf_ref[...], tril, triu)
    prep_b = _gdn_prepare(False, ub_ref[...], gbb_ref[...], triu, tril)

    first = lax.broadcasted_iota(jnp.int32, (dh, 2 * dh), 1) < dh
    chains = [(prep_f, sf_ref, of_ref, p, True) for p in range(pairs)] + \
             [(prep_b, sb_ref, ob_ref, p, False) for p in range(pairs)]
    state = [s_ref[p] for _, s_ref, _, p, _ in chains]
    for step in range(nch):
        key = [(step if fw else nch - 1 - step, p) for _, _, _, p, fw in chains]
        ws = [_dot(pr['wq'][kk], _head_blockdiag(st.astype(BF16))) for (pr, *_), kk, st in zip(chains, key, state)]
        v_new = [pr['u_val'][kk] - a[:c] for (pr, *_), kk, a in zip(chains, key, ws)]
        vb = [a.astype(BF16) for a in v_new]
        out = [a[c:] + _dot(pr['attn'][kk], _head_blockdiag(b)) for (pr, *_), kk, a, b in zip(chains, key, ws, vb)]
        cross = [_dot_tn(pr['k_state'][kk], b) for (pr, *_), kk, b in zip(chains, key, vb)]
        state = [st * pr['e_last'][kk] + jnp.where(first, x[:dh], x[dh:])
                 for (pr, *_), kk, st, x in zip(chains, key, state, cross)]
        for (_, _, o_ref, p, _), (ci, _), o in zip(chains, key, out):
            o_ref[ci * c:(ci + 1) * c, 2 * p * dh:2 * (p + 1) * dh] = o
    for (_, s_ref, _, p, _), st in zip(chains, state):
        s_ref[p] = st


def _gated_deltanet(p_d, p_ab, conv_w, alog_vec, dtb_vec, b, s):
    u, gb = _gdn_pre(p_d, p_ab, conv_w, alog_vec, dtb_vec, b, s)
    c = GDN_CHUNK
    t = c * GDN_CHUNKS_PER_STEP
    nb = s // t
    w = MIXER_WIDTH
    r = np.arange(c)
    tril = jnp.asarray(r[:, None] >= r[None, :], BF16)
    triu = jnp.asarray(r[:, None] <= r[None, :], BF16)
    fwd = lambda bi, n: (bi * nb + n, 0)
    bwd = lambda bi, n: (bi * nb + nb - 1 - n, 0)
    const = lambda a: pl.BlockSpec(a.shape, lambda bi, n: (0,) * a.ndim)
    return pl.pallas_call(
        _gdn_kernel,
        grid=(b, nb),
        in_specs=[pl.BlockSpec((t, 3 * w), fwd), pl.BlockSpec((t, LANES), fwd),
                  pl.BlockSpec((t, 3 * w), bwd), pl.BlockSpec((t, LANES), bwd), const(tril), const(triu)],
        out_specs=[pl.BlockSpec((t, w), fwd), pl.BlockSpec((t, w), bwd)],
        out_shape=[jax.ShapeDtypeStruct((b * s, w), F32), jax.ShapeDtypeStruct((b * s, w), F32)],
        scratch_shapes=[pltpu.VMEM((MIXER_HEADS // 2, HEAD_DIM, 2 * HEAD_DIM), F32),
                        pltpu.VMEM((MIXER_HEADS // 2, HEAD_DIM, 2 * HEAD_DIM), F32)],
        compiler_params=_cparams("arbitrary", "arbitrary"),
    )(u, gb, u, gb, tril, triu)


def _mem_kv_kernel(m_ref, g_ref, wk_ref, wv_ref, k_ref, v_ref):
    h = _rms(m_ref[0], g_ref[...]).astype(BF16)
    k_ref[0] = _dot(h, wk_ref[...]).astype(BF16)
    v_ref[0] = _dot(h, wv_ref[...]).astype(BF16)


def _mem_kv(mem, g, wk, wv):
    b, nm, d = mem.shape
    const = lambda a: pl.BlockSpec(a.shape, lambda i: (0,) * a.ndim)
    blk = pl.BlockSpec((1, nm, d), lambda i: (i, 0, 0))
    return pl.pallas_call(
        _mem_kv_kernel, grid=(b,),
        in_specs=[blk, const(g), const(wk), const(wv)], out_specs=[blk, blk],
        out_shape=[jax.ShapeDtypeStruct((b, nm, d), BF16)] * 2,
        compiler_params=_cparams("parallel"),
    )(mem, g, wk, wv)


def _route_lanes(lg):
    lane_i = lax.broadcasted_iota(jnp.int32, lg.shape, 1)
    lane = lane_i.astype(F32)
    first_at = lambda vals, m: jnp.min(jnp.where(vals == m, lane, float(LANES)), axis=-1, keepdims=True)
    g_mask = lane_i < N_GROUPS
    gl = jnp.where(g_mask, lg, NEG_BIG)
    g_max = jnp.max(gl, axis=-1, keepdims=True)
    g_sum = jnp.sum(jnp.where(g_mask, jnp.exp(gl - g_max), 0.0), axis=-1, keepdims=True)
    g_p = 1.0 / g_sum
    g_idx = first_at(gl, g_max)
    lo = N_GROUPS + EXPERTS_PER_GROUP * g_idx
    e_mask = (lane >= lo) & (lane < lo + EXPERTS_PER_GROUP)
    el = jnp.where(e_mask, lg, NEG_BIG)
    m1 = jnp.max(el, axis=-1, keepdims=True)
    i1 = first_at(el, m1)
    el2 = jnp.where(lane == i1, NEG_BIG, el)
    m2 = jnp.max(el2, axis=-1, keepdims=True)
    i2 = first_at(el2, m2)
    r = jnp.exp(m2 - m1)
    w1 = g_p / (1.0 + r)
    w2 = g_p * r / (1.0 + r)
    e1, e2 = i1 - N_GROUPS, i2 - N_GROUPS
    e_lo, e_hi = jnp.minimum(e1, e2), jnp.maximum(e1, e2)
    w_lo, w_hi = jnp.where(e1 < e2, w1, w2), jnp.where(e1 < e2, w2, w1)
    a, b = e_lo - EXPERTS_PER_GROUP * g_idx, e_hi - EXPERTS_PER_GROUP * g_idx
    cls = g_idx * PAIRS_PER_GROUP + a * EXPERTS_PER_GROUP - a * (a + 1.0) * 0.5 + (b - a - 1.0)
    out = jnp.where(lane_i == 0, e_lo, 0.0)
    out = jnp.where(lane_i == 1, e_hi, out)
    out = jnp.where(lane_i == 2, w_lo, out)
    out = jnp.where(lane_i == 3, w_hi, out)
    out = jnp.where(lane_i == 4, cls, out)
    return out, jnp.where(lane == cls, 1.0, 0.0)


def _post_kernel(x_ref, ya_ref, yb_ref, yc_ref, of_ref, ob_ref, gate_ref, seg_ref, gg_ref, wout_ref, gq_ref,
                 wq_ref, k_ref, v_ref, wo_ref, gf_ref, wr_ref, br_ref, xo_ref, hx_ref, rt_ref, hist_ref):
    w = MIXER_WIDTH
    o = of_ref[...] + ob_ref[...]
    ms = _dot((o * o).astype(BF16), seg_ref[...])
    yd = o * lax.rsqrt(ms + EPS) * gg_ref[...] * _silu(gate_ref[...])
    mix = _dot(ya_ref[...], wout_ref[0:w, :]) + _dot(yb_ref[...], wout_ref[w:2 * w, :])
    mix = mix + _dot(yc_ref[...], wout_ref[2 * w:3 * w, :]) + _dot(yd.astype(BF16), wout_ref[3 * w:4 * w, :])
    x1 = x_ref[...] + mix
    q = _dot(_rms(x1, gq_ref[...]).astype(BF16), wq_ref[...]).astype(BF16)
    cols = [slice(h * MEM_HEAD_DIM, (h + 1) * MEM_HEAD_DIM) for h in range(MEM_HEADS)]
    sc = [_dot_nt(q[:, sl], k_ref[0, :, sl]) * (MEM_HEAD_DIM ** -0.5) for sl in cols]
    p = [jnp.exp(a - jnp.max(a, axis=-1, keepdims=True)) for a in sc]
    l = [jnp.sum(a, axis=-1, keepdims=True) for a in p]
    heads = [(_dot(a.astype(BF16), v_ref[0, :, sl]) / b).astype(BF16) for a, b, sl in zip(p, l, cols)]
    x2 = x1 + _dot(jnp.concatenate(heads, axis=-1), wo_ref[...])
    xo_ref[...] = x2
    hx = _rms(x2, gf_ref[...])
    _to_row_tiles(hx_ref, hx)
    lg = _dot(hx.astype(BF16), wr_ref[...]) + br_ref[...]
    route, chosen = _route_lanes(lg)
    rt_ref[...] = route
    hist_ref[...] = jnp.broadcast_to(jnp.sum(chosen, axis=0, keepdims=True), hist_ref.shape)


def _post(x, y_a, y_b, y_c, o_f, o_b, p_d, seg, gdn_g, w_out, g_q, w_q, k_mem, v_mem, w_o, g_f, w_r, b_r, s):
    n = x.shape[0]
    tm = POST_TILE
    w = MIXER_WIDTH
    per_seq = s // tm
    row = lambda c: pl.BlockSpec((tm, c), lambda i: (i, 0))
    const = lambda a: pl.BlockSpec(a.shape, lambda i: (0,) * a.ndim)
    kv = pl.BlockSpec((1,) + k_mem.shape[1:], lambda i: (i // per_seq, 0, 0))
    return pl.pallas_call(
        _post_kernel, grid=(n // tm,),
        in_specs=[row(D_MODEL), row(w), row(w), row(w), row(w), row(w),
                  pl.BlockSpec((tm, w), lambda i: (i, 3)), const(seg), const(gdn_g), const(w_out), const(g_q),
                  const(w_q), kv, kv, const(w_o), const(g_f), const(w_r), const(b_r)],
        out_specs=[row(D_MODEL), pl.BlockSpec((tm * ROW_TILES, LANES), lambda i: (i, 0)), row(LANES),
                   pl.BlockSpec((SUBLANES, LANES), lambda i: (i, 0))],
        out_shape=[jax.ShapeDtypeStruct((n, D_MODEL), F32), jax.ShapeDtypeStruct((n * ROW_TILES, LANES), F32),
                   jax.ShapeDtypeStruct((n, LANES), F32), jax.ShapeDtypeStruct((n // tm * SUBLANES, LANES), F32)],
        compiler_params=_cparams("parallel"),
    )(x, y_a, y_b, y_c, o_f, o_b, p_d, seg, gdn_g, w_out, g_q, w_q, k_mem, v_mem, w_o, g_f, w_r, b_r)


def _pair_class_experts():
    lo, hi = [], []
    for g in range(N_GROUPS):
        for a in range(EXPERTS_PER_GROUP):
            for b in range(a + 1, EXPERTS_PER_GROUP):
                lo.append(g * EXPERTS_PER_GROUP + a)
                hi.append(g * EXPERTS_PER_GROUP + b)
    return np.asarray(lo, np.int32), np.asarray(hi, np.int32)


def _dispatch(route, hist):
    n = route.shape[0]
    blk = MOE_BLOCK
    counts = jnp.sum(hist.reshape(-1, SUBLANES, LANES)[:, 0, :N_CLASSES], axis=0).astype(jnp.int32)
    padded = (counts + blk - 1) // blk * blk
    pad_end = jnp.cumsum(padded)
    n_blocks = n // blk + N_CLASSES
    blk_start = jnp.arange(n_blocks, dtype=jnp.int32) * blk
    blk_cls = jnp.minimum(jnp.sum(pad_end[None, :] <= blk_start[:, None], axis=1), N_CLASSES - 1).astype(jnp.int32)
    cls_lo, cls_hi = _pair_class_experts()
    in_blk = jnp.arange(blk, dtype=jnp.int32)[None, :]
    pad_key = jnp.where(in_blk < (padded - counts)[:, None], jnp.arange(N_CLASSES, dtype=jnp.int32)[:, None], N_CLASSES)
    keys = jnp.concatenate([route[:, 4].astype(jnp.int32), pad_key.reshape(-1)])
    toks = jnp.concatenate([jnp.arange(n, dtype=jnp.int32), jnp.tile(n + in_blk[0], N_CLASSES)])
    zeros = jnp.zeros((N_CLASSES * blk,), F32)
    w_lo = jnp.concatenate([route[:, TOP_K], zeros])
    w_hi = jnp.concatenate([route[:, TOP_K + 1], zeros])
    _, row_tok, w_lo, w_hi = lax.sort((keys, toks, w_lo, w_hi), num_keys=1)
    n_used = (pad_end[-1] // blk).astype(jnp.int32).reshape(1)
    return (n_used, jnp.asarray(cls_lo)[blk_cls], jnp.asarray(cls_hi)[blk_cls],
            jnp.concatenate([n + in_blk[0], row_tok]), jnp.stack([w_lo, w_hi], axis=-1))


def _moe_kernel(n_tokens, n_blocks, nu_ref, ea_ref, eb_ref, dst_ref, hx_hbm, rw_ref, wga_ref, wua_ref, wda_ref,
                wgb_ref, wub_ref, wdb_ref, y_hbm, xbuf, ybuf, gsem, ssem):
    j = pl.program_id(0)
    blk, rt = MOE_BLOCK, ROW_TILES
    n_used = nu_ref[0]
    slot = lax.rem(j, MOE_BUFFERS)
    ahead = lax.rem(j + 2, MOE_BUFFERS)
    tile = lambda ref, i: ref.at[pl.ds(pl.multiple_of(i * rt, rt), rt), :]
    dst = lambda jj, i: dst_ref[(jj + 1) * blk + i]
    clamp = lambda jj: jnp.minimum(jj, n_blocks - 1)

    def gather_row(jj, sl, i):
        tok = jnp.minimum(dst(jj, i), n_tokens - 1)
        pltpu.make_async_copy(tile(hx_hbm, tok), tile(xbuf.at[sl], i), gsem.at[sl]).start()

    def scatter_row(jj, sl, i):
        pltpu.make_async_copy(tile(ybuf.at[sl], i), tile(y_hbm, dst(jj, i)), ssem.at[sl]).start()

    def wait_gather(sl):
        pltpu.make_async_copy(hx_hbm.at[pl.ds(0, blk * rt), :], xbuf.at[sl], gsem.at[sl]).wait()

    def wait_scatter(sl):
        pltpu.make_async_copy(ybuf.at[sl], y_hbm.at[pl.ds(0, blk * rt), :], ssem.at[sl]).wait()

    def rolled(fn):
        lax.fori_loop(0, blk, lambda i, carry: (fn(i), carry)[1], 0, unroll=DMA_UNROLL)

    @pl.when(j == 0)
    def _():
        ybuf[MOE_BUFFERS - 1] = jnp.zeros((blk * rt, LANES), F32)
        rolled(lambda i: gather_row(0, 0, i))
        rolled(lambda i: gather_row(clamp(1), 1, i))

    @pl.when(j < n_used)
    def _():
        wait_gather(slot)

        @pl.when(j >= 2)
        def _():
            wait_scatter(slot)

        def issue_copies(piece):
            for i in range(piece * blk // MOE_PIECES, (piece + 1) * blk // MOE_PIECES):
                gather_row(clamp(j + 2), ahead, i)
                scatter_row(j - 1, ahead, i)

        xs = _from_row_tiles(xbuf.at[slot], 0, blk, rt).astype(BF16)
        half = D_EXPERT // 2
        hidden = []
        for e, (wg_ref, wu_ref) in enumerate(((wga_ref, wua_ref), (wgb_ref, wub_ref))):
            acts = []
            for c in range(2):
                cols = slice(c * half, (c + 1) * half)
                g = _dot(xs, wg_ref[0, 0, :, cols])
                issue_copies(4 * e + 2 * c)
                u = _dot(xs, wu_ref[0, 0, :, cols])
                issue_copies(4 * e + 2 * c + 1)
                acts.append((_silu(g) * u).astype(BF16))
            hidden.append(jnp.concatenate(acts, axis=1))
        w_a, w_b = rw_ref[:, 0:1], rw_ref[:, 1:2]
        quarter = D_MODEL // 4
        for c in range(4):
            cols = slice(c * quarter, (c + 1) * quarter)
            y = w_a * _dot(hidden[0], wda_ref[0, 0, :, cols]) + w_b * _dot(hidden[1], wdb_ref[0, 0, :, cols])
            for k in range(quarter // LANES):
                ybuf[slot, pl.ds(c * (quarter // LANES) + k, blk, stride=rt), :] = y[:, k * LANES:(k + 1) * LANES]
            issue_copies(8 + c)

    @pl.when(j == n_used)
    def _():
        after = lax.rem(j + 1, MOE_BUFFERS)
        wait_gather(slot)
        wait_gather(after)

        @pl.when(j >= 2)
        def _():
            wait_scatter(slot)

        wait_scatter(after)
        rolled(lambda i: scatter_row(j - 1, ahead, i))
        wait_scatter(ahead)


def _moe(hx_tiles, n_used, blk_ea, blk_eb, row_tok, row_w, w_gate, w_up, w_down, layer, n):
    blk = MOE_BLOCK
    n_blocks = blk_ea.shape[0]
    block = lambda j: jnp.minimum(j, n_blocks - 1)
    first = lambda j, nu, ea, eb, dst: (layer, ea[block(j)], 0, 0)
    second = lambda j, nu, ea, eb, dst: (layer, eb[block(j)], 0, 0)
    up, down = (1, 1, D_MODEL, D_EXPERT), (1, 1, D_EXPERT, D_MODEL)
    weights = pl.BlockSpec
    grid_spec = pltpu.PrefetchScalarGridSpec(
        num_scalar_prefetch=4, grid=(n_blocks + 1,),
        in_specs=[pl.BlockSpec(memory_space=pl.ANY),
                  pl.BlockSpec((blk, TOP_K), lambda j, nu, ea, eb, dst: (block(j), 0)),
                  weights(up, first), weights(up, first), weights(down, first),
                  weights(up, second), weights(up, second), weights(down, second)],
        out_specs=pl.BlockSpec(memory_space=pl.ANY),
        scratch_shapes=[pltpu.VMEM((MOE_BUFFERS, blk * ROW_TILES, LANES), F32),
                        pltpu.VMEM((MOE_BUFFERS, blk * ROW_TILES, LANES), F32),
                        pltpu.SemaphoreType.DMA((MOE_BUFFERS,)), pltpu.SemaphoreType.DMA((MOE_BUFFERS,))])
    return pl.pallas_call(
        functools.partial(_moe_kernel, n, n_blocks), grid_spec=grid_spec,
        out_shape=jax.ShapeDtypeStruct(((n + blk) * ROW_TILES, LANES), F32),
        compiler_params=_cparams("arbitrary"),
    )(n_used, blk_ea, blk_eb, row_tok, hx_tiles, row_w, w_gate, w_up, w_down, w_gate, w_up, w_down)


def _final_kernel(x_ref, y_ref, g_ref, o_ref):
    o_ref[...] = _rms(_add_moe(x_ref[...], y_ref), g_ref[...])


def _final(x, moe, g):
    n = x.shape[0]
    tm = PROJ_TILE
    return pl.pallas_call(
        _final_kernel, grid=(n // tm,),
        in_specs=[pl.BlockSpec((tm, D_MODEL), lambda i: (i, 0)), _moe_out_spec(tm),
                  pl.BlockSpec(g.shape, lambda i: (0, 0))],
        out_specs=pl.BlockSpec((tm, D_MODEL), lambda i: (i, 0)),
        out_shape=jax.ShapeDtypeStruct((n, D_MODEL), F32),
        compiler_params=_cparams("parallel"),
    )(x, moe, g)


def _lane_vec(v, width=LANES):
    v = v.reshape(1, -1).astype(F32)
    return jnp.pad(v, ((0, 0), (0, width - v.shape[1])))


def _block_diag(blocks):
    g, c, _ = blocks.shape
    eye = jnp.eye(g, dtype=blocks.dtype)
    return (eye[:, None, :, None] * blocks[:, :, None, :]).reshape(g * c, g * c)


def kernel(x, mem, mix_norm_g, w_in, na_rpb, ret_norm_g, pool_w, pool_scale, gdn_conv_w, gdn_a_log, gdn_dt_bias, gdn_norm_g, w_out, mem_q_norm_g, mem_kv_norm_g, mem_w_q, mem_w_k, mem_w_v, mem_w_o, ffn_norm_g, w_group, b_group, w_router, b_router, w_gate, w_up, w_down, final_norm_g):
    b, s, d = x.shape
    n = b * s
    depth = w_in.shape[0]
    row = lambda v: v.reshape(1, -1).astype(F32)
    ret_tables = _retention_tables(s)
    seg = _block_diag(jnp.full((MIXER_HEADS, HEAD_DIM, HEAD_DIM), 1.0 / HEAD_DIM, BF16))
    wg_bf, wu_bf, wd_bf = w_gate.astype(BF16), w_up.astype(BF16), w_down.astype(BF16)
    xs = x.reshape(n, d)
    moe = None
    for l in range(depth):
        w_pad = jnp.pad(w_in[l], ((0, 0), (0, P_IN_PAD - w_in.shape[2]))).astype(BF16)
        xs, p_a, p_b, p_c, p_d, p_ab = _norm_proj(xs, moe, row(mix_norm_g[l]), w_pad)
        y_a = _neighbourhood_attention(p_a, _na_bias_table(na_rpb[l]), b, s)
        y_b = _retention(p_b, row(ret_norm_g[l]), ret_tables, b, s)
        y_c = _pool(p_c, _block_diag(pool_w[l]).astype(BF16), row(pool_scale[l]), b, s)
        o_f, o_b = _gated_deltanet(p_d, p_ab, gdn_conv_w[l].astype(F32), _lane_vec(gdn_a_log[l]),
                                   _lane_vec(gdn_dt_bias[l]), b, s)
        k_mem, v_mem = _mem_kv(mem, row(mem_kv_norm_g[l]), mem_w_k[l].astype(BF16), mem_w_v[l].astype(BF16))
        w_r = jnp.pad(jnp.concatenate([w_group[l], w_router[l]], axis=1),
                      ((0, 0), (0, LANES - N_GROUPS - N_EXPERTS))).astype(F32)
        b_r = _lane_vec(jnp.concatenate([b_group[l], b_router[l]]))
        xs, hx_tiles, route, hist = _post(
            xs, y_a, y_b, y_c, o_f, o_b, p_d, seg, row(jnp.tile(gdn_norm_g[l], MIXER_HEADS)), w_out[l].astype(BF16),
            row(mem_q_norm_g[l]), mem_w_q[l].astype(BF16), k_mem, v_mem, mem_w_o[l].astype(BF16),
            row(ffn_norm_g[l]), w_r.astype(BF16), b_r, s)
        moe = _moe(hx_tiles, *_dispatch(route, hist), wg_bf, wu_bf, wd_bf, l, n)
    return _final(xs, moe, row(final_norm_g)).reshape(b, s, d)
```

```python
import functools
import math

import jax
import jax.numpy as jnp
import numpy as np
from jax import lax
from jax.experimental import pallas as pl
from jax.experimental.pallas import tpu as pltpu

F32 = jnp.float32
BF16 = jnp.bfloat16
HIGHEST = lax.Precision.HIGHEST

D_MODEL = 1024
GRID_W = 64
HEAD_DIM = 64
MIXER_WIDTH = 256
MIXER_HEADS = 4
NA_KH = 8
NA_KW = 16
RET_CHUNK = 128
ROPE_BASE = 10000.0
POOL_WINDOWS = (2, 4, 8, 16)
POOL_GROUP = 64
GDN_CHUNK = 64
CONV_K = 4
MEM_HEADS = 4
MEM_HEAD_DIM = 256
N_GROUPS = 4
EXPERTS_PER_GROUP = 8
N_EXPERTS = 32
TOP_K = 2
D_EXPERT = 512
EPS = 1e-6

P_IN_PAD = 3200
V7X_VMEM_LIMIT = 56 * 1024 * 1024
LANES = 128
SUBLANES = 8
ROW_TILES = D_MODEL // LANES
NEG_BIG = -1e30

PROJ_TILE = 512
POOL_TILE = 512
POST_TILE = 512
NA_ROWS_PER_STEP = 8
NA_ROWS_LOCKSTEP = 4
GDN_CHUNKS_PER_STEP = 16
GDN_PRE_TILE = 512
RET_CHUNKS_PER_STEP = 8
PAIRS_PER_GROUP = EXPERTS_PER_GROUP * (EXPERTS_PER_GROUP - 1) // 2
N_CLASSES = N_GROUPS * PAIRS_PER_GROUP
MOE_BLOCK = 128
MOE_BUFFERS = 3
MOE_PIECES = 12
DMA_UNROLL = 8
HALO = 8


def _cparams(*sem):
    return pltpu.CompilerParams(dimension_semantics=sem, vmem_limit_bytes=V7X_VMEM_LIMIT)


def _dot(a, b):
    return jnp.dot(a, b, preferred_element_type=F32)


def _dot_nt(a, b):
    return lax.dot_general(a, b, (((1,), (1,)), ((), ())), preferred_element_type=F32)


def _dot_tn(a, b, precision=None):
    return lax.dot_general(a, b, (((0,), (0,)), ((), ())), preferred_element_type=F32, precision=precision)


def _silu(x):
    return x * (1.0 / (1.0 + jnp.exp(-x)))


def _rms(x, g):
    return x * lax.rsqrt(jnp.mean(x * x, axis=-1, keepdims=True) + EPS) * g


def _from_row_tiles(ref, first, count, stride):
    return jnp.concatenate([ref[pl.ds(first + k, count, stride=stride), :] for k in range(ROW_TILES)], axis=1)


def _to_row_tiles(ref, x):
    count = x.shape[0]
    for k in range(ROW_TILES):
        ref[pl.ds(k, count, stride=ROW_TILES), :] = x[:, k * LANES:(k + 1) * LANES]


def _add_moe(x, y_ref):
    return x + _from_row_tiles(y_ref, 0, x.shape[0], ROW_TILES)


def _split_bf16(x, parts):
    out = []
    for _ in range(parts):
        p = x.astype(BF16)
        out.append(p)
        x = x - p.astype(F32)
    return out


def _norm_proj_kernel(has_y, *refs):
    if has_y:
        x_ref, y_ref, g_ref, w_ref, xo_ref, pa_ref, pb_ref, pc_ref, pd_ref, pab_ref = refs
        x = _add_moe(x_ref[...], y_ref)
        xo_ref[...] = x
    else:
        x_ref, g_ref, w_ref, pa_ref, pb_ref, pc_ref, pd_ref, pab_ref = refs
        x = x_ref[...]
    h = _rms(x, g_ref[...]).astype(BF16)
    w = MIXER_WIDTH
    pa_ref[...] = _dot(h, w_ref[:, 0:3 * w]).astype(BF16)
    pb_ref[...] = _dot(h, w_ref[:, 3 * w:7 * w])
    pc_ref[...] = _dot(h, w_ref[:, 7 * w:8 * w])
    pd_ref[...] = _dot(h, w_ref[:, 8 * w:12 * w])
    pab_ref[...] = _dot(h, w_ref[:, 12 * w:P_IN_PAD])


def _moe_out_spec(tile):
    return pl.BlockSpec((tile * ROW_TILES, LANES), lambda i: (i, 0))


def _norm_proj(x, moe, g, w_pad):
    n = x.shape[0]
    tm = PROJ_TILE
    w = MIXER_WIDTH
    row = lambda c: pl.BlockSpec((tm, c), lambda i: (i, 0))
    const = lambda a: pl.BlockSpec(a.shape, lambda i: (0,) * a.ndim)
    outs = [jax.ShapeDtypeStruct((n, 3 * w), BF16), jax.ShapeDtypeStruct((n, 4 * w), F32),
            jax.ShapeDtypeStruct((n, w), F32), jax.ShapeDtypeStruct((n, 4 * w), F32),
            jax.ShapeDtypeStruct((n, LANES), F32)]
    out_specs = [row(3 * w), row(4 * w), row(w), row(4 * w), row(LANES)]
    if moe is None:
        ins, in_specs = (x, g, w_pad), [row(D_MODEL), const(g), const(w_pad)]
    else:
        ins = (x, moe, g, w_pad)
        in_specs = [row(D_MODEL), _moe_out_spec(tm), const(g), const(w_pad)]
        outs = [jax.ShapeDtypeStruct((n, D_MODEL), F32)] + outs
        out_specs = [row(D_MODEL)] + out_specs
    res = pl.pallas_call(
        functools.partial(_norm_proj_kernel, moe is not None),
        grid=(n // tm,), in_specs=in_specs, out_specs=out_specs, out_shape=outs,
        compiler_params=_cparams("parallel"),
    )(*ins)
    return res if moe is not None else [x] + list(res)


def _na_bias_table(rpb):
    qc = np.arange(GRID_W)
    kc = np.arange(GRID_W)
    d_col = np.clip(kc[None, :] - qc[:, None], -(NA_KW - 1), NA_KW - 1) + (NA_KW - 1)
    pick = np.zeros((2 * NA_KW - 1, GRID_W * GRID_W), np.float32)
    pick[d_col.reshape(-1), np.arange(GRID_W * GRID_W)] = 1.0
    win = np.clip(qc - NA_KW // 2, 0, GRID_W - NA_KW)
    ok = (kc[None, :] >= win[:, None]) & (kc[None, :] < win[:, None] + NA_KW)
    cols = jnp.einsum('hrc,cx->hrx', rpb.astype(F32), jnp.asarray(pick), precision=HIGHEST)
    cols = jnp.where(ok[None, None], cols.reshape(MIXER_HEADS, 2 * NA_KH - 1, GRID_W, GRID_W), NEG_BIG)
    per_e = [jnp.transpose(cols[:, NA_KH - 1 - e:2 * NA_KH - 1 - e], (0, 2, 1, 3)) for e in range(NA_KH)]
    return jnp.stack(per_e, axis=1).reshape(MIXER_HEADS, NA_KH, GRID_W, NA_KH * GRID_W)


def _na_kernel(rows, q_ref, k_ref, v_ref, t_ref, o_ref):
    i = pl.program_id(1)
    dh = HEAD_DIM

    def rows_body(it, carry):
        chains = []
        for k in range(NA_ROWS_LOCKSTEP):
            rr = it * NA_ROWS_LOCKSTEP + k
            r = i * NA_ROWS_PER_STEP + rr
            kr0 = jnp.clip(r - NA_KH // 2, 0, rows - NA_KH)
            q = q_ref[pl.ds(pl.multiple_of(rr * GRID_W, GRID_W), GRID_W), :]
            k0 = pl.multiple_of(kr0 * GRID_W, GRID_W)
            kb = k_ref[pl.ds(k0, NA_KH * GRID_W), :]
            vb = v_ref[pl.ds(k0, NA_KH * GRID_W), :]
            for h in range(MIXER_HEADS):
                sl = slice(h * dh, (h + 1) * dh)
                chains.append((q[:, sl], kb[:, sl], vb[:, sl], h, r - kr0))
        s = [_dot_nt(q, kb) * (dh ** -0.5) + t_ref[h, e] for q, kb, _, h, e in chains]
        m = [jnp.max(a, axis=-1, keepdims=True) for a in s]
        p = [jnp.exp(a - b) for a, b in zip(s, m)]
        l = [jnp.sum(a, axis=-1, keepdims=True) for a in p]
        o = [_dot(a.astype(BF16), c[2]) / b for a, b, c in zip(p, l, chains)]
        for k in range(NA_ROWS_LOCKSTEP):
            rr = it * NA_ROWS_LOCKSTEP + k
            row = jnp.concatenate(o[k * MIXER_HEADS:(k + 1) * MIXER_HEADS], axis=-1)
            o_ref[pl.ds(pl.multiple_of(rr * GRID_W, GRID_W), GRID_W), :] = row.astype(BF16)
        return carry

    lax.fori_loop(0, NA_ROWS_PER_STEP // NA_ROWS_LOCKSTEP, rows_body, 0)


def _neighbourhood_attention(p_a, table, b, s):
    rows = s // GRID_W
    assert rows >= NA_KH and rows % NA_ROWS_PER_STEP == 0
    steps = rows // NA_ROWS_PER_STEP
    tq = NA_ROWS_PER_STEP * GRID_W
    w = MIXER_WIDTH
    return pl.pallas_call(
        functools.partial(_na_kernel, rows),
        grid=(b, steps),
        in_specs=[pl.BlockSpec((tq, w), lambda bi, i: (bi * steps + i, 0)),
                  pl.BlockSpec((s, w), lambda bi, i: (bi, 1)),
                  pl.BlockSpec((s, w), lambda bi, i: (bi, 2)),
                  pl.BlockSpec(table.shape, lambda bi, i: (0, 0, 0, 0))],
        out_specs=pl.BlockSpec((tq, w), lambda bi, i: (bi * steps + i, 0)),
        out_shape=jax.ShapeDtypeStruct((b * s, w), BF16),
        compiler_params=_cparams("parallel", "arbitrary"),
    )(p_a, p_a, p_a, table)


def _retention_tables(s):
    h, dh, c = MIXER_HEADS, HEAD_DIM, RET_CHUNK
    half = dh // 2
    inv = ROPE_BASE ** (-jnp.arange(half, dtype=F32) / half)
    ang = jnp.arange(s, dtype=F32)[:, None] * inv[None, :]
    cos, sin = jnp.cos(ang), jnp.sin(ang)
    zero = jnp.zeros_like(sin)
    cos_t = jnp.tile(jnp.concatenate([cos, cos], axis=-1), (1, h))
    sin_lo = jnp.tile(jnp.concatenate([-sin, zero], axis=-1), (1, h))
    sin_hi = jnp.tile(jnp.concatenate([zero, sin], axis=-1), (1, h))
    log_f = np.log1p(-np.exp2(-5.0 - np.arange(h, dtype=np.float64)))
    log_b = log_f[::-1]
    pos = np.arange(c, dtype=np.float64)
    diff = pos[:, None] - pos[None, :]
    dmat = np.where(diff >= 0, np.exp(log_f[:, None, None] * np.maximum(diff, 0.0)), 0.0) \
        + np.where(diff < 0, np.exp(log_b[:, None, None] * np.maximum(-diff, 0.0)), 0.0)
    lanes = lambda t: np.repeat(t.T, dh, axis=1)
    dec = np.stack([lanes(np.exp(log_f[:, None] * (pos + 1.0))),
                    lanes(np.exp(log_f[:, None] * (c - 1.0 - pos))),
                    lanes(np.exp(log_b[:, None] * (c - pos))),
                    lanes(np.exp(log_b[:, None] * pos))])
    chunk_f = [float(np.exp(v * c)) for v in log_f]
    chunk_b = [float(np.exp(v * c)) for v in log_b]
    return cos_t, sin_lo, sin_hi, jnp.asarray(dmat, F32), jnp.asarray(dec, F32), chunk_f, chunk_b


def _rotary(x, cos, sin_lo, sin_hi):
    w = x.shape[-1]
    return x * cos + pltpu.roll(x, w - HEAD_DIM // 2, 1) * sin_lo + pltpu.roll(x, HEAD_DIM // 2, 1) * sin_hi


def _retention_kernel(nb, chunk_f, chunk_b, q_ref, k_ref, v_ref, gate_ref, cos_ref, slo_ref, shi_ref,
                      dmat_ref, dec_ref, ng_ref, o_ref, sf_ref, sb_ref, sball_ref):
    t = pl.program_id(1)
    dh, c, cps = HEAD_DIM, RET_CHUNK, RET_CHUNKS_PER_STEP
    cos, slo, shi = cos_ref[...], slo_ref[...], shi_ref[...]
    kr = _rotary(k_ref[...], cos, slo, shi) * (dh ** -0.5)
    vb = v_ref[...].astype(BF16)
    items = [(ci, h) for ci in range(cps) for h in range(MIXER_HEADS)]
    part = lambda x, ci, h: x[ci * c:(ci + 1) * c, h * dh:(h + 1) * dh]
    split = lambda x: {it: part(x, *it) for it in items}

    @pl.when(t == 0)
    def _():
        sf_ref[...] = jnp.zeros_like(sf_ref)
        sb_ref[...] = jnp.zeros_like(sb_ref)

    @pl.when(t < nb)
    def _():
        blk = nb - 1 - t
        kd, v = split((kr * dec_ref[3]).astype(BF16)), split(vb)
        kv = {it: _dot_tn(kd[it], v[it]) for it in items}
        for h in range(MIXER_HEADS):
            state = sb_ref[h]
            for ci in range(cps - 1, -1, -1):
                sball_ref[blk * cps + ci, h] = state
                state = state * chunk_b[h] + kv[ci, h]
            sb_ref[h] = state

    @pl.when(t >= nb)
    def _():
        blk = t - nb
        qr = _rotary(q_ref[...], cos, slo, shi)
        q, k, v = split(qr.astype(BF16)), split(kr.astype(BF16)), split(vb)
        qf, qbk = split((qr * dec_ref[0]).astype(BF16)), split((qr * dec_ref[2]).astype(BF16))
        kd = split((kr * dec_ref[1]).astype(BF16))
        sc = {it: (_dot_nt(q[it], k[it]) * dmat_ref[it[1]]).astype(BF16) for it in items}
        o = {it: _dot(sc[it], v[it]) for it in items}
        kv = {it: _dot_tn(kd[it], v[it]) for it in items}
        ob = {it: _dot(qbk[it], sball_ref[blk * cps + it[0], it[1]].astype(BF16)) for it in items}
        of = {}
        for h in range(MIXER_HEADS):
            state = sf_ref[h]
            for ci in range(cps):
                of[ci, h] = _dot(qf[ci, h], state.astype(BF16))
                state = state * chunk_f[h] + kv[ci, h]
            sf_ref[h] = state
        o = {it: o[it] + of[it] + ob[it] for it in items}
        mu = {it: jnp.mean(o[it], axis=-1, keepdims=True) for it in items}
        oc = {it: o[it] - mu[it] for it in items}
        var = {it: jnp.mean(oc[it] * oc[it], axis=-1, keepdims=True) for it in items}
        on = {it: oc[it] * lax.rsqrt(var[it] + EPS) for it in items}
        y = jnp.concatenate([jnp.concatenate([on[ci, h] for h in range(MIXER_HEADS)], axis=-1)
                             for ci in range(cps)], axis=0)
        o_ref[...] = (y * ng_ref[...] * _silu(gate_ref[...])).astype(BF16)


def _retention(p_b, norm_g, tables, b, s):
    cos_t, sin_lo, sin_hi, dmat, dec, chunk_f, chunk_b = tables
    cps = RET_CHUNKS_PER_STEP
    tb = RET_CHUNK * cps
    nb = s // tb
    w = MIXER_WIDTH
    dec = jnp.tile(dec, (1, cps, 1))
    block = lambda t: jnp.where(t < nb, nb - 1 - t, t - nb)
    col = lambda j: pl.BlockSpec((tb, w), lambda bi, t: (bi * nb + block(t), j))
    tab = pl.BlockSpec((tb, w), lambda bi, t: (block(t), 0))
    const = lambda a: pl.BlockSpec(a.shape, lambda bi, t: (0,) * a.ndim)
    return pl.pallas_call(
        functools.partial(_retention_kernel, nb, chunk_f, chunk_b),
        grid=(b, 2 * nb),
        in_specs=[col(0), col(1), col(2), col(3), tab, tab, tab, const(dmat), const(dec), const(norm_g)],
        out_specs=pl.BlockSpec((tb, w), lambda bi, t: (bi * nb + jnp.maximum(t - nb, 0), 0)),
        out_shape=jax.ShapeDtypeStruct((b * s, w), BF16),
        scratch_shapes=[pltpu.VMEM((MIXER_HEADS, HEAD_DIM, HEAD_DIM), F32),
                        pltpu.VMEM((MIXER_HEADS, HEAD_DIM, HEAD_DIM), F32),
                        pltpu.VMEM((s // RET_CHUNK, MIXER_HEADS, HEAD_DIM, HEAD_DIM), F32)],
        compiler_params=_cparams("arbitrary", "arbitrary"),
    )(p_b, p_b, p_b, p_b, cos_t, sin_lo, sin_hi, dmat, dec, norm_g)


def _halo_specs(tile, width, n_rows, block_of):
    per = tile // HALO
    last = n_rows // HALO - 1
    main = pl.BlockSpec((tile, width), lambda *g: (block_of(*g), 0))
    prev = pl.BlockSpec((HALO, width), lambda *g: (jnp.maximum(block_of(*g) * per - 1, 0), 0))
    nxt = pl.BlockSpec((HALO, width), lambda *g: (jnp.minimum((block_of(*g) + 1) * per, last), 0))
    return main, prev, nxt


def _fill_padded(pad_ref, main, prev, nxt, first, last):
    t = main.shape[0]
    pad_ref[0:HALO, :] = jnp.where(first, 0.0, prev)
    pad_ref[HALO:HALO + t, :] = main
    pad_ref[HALO + t:2 * HALO + t, :] = jnp.where(last, 0.0, nxt)


def _pool_kernel(s, tiles_per_seq, u_ref, prev_ref, next_ref, w_ref, scale_ref, o_ref, pad_ref):
    t = POOL_TILE
    tseq = pl.program_id(0) % tiles_per_seq
    u = u_ref[...]
    _fill_padded(pad_ref, u, prev_ref[...], next_ref[...], tseq == 0, tseq == tiles_per_seq - 1)
    lane = lax.broadcasted_iota(jnp.int32, (1, MIXER_WIDTH), 1)
    half = jnp.full((1, MIXER_WIDTH), POOL_WINDOWS[0] // 2, jnp.int32)
    for gi in range(1, len(POOL_WINDOWS)):
        half = jnp.where(lane >= gi * POOL_GROUP, POOL_WINDOWS[gi] // 2, half)
    max_half = POOL_WINDOWS[-1] // 2
    acc = jnp.zeros((t, MIXER_WIDTH), F32)
    for d in range(-max_half, max_half):
        inside = (d >= -half) & (d < half)
        acc = acc + jnp.where(inside, pad_ref[HALO + d:HALO + d + t, :], 0.0)
    pos = tseq * t + lax.broadcasted_iota(jnp.int32, (t, 1), 0)
    count = jnp.minimum(pos + half, s) - jnp.maximum(pos - half, 0)
    diff = acc / count.astype(F32) - u
    o_ref[...] = (_dot(diff.astype(BF16), w_ref[...]) * scale_ref[...]).astype(BF16)


def _pool(p_c, w_blockdiag, scale, b, s):
    t = POOL_TILE
    tiles_per_seq = s // t
    main, prev, nxt = _halo_specs(t, MIXER_WIDTH, b * s, lambda i: i)
    const = lambda a: pl.BlockSpec(a.shape, lambda i: (0,) * a.ndim)
    return pl.pallas_call(
        functools.partial(_pool_kernel, s, tiles_per_seq),
        grid=(b * tiles_per_seq,),
        in_specs=[main, prev, nxt, const(w_blockdiag), const(scale)],
        out_specs=pl.BlockSpec((t, MIXER_WIDTH), lambda i: (i, 0)),
        out_shape=jax.ShapeDtypeStruct((b * s, MIXER_WIDTH), BF16),
        scratch_shapes=[pltpu.VMEM((t + 2 * HALO, MIXER_WIDTH), F32)],
        compiler_params=_cparams("parallel"),
    )(p_c, p_c, p_c, w_blockdiag, scale)


def _head_blockdiag(x):
    lane = lax.broadcasted_iota(jnp.int32, x.shape, 1)
    zero = jnp.zeros_like(x)
    return jnp.concatenate([jnp.where(lane < HEAD_DIM, x, zero), jnp.where(lane >= HEAD_DIM, x, zero)], axis=0)


def _gdn_pre_kernel(tiles_per_seq, x_ref, prev_ref, next_ref, ab_ref, cw_ref, alog_ref, dtb_ref, ones_ref,
                    u_ref, gb_ref, pad_ref):
    t, w = GDN_PRE_TILE, MIXER_WIDTH
    tseq = pl.program_id(0) % tiles_per_seq
    _fill_padded(pad_ref, x_ref[...], prev_ref[...], next_ref[...], tseq == 0, tseq == tiles_per_seq - 1)
    acc = jnp.zeros((t, 3 * w), F32)
    for k in range(CONV_K):
        off = HALO + k - CONV_K // 2
        acc = acc + pad_ref[off:off + t, :] * cw_ref[k:k + 1, :]
    u = _silu(acc)
    head_sum = lambda a: sum(_dot(piece, ones_ref[...]) for piece in _split_bf16(a, 2))
    q, k = u[:, 0:w], u[:, w:2 * w]
    u_ref[:, 0:w] = q * lax.rsqrt(head_sum(q * q) + EPS) * (HEAD_DIM ** -0.5)
    u_ref[:, w:2 * w] = k * lax.rsqrt(head_sum(k * k) + EPS)
    u_ref[:, 2 * w:3 * w] = u[:, 2 * w:3 * w]
    ab = ab_ref[...]
    x = ab + dtb_ref[...]
    g = -jnp.exp(alog_ref[...]) * (jnp.maximum(x, 0.0) + jnp.log(1.0 + jnp.exp(-jnp.abs(x))))
    beta = 1.0 / (1.0 + jnp.exp(-ab))
    lane = lax.broadcasted_iota(jnp.int32, ab.shape, 1)
    gb_ref[...] = jnp.where(lane < 2 * MIXER_HEADS, g, beta)


def _gdn_pre(p_d, p_ab, conv_w, alog_vec, dtb_vec, b, s):
    t = GDN_PRE_TILE
    w = MIXER_WIDTH
    tiles_per_seq = s // t
    ones = _block_diag(jnp.ones((MIXER_HEADS, HEAD_DIM, HEAD_DIM), BF16))
    main, prev, nxt = _halo_specs(t, 3 * w, b * s, lambda i: i)
    const = lambda a: pl.BlockSpec(a.shape, lambda i: (0,) * a.ndim)
    row = lambda c: pl.BlockSpec((t, c), lambda i: (i, 0))
    return pl.pallas_call(
        functools.partial(_gdn_pre_kernel, tiles_per_seq),
        grid=(b * tiles_per_seq,),
        in_specs=[main, prev, nxt, row(LANES), const(conv_w), const(alog_vec), const(dtb_vec), const(ones)],
        out_specs=[row(3 * w), row(LANES)],
        out_shape=[jax.ShapeDtypeStruct((b * s, 3 * w), F32), jax.ShapeDtypeStruct((b * s, LANES), F32)],
        scratch_shapes=[pltpu.VMEM((t + 2 * HALO, 3 * w), F32)],
        compiler_params=_cparams("parallel"),
    )(p_d, p_d, p_d, p_ab, conv_w, alog_vec, dtb_vec, ones)


def _gdn_prepare(forward, u, gb, tri_col, tri_row):
    c, dh, w = GDN_CHUNK, HEAD_DIM, MIXER_WIDTH
    t = u.shape[0]
    a_off, b_off = (0, 2 * MIXER_HEADS) if forward else (MIXER_HEADS, 3 * MIXER_HEADS)
    g_all = beta_all = gb
    g_parts = _split_bf16(g_all, 3)
    chunk_rows = [slice(ci * c, (ci + 1) * c) for ci in range(t // c)]
    gc_col = jnp.concatenate([sum(_dot(tri_col, p[r]) for p in g_parts) for r in chunk_rows], axis=0)
    gc_row = jnp.concatenate([sum(_dot_tn(p[r], tri_row) for p in g_parts) for r in chunk_rows], axis=1)
    ii = lax.broadcasted_iota(jnp.int32, (c, 2 * dh), 0)
    lane = lax.broadcasted_iota(jnp.int32, (c, 2 * dh), 1)
    first = lane < dh
    jj = jnp.where(first, lane, lane - dh)
    incl = (ii >= jj) if forward else (ii <= jj)
    strict = (ii > jj) if forward else (ii < jj)
    last = c - 1 if forward else 0
    items = [(ci, p) for ci in range(t // c) for p in range(MIXER_HEADS // 2)]
    rows = lambda ci: slice(ci * c, (ci + 1) * c)
    part = lambda base: [u[rows(ci), base + 2 * p * dh:base + 2 * (p + 1) * dh] for ci, p in items]
    col_pair = lambda x, off: [jnp.where(first, x[rows(ci), off + 2 * p:off + 2 * p + 1],
                                         x[rows(ci), off + 2 * p + 1:off + 2 * p + 2]) for ci, p in items]
    q, k, v = part(0), part(w), part(2 * w)
    gcc = col_pair(gc_col, a_off)
    beta = col_pair(beta_all, b_off)
    gcr = [jnp.concatenate([gc_row[a_off + 2 * p:a_off + 2 * p + 1, rows(ci)],
                            gc_row[a_off + 2 * p + 1:a_off + 2 * p + 2, rows(ci)]], axis=1) for ci, p in items]
    g_last = [a[last:last + 1, :] for a in gcc]
    decay = [jnp.where(incl, jnp.exp(jnp.where(incl, a - b, 0.0)), 0.0) for a, b in zip(gcc, gcr)]
    kq = [_dot_nt(jnp.concatenate([kk, qq], axis=0).astype(BF16), _head_blockdiag(kk.astype(BF16)))
          for kk, qq in zip(k, q)]
    lmat = [jnp.where(strict, m[:c] * b * d, 0.0) for m, b, d in zip(kq, beta, decay)]
    attn = [(m[c:] * d).astype(BF16) for m, d in zip(kq, decay)]
    xinv = [-a for a in lmat]
    lp = [a.astype(BF16) for a in lmat]
    lp_bd = [_head_blockdiag(a) for a in lp]
    for _ in range(int(math.log2(c)) - 1):
        lp32 = [_dot(a, bd) for a, bd in zip(lp, lp_bd)]
        lp = [a.astype(BF16) for a in lp32]
        lp_bd = [_head_blockdiag(a) for a in lp]
        xinv = [xi + a32 + _dot(xi.astype(BF16), bd) for xi, a32, bd in zip(xinv, lp32, lp_bd)]
    egc = [jnp.exp(a) for a in gcc]
    xb = [a.astype(BF16) for a in xinv]
    vbeta = [vv * b for vv, b in zip(v, beta)]
    kbeta = [kk * (b * e) for kk, b, e in zip(k, beta, egc)]
    u_val = [r + _dot(xi, _head_blockdiag(r.astype(BF16))) for xi, r in zip(xb, vbeta)]
    w_dec = [r + _dot(xi, _head_blockdiag(r.astype(BF16))) for xi, r in zip(xb, kbeta)]
    wq = [jnp.concatenate([wd, qq * e], axis=0).astype(BF16) for wd, qq, e in zip(w_dec, q, egc)]
    k_state = [(kk * jnp.exp(gl - a)).astype(BF16) for kk, gl, a in zip(k, g_last, gcc)]
    e_last = [jnp.exp(gl) for gl in g_last]
    keyed = lambda vals: dict(zip(items, vals))
    return dict(wq=keyed(wq), u_val=keyed(u_val), attn=keyed(attn), k_state=keyed(k_state), e_last=keyed(e_last))


def _gdn_kernel(uf_ref, gbf_ref, ub_ref, gbb_ref, tril_ref, triu_ref, of_ref, ob_ref, sf_ref, sb_ref):
    n = pl.program_id(1)
    t = uf_ref.shape[0]
    c, dh = GDN_CHUNK, HEAD_DIM
    nch = t // c
    pairs = MIXER_HEADS // 2

    @pl.when(n == 0)
    def _():
        sf_ref[...] = jnp.zeros_like(sf_ref)
        sb_ref[...] = jnp.zeros_like(sb_ref)

    tril, triu = tril_ref[...], triu_ref[...]
    prep_f = _gdn_prepare(True, uf_ref[...], gbf_ref[...], tril, triu)
    prep_b = _gdn_prepare(False, ub_ref[...], gbb_ref[...], triu, tril)

    first = lax.broadcasted_iota(jnp.int32, (dh, 2 * dh), 1) < dh
    chains = [(prep_f, sf_ref, of_ref, p, True) for p in range(pairs)] + \
             [(prep_b, sb_ref, ob_ref, p, False) for p in range(pairs)]
    state = [s_ref[p] for _, s_ref, _, p, _ in chains]
    for step in range(nch):
        key = [(step if fw else nch - 1 - step, p) for _, _, _, p, fw in chains]
        ws = [_dot(pr['wq'][kk], _head_blockdiag(st.astype(BF16))) for (pr, *_), kk, st in zip(chains, key, state)]
        v_new = [pr['u_val'][kk] - a[:c] for (pr, *_), kk, a in zip(chains, key, ws)]
        vb = [a.astype(BF16) for a in v_new]
        out = [a[c:] + _dot(pr['attn'][kk], _head_blockdiag(b)) for (pr, *_), kk, a, b in zip(chains, key, ws, vb)]
        cross = [_dot_tn(pr['k_state'][kk], b) for (pr, *_), kk, b in zip(chains, key, vb)]
        state = [st * pr['e_last'][kk] + jnp.where(first, x[:dh], x[dh:])
                 for (pr, *_), kk, st, x in zip(chains, key, state, cross)]
        for (_, _, o_ref, p, _), (ci, _), o in zip(chains, key, out):
            o_ref[ci * c:(ci + 1) * c, 2 * p * dh:2 * (p + 1) * dh] = o
    for (_, s_ref, _, p, _), st in zip(chains, state):
        s_ref[p] = st


def _gated_deltanet(p_d, p_ab, conv_w, alog_vec, dtb_vec, b, s):
    u, gb = _gdn_pre(p_d, p_ab, conv_w, alog_vec, dtb_vec, b, s)
    c = GDN_CHUNK
    t = c * GDN_CHUNKS_PER_STEP
    nb = s // t
    w = MIXER_WIDTH
    r = np.arange(c)
    tril = jnp.asarray(r[:, None] >= r[None, :], BF16)
    triu = jnp.asarray(r[:, None] <= r[None, :], BF16)
    fwd = lambda bi, n: (bi * nb + n, 0)
    bwd = lambda bi, n: (bi * nb + nb - 1 - n, 0)
    const = lambda a: pl.BlockSpec(a.shape, lambda bi, n: (0,) * a.ndim)
    return pl.pallas_call(
        _gdn_kernel,
        grid=(b, nb),
        in_specs=[pl.BlockSpec((t, 3 * w), fwd), pl.BlockSpec((t, LANES), fwd),
                  pl.BlockSpec((t, 3 * w), bwd), pl.BlockSpec((t, LANES), bwd), const(tril), const(triu)],
        out_specs=[pl.BlockSpec((t, w), fwd), pl.BlockSpec((t, w), bwd)],
        out_shape=[jax.ShapeDtypeStruct((b * s, w), F32), jax.ShapeDtypeStruct((b * s, w), F32)],
        scratch_shapes=[pltpu.VMEM((MIXER_HEADS // 2, HEAD_DIM, 2 * HEAD_DIM), F32),
                        pltpu.VMEM((MIXER_HEADS // 2, HEAD_DIM, 2 * HEAD_DIM), F32)],
        compiler_params=_cparams("arbitrary", "arbitrary"),
    )(u, gb, u, gb, tril, triu)


def _mem_kv_kernel(m_ref, g_ref, wk_ref, wv_ref, k_ref, v_ref):
    h = _rms(m_ref[0], g_ref[...]).astype(BF16)
    k_ref[0] = _dot(h, wk_ref[...]).astype(BF16)
    v_ref[0] = _dot(h, wv_ref[...]).astype(BF16)


def _mem_kv(mem, g, wk, wv):
    b, nm, d = mem.shape
    const = lambda a: pl.BlockSpec(a.shape, lambda i: (0,) * a.ndim)
    blk = pl.BlockSpec((1, nm, d), lambda i: (i, 0, 0))
    return pl.pallas_call(
        _mem_kv_kernel, grid=(b,),
        in_specs=[blk, const(g), const(wk), const(wv)], out_specs=[blk, blk],
        out_shape=[jax.ShapeDtypeStruct((b, nm, d), BF16)] * 2,
        compiler_params=_cparams("parallel"),
    )(mem, g, wk, wv)


def _route_lanes(lg):
    lane_i = lax.broadcasted_iota(jnp.int32, lg.shape, 1)
    lane = lane_i.astype(F32)
    first_at = lambda vals, m: jnp.min(jnp.where(vals == m, lane, float(LANES)), axis=-1, keepdims=True)
    g_mask = lane_i < N_GROUPS
    gl = jnp.where(g_mask, lg, NEG_BIG)
    g_max = jnp.max(gl, axis=-1, keepdims=True)
    g_sum = jnp.sum(jnp.where(g_mask, jnp.exp(gl - g_max), 0.0), axis=-1, keepdims=True)
    g_p = 1.0 / g_sum
    g_idx = first_at(gl, g_max)
    lo = N_GROUPS + EXPERTS_PER_GROUP * g_idx
    e_mask = (lane >= lo) & (lane < lo + EXPERTS_PER_GROUP)
    el = jnp.where(e_mask, lg, NEG_BIG)
    m1 = jnp.max(el, axis=-1, keepdims=True)
    i1 = first_at(el, m1)
    el2 = jnp.where(lane == i1, NEG_BIG, el)
    m2 = jnp.max(el2, axis=-1, keepdims=True)
    i2 = first_at(el2, m2)
    r = jnp.exp(m2 - m1)
    w1 = g_p / (1.0 + r)
    w2 = g_p * r / (1.0 + r)
    e1, e2 = i1 - N_GROUPS, i2 - N_GROUPS
    e_lo, e_hi = jnp.minimum(e1, e2), jnp.maximum(e1, e2)
    w_lo, w_hi = jnp.where(e1 < e2, w1, w2), jnp.where(e1 < e2, w2, w1)
    a, b = e_lo - EXPERTS_PER_GROUP * g_idx, e_hi - EXPERTS_PER_GROUP * g_idx
    cls = g_idx * PAIRS_PER_GROUP + a * EXPERTS_PER_GROUP - a * (a + 1.0) * 0.5 + (b - a - 1.0)
    out = jnp.where(lane_i == 0, e_lo, 0.0)
    out = jnp.where(lane_i == 1, e_hi, out)
    out = jnp.where(lane_i == 2, w_lo, out)
    out = jnp.where(lane_i == 3, w_hi, out)
    out = jnp.where(lane_i == 4, cls, out)
    return out, jnp.where(lane == cls, 1.0, 0.0)


def _post_kernel(x_ref, ya_ref, yb_ref, yc_ref, of_ref, ob_ref, gate_ref, seg_ref, gg_ref, wout_ref, gq_ref,
                 wq_ref, k_ref, v_ref, wo_ref, gf_ref, wr_ref, br_ref, xo_ref, hx_ref, rt_ref, hist_ref):
    w = MIXER_WIDTH
    o = of_ref[...] + ob_ref[...]
    ms = _dot((o * o).astype(BF16), seg_ref[...])
    yd = o * lax.rsqrt(ms + EPS) * gg_ref[...] * _silu(gate_ref[...])
    mix = _dot(ya_ref[...], wout_ref[0:w, :]) + _dot(yb_ref[...], wout_ref[w:2 * w, :])
    mix = mix + _dot(yc_ref[...], wout_ref[2 * w:3 * w, :]) + _dot(yd.astype(BF16), wout_ref[3 * w:4 * w, :])
    x1 = x_ref[...] + mix
    q = _dot(_rms(x1, gq_ref[...]).astype(BF16), wq_ref[...]).astype(BF16)
    cols = [slice(h * MEM_HEAD_DIM, (h + 1) * MEM_HEAD_DIM) for h in range(MEM_HEADS)]
    sc = [_dot_nt(q[:, sl], k_ref[0, :, sl]) * (MEM_HEAD_DIM ** -0.5) for sl in cols]
    p = [jnp.exp(a - jnp.max(a, axis=-1, keepdims=True)) for a in sc]
    l = [jnp.sum(a, axis=-1, keepdims=True) for a in p]
    heads = [(_dot(a.astype(BF16), v_ref[0, :, sl]) / b).astype(BF16) for a, b, sl in zip(p, l, cols)]
    x2 = x1 + _dot(jnp.concatenate(heads, axis=-1), wo_ref[...])
    xo_ref[...] = x2
    hx = _rms(x2, gf_ref[...])
    _to_row_tiles(hx_ref, hx)
    lg = _dot(hx.astype(BF16), wr_ref[...]) + br_ref[...]
    route, chosen = _route_lanes(lg)
    rt_ref[...] = route
    hist_ref[...] = jnp.broadcast_to(jnp.sum(chosen, axis=0, keepdims=True), hist_ref.shape)


def _post(x, y_a, y_b, y_c, o_f, o_b, p_d, seg, gdn_g, w_out, g_q, w_q, k_mem, v_mem, w_o, g_f, w_r, b_r, s):
    n = x.shape[0]
    tm = POST_TILE
    w = MIXER_WIDTH
    per_seq = s // tm
    row = lambda c: pl.BlockSpec((tm, c), lambda i: (i, 0))
    const = lambda a: pl.BlockSpec(a.shape, lambda i: (0,) * a.ndim)
    kv = pl.BlockSpec((1,) + k_mem.shape[1:], lambda i: (i // per_seq, 0, 0))
    return pl.pallas_call(
        _post_kernel, grid=(n // tm,),
        in_specs=[row(D_MODEL), row(w), row(w), row(w), row(w), row(w),
                  pl.BlockSpec((tm, w), lambda i: (i, 3)), const(seg), const(gdn_g), const(w_out), const(g_q),
                  const(w_q), kv, kv, const(w_o), const(g_f), const(w_r), const(b_r)],
        out_specs=[row(D_MODEL), pl.BlockSpec((tm * ROW_TILES, LANES), lambda i: (i, 0)), row(LANES),
                   pl.BlockSpec((SUBLANES, LANES), lambda i: (i, 0))],
        out_shape=[jax.ShapeDtypeStruct((n, D_MODEL), F32), jax.ShapeDtypeStruct((n * ROW_TILES, LANES), F32),
                   jax.ShapeDtypeStruct((n, LANES), F32), jax.ShapeDtypeStruct((n // tm * SUBLANES, LANES), F32)],
        compiler_params=_cparams("parallel"),
    )(x, y_a, y_b, y_c, o_f, o_b, p_d, seg, gdn_g, w_out, g_q, w_q, k_mem, v_mem, w_o, g_f, w_r, b_r)


def _pair_class_experts():
    lo, hi = [], []
    for g in range(N_GROUPS):
        for a in range(EXPERTS_PER_GROUP):
            for b in range(a + 1, EXPERTS_PER_GROUP):
                lo.append(g * EXPERTS_PER_GROUP + a)
                hi.append(g * EXPERTS_PER_GROUP + b)
    return np.asarray(lo, np.int32), np.asarray(hi, np.int32)


def _dispatch(route, hist):
    n = route.shape[0]
    blk = MOE_BLOCK
    counts = jnp.sum(hist.reshape(-1, SUBLANES, LANES)[:, 0, :N_CLASSES], axis=0).astype(jnp.int32)
    padded = (counts + blk - 1) // blk * blk
    pad_end = jnp.cumsum(padded)
    n_blocks = n // blk + N_CLASSES
    blk_start = jnp.arange(n_blocks, dtype=jnp.int32) * blk
    blk_cls = jnp.minimum(jnp.sum(pad_end[None, :] <= blk_start[:, None], axis=1), N_CLASSES - 1).astype(jnp.int32)
    cls_lo, cls_hi = _pair_class_experts()
    in_blk = jnp.arange(blk, dtype=jnp.int32)[None, :]
    pad_key = jnp.where(in_blk < (padded - counts)[:, None], jnp.arange(N_CLASSES, dtype=jnp.int32)[:, None], N_CLASSES)
    keys = jnp.concatenate([route[:, 4].astype(jnp.int32), pad_key.reshape(-1)])
    toks = jnp.concatenate([jnp.arange(n, dtype=jnp.int32), jnp.tile(n + in_blk[0], N_CLASSES)])
    zeros = jnp.zeros((N_CLASSES * blk,), F32)
    w_lo = jnp.concatenate([route[:, TOP_K], zeros])
    w_hi = jnp.concatenate([route[:, TOP_K + 1], zeros])
    _, row_tok, w_lo, w_hi = lax.sort((keys, toks, w_lo, w_hi), num_keys=1)
    n_used = (pad_end[-1] // blk).astype(jnp.int32).reshape(1)
    return (n_used, jnp.asarray(cls_lo)[blk_cls], jnp.asarray(cls_hi)[blk_cls],
            jnp.concatenate([n + in_blk[0], row_tok]), jnp.stack([w_lo, w_hi], axis=-1))


def _moe_kernel(n_tokens, n_blocks, nu_ref, ea_ref, eb_ref, dst_ref, hx_hbm, rw_ref, wga_ref, wua_ref, wda_ref,
                wgb_ref, wub_ref, wdb_ref, y_hbm, xbuf, ybuf, gsem, ssem):
    j = pl.program_id(0)
    blk, rt = MOE_BLOCK, ROW_TILES
    n_used = nu_ref[0]
    slot = lax.rem(j, MOE_BUFFERS)
    ahead = lax.rem(j + 2, MOE_BUFFERS)
    tile = lambda ref, i: ref.at[pl.ds(pl.multiple_of(i * rt, rt), rt), :]
    dst = lambda jj, i: dst_ref[(jj + 1) * blk + i]
    clamp = lambda jj: jnp.minimum(jj, n_blocks - 1)

    def gather_row(jj, sl, i):
        tok = jnp.minimum(dst(jj, i), n_tokens - 1)
        pltpu.make_async_copy(tile(hx_hbm, tok), tile(xbuf.at[sl], i), gsem.at[sl]).start()

    def scatter_row(jj, sl, i):
        pltpu.make_async_copy(tile(ybuf.at[sl], i), tile(y_hbm, dst(jj, i)), ssem.at[sl]).start()

    def wait_gather(sl):
        pltpu.make_async_copy(hx_hbm.at[pl.ds(0, blk * rt), :], xbuf.at[sl], gsem.at[sl]).wait()

    def wait_scatter(sl):
        pltpu.make_async_copy(ybuf.at[sl], y_hbm.at[pl.ds(0, blk * rt), :], ssem.at[sl]).wait()

    def rolled(fn):
        lax.fori_loop(0, blk, lambda i, carry: (fn(i), carry)[1], 0, unroll=DMA_UNROLL)

    @pl.when(j == 0)
    def _():
        ybuf[MOE_BUFFERS - 1] = jnp.zeros((blk * rt, LANES), F32)
        rolled(lambda i: gather_row(0, 0, i))
        rolled(lambda i: gather_row(clamp(1), 1, i))

    @pl.when(j < n_used)
    def _():
        wait_gather(slot)

        @pl.when(j >= 2)
        def _():
            wait_scatter(slot)

        def issue_copies(piece):
            for i in range(piece * blk // MOE_PIECES, (piece + 1) * blk // MOE_PIECES):
                gather_row(clamp(j + 2), ahead, i)
                scatter_row(j - 1, ahead, i)

        xs = _from_row_tiles(xbuf.at[slot], 0, blk, rt).astype(BF16)
        half = D_EXPERT // 2
        hidden = []
        for e, (wg_ref, wu_ref) in enumerate(((wga_ref, wua_ref), (wgb_ref, wub_ref))):
            acts = []
            for c in range(2):
                cols = slice(c * half, (c + 1) * half)
                g = _dot(xs, wg_ref[0, 0, :, cols])
                issue_copies(4 * e + 2 * c)
                u = _dot(xs, wu_ref[0, 0, :, cols])
                issue_copies(4 * e + 2 * c + 1)
                acts.append((_silu(g) * u).astype(BF16))
            hidden.append(jnp.concatenate(acts, axis=1))
        w_a, w_b = rw_ref[:, 0:1], rw_ref[:, 1:2]
        quarter = D_MODEL // 4
        for c in range(4):
            cols = slice(c * quarter, (c + 1) * quarter)
            y = w_a * _dot(hidden[0], wda_ref[0, 0, :, cols]) + w_b * _dot(hidden[1], wdb_ref[0, 0, :, cols])
            for k in range(quarter // LANES):
                ybuf[slot, pl.ds(c * (quarter // LANES) + k, blk, stride=rt), :] = y[:, k * LANES:(k + 1) * LANES]
            issue_copies(8 + c)

    @pl.when(j == n_used)
    def _():
        after = lax.rem(j + 1, MOE_BUFFERS)
        wait_gather(slot)
        wait_gather(after)

        @pl.when(j >= 2)
        def _():
            wait_scatter(slot)

        wait_scatter(after)
        rolled(lambda i: scatter_row(j - 1, ahead, i))
        wait_scatter(ahead)


def _moe(hx_tiles, n_used, blk_ea, blk_eb, row_tok, row_w, w_gate, w_up, w_down, layer, n):
    blk = MOE_BLOCK
    n_blocks = blk_ea.shape[0]
    block = lambda j: jnp.minimum(j, n_blocks - 1)
    first = lambda j, nu, ea, eb, dst: (layer, ea[block(j)], 0, 0)
    second = lambda j, nu, ea, eb, dst: (layer, eb[block(j)], 0, 0)
    up, down = (1, 1, D_MODEL, D_EXPERT), (1, 1, D_EXPERT, D_MODEL)
    weights = pl.BlockSpec
    grid_spec = pltpu.PrefetchScalarGridSpec(
        num_scalar_prefetch=4, grid=(n_blocks + 1,),
        in_specs=[pl.BlockSpec(memory_space=pl.ANY),
                  pl.BlockSpec((blk, TOP_K), lambda j, nu, ea, eb, dst: (block(j), 0)),
                  weights(up, first), weights(up, first), weights(down, first),
                  weights(up, second), weights(up, second), weights(down, second)],
        out_specs=pl.BlockSpec(memory_space=pl.ANY),
        scratch_shapes=[pltpu.VMEM((MOE_BUFFERS, blk * ROW_TILES, LANES), F32),
                        pltpu.VMEM((MOE_BUFFERS, blk * ROW_TILES, LANES), F32),
                        pltpu.SemaphoreType.DMA((MOE_BUFFERS,)), pltpu.SemaphoreType.DMA((MOE_BUFFERS,))])
    return pl.pallas_call(
        functools.partial(_moe_kernel, n, n_blocks), grid_spec=grid_spec,
        out_shape=jax.ShapeDtypeStruct(((n + blk) * ROW_TILES, LANES), F32),
        compiler_params=_cparams("arbitrary"),
    )(n_used, blk_ea, blk_eb, row_tok, hx_tiles, row_w, w_gate, w_up, w_down, w_gate, w_up, w_down)


def _final_kernel(x_ref, y_ref, g_ref, o_ref):
    o_ref[...] = _rms(_add_moe(x_ref[...], y_ref), g_ref[...])


def _final(x, moe, g):
    n = x.shape[0]
    tm = PROJ_TILE
    return pl.pallas_call(
        _final_kernel, grid=(n // tm,),
        in_specs=[pl.BlockSpec((tm, D_MODEL), lambda i: (i, 0)), _moe_out_spec(tm),
                  pl.BlockSpec(g.shape, lambda i: (0, 0))],
        out_specs=pl.BlockSpec((tm, D_MODEL), lambda i: (i, 0)),
        out_shape=jax.ShapeDtypeStruct((n, D_MODEL), F32),
        compiler_params=_cparams("parallel"),
    )(x, moe, g)


def _lane_vec(v, width=LANES):
    v = v.reshape(1, -1).astype(F32)
    return jnp.pad(v, ((0, 0), (0, width - v.shape[1])))


def _block_diag(blocks):
    g, c, _ = blocks.shape
    eye = jnp.eye(g, dtype=blocks.dtype)
    return (eye[:, None, :, None] * blocks[:, :, None, :]).reshape(g * c, g * c)


def kernel(x, mem, mix_norm_g, w_in, na_rpb, ret_norm_g, pool_w, pool_scale, gdn_conv_w, gdn_a_log, gdn_dt_bias, gdn_norm_g, w_out, mem_q_norm_g, mem_kv_norm_g, mem_w_q, mem_w_k, mem_w_v, mem_w_o, ffn_norm_g, w_group, b_group, w_router, b_router, w_gate, w_up, w_down, final_norm_g):
    b, s, d = x.shape
    n = b * s
    depth = w_in.shape[0]
    row = lambda v: v.reshape(1, -1).astype(F32)
    ret_tables = _retention_tables(s)
    seg = _block_diag(jnp.full((MIXER_HEADS, HEAD_DIM, HEAD_DIM), 1.0 / HEAD_DIM, BF16))
    wg_bf, wu_bf, wd_bf = w_gate.astype(BF16), w_up.astype(BF16), w_down.astype(BF16)
    xs = x.reshape(n, d)
    moe = None
    for l in range(depth):
        w_pad = jnp.pad(w_in[l], ((0, 0), (0, P_IN_PAD - w_in.shape[2]))).astype(BF16)
        xs, p_a, p_b, p_c, p_d, p_ab = _norm_proj(xs, moe, row(mix_norm_g[l]), w_pad)
        y_a = _neighbourhood_attention(p_a, _na_bias_table(na_rpb[l]), b, s)
        y_b = _retention(p_b, row(ret_norm_g[l]), ret_tables, b, s)
        y_c = _pool(p_c, _block_diag(pool_w[l]).astype(BF16), row(pool_scale[l]), b, s)
        o_f, o_b = _gated_deltanet(p_d, p_ab, gdn_conv_w[l].astype(F32), _lane_vec(gdn_a_log[l]),
                                   _lane_vec(gdn_dt_bias[l]), b, s)
        k_mem, v_mem = _mem_kv(mem, row(mem_kv_norm_g[l]), mem_w_k[l].astype(BF16), mem_w_v[l].astype(BF16))
        w_r = jnp.pad(jnp.concatenate([w_group[l], w_router[l]], axis=1),
                      ((0, 0), (0, LANES - N_GROUPS - N_EXPERTS))).astype(F32)
        b_r = _lane_vec(jnp.concatenate([b_group[l], b_router[l]]))
        xs, hx_tiles, route, hist = _post(
            xs, y_a, y_b, y_c, o_f, o_b, p_d, seg, row(jnp.tile(gdn_norm_g[l], MIXER_HEADS)), w_out[l].astype(BF16),
            row(mem_q_norm_g[l]), mem_w_q[l].astype(BF16), k_mem, v_mem, mem_w_o[l].astype(BF16),
            row(ffn_norm_g[l]), w_r.astype(BF16), b_r, s)
        moe = _moe(hx_tiles, *_dispatch(route, hist), wg_bf, wu_bf, wd_bf, l, n)
    return _final(xs, moe, row(final_norm_g)).reshape(b, s, d)
```
